```python
import jax, jax.numpy as jnp
from jax import lax
import numpy as np

D_MODEL = 2048
BATCH = 8
SEQ = 4096
DEPTH = 1

HEAD_DIM = 128
ROPE_THETA = 10000.0
NSA_HEADS = 8
NSA_KV_GROUPS = 2
NSA_REP = NSA_HEADS // NSA_KV_GROUPS
CMP_LEN = 32
CMP_STRIDE = 16
CMP_HIDDEN = 256
SEL_LEN = 64
SEL_TOPK = 16
WIN_LEN = 512
SEL_QUERY_CHUNK = 64
BAND_BLOCK = 128
DIL_CONFIGS = ((128, 1), (512, 4), (2048, 16))
N_DIL = 3
DIL_HEADS = 4
N_EXPERTS = 32
TOP_K = 4
D_EXPERT = D_MODEL
SWIGLU_LIMIT = 7.0
SWIGLU_ALPHA = 1.702
MOE_BLOCK = 128
DN_ALPHA = (2.0 * DEPTH) ** 0.25
DN_BETA = (8.0 * DEPTH) ** -0.25
LN_EPS = 1e-5
NEG = -1e30
FORCE = 1e9
NSA_Q_W = NSA_HEADS * HEAD_DIM
NSA_KV_W = NSA_KV_GROUPS * HEAD_DIM
NSA_GATE_W = 3 * NSA_HEADS
DIL_GROUP_W = DIL_HEADS * HEAD_DIM
IN_W = NSA_Q_W + 6 * NSA_KV_W + NSA_GATE_W + 3 * N_DIL * DIL_GROUP_W + 2 * D_MODEL

kernel_name = "hybrid_nsa_dilated_moe_deepnorm"


def layer_norm(x, g, b):
    xf = x.astype(jnp.float32)
    mu = xf.mean(-1, keepdims=True)
    var = jnp.square(xf - mu).mean(-1, keepdims=True)
    return ((xf - mu) * lax.rsqrt(var + LN_EPS) * g + b).astype(x.dtype)


def rope_tables(seq):
    pos = jnp.arange(seq, dtype=jnp.float32)
    inv = ROPE_THETA ** (-jnp.arange(0, HEAD_DIM, 2, dtype=jnp.float32) / HEAD_DIM)
    ang = pos[:, None] * inv[None, :]
    return jnp.cos(ang), jnp.sin(ang)


def apply_rope(x, cos, sin):
    half = HEAD_DIM // 2
    c = cos[:, None, :].astype(x.dtype)
    s = sin[:, None, :].astype(x.dtype)
    x1, x2 = x[..., :half], x[..., half:]
    return jnp.concatenate([x1 * c - x2 * s, x2 * c + x1 * s], axis=-1)


def masked_softmax(scores, mask):
    s = jnp.where(mask, scores, NEG)
    m = s.max(-1, keepdims=True)
    p = jnp.where(mask, jnp.exp(s - m), 0.0)
    den = p.sum(-1, keepdims=True)
    safe = jnp.where(den > 0, den, 1.0)
    return p / safe, (m + jnp.log(safe))[..., 0]


def banded_attention(q, k, v, max_dist, blk):
    B, L, G, R, Dh = q.shape
    n_prev = -(-max_dist // blk)
    nb = -(-L // blk)
    pad = nb * blk - L
    q = jnp.pad(q, ((0, 0), (0, pad), (0, 0), (0, 0), (0, 0)))
    k = jnp.pad(k, ((0, 0), (n_prev * blk, pad), (0, 0), (0, 0)))
    v = jnp.pad(v, ((0, 0), (n_prev * blk, pad), (0, 0), (0, 0)))
    qb = q.reshape(B, nb, blk, G, R, Dh)
    kw = jnp.concatenate([k[:, i * blk:(i + nb) * blk].reshape(B, nb, blk, G, Dh) for i in range(n_prev + 1)], axis=2)
    vw = jnp.concatenate([v[:, i * blk:(i + nb) * blk].reshape(B, nb, blk, G, Dh) for i in range(n_prev + 1)], axis=2)
    nk = (n_prev + 1) * blk
    qi = jnp.arange(blk)[:, None]
    ki = jnp.arange(nk)[None, :]
    diff = n_prev * blk + qi - ki
    k_pos = (jnp.arange(nb)[:, None, None] - n_prev) * blk + ki[None]
    mask = (diff >= 0)[None] & (diff <= max_dist)[None] & (k_pos >= 0)
    s = jnp.einsum("bnqgrd,bnkgd->bngrqk", qb, kw).astype(jnp.float32) * (Dh ** -0.5)
    p, lse = masked_softmax(s, mask[None, :, None, None])
    o = jnp.einsum("bngrqk,bnkgd->bnqgrd", p.astype(vw.dtype), vw)
    o = o.reshape(B, nb * blk, G, R, Dh)[:, :L]
    lse = lse.transpose(0, 1, 4, 2, 3).reshape(B, nb * blk, G, R)[:, :L]
    return o, lse


def nsa_attention(q, k_c, v_c, k_s, v_s, k_w, v_w, gate_logits, pos_k, pos_v, ck_w1, ck_w2, cv_w1, cv_w2):
    B, S, G, R, Dh = q.shape
    scale = Dh ** -0.5
    t = jnp.arange(S)
    n_cmp = (S - CMP_LEN) // CMP_STRIDE + 1
    c_start = jnp.arange(n_cmp) * CMP_STRIDE
    idx = c_start[:, None] + jnp.arange(CMP_LEN)[None, :]

    def compress(src, pos, w1, w2):
        blocks = src[:, idx] + pos[None, None, :, None, :]
        flat = blocks.transpose(0, 1, 3, 2, 4).reshape(B, n_cmp, G, CMP_LEN * Dh)
        return jax.nn.gelu(flat @ w1) @ w2

    kc = compress(k_c, pos_k, ck_w1, ck_w2)
    vc = compress(v_c, pos_v, cv_w1, cv_w2)
    s_cmp = jnp.einsum("bsgrd,bcgd->bgrsc", q, kc).astype(jnp.float32) * scale
    mask_cmp = (c_start + CMP_LEN - 1)[None, :] <= t[:, None]
    p_cmp, _ = masked_softmax(s_cmp, mask_cmp)
    o_cmp = jnp.einsum("bgrsc,bcgd->bsgrd", p_cmp.astype(vc.dtype), vc)
    n_slc = S // SEL_LEN
    j = jnp.arange(n_slc)
    overlap = (c_start[:, None] < (j[None, :] + 1) * SEL_LEN) & (c_start[:, None] + CMP_LEN > j[None, :] * SEL_LEN)
    imp = jnp.einsum("bgrsc,cj->bgsj", p_cmp, overlap.astype(jnp.float32))
    cur = t // SEL_LEN
    forced = (j[None, :] == 0) | (j[None, :] == cur[:, None]) | (j[None, :] == cur[:, None] - 1)
    future = j[None, :] > cur[:, None]
    imp = jnp.where(forced, FORCE, jnp.where(future, NEG, imp))
    n_sel = min(SEL_TOPK, n_slc)
    _, sel = lax.top_k(imp, n_sel)
    kb = k_s.reshape(B, n_slc, SEL_LEN, G, Dh).transpose(0, 3, 1, 2, 4)
    vb = v_s.reshape(B, n_slc, SEL_LEN, G, Dh).transpose(0, 3, 1, 2, 4)
    qc_len = SEL_QUERY_CHUNK
    nq = S // qc_len
    q_ch = q.reshape(B, nq, qc_len, G, R, Dh).transpose(1, 0, 2, 3, 4, 5)
    sel_ch = sel.reshape(B, G, nq, qc_len, n_sel).transpose(2, 0, 1, 3, 4)
    pos_ch = t.reshape(nq, qc_len)
    bi = jnp.arange(B)[:, None, None, None]
    gi = jnp.arange(G)[None, :, None, None]
    nk = n_sel * SEL_LEN

    def sel_block(args):
        qc, sc, pc = args
        kg = kb[bi, gi, sc].reshape(B, G, qc_len, nk, Dh)
        vg = vb[bi, gi, sc].reshape(B, G, qc_len, nk, Dh)
        kpos = (sc[..., None] * SEL_LEN + jnp.arange(SEL_LEN)).reshape(B, G, qc_len, nk)
        mask = (kpos <= pc[None, None, :, None])[:, :, None]
        s = jnp.einsum("bqgrd,bgqkd->bgrqk", qc, kg).astype(jnp.float32) * scale
        p, _ = masked_softmax(s, mask)
        return jnp.einsum("bgrqk,bgqkd->bqgrd", p.astype(vg.dtype), vg)

    o_slc = lax.map(sel_block, (q_ch, sel_ch, pos_ch))
    o_slc = o_slc.transpose(1, 0, 2, 3, 4, 5).reshape(B, S, G, R, Dh)
    o_win, _ = banded_attention(q, k_w, v_w, WIN_LEN - 1, BAND_BLOCK)
    g = jax.nn.sigmoid(gate_logits)
    return g[..., 0:1] * o_cmp + g[..., 1:2] * o_slc + g[..., 2:3] * o_win


def dilated_attention(qs, ks, vs):
    outs, lses = [], []
    for (w, d), q, k, v in zip(DIL_CONFIGS, qs, ks, vs):
        B, S, H, Dh = q.shape
        ls = S // d

        def to_sub(a):
            return a.reshape(B, ls, d, H, Dh).transpose(0, 2, 1, 3, 4).reshape(B * d, ls, H, Dh)

        o, lse = banded_attention(to_sub(q)[:, :, :, None], to_sub(k), to_sub(v), w // d, BAND_BLOCK)
        outs.append(o[:, :, :, 0].reshape(B, d, ls, H, Dh).transpose(0, 2, 1, 3, 4).reshape(B, S, H, Dh))
        lses.append(lse[..., 0].reshape(B, d, ls, H).transpose(0, 2, 1, 3).reshape(B, S, H))
    wts = jax.nn.softmax(jnp.stack(lses), axis=0)
    return jnp.einsum("gbsh,gbshd->bshd", wts.astype(outs[0].dtype), jnp.stack(outs))


def hybrid_mixer(x, w_in, b_in, pos_k, pos_v, ck_w1, ck_w2, cv_w1, cv_w2, w_br_nsa, w_br_dil, w_out):
    B, S, _ = x.shape
    proj = x @ w_in + b_in
    sizes = [NSA_Q_W] + [NSA_KV_W] * 6 + [NSA_GATE_W] + [DIL_GROUP_W] * (3 * N_DIL) + [D_MODEL, D_MODEL]
    offs, acc = [], 0
    for sz in sizes[:-1]:
        acc += sz
        offs.append(acc)
    parts = jnp.split(proj, offs, axis=-1)
    cos, sin = rope_tables(S)

    def heads(a):
        return a.reshape(B, S, -1, HEAD_DIM)

    def rope(a):
        return apply_rope(heads(a), cos, sin)

    q_n = rope(parts[0]).reshape(B, S, NSA_KV_GROUPS, NSA_REP, HEAD_DIM)
    k_c, k_s, k_w = rope(parts[1]), rope(parts[3]), rope(parts[5])
    v_c, v_s, v_w = heads(parts[2]), heads(parts[4]), heads(parts[6])
    gl = parts[7].reshape(B, S, NSA_KV_GROUPS, NSA_REP, 3)
    o_nsa = nsa_attention(q_n, k_c, v_c, k_s, v_s, k_w, v_w, gl, pos_k, pos_v, ck_w1, ck_w2, cv_w1, cv_w2)
    dil = parts[8:8 + 3 * N_DIL]
    o_dil = dilated_attention([rope(dil[3 * i]) for i in range(N_DIL)],
                              [rope(dil[3 * i + 1]) for i in range(N_DIL)],
                              [heads(dil[3 * i + 2]) for i in range(N_DIL)])
    gate_a, gate_b = parts[8 + 3 * N_DIL], parts[9 + 3 * N_DIL]
    y_a = o_nsa.reshape(B, S, -1) @ w_br_nsa
    y_b = o_dil.reshape(B, S, -1) @ w_br_dil
    merged = jax.nn.sigmoid(gate_a) * y_a + jax.nn.sigmoid(gate_b) * y_b
    return merged @ w_out


def clamped_swiglu(hu):
    glu, lin = jnp.split(hu, 2, axis=-1)
    glu = jnp.minimum(glu, SWIGLU_LIMIT)
    lin = jnp.clip(lin, -SWIGLU_LIMIT, SWIGLU_LIMIT)
    return glu * jax.nn.sigmoid(SWIGLU_ALPHA * glu) * (lin + 1.0)


def moe(h, w_router, b_router, w_up, b_up, w_down, b_down):
    B, S, D = h.shape
    T = B * S
    hf = h.reshape(T, D)
    logits = (hf @ w_router + b_router).astype(jnp.float32)
    top_vals, top_idx = lax.top_k(logits, TOP_K)
    gates = jax.nn.softmax(top_vals, axis=-1)
    A = T * TOP_K
    e_flat = top_idx.reshape(A)
    tok_flat = jnp.arange(A) // TOP_K
    w_flat = gates.reshape(A)
    order = jnp.argsort(e_flat)
    e_sorted, tok_sorted, w_sorted = e_flat[order], tok_flat[order], w_flat[order]
    counts = jnp.bincount(e_flat, length=N_EXPERTS)
    starts = jnp.cumsum(counts) - counts
    padded = (counts + MOE_BLOCK - 1) // MOE_BLOCK * MOE_BLOCK
    pad_end = jnp.cumsum(padded)
    pad_start = pad_end - padded
    dest = pad_start[e_sorted] + (jnp.arange(A) - starts[e_sorted])
    P = A + N_EXPERTS * MOE_BLOCK
    n_blk = P // MOE_BLOCK
    buf_tok = jnp.full((P,), T, dtype=jnp.int32).at[dest].set(tok_sorted.astype(jnp.int32))
    buf_w = jnp.zeros((P,), jnp.float32).at[dest].set(w_sorted)
    blk_expert = jnp.minimum(jnp.searchsorted(pad_end, jnp.arange(n_blk) * MOE_BLOCK, side="right"), N_EXPERTS - 1)
    h_pad = jnp.concatenate([hf, jnp.zeros((1, D), hf.dtype)], axis=0)
    xb = h_pad[buf_tok].reshape(n_blk, MOE_BLOCK, D)

    def expert_block(args):
        xblk, e = args
        hu = xblk @ w_up[e] + b_up[e]
        return clamped_swiglu(hu) @ w_down[e] + b_down[e]

    yb = lax.map(expert_block, (xb, blk_expert))
    y = yb.reshape(P, D) * buf_w[:, None].astype(yb.dtype)
    out = jax.ops.segment_sum(y, buf_tok, num_segments=T + 1)[:T]
    return out.reshape(B, S, D)


def setup_inputs(seed: int = 0) -> dict:
    key = jax.random.key(seed)
    ks = jax.random.split(key, 22)
    L = DEPTH

    def nrm(k, shape, scale):
        return jax.random.normal(k, shape, jnp.float32) * scale

    return {
        "x": nrm(ks[0], (BATCH, SEQ, D_MODEL), 1.0),
        "w_in": nrm(ks[1], (L, D_MODEL, IN_W), D_MODEL ** -0.5),
        "b_in": nrm(ks[2], (L, IN_W), 0.01),
        "cmp_pos_k": nrm(ks[3], (L, CMP_LEN, HEAD_DIM), 0.1),
        "cmp_pos_v": nrm(ks[4], (L, CMP_LEN, HEAD_DIM), 0.1),
        "cmp_k_w1": nrm(ks[5], (L, CMP_LEN * HEAD_DIM, CMP_HIDDEN), (CMP_LEN * HEAD_DIM) ** -0.5),
        "cmp_k_w2": nrm(ks[6], (L, CMP_HIDDEN, HEAD_DIM), CMP_HIDDEN ** -0.5),
        "cmp_v_w1": nrm(ks[7], (L, CMP_LEN * HEAD_DIM, CMP_HIDDEN), (CMP_LEN * HEAD_DIM) ** -0.5),
        "cmp_v_w2": nrm(ks[8], (L, CMP_HIDDEN, HEAD_DIM), CMP_HIDDEN ** -0.5),
        "w_br_nsa": nrm(ks[9], (L, NSA_HEADS * HEAD_DIM, D_MODEL), (NSA_HEADS * HEAD_DIM) ** -0.5),
        "w_br_dil": nrm(ks[10], (L, DIL_HEADS * HEAD_DIM, D_MODEL), (DIL_HEADS * HEAD_DIM) ** -0.5),
        "w_out": nrm(ks[11], (L, D_MODEL, D_MODEL), D_MODEL ** -0.5 * DN_BETA),
        "ln1_g": 1.0 + nrm(ks[12], (L, D_MODEL), 0.02),
        "ln1_b": nrm(ks[13], (L, D_MODEL), 0.02),
        "w_router": nrm(ks[14], (L, D_MODEL, N_EXPERTS), D_MODEL ** -0.5),
        "b_router": nrm(ks[15], (L, N_EXPERTS), 0.01),
        "w_up": nrm(ks[16], (L, N_EXPERTS, D_MODEL, 2 * D_EXPERT), D_MODEL ** -0.5),
        "b_up": nrm(ks[17], (L, N_EXPERTS, 2 * D_EXPERT), 0.01),
        "w_down": nrm(ks[18], (L, N_EXPERTS, D_EXPERT, D_MODEL), D_EXPERT ** -0.5 * DN_BETA),
        "b_down": nrm(ks[19], (L, N_EXPERTS, D_MODEL), 0.01),
        "ln2_g": 1.0 + nrm(ks[20], (L, D_MODEL), 0.02),
        "ln2_b": nrm(ks[21], (L, D_MODEL), 0.02),
    }


def reference(x, w_in, b_in, cmp_pos_k, cmp_pos_v, cmp_k_w1, cmp_k_w2, cmp_v_w1, cmp_v_w2,
              w_br_nsa, w_br_dil, w_out, ln1_g, ln1_b, w_router, b_router, w_up, b_up,
              w_down, b_down, ln2_g, ln2_b):
    h = x
    for l in range(DEPTH):
        mix = hybrid_mixer(h, w_in[l], b_in[l], cmp_pos_k[l], cmp_pos_v[l], cmp_k_w1[l], cmp_k_w2[l],
                           cmp_v_w1[l], cmp_v_w2[l], w_br_nsa[l], w_br_dil[l], w_out[l])
        h = layer_norm(DN_ALPHA * h + mix, ln1_g[l], ln1_b[l])
        ffn = moe(h, w_router[l], b_router[l], w_up[l], b_up[l], w_down[l], b_down[l])
        h = layer_norm(DN_ALPHA * h + ffn, ln2_g[l], ln2_b[l])
    return h
```

```python
import functools

import jax
import jax.numpy as jnp
from jax import lax
from jax.experimental import pallas as pl
from jax.experimental.pallas import tpu as pltpu

F32 = jnp.float32
BF16 = jnp.bfloat16

HEAD_DIM = 128
LANES = 128
ROPE_THETA = 10000.0
NSA_HEADS = 8
NSA_KV_GROUPS = 2
NSA_REP = NSA_HEADS // NSA_KV_GROUPS
CMP_LEN = 32
CMP_STRIDE = 16
SEL_LEN = 64
SEL_TOPK = 16
WIN_LEN = 512
DIL_CONFIGS = ((128, 1), (512, 4), (2048, 16))
N_DIL = len(DIL_CONFIGS)
DIL_HEADS = 4
TOP_K = 4
SWIGLU_LIMIT = 7.0
SWIGLU_ALPHA = 1.702
LN_EPS = 1e-5
NEG = -1e30
FORCE = 1e9
SCALE = HEAD_DIM ** -0.5

GROUP_W = NSA_REP * HEAD_DIM
DIL_W = DIL_HEADS * HEAD_DIM
VMEM_LIMIT = 56 * 1024 * 1024


def _cparams(sem, vmem=None):
    return pltpu.CompilerParams(dimension_semantics=sem, vmem_limit_bytes=vmem)


def _masked_softmax(s, mask):
    s = jnp.where(mask, s, NEG)
    m = s.max(-1, keepdims=True)
    p = jnp.where(mask, jnp.exp(s - m), 0.0)
    den = p.sum(-1, keepdims=True)
    safe = jnp.where(den > 0, den, 1.0)
    return p, m, safe


def _qk(q, k):
    return lax.dot_general(q, k, (((1,), (1,)), ((), ())), preferred_element_type=F32)


def _proj_kernel(flags_ref, x_ref, w_ref, b_ref, cos_ref, sin_ref, o_ref, xb_ref):
    j = pl.program_id(3)
    n_sub = o_ref.shape[-1] // LANES

    @pl.when(j == 0)
    def _():
        xb_ref[...] = x_ref[...].astype(BF16)

    acc = jnp.dot(xb_ref[...], w_ref[...], preferred_element_type=F32) + b_ref[...]
    for u in range(n_sub):
        a = acc[:, u * LANES:(u + 1) * LANES]
        roped = a * cos_ref[...] + pltpu.roll(a, HEAD_DIM // 2, 1) * sin_ref[...]
        use_rope = flags_ref[j * n_sub + u] == 1
        o_ref[:, u * LANES:(u + 1) * LANES] = jnp.where(use_rope, roped, a).astype(o_ref.dtype)


def _project(x, w, b, flags, cosx, sinx, d, tm, tn):
    B, S, D = x.shape
    N = w.shape[1]
    L = S // d
    tm = min(tm, L)
    xv = x.reshape(B, L, d * D)
    cv = cosx.reshape(L, d * HEAD_DIM)
    sv = sinx.reshape(L, d * HEAD_DIM)
    grid_spec = pltpu.PrefetchScalarGridSpec(
        num_scalar_prefetch=1,
        grid=(B, d, L // tm, N // tn),
        in_specs=[
            pl.BlockSpec((None, tm, D), lambda bb, r, i, j, f: (bb, i, r)),
            pl.BlockSpec((D, tn), lambda bb, r, i, j, f: (0, j)),
            pl.BlockSpec((1, tn), lambda bb, r, i, j, f: (0, j)),
            pl.BlockSpec((tm, HEAD_DIM), lambda bb, r, i, j, f: (i, r)),
            pl.BlockSpec((tm, HEAD_DIM), lambda bb, r, i, j, f: (i, r)),
        ],
        out_specs=pl.BlockSpec((None, None, tm, tn), lambda bb, r, i, j, f: (bb, r, i, j)),
        scratch_shapes=[pltpu.VMEM((tm, D), BF16)],
    )
    return pl.pallas_call(
        _proj_kernel,
        out_shape=jax.ShapeDtypeStruct((B, d, L, N), BF16),
        grid_spec=grid_spec,
        compiler_params=_cparams(("parallel", "arbitrary", "arbitrary", "arbitrary"), VMEM_LIMIT),
        name="proj_rope",
    )(flags, xv, w, b, cv, sv)


def _gelu_tanh(x):
    return 0.5 * x * (1.0 + jnp.tanh(0.7978845608028654 * (x + 0.044715 * x * x * x)))


def _compress_kernel(x_ref, pos_ref, w1_ref, w2_ref, o_ref):
    half = x_ref.shape[-1]
    nc = x_ref.shape[0]
    x = x_ref[...].astype(F32)
    lo = (x + pos_ref[0:1, :]).astype(BF16)
    hi = (x + pos_ref[1:2, :]).astype(BF16)
    y_lo = jnp.dot(lo, w1_ref[0:half, :], preferred_element_type=F32)
    y_hi = jnp.dot(hi, w1_ref[half:2 * half, :], preferred_element_type=F32)
    h = y_lo + pltpu.roll(y_hi, nc - 1, 0)
    g = _gelu_tanh(h).astype(BF16)
    o_ref[...] = jnp.dot(g, w2_ref[...], preferred_element_type=F32).astype(o_ref.dtype)


def _compress(xkv, pos, w1, w2):
    _, B, G, nC, half = xkv.shape
    hid = w1.shape[-1]
    return pl.pallas_call(
        _compress_kernel,
        out_shape=jax.ShapeDtypeStruct((2, B, G, nC, HEAD_DIM), BF16),
        grid=(2, B, G),
        in_specs=[
            pl.BlockSpec((None, None, None, nC, half), lambda a, bb, g: (a, bb, g, 0, 0)),
            pl.BlockSpec((None, 2, half), lambda a, bb, g: (a, 0, 0)),
            pl.BlockSpec((None, 2 * half, hid), lambda a, bb, g: (a, 0, 0)),
            pl.BlockSpec((None, hid, HEAD_DIM), lambda a, bb, g: (a, 0, 0)),
        ],
        out_specs=pl.BlockSpec((None, None, None, nC, HEAD_DIM), lambda a, bb, g: (a, bb, g, 0, 0)),
        compiler_params=_cparams(("arbitrary", "arbitrary", "arbitrary"), VMEM_LIMIT),
        name="compress_mlp",
    )(xkv, pos, w1, w2)


def _cmp_attn_kernel(q_ref, kc_ref, vc_ref, ov_ref, o_ref, sel_ref, *, n_slc, n_sel):
    qi = pl.program_id(2)
    tq = q_ref.shape[0]
    nc = kc_ref.shape[0]
    t = qi * tq + lax.broadcasted_iota(jnp.int32, (tq, nc), 0)
    c = lax.broadcasted_iota(jnp.int32, (tq, nc), 1)
    mask = (c * CMP_STRIDE + (CMP_LEN - 1)) <= t
    kc = kc_ref[...]
    vc = vc_ref[...]
    ps = jnp.zeros((tq, nc), F32)
    for h in range(NSA_REP):
        s = _qk(q_ref[:, h * HEAD_DIM:(h + 1) * HEAD_DIM], kc) * SCALE
        p, _, safe = _masked_softmax(s, mask)
        p = p / safe
        o = jnp.dot(p.astype(BF16), vc, preferred_element_type=F32)
        o_ref[:, h * HEAD_DIM:(h + 1) * HEAD_DIM] = o.astype(o_ref.dtype)
        ps = ps + p
    imp = jnp.dot(ps.astype(BF16), ov_ref[...], preferred_element_type=F32)
    tj = qi * tq + lax.broadcasted_iota(jnp.int32, (tq, LANES), 0)
    j = lax.broadcasted_iota(jnp.int32, (tq, LANES), 1)
    cur = tj // SEL_LEN
    forced = (j == 0) | (j == cur) | (j == cur - 1)
    imp = jnp.where(forced, FORCE, jnp.where(j > cur, NEG, imp))
    imp_t = imp.T[0:n_slc, :]
    jj = lax.broadcasted_iota(jnp.int32, (n_slc, tq), 0)
    rank = jnp.zeros((n_slc, tq), jnp.int32)
    for j2 in range(n_slc):
        row = imp_t[j2:j2 + 1, :]
        ahead = (row > imp_t) | ((row == imp_t) & (jj > j2))
        rank = rank + ahead.astype(jnp.int32)
    sel_t = (rank < n_sel).astype(F32)
    if n_slc < LANES:
        sel_t = jnp.concatenate([sel_t, jnp.zeros((LANES - n_slc, tq), F32)], axis=0)
    sel_ref[...] = sel_t.T.astype(sel_ref.dtype)


def _cmp_attention(main, kvc, overlap, q_blk0, tq):
    B, _, S, _ = main.shape
    G = NSA_KV_GROUPS
    nC = kvc.shape[3]
    n_slc = S // SEL_LEN
    n_sel = min(SEL_TOPK, n_slc)
    kern = functools.partial(_cmp_attn_kernel, n_slc=n_slc, n_sel=n_sel)
    return pl.pallas_call(
        kern,
        out_shape=(jax.ShapeDtypeStruct((B, S, NSA_HEADS * HEAD_DIM), BF16),
                   jax.ShapeDtypeStruct((B, G, S, LANES), BF16)),
        grid=(B, G, S // tq),
        in_specs=[
            pl.BlockSpec((None, None, tq, GROUP_W), lambda bb, g, i: (bb, 0, i, q_blk0 + g)),
            pl.BlockSpec((None, None, None, nC, HEAD_DIM), lambda bb, g, i: (0, bb, g, 0, 0)),
            pl.BlockSpec((None, None, None, nC, HEAD_DIM), lambda bb, g, i: (1, bb, g, 0, 0)),
            pl.BlockSpec((nC, LANES), lambda bb, g, i: (0, 0)),
        ],
        out_specs=(pl.BlockSpec((None, tq, GROUP_W), lambda bb, g, i: (bb, i, g)),
                   pl.BlockSpec((None, None, tq, LANES), lambda bb, g, i: (bb, g, i, 0))),
        compiler_params=_cparams(("parallel", "arbitrary", "arbitrary"), VMEM_LIMIT),
        name="cmp_attn_select",
    )(main, kvc, kvc, overlap)


def _sel_attn_kernel(q_ref, k_ref, v_ref, sel_ref, e_ref, o_ref, m_sc, l_sc, acc_sc):
    qi = pl.program_id(2)
    kj = pl.program_id(3)
    tq = q_ref.shape[0]
    tk = k_ref.shape[0]

    @pl.when(kj == 0)
    def _():
        m_sc[...] = jnp.full(m_sc.shape, NEG, F32)
        l_sc[...] = jnp.zeros(l_sc.shape, F32)
        acc_sc[...] = jnp.zeros(acc_sc.shape, F32)

    @pl.when(kj * tk <= qi * tq + (tq - 1))
    def _():
        selm = jnp.dot(sel_ref[...], e_ref[...], preferred_element_type=F32)
        qpos = qi * tq + lax.broadcasted_iota(jnp.int32, (tq, tk), 0)
        kpos = kj * tk + lax.broadcasted_iota(jnp.int32, (tq, tk), 1)
        valid = (selm > 0.5) & (kpos <= qpos)
        k = k_ref[...]
        v = v_ref[...]
        for h in range(NSA_REP):
            s = _qk(q_ref[:, h * HEAD_DIM:(h + 1) * HEAD_DIM], k) * SCALE
            s = jnp.where(valid, s, NEG)
            m_old = m_sc[h]
            m_new = jnp.maximum(m_old, s.max(-1, keepdims=True))
            p = jnp.where(valid, jnp.exp(s - m_new), 0.0)
            alpha = jnp.exp(m_old - m_new)
            l_sc[h] = alpha * l_sc[h] + p.sum(-1, keepdims=True)
            acc_sc[h] = alpha * acc_sc[h] + jnp.dot(p.astype(BF16), v, preferred_element_type=F32)
            m_sc[h] = m_new

    @pl.when(kj == pl.num_programs(3) - 1)
    def _():
        for h in range(NSA_REP):
            l = l_sc[h]
            safe = jnp.where(l > 0, l, 1.0)
            o_ref[:, h * HEAD_DIM:(h + 1) * HEAD_DIM] = (acc_sc[h] / safe).astype(o_ref.dtype)


def _sel_attention(main, sel, expand, q_blk0, k_tile0, v_tile0, tq, tk):
    B, _, S, _ = main.shape
    G = NSA_KV_GROUPS
    tq = min(tq, S)
    tk = min(tk, S)

    def kv_idx(i, j):
        return jnp.minimum(j, (i * tq + tq - 1) // tk)

    return pl.pallas_call(
        _sel_attn_kernel,
        out_shape=jax.ShapeDtypeStruct((B, S, NSA_HEADS * HEAD_DIM), BF16),
        grid=(B, G, S // tq, S // tk),
        in_specs=[
            pl.BlockSpec((None, None, tq, GROUP_W), lambda bb, g, i, j: (bb, 0, i, q_blk0 + g)),
            pl.BlockSpec((None, None, tk, HEAD_DIM), lambda bb, g, i, j: (bb, 0, kv_idx(i, j), k_tile0 + g)),
            pl.BlockSpec((None, None, tk, HEAD_DIM), lambda bb, g, i, j: (bb, 0, kv_idx(i, j), v_tile0 + g)),
            pl.BlockSpec((None, None, tq, LANES), lambda bb, g, i, j: (bb, g, i, 0)),
            pl.BlockSpec((LANES, tk), lambda bb, g, i, j: (0, kv_idx(i, j))),
        ],
        out_specs=pl.BlockSpec((None, tq, GROUP_W), lambda bb, g, i, j: (bb, i, g)),
        scratch_shapes=[pltpu.VMEM((NSA_REP, tq, 1), F32), pltpu.VMEM((NSA_REP, tq, 1), F32),
                        pltpu.VMEM((NSA_REP, tq, HEAD_DIM), F32)],
        compiler_params=_cparams(("parallel", "arbitrary", "arbitrary", "arbitrary"), VMEM_LIMIT),
        name="selected_attn",
    )(main, main, main, sel, expand)


def _band_kernel(*refs, n_kt, max_dist, kv_heads, with_lse):
    q_ref = refs[0]
    k_refs = refs[1:1 + n_kt]
    v_refs = refs[1 + n_kt:1 + 2 * n_kt]
    o_ref = refs[1 + 2 * n_kt]
    qi = pl.program_id(2)
    tq = q_ref.shape[0]
    tk = k_refs[0].shape[0]
    nk = n_kt * tk
    qpos = qi * tq + lax.broadcasted_iota(jnp.int32, (tq, nk), 0)
    kpos = (qi - (n_kt - 1)) * tk + lax.broadcasted_iota(jnp.int32, (tq, nk), 1)
    diff = qpos - kpos
    mask = (kpos >= 0) & (diff >= 0) & (diff <= max_dist)
    k = jnp.concatenate([r[...] for r in k_refs], axis=0)
    v = jnp.concatenate([r[...] for r in v_refs], axis=0)
    for h in range(q_ref.shape[1] // HEAD_DIM):
        hk = h if kv_heads > 1 else 0
        s = _qk(q_ref[:, h * HEAD_DIM:(h + 1) * HEAD_DIM], k[:, hk * HEAD_DIM:(hk + 1) * HEAD_DIM]) * SCALE
        p, m, safe = _masked_softmax(s, mask)
        o = jnp.dot(p.astype(BF16), v[:, hk * HEAD_DIM:(hk + 1) * HEAD_DIM], preferred_element_type=F32)
        o_ref[:, h * HEAD_DIM:(h + 1) * HEAD_DIM] = (o / safe).astype(o_ref.dtype)
        if with_lse:
            lse_ref = refs[2 + 2 * n_kt]
            lse_ref[:, h * HEAD_DIM:(h + 1) * HEAD_DIM] = jnp.broadcast_to(m + jnp.log(safe), (tq, HEAD_DIM))


def _band_attention(src, grid, q_map, k_map, v_map, kv_width, out_shape, o_map, t, max_dist, with_lse):
    n_kt = -(-max_dist // t) + 1
    kern = functools.partial(_band_kernel, n_kt=n_kt, max_dist=max_dist,
                             kv_heads=kv_width // HEAD_DIM, with_lse=with_lse)

    def shifted(fn, u):
        def index_map(bb, a, i):
            return fn(bb, a, jnp.maximum(i - (n_kt - 1) + u, 0))
        return index_map

    in_specs = [pl.BlockSpec((None, None, t, GROUP_W), q_map)]
    in_specs += [pl.BlockSpec((None, None, t, kv_width), shifted(k_map, u)) for u in range(n_kt)]
    in_specs += [pl.BlockSpec((None, None, t, kv_width), shifted(v_map, u)) for u in range(n_kt)]
    o_spec = pl.BlockSpec((None, t, GROUP_W), o_map)
    if with_lse:
        out_shapes = (jax.ShapeDtypeStruct(out_shape, BF16), jax.ShapeDtypeStruct(out_shape, F32))
        out_specs = (o_spec, pl.BlockSpec((None, t, GROUP_W), o_map))
    else:
        out_shapes = jax.ShapeDtypeStruct(out_shape, BF16)
        out_specs = o_spec
    return pl.pallas_call(
        kern, out_shape=out_shapes, grid=grid, in_specs=in_specs, out_specs=out_specs,
        compiler_params=_cparams(("parallel", "arbitrary", "arbitrary"), VMEM_LIMIT),
        name="band_attn",
    )(*([src] * (1 + 2 * n_kt)))


def _layer_norm(z, g, b):
    mu = z.mean(-1, keepdims=True)
    zc = z - mu
    var = (zc * zc).mean(-1, keepdims=True)
    return zc * lax.rsqrt(var + LN_EPS) * g + b


def _merge_kernel(ocmp_ref, oslc_ref, owin_ref, gl_ref, ga_ref, gb_ref,
                  d0_ref, d1_ref, d2_ref, l0_ref, l1_ref, l2_ref, x_ref,
                  wa_ref, wb_ref, wo_ref, g_ref, b_ref, hf_ref, hb_ref, *, alpha):
    tt = x_ref.shape[0]
    gates = jax.nn.sigmoid(gl_ref[...].astype(F32))
    parts = []
    for h in range(NSA_HEADS):
        sl = slice(h * HEAD_DIM, (h + 1) * HEAD_DIM)
        acc = jnp.zeros((tt, HEAD_DIM), F32)
        for br, ref in enumerate((ocmp_ref, oslc_ref, owin_ref)):
            gcol = gates[:, 3 * h + br:3 * h + br + 1]
            acc = acc + gcol * ref[:, sl].astype(F32)
        parts.append(acc.astype(BF16))
    o_nsa = jnp.concatenate(parts, axis=1)
    l0, l1, l2 = l0_ref[...], l1_ref[...], l2_ref[...]
    lm = jnp.maximum(jnp.maximum(l0, l1), l2)
    e0, e1, e2 = jnp.exp(l0 - lm), jnp.exp(l1 - lm), jnp.exp(l2 - lm)
    o_dil = (e0 * d0_ref[...].astype(F32) + e1 * d1_ref[...].astype(F32)
             + e2 * d2_ref[...].astype(F32)) / (e0 + e1 + e2)
    y_a = jnp.dot(o_nsa, wa_ref[...], preferred_element_type=F32)
    y_b = jnp.dot(o_dil.astype(BF16), wb_ref[...], preferred_element_type=F32)
    merged = (jax.nn.sigmoid(ga_ref[...].astype(F32)) * y_a
              + jax.nn.sigmoid(gb_ref[...].astype(F32)) * y_b)
    mix = jnp.dot(merged.astype(BF16), wo_ref[...], preferred_element_type=F32)
    h = _layer_norm(alpha * x_ref[...] + mix, g_ref[...], b_ref[...])
    hf_ref[...] = h
    hb_ref[...] = h.astype(BF16)


def _merge(o_cmp, o_slc, o_win, main2d, gl_tile, ga_blk, gb_blk, dil_o, dil_lse, x2d,
           w_a, w_b, w_o, ln_g, ln_b, alpha, tt):
    T, D = x2d.shape
    nsa_w = NSA_HEADS * HEAD_DIM
    row = lambda i: (i, 0)
    const = lambda i: (0, 0)
    in_specs = [
        pl.BlockSpec((tt, nsa_w), row), pl.BlockSpec((tt, nsa_w), row), pl.BlockSpec((tt, nsa_w), row),
        pl.BlockSpec((tt, LANES), lambda i: (i, gl_tile)),
        pl.BlockSpec((tt, D), lambda i: (i, ga_blk)),
        pl.BlockSpec((tt, D), lambda i: (i, gb_blk)),
    ]
    in_specs += [pl.BlockSpec((tt, DIL_W), row)] * 6
    in_specs += [
        pl.BlockSpec((tt, D), row),
        pl.BlockSpec((nsa_w, D), const), pl.BlockSpec((DIL_W, D), const), pl.BlockSpec((D, D), const),
        pl.BlockSpec((1, D), const), pl.BlockSpec((1, D), const),
    ]
    return pl.pallas_call(
        functools.partial(_merge_kernel, alpha=alpha),
        out_shape=(jax.ShapeDtypeStruct((T, D), F32), jax.ShapeDtypeStruct((T, D), BF16)),
        grid=(T // tt,),
        in_specs=in_specs,
        out_specs=(pl.BlockSpec((tt, D), row), pl.BlockSpec((tt, D), row)),
        compiler_params=_cparams(("parallel",), VMEM_LIMIT),
        name="merge_ln1",
    )(o_cmp, o_slc, o_win, main2d, main2d, main2d, *dil_o, *dil_lse, x2d, w_a, w_b, w_o, ln_g, ln_b)


def _router_kernel(h_ref, w_ref, b_ref, tri_ref, meta_ref, cnt_ref, carry_sc):
    i = pl.program_id(0)
    tt = h_ref.shape[0]

    @pl.when(i == 0)
    def _():
        carry_sc[...] = jnp.zeros(carry_sc.shape, F32)

    logits = jnp.dot(h_ref[...], w_ref[...], preferred_element_type=F32) + b_ref[...]
    lane = lax.broadcasted_iota(jnp.int32, (tt, LANES), 1)
    v = logits
    onehot = jnp.zeros((tt, LANES), F32)
    vals, idxs = [], []
    for _ in range(TOP_K):
        m = v.max(-1, keepdims=True)
        idx = jnp.where(v == m, lane, LANES).min(-1, keepdims=True)
        hit = lane == idx
        vals.append(m)
        idxs.append(idx)
        onehot = onehot + hit.astype(F32)
        v = jnp.where(hit, -jnp.inf, v)
    exps = [jnp.exp(vk - vals[0]) for vk in vals]
    den = exps[0] + exps[1] + exps[2] + exps[3]
    before = jnp.dot(tri_ref[...], onehot.astype(BF16), preferred_element_type=F32) + carry_sc[0:1, :]
    meta = jnp.zeros((tt, LANES), F32)
    for k in range(TOP_K):
        rank = jnp.where(lane == idxs[k], before, 0.0).sum(-1, keepdims=True)
        meta = jnp.where(lane == k, idxs[k].astype(F32), meta)
        meta = jnp.where(lane == TOP_K + k, exps[k] / den, meta)
        meta = jnp.where(lane == 2 * TOP_K + k, rank, meta)
    meta_ref[...] = meta
    carry_sc[...] = carry_sc[...] + jnp.broadcast_to(onehot.sum(0, keepdims=True), carry_sc.shape)
    cnt_ref[...] = carry_sc[...]


def _router(h_bf, w_r, b_r, tt):
    T, D = h_bf.shape
    tri = (jnp.arange(tt)[:, None] > jnp.arange(tt)[None, :]).astype(BF16)
    return pl.pallas_call(
        _router_kernel,
        out_shape=(jax.ShapeDtypeStruct((T, LANES), F32), jax.ShapeDtypeStruct((8, LANES), F32)),
        grid=(T // tt,),
        in_specs=[
            pl.BlockSpec((tt, D), lambda i: (i, 0)),
            pl.BlockSpec((D, LANES), lambda i: (0, 0)),
            pl.BlockSpec((1, LANES), lambda i: (0, 0)),
            pl.BlockSpec((tt, tt), lambda i: (0, 0)),
        ],
        out_specs=(pl.BlockSpec((tt, LANES), lambda i: (i, 0)), pl.BlockSpec((8, LANES), lambda i: (0, 0))),
        scratch_shapes=[pltpu.VMEM((8, LANES), F32)],
        compiler_params=_cparams(("arbitrary",), VMEM_LIMIT),
        name="router_top4",
    )(h_bf, w_r, b_r, tri)


def _dispatch_kernel(dest_ref, h_hbm, xs_in_hbm, xs_hbm, sem):
    del xs_in_hbm
    i = pl.program_id(0)
    n = dest_ref.shape[0] // TOP_K

    def issue(t, c):
        for k in range(TOP_K):
            pltpu.make_async_copy(h_hbm.at[i * n + t], xs_hbm.at[dest_ref[t * TOP_K + k]], sem).start()
        return c

    lax.fori_loop(0, n, issue, 0)

    def drain(t, c):
        for k in range(TOP_K):
            pltpu.make_async_copy(h_hbm.at[0], xs_hbm.at[0], sem).wait()
        return c

    lax.fori_loop(0, n, drain, 0)


def _dispatch(h3, dest_flat, n_rows, tt):
    T = h3.shape[0]
    xs0 = jnp.zeros((n_rows,) + h3.shape[1:], h3.dtype)
    return pl.pallas_call(
        _dispatch_kernel,
        out_shape=jax.ShapeDtypeStruct(xs0.shape, xs0.dtype),
        grid=(T // tt,),
        in_specs=[
            pl.BlockSpec((tt * TOP_K,), lambda i: (i,), memory_space=pltpu.SMEM),
            pl.BlockSpec(memory_space=pl.ANY),
            pl.BlockSpec(memory_space=pl.ANY),
        ],
        out_specs=pl.BlockSpec(memory_space=pl.ANY),
        scratch_shapes=[pltpu.SemaphoreType.DMA(())],
        input_output_aliases={2: 0},
        compiler_params=_cparams(("arbitrary",)),
        name="moe_dispatch",
    )(dest_flat, h3, xs0)


def _expert_kernel(be_ref, nu_ref, x_ref, wg_ref, wl_ref, bg_ref, bl_ref, wd_ref, bd_ref, y_ref, acc_sc):
    i = pl.program_id(0)
    c = pl.program_id(1)

    @pl.when(i < nu_ref[0])
    def _():
        @pl.when(c == 0)
        def _():
            acc_sc[...] = jnp.zeros(acc_sc.shape, F32)

        x = x_ref[...]
        glu = jnp.dot(x, wg_ref[...], preferred_element_type=F32) + bg_ref[...]
        lin = jnp.dot(x, wl_ref[...], preferred_element_type=F32) + bl_ref[...]
        glu = jnp.minimum(glu, SWIGLU_LIMIT)
        lin = jnp.clip(lin, -SWIGLU_LIMIT, SWIGLU_LIMIT)
        act = glu * jax.nn.sigmoid(SWIGLU_ALPHA * glu) * (lin + 1.0)
        acc_sc[...] += jnp.dot(act.astype(BF16), wd_ref[...], preferred_element_type=F32)

        @pl.when(c == pl.num_programs(1) - 1)
        def _():
            y_ref[...] = (acc_sc[...] + bd_ref[...]).astype(y_ref.dtype)


def _experts(xs, blk_expert, n_used, w_up, b_up, w_down, b_down, tm, th):
    P, D = xs.shape
    E, _, two_dh = w_up.shape
    dh = two_dh // 2
    th = min(th, dh)
    nc = dh // th

    def row(i, c, be, nu):
        return (jnp.minimum(i, nu[0] - 1), 0)

    grid_spec = pltpu.PrefetchScalarGridSpec(
        num_scalar_prefetch=2,
        grid=(P // tm, nc),
        in_specs=[
            pl.BlockSpec((tm, D), row),
            pl.BlockSpec((None, D, th), lambda i, c, be, nu: (be[i], 0, c)),
            pl.BlockSpec((None, D, th), lambda i, c, be, nu: (be[i], 0, nc + c)),
            pl.BlockSpec((None, 1, th), lambda i, c, be, nu: (be[i], 0, c)),
            pl.BlockSpec((None, 1, th), lambda i, c, be, nu: (be[i], 0, nc + c)),
            pl.BlockSpec((None, th, D), lambda i, c, be, nu: (be[i], c, 0)),
            pl.BlockSpec((None, 1, D), lambda i, c, be, nu: (be[i], 0, 0)),
        ],
        out_specs=pl.BlockSpec((tm, D), row),
        scratch_shapes=[pltpu.VMEM((tm, D), F32)],
    )
    return pl.pallas_call(
        _expert_kernel,
        out_shape=jax.ShapeDtypeStruct((P, D), BF16),
        grid_spec=grid_spec,
        compiler_params=_cparams(("arbitrary", "arbitrary"), VMEM_LIMIT),
        name="moe_experts",
    )(blk_expert, n_used, xs, w_up, w_up, b_up, b_up, w_down, b_down)


def _combine_kernel(dest_ref, gate_ref, y_hbm, o_ref, buf, sem):
    n = o_ref.shape[0]

    def issue(t, c):
        for k in range(TOP_K):
            pltpu.make_async_copy(y_hbm.at[dest_ref[t * TOP_K + k]], buf.at[k * n + t], sem).start()
        return c

    lax.fori_loop(0, n, issue, 0)

    def drain(t, c):
        for k in range(TOP_K):
            pltpu.make_async_copy(y_hbm.at[0], buf.at[0], sem).wait()
        return c

    lax.fori_loop(0, n, drain, 0)

    def reduce(t, c):
        acc = gate_ref[t * TOP_K] * buf[t].astype(F32)
        for k in range(1, TOP_K):
            acc = acc + gate_ref[t * TOP_K + k] * buf[k * n + t].astype(F32)
        o_ref[t] = acc
        return c

    lax.fori_loop(0, n, reduce, 0)


def _combine(y3, dest_flat, gate_flat, T, tt):
    _, s, l = y3.shape
    return pl.pallas_call(
        _combine_kernel,
        out_shape=jax.ShapeDtypeStruct((T, s, l), F32),
        grid=(T // tt,),
        in_specs=[
            pl.BlockSpec((tt * TOP_K,), lambda i: (i,), memory_space=pltpu.SMEM),
            pl.BlockSpec((tt * TOP_K,), lambda i: (i,), memory_space=pltpu.SMEM),
            pl.BlockSpec(memory_space=pl.ANY),
        ],
        out_specs=pl.BlockSpec((tt, s, l), lambda i: (i, 0, 0)),
        scratch_shapes=[pltpu.VMEM((TOP_K * tt, s, l), y3.dtype), pltpu.SemaphoreType.DMA(())],
        compiler_params=_cparams(("arbitrary",), VMEM_LIMIT),
        name="moe_combine",
    )(dest_flat, gate_flat, y3)


def _final_kernel(h_ref, f_ref, g_ref, b_ref, o_ref, *, alpha):
    o_ref[...] = _layer_norm(alpha * h_ref[...] + f_ref[...], g_ref[...], b_ref[...])


def _final_ln(h, ffn, g, b, alpha, tt):
    T, D = h.shape
    return pl.pallas_call(
        functools.partial(_final_kernel, alpha=alpha),
        out_shape=jax.ShapeDtypeStruct((T, D), F32),
        grid=(T // tt,),
        in_specs=[pl.BlockSpec((tt, D), lambda i: (i, 0)), pl.BlockSpec((tt, D), lambda i: (i, 0)),
                  pl.BlockSpec((1, D), lambda i: (0, 0)), pl.BlockSpec((1, D), lambda i: (0, 0))],
        out_specs=pl.BlockSpec((tt, D), lambda i: (i, 0)),
        compiler_params=_cparams(("parallel",), VMEM_LIMIT),
        name="residual_ln2",
    )(h, ffn, g, b)


def _layer(x, w_in, b_in, pos_k, pos_v, ck_w1, ck_w2, cv_w1, cv_w2, w_br_nsa, w_br_dil, w_out,
           ln1_g, ln1_b, w_router, b_router, w_up, b_up, w_down, b_down, ln2_g, ln2_b, alpha):
    B, S, D = x.shape
    T = B * S
    nd = D // LANES
    G = NSA_KV_GROUPS
    kvw = G * HEAD_DIM
    n_exp = w_router.shape[1]

    o_q = 0
    o_kv = NSA_HEADS * HEAD_DIM
    o_gl = o_kv + 6 * kvw
    o_dil = o_gl + 3 * NSA_HEADS
    o_ga = o_dil + 3 * N_DIL * DIL_W
    o_gb = o_ga + D

    def wcols(a, n):
        return w_in[:, a:a + n], b_in[a:a + n]

    def kv(i):
        return wcols(o_kv + i * kvw, kvw)

    tn = 512
    gl_w, gl_b = wcols(o_gl, 3 * NSA_HEADS)
    pieces = [wcols(o_ga, D), wcols(o_gb, D), wcols(o_q, NSA_HEADS * HEAD_DIM),
              kv(0), kv(2), kv(4), kv(1), kv(3), kv(5), (gl_w, gl_b)]
    used = sum(p[0].shape[1] for p in pieces)
    n_main = -(-used // tn) * tn
    pieces.append((jnp.zeros((D, n_main - used), F32), jnp.zeros((n_main - used,), F32)))
    w_main = jnp.concatenate([p[0] for p in pieces], axis=1).astype(BF16)
    b_main = jnp.concatenate([p[1] for p in pieces])[None, :]
    t_q = 2 * nd
    t_kc, t_ks, t_kw = t_q + 8, t_q + 10, t_q + 12
    t_vc, t_vs, t_vw = t_q + 14, t_q + 16, t_q + 18
    t_gl = t_q + 20
    flags_main = ((jnp.arange(n_main // LANES) >= t_q) & (jnp.arange(n_main // LANES) < t_vc)).astype(jnp.int32)
    q_blk0 = t_q // NSA_REP

    pos = jnp.arange(S, dtype=F32)
    inv = ROPE_THETA ** (-jnp.arange(0, HEAD_DIM, 2, dtype=F32) / HEAD_DIM)
    ang = pos[:, None] * inv[None, :]
    cosx = jnp.concatenate([jnp.cos(ang), jnp.cos(ang)], axis=-1)
    sinx = jnp.concatenate([-jnp.sin(ang), jnp.sin(ang)], axis=-1)

    main = _project(x, w_main, b_main, flags_main, cosx, sinx, 1, 1024, tn)

    nC = S // CMP_STRIDE
    half = CMP_STRIDE * HEAD_DIM

    def cmp_in(tile):
        a = main[:, 0, :, tile * LANES:(tile + G) * LANES].reshape(B, S, G, HEAD_DIM)
        return a.transpose(0, 2, 1, 3).reshape(B, G, nC, half)

    xkv = jnp.stack([cmp_in(t_kc), cmp_in(t_vc)])
    pos_kv = jnp.stack([pos_k.reshape(2, half), pos_v.reshape(2, half)])
    w1 = jnp.stack([ck_w1, cv_w1]).astype(BF16)
    w2 = jnp.stack([ck_w2, cv_w2]).astype(BF16)
    kvc = _compress(xkv, pos_kv, w1, w2)

    n_slc = S // SEL_LEN
    assert n_slc <= LANES
    c_start = jnp.arange(nC) * CMP_STRIDE
    jb = jnp.arange(LANES)
    overlap = ((c_start[:, None] < (jb[None, :] + 1) * SEL_LEN) & (c_start[:, None] + CMP_LEN > jb[None, :] * SEL_LEN)
               & (jb[None, :] < n_slc) & (c_start[:, None] + CMP_LEN <= S)).astype(BF16)
    tq = min(256, S)
    o_cmp, sel = _cmp_attention(main, kvc, overlap, q_blk0, tq)

    expand = (jnp.arange(S)[None, :] // SEL_LEN == jnp.arange(LANES)[:, None]).astype(BF16)
    o_slc = _sel_attention(main, sel, expand, q_blk0, t_ks, t_vs, 256, 512)

    tw = min(256, S)
    o_win = _band_attention(
        main, (B, G, S // tw),
        lambda bb, g, i: (bb, 0, i, q_blk0 + g),
        lambda bb, g, i: (bb, 0, i, t_kw + g),
        lambda bb, g, i: (bb, 0, i, t_vw + g),
        HEAD_DIM, (B, S, NSA_HEADS * HEAD_DIM), lambda bb, g, i: (bb, i, g), tw, WIN_LEN - 1, False)

    flags_dil = jnp.array([1] * (2 * DIL_HEADS) + [0] * DIL_HEADS, jnp.int32)
    dil_o, dil_lse = [], []
    for gi, (w, d) in enumerate(DIL_CONFIGS):
        wd, bd = wcols(o_dil + gi * 3 * DIL_W, 3 * DIL_W)
        sub = _project(x, wd.astype(BF16), bd[None, :], flags_dil, cosx, sinx, d, 256, 3 * DIL_W)
        L = S // d
        td = min(128, L)
        o_g, lse_g = _band_attention(
            sub, (B, d, L // td),
            lambda bb, r, i: (bb, r, i, 0),
            lambda bb, r, i: (bb, r, i, 1),
            lambda bb, r, i: (bb, r, i, 2),
            DIL_W, (B, L, d * DIL_W), lambda bb, r, i: (bb, i, r), td, w // d, True)
        dil_o.append(o_g.reshape(T, DIL_W))
        dil_lse.append(lse_g.reshape(T, DIL_W))

    h_f, h_b = _merge(
        o_cmp.reshape(T, -1), o_slc.reshape(T, -1), o_win.reshape(T, -1), main.reshape(T, n_main),
        t_gl, 0, 1, dil_o, dil_lse, x.reshape(T, D),
        w_br_nsa.astype(BF16), w_br_dil.astype(BF16), w_out.astype(BF16),
        ln1_g[None, :], ln1_b[None, :], alpha, min(256, T))

    w_r = jnp.concatenate([w_router, jnp.zeros((D, LANES - n_exp), F32)], axis=1).astype(BF16)
    b_r = jnp.concatenate([b_router, jnp.full((LANES - n_exp,), NEG, F32)])[None, :]
    meta, cnt = _router(h_b, w_r, b_r, min(512, T))
    top_idx = meta[:, 0:TOP_K].astype(jnp.int32)
    gates = meta[:, TOP_K:2 * TOP_K]
    rank = meta[:, 2 * TOP_K:3 * TOP_K].astype(jnp.int32)

    tm = 512 if T * TOP_K >= 512 * n_exp else 128
    counts = cnt[0, :n_exp].astype(jnp.int32)
    padded = (counts + tm - 1) // tm * tm
    pad_end = jnp.cumsum(padded)
    pad_start = pad_end - padded
    dest = (pad_start[top_idx] + rank).reshape(T * TOP_K)
    n_rows = T * TOP_K + n_exp * tm
    n_blk = n_rows // tm
    blk_expert = jnp.minimum(jnp.searchsorted(pad_end, jnp.arange(n_blk) * tm, side="right"),
                             n_exp - 1).astype(jnp.int32)
    n_used = (pad_end[-1:] // tm).astype(jnp.int32)

    tt = min(512, T)
    xs = _dispatch(h_b.reshape(T, nd, LANES), dest, n_rows, tt)
    y = _experts(xs.reshape(n_rows, D), blk_expert, n_used,
                 w_up.astype(BF16), b_up[:, None, :], w_down.astype(BF16), b_down[:, None, :], tm, 512)
    ffn = _combine(y.reshape(n_rows, nd, LANES), dest, gates.reshape(T * TOP_K), T, min(256, T))

    out = _final_ln(h_f, ffn.reshape(T, D), ln2_g[None, :], ln2_b[None, :], alpha, min(512, T))
    return out.reshape(B, S, D)


def kernel(x, w_in, b_in, cmp_pos_k, cmp_pos_v, cmp_k_w1, cmp_k_w2, cmp_v_w1, cmp_v_w2, w_br_nsa, w_br_dil,
           w_out, ln1_g, ln1_b, w_router, b_router, w_up, b_up, w_down, b_down, ln2_g, ln2_b):
    depth = w_in.shape[0]
    alpha = (2.0 * depth) ** 0.25
    h = x
    for l in range(depth):
        h = _layer(h, w_in[l], b_in[l], cmp_pos_k[l], cmp_pos_v[l], cmp_k_w1[l], cmp_k_w2[l],
                   cmp_v_w1[l], cmp_v_w2[l], w_br_nsa[l], w_br_dil[l], w_out[l], ln1_g[l], ln1_b[l],
                   w_router[l], b_router[l], w_up[l], b_up[l], w_down[l], b_down[l], ln2_g[l], ln2_b[l], alpha)
    return h
```

```python
import functools

import jax
import jax.numpy as jnp
from jax import lax
from jax.experimental import pallas as pl
from jax.experimental.pallas import tpu as pltpu

F32 = jnp.float32
BF16 = jnp.bfloat16

HEAD_DIM = 128
LANES = 128
ROPE_THETA = 10000.0
NSA_HEADS = 8
NSA_KV_GROUPS = 2
NSA_REP = NSA_HEADS // NSA_KV_GROUPS
CMP_LEN = 32
CMP_STRIDE = 16
SEL_LEN = 64
SEL_TOPK = 16
WIN_LEN = 512
DIL_CONFIGS = ((128, 1), (512, 4), (2048, 16))
N_DIL = len(DIL_CONFIGS)
DIL_HEADS = 4
TOP_K = 4
SWIGLU_LIMIT = 7.0
SWIGLU_ALPHA = 1.702
LN_EPS = 1e-5
NEG = -1e30
FORCE = 1e9
SCALE = HEAD_DIM ** -0.5

GROUP_W = NSA_REP * HEAD_DIM
DIL_W = DIL_HEADS * HEAD_DIM
VMEM_LIMIT = 56 * 1024 * 1024


def _cparams(sem, vmem=None):
    return pltpu.CompilerParams(dimension_semantics=sem, vmem_limit_bytes=vmem)


def _masked_softmax(s, mask):
    s = jnp.where(mask, s, NEG)
    m = s.max(-1, keepdims=True)
    p = jnp.where(mask, jnp.exp(s - m), 0.0)
    den = p.sum(-1, keepdims=True)
    safe = jnp.where(den > 0, den, 1.0)
    return p, m, safe


def _qk(q, k):
    return lax.dot_general(q, k, (((1,), (1,)), ((), ())), preferred_element_type=F32)


def _pack_pairs(xb):
    m = xb.shape[1] // 2
    lo = lax.bitcast_convert_type(xb[:, :m].astype(F32), jnp.uint32)
    hi = lax.bitcast_convert_type(xb[:, m:].astype(F32), jnp.uint32)
    return (lo >> 16) | (hi & jnp.uint32(0xFFFF0000))


def _unpack_pairs(w):
    lo = lax.bitcast_convert_type(w << 16, F32)
    hi = lax.bitcast_convert_type(w & jnp.uint32(0xFFFF0000), F32)
    return lo, hi


def _proj_kernel(flags_ref, x_ref, w_ref, b_ref, cos_ref, sin_ref, o_ref, xb_ref):
    j = pl.program_id(3)
    n_sub = o_ref.shape[-1] // LANES

    @pl.when(j == 0)
    def _():
        xb_ref[...] = x_ref[...].astype(BF16)

    acc = jnp.dot(xb_ref[...], w_ref[...], preferred_element_type=F32) + b_ref[...]
    for u in range(n_sub):
        a = acc[:, u * LANES:(u + 1) * LANES]
        roped = a * cos_ref[...] + pltpu.roll(a, HEAD_DIM // 2, 1) * sin_ref[...]
        use_rope = flags_ref[j * n_sub + u] == 1
        o_ref[:, u * LANES:(u + 1) * LANES] = jnp.where(use_rope, roped, a).astype(o_ref.dtype)


def _project(x, w, b, flags, cosx, sinx, d, tm, tn):
    B, S, D = x.shape
    N = w.shape[1]
    L = S // d
    tm = min(tm, L)
    xv = x.reshape(B, L, d * D)
    cv = cosx.reshape(L, d * HEAD_DIM)
    sv = sinx.reshape(L, d * HEAD_DIM)
    grid_spec = pltpu.PrefetchScalarGridSpec(
        num_scalar_prefetch=1,
        grid=(B, d, L // tm, N // tn),
        in_specs=[
            pl.BlockSpec((None, tm, D), lambda bb, r, i, j, f: (bb, i, r)),
            pl.BlockSpec((D, tn), lambda bb, r, i, j, f: (0, j)),
            pl.BlockSpec((1, tn), lambda bb, r, i, j, f: (0, j)),
            pl.BlockSpec((tm, HEAD_DIM), lambda bb, r, i, j, f: (i, r)),
            pl.BlockSpec((tm, HEAD_DIM), lambda bb, r, i, j, f: (i, r)),
        ],
        out_specs=pl.BlockSpec((None, None, tm, tn), lambda bb, r, i, j, f: (bb, r, i, j)),
        scratch_shapes=[pltpu.VMEM((tm, D), BF16)],
    )
    return pl.pallas_call(
        _proj_kernel,
        out_shape=jax.ShapeDtypeStruct((B, d, L, N), BF16),
        grid_spec=grid_spec,
        compiler_params=_cparams(("parallel", "arbitrary", "arbitrary", "arbitrary"), VMEM_LIMIT),
        name="proj_rope",
    )(flags, xv, w, b, cv, sv)


def _gelu_tanh(x):
    return 0.5 * x * (1.0 + jnp.tanh(0.7978845608028654 * (x + 0.044715 * x * x * x)))


def _compress_kernel(x_ref, pos_ref, w1_ref, w2_ref, o_ref):
    half = x_ref.shape[-1]
    nc = x_ref.shape[0]
    x = x_ref[...].astype(F32)
    lo = (x + pos_ref[0:1, :]).astype(BF16)
    hi = (x + pos_ref[1:2, :]).astype(BF16)
    y_lo = jnp.dot(lo, w1_ref[0:half, :], preferred_element_type=F32)
    y_hi = jnp.dot(hi, w1_ref[half:2 * half, :], preferred_element_type=F32)
    h = y_lo + pltpu.roll(y_hi, nc - 1, 0)
    g = _gelu_tanh(h).astype(BF16)
    o_ref[...] = jnp.dot(g, w2_ref[...], preferred_element_type=F32).astype(o_ref.dtype)


def _compress(xkv, pos, w1, w2):
    _, B, G, nC, half = xkv.shape
    hid = w1.shape[-1]
    return pl.pallas_call(
        _compress_kernel,
        out_shape=jax.ShapeDtypeStruct((2, B, G, nC, HEAD_DIM), BF16),
        grid=(2, B, G),
        in_specs=[
            pl.BlockSpec((None, None, None, nC, half), lambda a, bb, g: (a, bb, g, 0, 0)),
            pl.BlockSpec((None, 2, half), lambda a, bb, g: (a, 0, 0)),
            pl.BlockSpec((None, 2 * half, hid), lambda a, bb, g: (a, 0, 0)),
            pl.BlockSpec((None, hid, HEAD_DIM), lambda a, bb, g: (a, 0, 0)),
        ],
        out_specs=pl.BlockSpec((None, None, None, nC, HEAD_DIM), lambda a, bb, g: (a, bb, g, 0, 0)),
        compiler_params=_cparams(("arbitrary", "arbitrary", "arbitrary"), VMEM_LIMIT),
        name="compress_mlp",
    )(xkv, pos, w1, w2)


def _cmp_attn_kernel(q_ref, kc_ref, vc_ref, ov_ref, o_ref, sel_ref, *, n_slc, n_sel):
    qi = pl.program_id(2)
    tq = q_ref.shape[0]
    nc = kc_ref.shape[0]
    t = qi * tq + lax.broadcasted_iota(jnp.int32, (tq, nc), 0)
    c = lax.broadcasted_iota(jnp.int32, (tq, nc), 1)
    mask = (c * CMP_STRIDE + (CMP_LEN - 1)) <= t
    kc = kc_ref[...]
    vc = vc_ref[...]
    ps = jnp.zeros((tq, nc), F32)
    for h in range(NSA_REP):
        s = _qk(q_ref[:, h * HEAD_DIM:(h + 1) * HEAD_DIM], kc) * SCALE
        p, _, safe = _masked_softmax(s, mask)
        p = p / safe
        o = jnp.dot(p.astype(BF16), vc, preferred_element_type=F32)
        o_ref[:, h * HEAD_DIM:(h + 1) * HEAD_DIM] = o.astype(o_ref.dtype)
        ps = ps + p
    imp = jnp.dot(ps.astype(BF16), ov_ref[...], preferred_element_type=F32)
    tj = qi * tq + lax.broadcasted_iota(jnp.int32, (tq, LANES), 0)
    j = lax.broadcasted_iota(jnp.int32, (tq, LANES), 1)
    cur = tj // SEL_LEN
    forced = (j == 0) | (j == cur) | (j == cur - 1)
    imp = jnp.where(forced, FORCE, jnp.where(j > cur, NEG, imp))
    imp_t = imp.T[0:n_slc, :]
    jj = lax.broadcasted_iota(jnp.int32, (n_slc, tq), 0)
    rank = jnp.zeros((n_slc, tq), jnp.int32)
    for j2 in range(n_slc):
        row = imp_t[j2:j2 + 1, :]
        ahead = (row > imp_t) | ((row == imp_t) & (jj > j2))
        rank = rank + ahead.astype(jnp.int32)
    sel_t = (rank < n_sel).astype(F32)
    if n_slc < LANES:
        sel_t = jnp.concatenate([sel_t, jnp.zeros((LANES - n_slc, tq), F32)], axis=0)
    sel_ref[...] = sel_t.T.astype(sel_ref.dtype)


def _cmp_attention(main, kvc, overlap, q_blk0, tq):
    B, _, S, _ = main.shape
    G = NSA_KV_GROUPS
    nC = kvc.shape[3]
    n_slc = S // SEL_LEN
    n_sel = min(SEL_TOPK, n_slc)
    kern = functools.partial(_cmp_attn_kernel, n_slc=n_slc, n_sel=n_sel)
    return pl.pallas_call(
        kern,
        out_shape=(jax.ShapeDtypeStruct((B, S, NSA_HEADS * HEAD_DIM), BF16),
                   jax.ShapeDtypeStruct((B, G, S, LANES), BF16)),
        grid=(B, G, S // tq),
        in_specs=[
            pl.BlockSpec((None, None, tq, GROUP_W), lambda bb, g, i: (bb, 0, i, q_blk0 + g)),
            pl.BlockSpec((None, None, None, nC, HEAD_DIM), lambda bb, g, i: (0, bb, g, 0, 0)),
            pl.BlockSpec((None, None, None, nC, HEAD_DIM), lambda bb, g, i: (1, bb, g, 0, 0)),
            pl.BlockSpec((nC, LANES), lambda bb, g, i: (0, 0)),
        ],
        out_specs=(pl.BlockSpec((None, tq, GROUP_W), lambda bb, g, i: (bb, i, g)),
                   pl.BlockSpec((None, None, tq, LANES), lambda bb, g, i: (bb, g, i, 0))),
        compiler_params=_cparams(("parallel", "arbitrary", "arbitrary"), VMEM_LIMIT),
        name="cmp_attn_select",
    )(main, kvc, kvc, overlap)


def _sel_attn_kernel(q_ref, k_ref, v_ref, sel_ref, e_ref, o_ref, m_sc, l_sc, acc_sc):
    qi = pl.program_id(2)
    kj = pl.program_id(3)
    tq = q_ref.shape[0]
    tk = k_ref.shape[0]

    @pl.when(kj == 0)
    def _():
        m_sc[...] = jnp.full(m_sc.shape, NEG, F32)
        l_sc[...] = jnp.zeros(l_sc.shape, F32)
        acc_sc[...] = jnp.zeros(acc_sc.shape, F32)

    @pl.when(kj * tk <= qi * tq + (tq - 1))
    def _():
        selm = jnp.dot(sel_ref[...], e_ref[...], preferred_element_type=F32)
        qpos = qi * tq + lax.broadcasted_iota(jnp.int32, (tq, tk), 0)
        kpos = kj * tk + lax.broadcasted_iota(jnp.int32, (tq, tk), 1)
        valid = (selm > 0.5) & (kpos <= qpos)
        k = k_ref[...]
        v = v_ref[...]
        for h in range(NSA_REP):
            s = _qk(q_ref[:, h * HEAD_DIM:(h + 1) * HEAD_DIM], k) * SCALE
            s = jnp.where(valid, s, NEG)
            m_old = m_sc[h]
            m_new = jnp.maximum(m_old, s.max(-1, keepdims=True))
            p = jnp.where(valid, jnp.exp(s - m_new), 0.0)
            alpha = jnp.exp(m_old - m_new)
            l_sc[h] = alpha * l_sc[h] + p.sum(-1, keepdims=True)
            acc_sc[h] = alpha * acc_sc[h] + jnp.dot(p.astype(BF16), v, preferred_element_type=F32)
            m_sc[h] = m_new

    @pl.when(kj == pl.num_programs(3) - 1)
    def _():
        for h in range(NSA_REP):
            l = l_sc[h]
            safe = jnp.where(l > 0, l, 1.0)
            o_ref[:, h * HEAD_DIM:(h + 1) * HEAD_DIM] = (acc_sc[h] / safe).astype(o_ref.dtype)


def _sel_attention(main, sel, expand, q_blk0, k_tile0, v_tile0, tq, tk):
    B, _, S, _ = main.shape
    G = NSA_KV_GROUPS
    tq = min(tq, S)
    tk = min(tk, S)

    def kv_idx(i, j):
        return jnp.minimum(j, (i * tq + tq - 1) // tk)

    return pl.pallas_call(
        _sel_attn_kernel,
        out_shape=jax.ShapeDtypeStruct((B, S, NSA_HEADS * HEAD_DIM), BF16),
        grid=(B, G, S // tq, S // tk),
        in_specs=[
            pl.BlockSpec((None, None, tq, GROUP_W), lambda bb, g, i, j: (bb, 0, i, q_blk0 + g)),
            pl.BlockSpec((None, None, tk, HEAD_DIM), lambda bb, g, i, j: (bb, 0, kv_idx(i, j), k_tile0 + g)),
            pl.BlockSpec((None, None, tk, HEAD_DIM), lambda bb, g, i, j: (bb, 0, kv_idx(i, j), v_tile0 + g)),
            pl.BlockSpec((None, None, tq, LANES), lambda bb, g, i, j: (bb, g, i, 0)),
            pl.BlockSpec((LANES, tk), lambda bb, g, i, j: (0, kv_idx(i, j))),
        ],
        out_specs=pl.BlockSpec((None, tq, GROUP_W), lambda bb, g, i, j: (bb, i, g)),
        scratch_shapes=[pltpu.VMEM((NSA_REP, tq, 1), F32), pltpu.VMEM((NSA_REP, tq, 1), F32),
                        pltpu.VMEM((NSA_REP, tq, HEAD_DIM), F32)],
        compiler_params=_cparams(("parallel", "arbitrary", "arbitrary", "arbitrary"), VMEM_LIMIT),
        name="selected_attn",
    )(main, main, main, sel, expand)


def _band_kernel(*refs, n_kt, max_dist, kv_heads, with_lse):
    q_ref = refs[0]
    k_refs = refs[1:1 + n_kt]
    v_refs = refs[1 + n_kt:1 + 2 * n_kt]
    o_ref = refs[1 + 2 * n_kt]
    qi = pl.program_id(2)
    tq = q_ref.shape[0]
    tk = k_refs[0].shape[0]
    nk = n_kt * tk
    qpos = qi * tq + lax.broadcasted_iota(jnp.int32, (tq, nk), 0)
    kpos = (qi - (n_kt - 1)) * tk + lax.broadcasted_iota(jnp.int32, (tq, nk), 1)
    diff = qpos - kpos
    mask = (kpos >= 0) & (diff >= 0) & (diff <= max_dist)
    k = jnp.concatenate([r[...] for r in k_refs], axis=0)
    v = jnp.concatenate([r[...] for r in v_refs], axis=0)
    for h in range(q_ref.shape[1] // HEAD_DIM):
        hk = h if kv_heads > 1 else 0
        s = _qk(q_ref[:, h * HEAD_DIM:(h + 1) * HEAD_DIM], k[:, hk * HEAD_DIM:(hk + 1) * HEAD_DIM]) * SCALE
        p, m, safe = _masked_softmax(s, mask)
        o = jnp.dot(p.astype(BF16), v[:, hk * HEAD_DIM:(hk + 1) * HEAD_DIM], preferred_element_type=F32)
        o_ref[:, h * HEAD_DIM:(h + 1) * HEAD_DIM] = (o / safe).astype(o_ref.dtype)
        if with_lse:
            lse_ref = refs[2 + 2 * n_kt]
            lse_ref[:, h * HEAD_DIM:(h + 1) * HEAD_DIM] = jnp.broadcast_to(m + jnp.log(safe), (tq, HEAD_DIM))


def _band_attention(src, grid, q_map, k_map, v_map, kv_width, out_shape, o_map, t, max_dist, with_lse):
    n_kt = -(-max_dist // t) + 1
    kern = functools.partial(_band_kernel, n_kt=n_kt, max_dist=max_dist,
                             kv_heads=kv_width // HEAD_DIM, with_lse=with_lse)

    def shifted(fn, u):
        def index_map(bb, a, i):
            return fn(bb, a, jnp.maximum(i - (n_kt - 1) + u, 0))
        return index_map

    in_specs = [pl.BlockSpec((None, None, t, GROUP_W), q_map)]
    in_specs += [pl.BlockSpec((None, None, t, kv_width), shifted(k_map, u)) for u in range(n_kt)]
    in_specs += [pl.BlockSpec((None, None, t, kv_width), shifted(v_map, u)) for u in range(n_kt)]
    o_spec = pl.BlockSpec((None, t, GROUP_W), o_map)
    if with_lse:
        out_shapes = (jax.ShapeDtypeStruct(out_shape, BF16), jax.ShapeDtypeStruct(out_shape, F32))
        out_specs = (o_spec, pl.BlockSpec((None, t, GROUP_W), o_map))
    else:
        out_shapes = jax.ShapeDtypeStruct(out_shape, BF16)
        out_specs = o_spec
    return pl.pallas_call(
        kern, out_shape=out_shapes, grid=grid, in_specs=in_specs, out_specs=out_specs,
        compiler_params=_cparams(("parallel", "arbitrary", "arbitrary"), VMEM_LIMIT),
        name="band_attn",
    )(*([src] * (1 + 2 * n_kt)))


def _layer_norm(z, g, b):
    mu = z.mean(-1, keepdims=True)
    zc = z - mu
    var = (zc * zc).mean(-1, keepdims=True)
    return zc * lax.rsqrt(var + LN_EPS) * g + b


def _merge_kernel(ocmp_ref, oslc_ref, owin_ref, gl_ref, ga_ref, gb_ref,
                  d0_ref, d1_ref, d2_ref, l0_ref, l1_ref, l2_ref, x_ref,
                  wa_ref, wb_ref, wo_ref, g_ref, b_ref, wr_ref, br_ref, hf_ref, hp_ref, lg_ref, *, alpha):
    tt = x_ref.shape[0]
    gates = jax.nn.sigmoid(gl_ref[...].astype(F32))
    parts = []
    for h in range(NSA_HEADS):
        sl = slice(h * HEAD_DIM, (h + 1) * HEAD_DIM)
        acc = jnp.zeros((tt, HEAD_DIM), F32)
        for br, ref in enumerate((ocmp_ref, oslc_ref, owin_ref)):
            gcol = gates[:, 3 * h + br:3 * h + br + 1]
            acc = acc + gcol * ref[:, sl].astype(F32)
        parts.append(acc.astype(BF16))
    o_nsa = jnp.concatenate(parts, axis=1)
    l0, l1, l2 = l0_ref[...], l1_ref[...], l2_ref[...]
    lm = jnp.maximum(jnp.maximum(l0, l1), l2)
    e0, e1, e2 = jnp.exp(l0 - lm), jnp.exp(l1 - lm), jnp.exp(l2 - lm)
    o_dil = (e0 * d0_ref[...].astype(F32) + e1 * d1_ref[...].astype(F32)
             + e2 * d2_ref[...].astype(F32)) / (e0 + e1 + e2)
    y_a = jnp.dot(o_nsa, wa_ref[...], preferred_element_type=F32)
    y_b = jnp.dot(o_dil.astype(BF16), wb_ref[...], preferred_element_type=F32)
    merged = (jax.nn.sigmoid(ga_ref[...].astype(F32)) * y_a
              + jax.nn.sigmoid(gb_ref[...].astype(F32)) * y_b)
    mix = jnp.dot(merged.astype(BF16), wo_ref[...], preferred_element_type=F32)
    h = _layer_norm(alpha * x_ref[...] + mix, g_ref[...], b_ref[...])
    hf_ref[...] = h
    hb = h.astype(BF16)
    hp_ref[...] = _pack_pairs(hb)
    lg_ref[...] = jnp.dot(hb, wr_ref[...], preferred_element_type=F32) + br_ref[...]


def _merge(o_cmp, o_slc, o_win, main2d, gl_tile, ga_blk, gb_blk, dil_o, dil_lse, x2d,
           w_a, w_b, w_o, ln_g, ln_b, w_r, b_r, alpha, tt):
    T, D = x2d.shape
    nsa_w = NSA_HEADS * HEAD_DIM
    row = lambda i: (i, 0)
    const = lambda i: (0, 0)
    in_specs = [
        pl.BlockSpec((tt, nsa_w), row), pl.BlockSpec((tt, nsa_w), row), pl.BlockSpec((tt, nsa_w), row),
        pl.BlockSpec((tt, LANES), lambda i: (i, gl_tile)),
        pl.BlockSpec((tt, D), lambda i: (i, ga_blk)),
        pl.BlockSpec((tt, D), lambda i: (i, gb_blk)),
    ]
    in_specs += [pl.BlockSpec((tt, DIL_W), row)] * 6
    in_specs += [
        pl.BlockSpec((tt, D), row),
        pl.BlockSpec((nsa_w, D), const), pl.BlockSpec((DIL_W, D), const), pl.BlockSpec((D, D), const),
        pl.BlockSpec((1, D), const), pl.BlockSpec((1, D), const),
        pl.BlockSpec((D, LANES), const), pl.BlockSpec((1, LANES), const),
    ]
    return pl.pallas_call(
        functools.partial(_merge_kernel, alpha=alpha),
        out_shape=(jax.ShapeDtypeStruct((T, D), F32), jax.ShapeDtypeStruct((T, D // 2), jnp.uint32),
                   jax.ShapeDtypeStruct((T, LANES), F32)),
        grid=(T // tt,),
        in_specs=in_specs,
        out_specs=(pl.BlockSpec((tt, D), row), pl.BlockSpec((tt, D // 2), row), pl.BlockSpec((tt, LANES), row)),
        compiler_params=_cparams(("parallel",), VMEM_LIMIT),
        name="merge_ln1",
    )(o_cmp, o_slc, o_win, main2d, main2d, main2d, *dil_o, *dil_lse, x2d, w_a, w_b, w_o, ln_g, ln_b, w_r, b_r)


def _router_kernel(lg_ref, tri_ref, meta_ref, cnt_ref, carry_sc):
    i = pl.program_id(0)
    tt = lg_ref.shape[0]

    @pl.when(i == 0)
    def _():
        carry_sc[...] = jnp.zeros(carry_sc.shape, F32)

    logits = lg_ref[...]
    lane = lax.broadcasted_iota(jnp.int32, (tt, LANES), 1)
    v = logits
    onehot = jnp.zeros((tt, LANES), F32)
    vals, idxs = [], []
    for _ in range(TOP_K):
        m = v.max(-1, keepdims=True)
        idx = jnp.where(v == m, lane, LANES).min(-1, keepdims=True)
        hit = lane == idx
        vals.append(m)
        idxs.append(idx)
        onehot = onehot + hit.astype(F32)
        v = jnp.where(hit, -jnp.inf, v)
    exps = [jnp.exp(vk - vals[0]) for vk in vals]
    den = exps[0] + exps[1] + exps[2] + exps[3]
    before = jnp.dot(tri_ref[...], onehot.astype(BF16), preferred_element_type=F32) + carry_sc[0:1, :]
    meta = jnp.zeros((tt, LANES), F32)
    for k in range(TOP_K):
        rank = jnp.where(lane == idxs[k], before, 0.0).sum(-1, keepdims=True)
        meta = jnp.where(lane == k, idxs[k].astype(F32), meta)
        meta = jnp.where(lane == TOP_K + k, exps[k] / den, meta)
        meta = jnp.where(lane == 2 * TOP_K + k, rank, meta)
    meta_ref[...] = meta
    carry_sc[...] = carry_sc[...] + jnp.broadcast_to(onehot.sum(0, keepdims=True), carry_sc.shape)
    cnt_ref[...] = carry_sc[...]


def _router(logits, tt):
    T = logits.shape[0]
    tri = (jnp.arange(tt)[:, None] > jnp.arange(tt)[None, :]).astype(BF16)
    return pl.pallas_call(
        _router_kernel,
        out_shape=(jax.ShapeDtypeStruct((T, LANES), F32), jax.ShapeDtypeStruct((8, LANES), F32)),
        grid=(T // tt,),
        in_specs=[
            pl.BlockSpec((tt, LANES), lambda i: (i, 0)),
            pl.BlockSpec((tt, tt), lambda i: (0, 0)),
        ],
        out_specs=(pl.BlockSpec((tt, LANES), lambda i: (i, 0)), pl.BlockSpec((8, LANES), lambda i: (0, 0))),
        scratch_shapes=[pltpu.VMEM((8, LANES), F32)],
        compiler_params=_cparams(("arbitrary",), VMEM_LIMIT),
        name="router_top4",
    )(logits, tri)


def _dispatch_kernel(dest_ref, h_ref, xs_in_hbm, xs_hbm, sem):
    del xs_in_hbm
    n = h_ref.shape[0]

    def row_copy(t, d):
        return pltpu.make_async_copy(h_ref.at[pl.ds(t, 1), :], xs_hbm.at[pl.ds(d, 1), :], sem)

    def issue(t, c):
        for k in range(TOP_K):
            row_copy(t, dest_ref[t * TOP_K + k]).start()
        return c

    lax.fori_loop(0, n, issue, 0)

    def drain(t, c):
        for k in range(TOP_K):
            row_copy(0, 0).wait()
        return c

    lax.fori_loop(0, n, drain, 0)


def _dispatch(hp, dest_flat, n_rows, tt):
    T, width = hp.shape
    xs0 = jnp.zeros((n_rows, width), hp.dtype)
    return pl.pallas_call(
        _dispatch_kernel,
        out_shape=jax.ShapeDtypeStruct(xs0.shape, xs0.dtype),
        grid=(T // tt,),
        in_specs=[
            pl.BlockSpec((tt * TOP_K,), lambda i: (i,), memory_space=pltpu.SMEM),
            pl.BlockSpec((tt, width), lambda i: (i, 0)),
            pl.BlockSpec(memory_space=pl.ANY),
        ],
        out_specs=pl.BlockSpec(memory_space=pl.ANY),
        scratch_shapes=[pltpu.SemaphoreType.DMA(())],
        input_output_aliases={2: 0},
        compiler_params=_cparams(("arbitrary",)),
        name="moe_dispatch",
    )(dest_flat, hp, xs0)


def _expert_kernel(be_ref, nu_ref, x_ref, wg_ref, wl_ref, bg_ref, bl_ref, wd_ref, bd_ref, y_ref, acc_sc, xb_sc):
    i = pl.program_id(0)
    c = pl.program_id(1)
    half = x_ref.shape[1]

    @pl.when(i < nu_ref[0])
    def _():
        @pl.when(c == 0)
        def _():
            acc_sc[...] = jnp.zeros(acc_sc.shape, F32)
            lo, hi = _unpack_pairs(x_ref[...])
            xb_sc[:, :half] = lo.astype(BF16)
            xb_sc[:, half:] = hi.astype(BF16)

        x = xb_sc[...]
        glu = jnp.dot(x, wg_ref[...], preferred_element_type=F32) + bg_ref[...]
        lin = jnp.dot(x, wl_ref[...], preferred_element_type=F32) + bl_ref[...]
        glu = jnp.minimum(glu, SWIGLU_LIMIT)
        lin = jnp.clip(lin, -SWIGLU_LIMIT, SWIGLU_LIMIT)
        act = glu * jax.nn.sigmoid(SWIGLU_ALPHA * glu) * (lin + 1.0)
        acc_sc[...] += jnp.dot(act.astype(BF16), wd_ref[...], preferred_element_type=F32)

        @pl.when(c == pl.num_programs(1) - 1)
        def _():
            y_ref[...] = _pack_pairs((acc_sc[...] + bd_ref[...]).astype(BF16))


def _experts(xs, blk_expert, n_used, w_up, b_up, w_down, b_down, tm, th):
    P = xs.shape[0]
    E, D, two_dh = w_up.shape
    dh = two_dh // 2
    th = min(th, dh)
    nc = dh // th

    def row(i, c, be, nu):
        return (jnp.minimum(i, nu[0] - 1), 0)

    grid_spec = pltpu.PrefetchScalarGridSpec(
        num_scalar_prefetch=2,
        grid=(P // tm, nc),
        in_specs=[
            pl.BlockSpec((tm, D // 2), row),
            pl.BlockSpec((None, D, th), lambda i, c, be, nu: (be[i], 0, c)),
            pl.BlockSpec((None, D, th), lambda i, c, be, nu: (be[i], 0, nc + c)),
            pl.BlockSpec((None, 1, th), lambda i, c, be, nu: (be[i], 0, c)),
            pl.BlockSpec((None, 1, th), lambda i, c, be, nu: (be[i], 0, nc + c)),
            pl.BlockSpec((None, th, D), lambda i, c, be, nu: (be[i], c, 0)),
            pl.BlockSpec((None, 1, D), lambda i, c, be, nu: (be[i], 0, 0)),
        ],
        out_specs=pl.BlockSpec((tm, D // 2), row),
        scratch_shapes=[pltpu.VMEM((tm, D), F32), pltpu.VMEM((tm, D), BF16)],
    )
    return pl.pallas_call(
        _expert_kernel,
        out_shape=jax.ShapeDtypeStruct((P, D // 2), jnp.uint32),
        grid_spec=grid_spec,
        compiler_params=_cparams(("arbitrary", "arbitrary"), VMEM_LIMIT),
        name="moe_experts",
    )(blk_expert, n_used, xs, w_up, w_up, b_up, b_up, w_down, b_down)


def _combine_kernel(dest_ref, meta_ref, h_ref, g_ref, b_ref, y_hbm, o_ref, buf, sem, *, alpha):
    n = o_ref.shape[0]

    def row_copy(d, r):
        return pltpu.make_async_copy(y_hbm.at[pl.ds(d, 1), :], buf.at[pl.ds(r, 1), :], sem)

    def issue(t, c):
        for k in range(TOP_K):
            row_copy(dest_ref[t * TOP_K + k], k * n + t).start()
        return c

    lax.fori_loop(0, n, issue, 0)

    def drain(t, c):
        for k in range(TOP_K):
            row_copy(0, 0).wait()
        return c

    lax.fori_loop(0, n, drain, 0)

    ffn_lo = ffn_hi = None
    for k in range(TOP_K):
        lo, hi = _unpack_pairs(buf[k * n:(k + 1) * n, :])
        gate = meta_ref[:, TOP_K + k:TOP_K + k + 1]
        ffn_lo = gate * lo if k == 0 else ffn_lo + gate * lo
        ffn_hi = gate * hi if k == 0 else ffn_hi + gate * hi
    ffn = jnp.concatenate([ffn_lo, ffn_hi], axis=1)
    o_ref[...] = _layer_norm(alpha * h_ref[...] + ffn, g_ref[...], b_ref[...])


def _combine_ln(y, dest_flat, meta, h, g, b, alpha, tt):
    T, D = h.shape
    return pl.pallas_call(
        functools.partial(_combine_kernel, alpha=alpha),
        out_shape=jax.ShapeDtypeStruct((T, D), F32),
        grid=(T // tt,),
        in_specs=[
            pl.BlockSpec((tt * TOP_K,), lambda i: (i,), memory_space=pltpu.SMEM),
            pl.BlockSpec((tt, LANES), lambda i: (i, 0)),
            pl.BlockSpec((tt, D), lambda i: (i, 0)),
            pl.BlockSpec((1, D), lambda i: (0, 0)),
            pl.BlockSpec((1, D), lambda i: (0, 0)),
            pl.BlockSpec(memory_space=pl.ANY),
        ],
        out_specs=pl.BlockSpec((tt, D), lambda i: (i, 0)),
        scratch_shapes=[pltpu.VMEM((TOP_K * tt, D // 2), y.dtype), pltpu.SemaphoreType.DMA(())],
        compiler_params=_cparams(("arbitrary",), VMEM_LIMIT),
        name="moe_combine_ln2",
    )(dest_flat, meta, h, g, b, y)


def _layer(x, w_in, b_in, pos_k, pos_v, ck_w1, ck_w2, cv_w1, cv_w2, w_br_nsa, w_br_dil, w_out,
           ln1_g, ln1_b, w_router, b_router, w_up, b_up, w_down, b_down, ln2_g, ln2_b, alpha):
    B, S, D = x.shape
    T = B * S
    nd = D // LANES
    G = NSA_KV_GROUPS
    kvw = G * HEAD_DIM
    n_exp = w_router.shape[1]

    o_q = 0
    o_kv = NSA_HEADS * HEAD_DIM
    o_gl = o_kv + 6 * kvw
    o_dil = o_gl + 3 * NSA_HEADS
    o_ga = o_dil + 3 * N_DIL * DIL_W
    o_gb = o_ga + D

    def wcols(a, n):
        return w_in[:, a:a + n], b_in[a:a + n]

    def kv(i):
        return wcols(o_kv + i * kvw, kvw)

    tn = 512
    gl_w, gl_b = wcols(o_gl, 3 * NSA_HEADS)
    pieces = [wcols(o_ga, D), wcols(o_gb, D), wcols(o_q, NSA_HEADS * HEAD_DIM),
              kv(0), kv(2), kv(4), kv(1), kv(3), kv(5), (gl_w, gl_b)]
    used = sum(p[0].shape[1] for p in pieces)
    n_main = -(-used // tn) * tn
    pieces.append((jnp.zeros((D, n_main - used), F32), jnp.zeros((n_main - used,), F32)))
    w_main = jnp.concatenate([p[0] for p in pieces], axis=1).astype(BF16)
    b_main = jnp.concatenate([p[1] for p in pieces])[None, :]
    t_q = 2 * nd
    t_kc, t_ks, t_kw = t_q + 8, t_q + 10, t_q + 12
    t_vc, t_vs, t_vw = t_q + 14, t_q + 16, t_q + 18
    t_gl = t_q + 20
    flags_main = ((jnp.arange(n_main // LANES) >= t_q) & (jnp.arange(n_main // LANES) < t_vc)).astype(jnp.int32)
    q_blk0 = t_q // NSA_REP

    pos = jnp.arange(S, dtype=F32)
    inv = ROPE_THETA ** (-jnp.arange(0, HEAD_DIM, 2, dtype=F32) / HEAD_DIM)
    ang = pos[:, None] * inv[None, :]
    cosx = jnp.concatenate([jnp.cos(ang), jnp.cos(ang)], axis=-1)
    sinx = jnp.concatenate([-jnp.sin(ang), jnp.sin(ang)], axis=-1)

    main = _project(x, w_main, b_main, flags_main, cosx, sinx, 1, 1024, tn)

    nC = S // CMP_STRIDE
    half = CMP_STRIDE * HEAD_DIM

    def cmp_in(tile):
        a = main[:, 0, :, tile * LANES:(tile + G) * LANES].reshape(B, S, G, HEAD_DIM)
        return a.transpose(0, 2, 1, 3).reshape(B, G, nC, half)

    xkv = jnp.stack([cmp_in(t_kc), cmp_in(t_vc)])
    pos_kv = jnp.stack([pos_k.reshape(2, half), pos_v.reshape(2, half)])
    w1 = jnp.stack([ck_w1, cv_w1]).astype(BF16)
    w2 = jnp.stack([ck_w2, cv_w2]).astype(BF16)
    kvc = _compress(xkv, pos_kv, w1, w2)

    n_slc = S // SEL_LEN
    assert n_slc <= LANES
    c_start = jnp.arange(nC) * CMP_STRIDE
    jb = jnp.arange(LANES)
    overlap = ((c_start[:, None] < (jb[None, :] + 1) * SEL_LEN) & (c_start[:, None] + CMP_LEN > jb[None, :] * SEL_LEN)
               & (jb[None, :] < n_slc) & (c_start[:, None] + CMP_LEN <= S)).astype(BF16)
    tq = min(256, S)
    o_cmp, sel = _cmp_attention(main, kvc, overlap, q_blk0, tq)

    expand = (jnp.arange(S)[None, :] // SEL_LEN == jnp.arange(LANES)[:, None]).astype(BF16)
    o_slc = _sel_attention(main, sel, expand, q_blk0, t_ks, t_vs, 256, 512)

    tw = min(256, S)
    o_win = _band_attention(
        main, (B, G, S // tw),
        lambda bb, g, i: (bb, 0, i, q_blk0 + g),
        lambda bb, g, i: (bb, 0, i, t_kw + g),
        lambda bb, g, i: (bb, 0, i, t_vw + g),
        HEAD_DIM, (B, S, NSA_HEADS * HEAD_DIM), lambda bb, g, i: (bb, i, g), tw, WIN_LEN - 1, False)

    flags_dil = jnp.array([1] * (2 * DIL_HEADS) + [0] * DIL_HEADS, jnp.int32)
    dil_o, dil_lse = [], []
    for gi, (w, d) in enumerate(DIL_CONFIGS):
        wd, bd = wcols(o_dil + gi * 3 * DIL_W, 3 * DIL_W)
        sub = _project(x, wd.astype(BF16), bd[None, :], flags_dil, cosx, sinx, d, 256, 3 * DIL_W)
        L = S // d
        td = min(128, L)
        o_g, lse_g = _band_attention(
            sub, (B, d, L // td),
            lambda bb, r, i: (bb, r, i, 0),
            lambda bb, r, i: (bb, r, i, 1),
            lambda bb, r, i: (bb, r, i, 2),
            DIL_W, (B, L, d * DIL_W), lambda bb, r, i: (bb, i, r), td, w // d, True)
        dil_o.append(o_g.reshape(T, DIL_W))
        dil_lse.append(lse_g.reshape(T, DIL_W))

    w_r = jnp.concatenate([w_router, jnp.zeros((D, LANES - n_exp), F32)], axis=1).astype(BF16)
    b_r = jnp.concatenate([b_router, jnp.full((LANES - n_exp,), NEG, F32)])[None, :]
    h_f, h_p, logits = _merge(
        o_cmp.reshape(T, -1), o_slc.reshape(T, -1), o_win.reshape(T, -1), main.reshape(T, n_main),
        t_gl, 0, 1, dil_o, dil_lse, x.reshape(T, D),
        w_br_nsa.astype(BF16), w_br_dil.astype(BF16), w_out.astype(BF16),
        ln1_g[None, :], ln1_b[None, :], w_r, b_r, alpha, min(256, T))

    meta, cnt = _router(logits, min(512, T))
    top_idx = meta[:, 0:TOP_K].astype(jnp.int32)
    gates = meta[:, TOP_K:2 * TOP_K]
    rank = meta[:, 2 * TOP_K:3 * TOP_K].astype(jnp.int32)

    tm = 512 if T * TOP_K >= 512 * n_exp else 128
    counts = cnt[0, :n_exp].astype(jnp.int32)
    padded = (counts + tm - 1) // tm * tm
    pad_end = jnp.cumsum(padded)
    pad_start = pad_end - padded
    dest = (pad_start[top_idx] + rank).reshape(T * TOP_K)
    n_rows = T * TOP_K + n_exp * tm
    n_blk = n_rows // tm
    blk_expert = jnp.minimum(jnp.searchsorted(pad_end, jnp.arange(n_blk) * tm, side="right"),
                             n_exp - 1).astype(jnp.int32)
    n_used = (pad_end[-1:] // tm).astype(jnp.int32)

    xs = _dispatch(h_p, dest, n_rows, min(512, T))
    y = _experts(xs, blk_expert, n_used,
                 w_up.astype(BF16), b_up[:, None, :], w_down.astype(BF16), b_down[:, None, :], tm, 512)
    out = _combine_ln(y, dest, meta, h_f, ln2_g[None, :], ln2_b[None, :], alpha, min(256, T))
    return out.reshape(B, S, D)


def kernel(x, w_in, b_in, cmp_pos_k, cmp_pos_v, cmp_k_w1, cmp_k_w2, cmp_v_w1, cmp_v_w2, w_br_nsa, w_br_dil,
           w_out, ln1_g, ln1_b, w_router, b_router, w_up, b_up, w_down, b_down, ln2_g, ln2_b):
    depth = w_in.shape[0]
    alpha = (2.0 * depth) ** 0.25
    h = x
    for l in range(depth):
        h = _layer(h, w_in[l], b_in[l], cmp_pos_k[l], cmp_pos_v[l], cmp_k_w1[l], cmp_k_w2[l],
                   cmp_v_w1[l], cmp_v_w2[l], w_br_nsa[l], w_br_dil[l], w_out[l], ln1_g[l], ln1_b[l],
                   w_router[l], b_router[l], w_up[l], b_up[l], w_down[l], b_down[l], ln2_g[l], ln2_b[l], alpha)
    return h
```

```python
import functools

import jax
import jax.numpy as jnp
from jax import lax
from jax.experimental import pallas as pl
from jax.experimental.pallas import tpu as pltpu

F32 = jnp.float32
BF16 = jnp.bfloat16

HEAD_DIM = 128
LANES = 128
ROPE_THETA = 10000.0
NSA_HEADS = 8
NSA_KV_GROUPS = 2
NSA_REP = NSA_HEADS // NSA_KV_GROUPS
CMP_LEN = 32
CMP_STRIDE = 16
SEL_LEN = 64
SEL_TOPK = 16
WIN_LEN = 512
DIL_CONFIGS = ((128, 1), (512, 4), (2048, 16))
N_DIL = len(DIL_CONFIGS)
DIL_HEADS = 4
TOP_K = 4
SWIGLU_LIMIT = 7.0
SWIGLU_ALPHA = 1.702
LN_EPS = 1e-5
NEG = -1e30
FORCE = 1e9
SCALE = HEAD_DIM ** -0.5
LOG2E = 1.4426950408889634
Q_SCALE = SCALE * LOG2E
PLAIN, ROPE, ROPE_Q = 0, 1, 2

GROUP_W = NSA_REP * HEAD_DIM
DIL_W = DIL_HEADS * HEAD_DIM
VMEM_LIMIT = 56 * 1024 * 1024


def _cparams(sem, vmem=None):
    return pltpu.CompilerParams(dimension_semantics=sem, vmem_limit_bytes=vmem)


def _masked_softmax(s, mask):
    s = jnp.where(mask, s, NEG)
    m = s.max(-1, keepdims=True)
    p = jnp.where(mask, jnp.exp2(s - m), 0.0)
    den = p.sum(-1, keepdims=True)
    safe = jnp.where(den > 0, den, 1.0)
    return p, m, safe


def _qk(q, k):
    return lax.dot_general(q, k, (((1,), (1,)), ((), ())), preferred_element_type=F32)


def _pack_pairs(xb):
    m = xb.shape[1] // 2
    lo = lax.bitcast_convert_type(xb[:, :m].astype(F32), jnp.uint32)
    hi = lax.bitcast_convert_type(xb[:, m:].astype(F32), jnp.uint32)
    return (lo >> 16) | (hi & jnp.uint32(0xFFFF0000))


def _unpack_pairs(w):
    lo = lax.bitcast_convert_type(w << 16, F32)
    hi = lax.bitcast_convert_type(w & jnp.uint32(0xFFFF0000), F32)
    return lo, hi


def _proj_kernel(flags_ref, x_ref, w_ref, b_ref, cos_ref, sin_ref, o_ref, xb_ref):
    j = pl.program_id(3)
    n_sub = o_ref.shape[-1] // LANES

    @pl.when(j == 0)
    def _():
        xb_ref[...] = x_ref[...].astype(BF16)

    acc = jnp.dot(xb_ref[...], w_ref[...], preferred_element_type=F32) + b_ref[...]
    for u in range(n_sub):
        a = acc[:, u * LANES:(u + 1) * LANES]
        roped = a * cos_ref[...] + pltpu.roll(a, HEAD_DIM // 2, 1) * sin_ref[...]
        flag = flags_ref[j * n_sub + u]
        mult = jnp.where(flag == ROPE_Q, Q_SCALE, 1.0).astype(F32)
        o_ref[:, u * LANES:(u + 1) * LANES] = (jnp.where(flag == PLAIN, a, roped) * mult).astype(o_ref.dtype)


def _project(x, w, b, flags, cosx, sinx, d, tm, tn):
    B, S, D = x.shape
    N = w.shape[1]
    L = S // d
    tm = min(tm, L)
    xv = x.reshape(B, L, d * D)
    cv = cosx.reshape(L, d * HEAD_DIM)
    sv = sinx.reshape(L, d * HEAD_DIM)
    grid_spec = pltpu.PrefetchScalarGridSpec(
        num_scalar_prefetch=1,
        grid=(B, d, L // tm, N // tn),
        in_specs=[
            pl.BlockSpec((None, tm, D), lambda bb, r, i, j, f: (bb, i, r)),
            pl.BlockSpec((D, tn), lambda bb, r, i, j, f: (0, j)),
            pl.BlockSpec((1, tn), lambda bb, r, i, j, f: (0, j)),
            pl.BlockSpec((tm, HEAD_DIM), lambda bb, r, i, j, f: (i, r)),
            pl.BlockSpec((tm, HEAD_DIM), lambda bb, r, i, j, f: (i, r)),
        ],
        out_specs=pl.BlockSpec((None, None, tm, tn), lambda bb, r, i, j, f: (bb, r, i, j)),
        scratch_shapes=[pltpu.VMEM((tm, D), BF16)],
    )
    return pl.pallas_call(
        _proj_kernel,
        out_shape=jax.ShapeDtypeStruct((B, d, L, N), BF16),
        grid_spec=grid_spec,
        compiler_params=_cparams(("parallel", "arbitrary", "arbitrary", "arbitrary"), VMEM_LIMIT),
        name="proj_rope",
    )(flags, xv, w, b, cv, sv)


def _gelu_tanh(x):
    return 0.5 * x * (1.0 + jnp.tanh(0.7978845608028654 * (x + 0.044715 * x * x * x)))


def _compress_kernel(x_ref, pos_ref, w1_ref, w2_ref, o_ref):
    half = x_ref.shape[-1]
    nc = x_ref.shape[0]
    x = x_ref[...].astype(F32)
    lo = (x + pos_ref[0:1, :]).astype(BF16)
    hi = (x + pos_ref[1:2, :]).astype(BF16)
    y_lo = jnp.dot(lo, w1_ref[0:half, :], preferred_element_type=F32)
    y_hi = jnp.dot(hi, w1_ref[half:2 * half, :], preferred_element_type=F32)
    h = y_lo + pltpu.roll(y_hi, nc - 1, 0)
    g = _gelu_tanh(h).astype(BF16)
    o_ref[...] = jnp.dot(g, w2_ref[...], preferred_element_type=F32).astype(o_ref.dtype)


def _compress(xkv, pos, w1, w2):
    _, B, G, nC, half = xkv.shape
    hid = w1.shape[-1]
    return pl.pallas_call(
        _compress_kernel,
        out_shape=jax.ShapeDtypeStruct((2, B, G, nC, HEAD_DIM), BF16),
        grid=(2, B, G),
        in_specs=[
            pl.BlockSpec((None, None, None, nC, half), lambda a, bb, g: (a, bb, g, 0, 0)),
            pl.BlockSpec((None, 2, half), lambda a, bb, g: (a, 0, 0)),
            pl.BlockSpec((None, 2 * half, hid), lambda a, bb, g: (a, 0, 0)),
            pl.BlockSpec((None, hid, HEAD_DIM), lambda a, bb, g: (a, 0, 0)),
        ],
        out_specs=pl.BlockSpec((None, None, None, nC, HEAD_DIM), lambda a, bb, g: (a, bb, g, 0, 0)),
        compiler_params=_cparams(("arbitrary", "arbitrary", "arbitrary"), VMEM_LIMIT),
        name="compress_mlp",
    )(xkv, pos, w1, w2)


def _cmp_attn_kernel(q_ref, kc_ref, vc_ref, ov_ref, o_ref, sel_ref, *, n_slc, n_sel):
    qi = pl.program_id(2)
    tq = q_ref.shape[0]
    nc = kc_ref.shape[0]
    t = qi * tq + lax.broadcasted_iota(jnp.int32, (tq, nc), 0)
    c = lax.broadcasted_iota(jnp.int32, (tq, nc), 1)
    mask = (c * CMP_STRIDE + (CMP_LEN - 1)) <= t
    kc = kc_ref[...]
    vc = vc_ref[...]
    ps = jnp.zeros((tq, nc), F32)
    for h in range(NSA_REP):
        s = _qk(q_ref[:, h * HEAD_DIM:(h + 1) * HEAD_DIM], kc)
        p, _, safe = _masked_softmax(s, mask)
        p = p / safe
        o = jnp.dot(p.astype(BF16), vc, preferred_element_type=F32)
        o_ref[:, h * HEAD_DIM:(h + 1) * HEAD_DIM] = o.astype(o_ref.dtype)
        ps = ps + p
    imp = jnp.dot(ps.astype(BF16), ov_ref[...], preferred_element_type=F32)
    tj = qi * tq + lax.broadcasted_iota(jnp.int32, (tq, LANES), 0)
    j = lax.broadcasted_iota(jnp.int32, (tq, LANES), 1)
    cur = tj // SEL_LEN
    forced = (j == 0) | (j == cur) | (j == cur - 1)
    imp = jnp.where(forced, FORCE, jnp.where(j > cur, NEG, imp))
    imp_t = imp.T[0:n_slc, :]
    jj = lax.broadcasted_iota(jnp.int32, (n_slc, tq), 0)
    rank = jnp.zeros((n_slc, tq), jnp.int32)
    for j2 in range(n_slc):
        row = imp_t[j2:j2 + 1, :]
        ahead = (row > imp_t) | ((row == imp_t) & (jj > j2))
        rank = rank + ahead.astype(jnp.int32)
    sel_t = jnp.where(rank < n_sel, 0.0, NEG)
    if n_slc < LANES:
        sel_t = jnp.concatenate([sel_t, jnp.zeros((LANES - n_slc, tq), F32)], axis=0)
    sel_ref[...] = sel_t.T.astype(sel_ref.dtype)


def _cmp_attention(main, kvc, overlap, q_blk0, tq):
    B, _, S, _ = main.shape
    G = NSA_KV_GROUPS
    nC = kvc.shape[3]
    n_slc = S // SEL_LEN
    n_sel = min(SEL_TOPK, n_slc)
    kern = functools.partial(_cmp_attn_kernel, n_slc=n_slc, n_sel=n_sel)
    return pl.pallas_call(
        kern,
        out_shape=(jax.ShapeDtypeStruct((B, S, NSA_HEADS * HEAD_DIM), BF16),
                   jax.ShapeDtypeStruct((B, G, S, LANES), BF16)),
        grid=(B, G, S // tq),
        in_specs=[
            pl.BlockSpec((None, None, tq, GROUP_W), lambda bb, g, i: (bb, 0, i, q_blk0 + g)),
            pl.BlockSpec((None, None, None, nC, HEAD_DIM), lambda bb, g, i: (0, bb, g, 0, 0)),
            pl.BlockSpec((None, None, None, nC, HEAD_DIM), lambda bb, g, i: (1, bb, g, 0, 0)),
            pl.BlockSpec((nC, LANES), lambda bb, g, i: (0, 0)),
        ],
        out_specs=(pl.BlockSpec((None, tq, GROUP_W), lambda bb, g, i: (bb, i, g)),
                   pl.BlockSpec((None, None, tq, LANES), lambda bb, g, i: (bb, g, i, 0))),
        compiler_params=_cparams(("parallel", "arbitrary", "arbitrary"), VMEM_LIMIT),
        name="cmp_attn_select",
    )(main, kvc, kvc, overlap)


def _fold_lanes(x, op):
    out = x[:, 0:LANES]
    for u in range(1, x.shape[1] // LANES):
        out = op(out, x[:, u * LANES:(u + 1) * LANES])
    return out


def _sel_attn_kernel(q_ref, k_ref, v_ref, bias_ref, et_ref, o_ref, qx_sc, s_sc, m_sc, l_sc, acc_sc):
    qi = pl.program_id(2)
    t = q_ref.shape[0]
    rows = NSA_REP * t
    for h in range(NSA_REP):
        qx_sc[h * t:(h + 1) * t, 0:HEAD_DIM] = q_ref[:, h * HEAD_DIM:(h + 1) * HEAD_DIM]
        qx_sc[h * t:(h + 1) * t, HEAD_DIM:2 * HEAD_DIM] = bias_ref[...]

    def scores(c):
        start = pl.multiple_of(c * t, t)
        kx = jnp.concatenate([k_ref[pl.ds(start, t), :], et_ref[pl.ds(start, t), :]], axis=1)
        return _qk(qx_sc[...], kx)

    m_sc[...] = jnp.full(m_sc.shape, NEG, F32)

    def max_pass(c, carry):
        s = scores(c)
        s_sc[c] = s
        m_sc[...] = jnp.maximum(m_sc[...], _fold_lanes(s, jnp.maximum))
        return carry

    lax.fori_loop(0, qi, max_pass, 0)
    qrow = lax.broadcasted_iota(jnp.int32, (rows, t), 0) % t
    kcol = lax.broadcasted_iota(jnp.int32, (rows, t), 1)
    s = jnp.where(kcol <= qrow, scores(qi), NEG)
    s_sc[qi] = s
    m = jnp.maximum(m_sc[...], _fold_lanes(s, jnp.maximum)).max(-1, keepdims=True)
    m_sc[...] = jnp.broadcast_to(m, m_sc.shape)
    l_sc[...] = jnp.zeros(l_sc.shape, F32)
    acc_sc[...] = jnp.zeros(acc_sc.shape, F32)

    def exp_pass(c, carry):
        mb = m_sc[...]
        sc = s_sc[c]
        p = jnp.concatenate([jnp.exp2(sc[:, u * LANES:(u + 1) * LANES] - mb) for u in range(t // LANES)], axis=1)
        l_sc[...] += _fold_lanes(p, jnp.add)
        v = v_ref[pl.ds(pl.multiple_of(c * t, t), t), :]
        acc_sc[...] += jnp.dot(p.astype(BF16), v, preferred_element_type=F32)
        return carry

    lax.fori_loop(0, qi + 1, exp_pass, 0)
    l = l_sc[...].sum(-1, keepdims=True)
    o = acc_sc[...] / jnp.where(l > 0, l, 1.0)
    for h in range(NSA_REP):
        o_ref[:, h * HEAD_DIM:(h + 1) * HEAD_DIM] = o[h * t:(h + 1) * t].astype(o_ref.dtype)


def _sel_attention(main, bias, onehot_t, q_blk0, k_tile0, v_tile0, t):
    B, _, S, _ = main.shape
    G = NSA_KV_GROUPS
    t = min(t, S)
    rows = NSA_REP * t
    return pl.pallas_call(
        _sel_attn_kernel,
        out_shape=jax.ShapeDtypeStruct((B, S, NSA_HEADS * HEAD_DIM), BF16),
        grid=(B, G, S // t),
        in_specs=[
            pl.BlockSpec((None, None, t, GROUP_W), lambda bb, g, i: (bb, 0, i, q_blk0 + g)),
            pl.BlockSpec((None, None, S, HEAD_DIM), lambda bb, g, i: (bb, 0, 0, k_tile0 + g)),
            pl.BlockSpec((None, None, S, HEAD_DIM), lambda bb, g, i: (bb, 0, 0, v_tile0 + g)),
            pl.BlockSpec((None, None, t, LANES), lambda bb, g, i: (bb, g, i, 0)),
            pl.BlockSpec((S, LANES), lambda bb, g, i: (0, 0)),
        ],
        out_specs=pl.BlockSpec((None, t, GROUP_W), lambda bb, g, i: (bb, i, g)),
        scratch_shapes=[pltpu.VMEM((rows, 2 * HEAD_DIM), BF16), pltpu.VMEM((S // t, rows, t), F32),
                        pltpu.VMEM((rows, LANES), F32), pltpu.VMEM((rows, LANES), F32),
                        pltpu.VMEM((rows, HEAD_DIM), F32)],
        compiler_params=_cparams(("parallel", "arbitrary", "arbitrary"), VMEM_LIMIT),
        name="selected_attn",
    )(main, main, main, bias, onehot_t)


def _band_kernel(*refs, n_kt, max_dist, kv_heads, with_lse):
    q_ref = refs[0]
    k_refs = refs[1:1 + n_kt]
    v_refs = refs[1 + n_kt:1 + 2 * n_kt]
    o_ref = refs[1 + 2 * n_kt]
    qi = pl.program_id(2)
    tq = q_ref.shape[0]
    tk = k_refs[0].shape[0]
    nk = n_kt * tk
    qpos = qi * tq + lax.broadcasted_iota(jnp.int32, (tq, nk), 0)
    kpos = (qi - (n_kt - 1)) * tk + lax.broadcasted_iota(jnp.int32, (tq, nk), 1)
    diff = qpos - kpos
    mask = (kpos >= 0) & (diff >= 0) & (diff <= max_dist)
    k = jnp.concatenate([r[...] for r in k_refs], axis=0)
    v = jnp.concatenate([r[...] for r in v_refs], axis=0)
    n_heads = q_ref.shape[1] // HEAD_DIM
    lane = lax.broadcasted_iota(jnp.int32, (tq, LANES), 1)
    lse = jnp.zeros((tq, LANES), F32)
    for h in range(n_heads):
        hk = h if kv_heads > 1 else 0
        s = _qk(q_ref[:, h * HEAD_DIM:(h + 1) * HEAD_DIM], k[:, hk * HEAD_DIM:(hk + 1) * HEAD_DIM])
        p, m, safe = _masked_softmax(s, mask)
        o = jnp.dot(p.astype(BF16), v[:, hk * HEAD_DIM:(hk + 1) * HEAD_DIM], preferred_element_type=F32)
        o_ref[:, h * HEAD_DIM:(h + 1) * HEAD_DIM] = (o / safe).astype(o_ref.dtype)
        lse = jnp.where(lane == h, m + jnp.log2(safe), lse)
    if with_lse:
        refs[2 + 2 * n_kt][...] = lse


def _band_attention(src, grid, q_map, k_map, v_map, kv_width, out_shape, o_map, t, max_dist, with_lse):
    n_kt = -(-max_dist // t) + 1
    kern = functools.partial(_band_kernel, n_kt=n_kt, max_dist=max_dist,
                             kv_heads=kv_width // HEAD_DIM, with_lse=with_lse)

    def shifted(fn, u):
        def index_map(bb, a, i):
            return fn(bb, a, jnp.maximum(i - (n_kt - 1) + u, 0))
        return index_map

    in_specs = [pl.BlockSpec((None, None, t, GROUP_W), q_map)]
    in_specs += [pl.BlockSpec((None, None, t, kv_width), shifted(k_map, u)) for u in range(n_kt)]
    in_specs += [pl.BlockSpec((None, None, t, kv_width), shifted(v_map, u)) for u in range(n_kt)]
    o_spec = pl.BlockSpec((None, t, GROUP_W), o_map)
    if with_lse:
        lse_shape = out_shape[:-1] + (out_shape[-1] // GROUP_W * LANES,)
        out_shapes = (jax.ShapeDtypeStruct(out_shape, BF16), jax.ShapeDtypeStruct(lse_shape, F32))
        out_specs = (o_spec, pl.BlockSpec((None, t, LANES), o_map))
    else:
        out_shapes = jax.ShapeDtypeStruct(out_shape, BF16)
        out_specs = o_spec
    return pl.pallas_call(
        kern, out_shape=out_shapes, grid=grid, in_specs=in_specs, out_specs=out_specs,
        compiler_params=_cparams(("parallel", "arbitrary", "arbitrary"), VMEM_LIMIT),
        name="band_attn",
    )(*([src] * (1 + 2 * n_kt)))


def _layer_norm(z, g, b):
    mu = z.mean(-1, keepdims=True)
    zc = z - mu
    var = (zc * zc).mean(-1, keepdims=True)
    return zc * lax.rsqrt(var + LN_EPS) * g + b


def _merge_kernel(ocmp_ref, oslc_ref, owin_ref, gl_ref, ga_ref, gb_ref,
                  d0_ref, d1_ref, d2_ref, l0_ref, l1_ref, l2_ref, x_ref,
                  wa_ref, wb_ref, wo_ref, g_ref, b_ref, wr_ref, br_ref, hf_ref, hp_ref, lg_ref, *, alpha):
    tt = x_ref.shape[0]
    gates = jax.nn.sigmoid(gl_ref[...].astype(F32))
    parts = []
    for h in range(NSA_HEADS):
        sl = slice(h * HEAD_DIM, (h + 1) * HEAD_DIM)
        acc = jnp.zeros((tt, HEAD_DIM), F32)
        for br, ref in enumerate((ocmp_ref, oslc_ref, owin_ref)):
            gcol = gates[:, 3 * h + br:3 * h + br + 1]
            acc = acc + gcol * ref[:, sl].astype(F32)
        parts.append(acc.astype(BF16))
    o_nsa = jnp.concatenate(parts, axis=1)
    l0, l1, l2 = l0_ref[...], l1_ref[...], l2_ref[...]
    lm = jnp.maximum(jnp.maximum(l0, l1), l2)
    e0, e1, e2 = jnp.exp2(l0 - lm), jnp.exp2(l1 - lm), jnp.exp2(l2 - lm)
    inv = 1.0 / (e0 + e1 + e2)
    w0, w1, w2 = e0 * inv, e1 * inv, e2 * inv
    parts = []
    for h in range(DIL_HEADS):
        sl = slice(h * HEAD_DIM, (h + 1) * HEAD_DIM)
        parts.append(w0[:, h:h + 1] * d0_ref[:, sl].astype(F32) + w1[:, h:h + 1] * d1_ref[:, sl].astype(F32)
                     + w2[:, h:h + 1] * d2_ref[:, sl].astype(F32))
    o_dil = jnp.concatenate(parts, axis=1)
    y_a = jnp.dot(o_nsa, wa_ref[...], preferred_element_type=F32)
    y_b = jnp.dot(o_dil.astype(BF16), wb_ref[...], preferred_element_type=F32)
    merged = (jax.nn.sigmoid(ga_ref[...].astype(F32)) * y_a
              + jax.nn.sigmoid(gb_ref[...].astype(F32)) * y_b)
    mix = jnp.dot(merged.astype(BF16), wo_ref[...], preferred_element_type=F32)
    h = _layer_norm(alpha * x_ref[...] + mix, g_ref[...], b_ref[...])
    hf_ref[...] = h
    hb = h.astype(BF16)
    hp_ref[...] = _pack_pairs(hb)
    lg_ref[...] = jnp.dot(hb, wr_ref[...], preferred_element_type=F32) + br_ref[...]


def _merge(o_cmp, o_slc, o_win, main2d, gl_tile, ga_blk, gb_blk, dil_o, dil_lse, x2d,
           w_a, w_b, w_o, ln_g, ln_b, w_r, b_r, alpha, tt):
    T, D = x2d.shape
    nsa_w = NSA_HEADS * HEAD_DIM
    row = lambda i: (i, 0)
    const = lambda i: (0, 0)
    in_specs = [
        pl.BlockSpec((tt, nsa_w), row), pl.BlockSpec((tt, nsa_w), row), pl.BlockSpec((tt, nsa_w), row),
        pl.BlockSpec((tt, LANES), lambda i: (i, gl_tile)),
        pl.BlockSpec((tt, D), lambda i: (i, ga_blk)),
        pl.BlockSpec((tt, D), lambda i: (i, gb_blk)),
    ]
    in_specs += [pl.BlockSpec((tt, DIL_W), row)] * 3 + [pl.BlockSpec((tt, LANES), row)] * 3
    in_specs += [
        pl.BlockSpec((tt, D), row),
        pl.BlockSpec((nsa_w, D), const), pl.BlockSpec((DIL_W, D), const), pl.BlockSpec((D, D), const),
        pl.BlockSpec((1, D), const), pl.BlockSpec((1, D), const),
        pl.BlockSpec((D, LANES), const), pl.BlockSpec((1, LANES), const),
    ]
    return pl.pallas_call(
        functools.partial(_merge_kernel, alpha=alpha),
        out_shape=(jax.ShapeDtypeStruct((T, D), F32), jax.ShapeDtypeStruct((T, D // 2), jnp.uint32),
                   jax.ShapeDtypeStruct((T, LANES), F32)),
        grid=(T // tt,),
        in_specs=in_specs,
        out_specs=(pl.BlockSpec((tt, D), row), pl.BlockSpec((tt, D // 2), row), pl.BlockSpec((tt, LANES), row)),
        compiler_params=_cparams(("parallel",), VMEM_LIMIT),
        name="merge_ln1",
    )(o_cmp, o_slc, o_win, main2d, main2d, main2d, *dil_o, *dil_lse, x2d, w_a, w_b, w_o, ln_g, ln_b, w_r, b_r)


def _router_kernel(lg_ref, tri_ref, meta_ref, cnt_ref, carry_sc):
    i = pl.program_id(0)
    tt = lg_ref.shape[0]

    @pl.when(i == 0)
    def _():
        carry_sc[...] = jnp.zeros(carry_sc.shape, F32)

    logits = lg_ref[...]
    lane = lax.broadcasted_iota(jnp.int32, (tt, LANES), 1)
    v = logits
    onehot = jnp.zeros((tt, LANES), F32)
    vals, idxs = [], []
    for _ in range(TOP_K):
        m = v.max(-1, keepdims=True)
        idx = jnp.where(v == m, lane, LANES).min(-1, keepdims=True)
        hit = lane == idx
        vals.append(m)
        idxs.append(idx)
        onehot = onehot + hit.astype(F32)
        v = jnp.where(hit, -jnp.inf, v)
    exps = [jnp.exp(vk - vals[0]) for vk in vals]
    den = exps[0] + exps[1] + exps[2] + exps[3]
    before = jnp.dot(tri_ref[...], onehot.astype(BF16), preferred_element_type=F32) + carry_sc[0:1, :]
    meta = jnp.zeros((tt, LANES), F32)
    for k in range(TOP_K):
        rank = jnp.where(lane == idxs[k], before, 0.0).sum(-1, keepdims=True)
        meta = jnp.where(lane == k, idxs[k].astype(F32), meta)
        meta = jnp.where(lane == TOP_K + k, exps[k] / den, meta)
        meta = jnp.where(lane == 2 * TOP_K + k, rank, meta)
    meta_ref[...] = meta
    carry_sc[...] = carry_sc[...] + jnp.broadcast_to(onehot.sum(0, keepdims=True), carry_sc.shape)
    cnt_ref[...] = carry_sc[...]


def _router(logits, tt):
    T = logits.shape[0]
    tri = (jnp.arange(tt)[:, None] > jnp.arange(tt)[None, :]).astype(BF16)
    return pl.pallas_call(
        _router_kernel,
        out_shape=(jax.ShapeDtypeStruct((T, LANES), F32), jax.ShapeDtypeStruct((8, LANES), F32)),
        grid=(T // tt,),
        in_specs=[
            pl.BlockSpec((tt, LANES), lambda i: (i, 0)),
            pl.BlockSpec((tt, tt), lambda i: (0, 0)),
        ],
        out_specs=(pl.BlockSpec((tt, LANES), lambda i: (i, 0)), pl.BlockSpec((8, LANES), lambda i: (0, 0))),
        scratch_shapes=[pltpu.VMEM((8, LANES), F32)],
        compiler_params=_cparams(("arbitrary",), VMEM_LIMIT),
        name="router_top4",
    )(logits, tri)


def _dispatch_kernel(dest_ref, h_ref, xs_in_hbm, xs_hbm, sem):
    del xs_in_hbm
    n = h_ref.shape[0]

    def row_copy(t, d):
        return pltpu.make_async_copy(h_ref.at[pl.ds(t, 1), :], xs_hbm.at[pl.ds(d, 1), :], sem)

    def issue(t, c):
        for k in range(TOP_K):
            row_copy(t, dest_ref[t * TOP_K + k]).start(priority=k % 2)
        return c

    lax.fori_loop(0, n, issue, 0)

    def drain(t, c):
        for k in range(TOP_K):
            row_copy(0, 0).wait()
        return c

    lax.fori_loop(0, n, drain, 0)


def _dispatch(hp, dest_flat, n_rows, tt):
    T, width = hp.shape
    xs0 = jnp.zeros((n_rows, width), hp.dtype)
    return pl.pallas_call(
        _dispatch_kernel,
        out_shape=jax.ShapeDtypeStruct(xs0.shape, xs0.dtype),
        grid=(T // tt,),
        in_specs=[
            pl.BlockSpec((tt * TOP_K,), lambda i: (i,), memory_space=pltpu.SMEM),
            pl.BlockSpec((tt, width), lambda i: (i, 0)),
            pl.BlockSpec(memory_space=pl.ANY),
        ],
        out_specs=pl.BlockSpec(memory_space=pl.ANY),
        scratch_shapes=[pltpu.SemaphoreType.DMA(())],
        input_output_aliases={2: 0},
        compiler_params=_cparams(("arbitrary",)),
        name="moe_dispatch",
    )(dest_flat, hp, xs0)


def _expert_kernel(be_ref, nu_ref, x_ref, wg_ref, wl_ref, bg_ref, bl_ref, wd_ref, bd_ref, y_ref, acc_sc, xb_sc):
    i = pl.program_id(0)
    c = pl.program_id(1)
    half = x_ref.shape[1]

    @pl.when(i < nu_ref[0])
    def _():
        @pl.when(c == 0)
        def _():
            acc_sc[...] = jnp.zeros(acc_sc.shape, F32)
            lo, hi = _unpack_pairs(x_ref[...])
            xb_sc[:, :half] = lo.astype(BF16)
            xb_sc[:, half:] = hi.astype(BF16)

        x = xb_sc[...]
        glu = jnp.dot(x, wg_ref[...], preferred_element_type=F32) + bg_ref[...]
        lin = jnp.dot(x, wl_ref[...], preferred_element_type=F32) + bl_ref[...]
        glu = jnp.minimum(glu, SWIGLU_LIMIT)
        lin = jnp.clip(lin, -SWIGLU_LIMIT, SWIGLU_LIMIT)
        act = glu * jax.nn.sigmoid(SWIGLU_ALPHA * glu) * (lin + 1.0)
        acc_sc[...] += jnp.dot(act.astype(BF16), wd_ref[...], preferred_element_type=F32)

        @pl.when(c == pl.num_programs(1) - 1)
        def _():
            y_ref[...] = _pack_pairs((acc_sc[...] + bd_ref[...]).astype(BF16))


def _experts(xs, blk_expert, n_used, w_up, b_up, w_down, b_down, tm, th):
    P = xs.shape[0]
    E, D, two_dh = w_up.shape
    dh = two_dh // 2
    th = min(th, dh)
    nc = dh // th

    def row(i, c, be, nu):
        return (jnp.minimum(i, nu[0] - 1), 0)

    grid_spec = pltpu.PrefetchScalarGridSpec(
        num_scalar_prefetch=2,
        grid=(P // tm, nc),
        in_specs=[
            pl.BlockSpec((tm, D // 2), row),
            pl.BlockSpec((None, D, th), lambda i, c, be, nu: (be[i], 0, c)),
            pl.BlockSpec((None, D, th), lambda i, c, be, nu: (be[i], 0, nc + c)),
            pl.BlockSpec((None, 1, th), lambda i, c, be, nu: (be[i], 0, c)),
            pl.BlockSpec((None, 1, th), lambda i, c, be, nu: (be[i], 0, nc + c)),
            pl.BlockSpec((None, th, D), lambda i, c, be, nu: (be[i], c, 0)),
            pl.BlockSpec((None, 1, D), lambda i, c, be, nu: (be[i], 0, 0)),
        ],
        out_specs=pl.BlockSpec((tm, D // 2), row),
        scratch_shapes=[pltpu.VMEM((tm, D), F32), pltpu.VMEM((tm, D), BF16)],
    )
    return pl.pallas_call(
        _expert_kernel,
        out_shape=jax.ShapeDtypeStruct((P, D // 2), jnp.uint32),
        grid_spec=grid_spec,
        compiler_params=_cparams(("arbitrary", "arbitrary"), VMEM_LIMIT),
        name="moe_experts",
    )(blk_expert, n_used, xs, w_up, w_up, b_up, b_up, w_down, b_down)


def _combine_kernel(dest_ref, meta_ref, h_ref, g_ref, b_ref, y_hbm, o_ref, buf, sem, *, alpha):
    n = o_ref.shape[0]

    def row_copy(d, r):
        return pltpu.make_async_copy(y_hbm.at[pl.ds(d, 1), :], buf.at[pl.ds(r, 1), :], sem)

    def issue(t, c):
        for k in range(TOP_K):
            row_copy(dest_ref[t * TOP_K + k], k * n + t).start(priority=k % 2)
        return c

    lax.fori_loop(0, n, issue, 0)

    def drain(t, c):
        for k in range(TOP_K):
            row_copy(0, 0).wait()
        return c

    lax.fori_loop(0, n, drain, 0)

    ffn_lo = ffn_hi = None
    for k in range(TOP_K):
        lo, hi = _unpack_pairs(buf[k * n:(k + 1) * n, :])
        gate = meta_ref[:, TOP_K + k:TOP_K + k + 1]
        ffn_lo = gate * lo if k == 0 else ffn_lo + gate * lo
        ffn_hi = gate * hi if k == 0 else ffn_hi + gate * hi
    ffn = jnp.concatenate([ffn_lo, ffn_hi], axis=1)
    o_ref[...] = _layer_norm(alpha * h_ref[...] + ffn, g_ref[...], b_ref[...])


def _combine_ln(y, dest_flat, meta, h, g, b, alpha, tt):
    T, D = h.shape
    return pl.pallas_call(
        functools.partial(_combine_kernel, alpha=alpha),
        out_shape=jax.ShapeDtypeStruct((T, D), F32),
        grid=(T // tt,),
        in_specs=[
            pl.BlockSpec((tt * TOP_K,), lambda i: (i,), memory_space=pltpu.SMEM),
            pl.BlockSpec((tt, LANES), lambda i: (i, 0)),
            pl.BlockSpec((tt, D), lambda i: (i, 0)),
            pl.BlockSpec((1, D), lambda i: (0, 0)),
            pl.BlockSpec((1, D), lambda i: (0, 0)),
            pl.BlockSpec(memory_space=pl.ANY),
        ],
        out_specs=pl.BlockSpec((tt, D), lambda i: (i, 0)),
        scratch_shapes=[pltpu.VMEM((TOP_K * tt, D // 2), y.dtype), pltpu.SemaphoreType.DMA(())],
        compiler_params=_cparams(("arbitrary",), VMEM_LIMIT),
        name="moe_combine_ln2",
    )(dest_flat, meta, h, g, b, y)


def _layer(x, w_in, b_in, pos_k, pos_v, ck_w1, ck_w2, cv_w1, cv_w2, w_br_nsa, w_br_dil, w_out,
           ln1_g, ln1_b, w_router, b_router, w_up, b_up, w_down, b_down, ln2_g, ln2_b, alpha):
    B, S, D = x.shape
    T = B * S
    nd = D // LANES
    G = NSA_KV_GROUPS
    kvw = G * HEAD_DIM
    n_exp = w_router.shape[1]

    o_q = 0
    o_kv = NSA_HEADS * HEAD_DIM
    o_gl = o_kv + 6 * kvw
    o_dil = o_gl + 3 * NSA_HEADS
    o_ga = o_dil + 3 * N_DIL * DIL_W
    o_gb = o_ga + D

    def wcols(a, n):
        return w_in[:, a:a + n], b_in[a:a + n]

    def kv(i):
        return wcols(o_kv + i * kvw, kvw)

    tn = 512
    gl_w, gl_b = wcols(o_gl, 3 * NSA_HEADS)
    pieces = [wcols(o_ga, D), wcols(o_gb, D), wcols(o_q, NSA_HEADS * HEAD_DIM),
              kv(0), kv(2), kv(4), kv(1), kv(3), kv(5), (gl_w, gl_b)]
    used = sum(p[0].shape[1] for p in pieces)
    n_main = -(-used // tn) * tn
    pieces.append((jnp.zeros((D, n_main - used), F32), jnp.zeros((n_main - used,), F32)))
    w_main = jnp.concatenate([p[0] for p in pieces], axis=1).astype(BF16)
    b_main = jnp.concatenate([p[1] for p in pieces])[None, :]
    t_q = 2 * nd
    t_kc, t_ks, t_kw = t_q + 8, t_q + 10, t_q + 12
    t_vc, t_vs, t_vw = t_q + 14, t_q + 16, t_q + 18
    t_gl = t_q + 20
    tile_id = jnp.arange(n_main // LANES)
    flags_main = jnp.where((tile_id >= t_q) & (tile_id < t_kc), ROPE_Q,
                           jnp.where((tile_id >= t_kc) & (tile_id < t_vc), ROPE, PLAIN)).astype(jnp.int32)
    q_blk0 = t_q // NSA_REP

    pos = jnp.arange(S, dtype=F32)
    inv = ROPE_THETA ** (-jnp.arange(0, HEAD_DIM, 2, dtype=F32) / HEAD_DIM)
    ang = pos[:, None] * inv[None, :]
    cosx = jnp.concatenate([jnp.cos(ang), jnp.cos(ang)], axis=-1)
    sinx = jnp.concatenate([-jnp.sin(ang), jnp.sin(ang)], axis=-1)

    main = _project(x, w_main, b_main, flags_main, cosx, sinx, 1, 1024, tn)

    nC = S // CMP_STRIDE
    half = CMP_STRIDE * HEAD_DIM

    def cmp_in(tile):
        a = main[:, 0, :, tile * LANES:(tile + G) * LANES].reshape(B, S, G, HEAD_DIM)
        return a.transpose(0, 2, 1, 3).reshape(B, G, nC, half)

    xkv = jnp.stack([cmp_in(t_kc), cmp_in(t_vc)])
    pos_kv = jnp.stack([pos_k.reshape(2, half), pos_v.reshape(2, half)])
    w1 = jnp.stack([ck_w1, cv_w1]).astype(BF16)
    w2 = jnp.stack([ck_w2, cv_w2]).astype(BF16)
    kvc = _compress(xkv, pos_kv, w1, w2)

    n_slc = S // SEL_LEN
    assert n_slc <= LANES
    c_start = jnp.arange(nC) * CMP_STRIDE
    jb = jnp.arange(LANES)
    overlap = ((c_start[:, None] < (jb[None, :] + 1) * SEL_LEN) & (c_start[:, None] + CMP_LEN > jb[None, :] * SEL_LEN)
               & (jb[None, :] < n_slc) & (c_start[:, None] + CMP_LEN <= S)).astype(BF16)
    tq = min(256, S)
    o_cmp, sel = _cmp_attention(main, kvc, overlap, q_blk0, tq)

    onehot_t = (jnp.arange(S)[:, None] // SEL_LEN == jnp.arange(LANES)[None, :]).astype(BF16)
    o_slc = _sel_attention(main, sel, onehot_t, q_blk0, t_ks, t_vs, 256)

    tw = min(256, S)
    o_win = _band_attention(
        main, (B, G, S // tw),
        lambda bb, g, i: (bb, 0, i, q_blk0 + g),
        lambda bb, g, i: (bb, 0, i, t_kw + g),
        lambda bb, g, i: (bb, 0, i, t_vw + g),
        HEAD_DIM, (B, S, NSA_HEADS * HEAD_DIM), lambda bb, g, i: (bb, i, g), tw, WIN_LEN - 1, False)

    flags_dil = jnp.array([ROPE_Q] * DIL_HEADS + [ROPE] * DIL_HEADS + [PLAIN] * DIL_HEADS, jnp.int32)
    dil_o, dil_lse = [], []
    for gi, (w, d) in enumerate(DIL_CONFIGS):
        wd, bd = wcols(o_dil + gi * 3 * DIL_W, 3 * DIL_W)
        sub = _project(x, wd.astype(BF16), bd[None, :], flags_dil, cosx, sinx, d, 256, 3 * DIL_W)
        L = S // d
        td = min(128, L)
        o_g, lse_g = _band_attention(
            sub, (B, d, L // td),
            lambda bb, r, i: (bb, r, i, 0),
            lambda bb, r, i: (bb, r, i, 1),
            lambda bb, r, i: (bb, r, i, 2),
            DIL_W, (B, L, d * DIL_W), lambda bb, r, i: (bb, i, r), td, w // d, True)
        dil_o.append(o_g.reshape(T, DIL_W))
        dil_lse.append(lse_g.reshape(T, LANES))

    w_r = jnp.concatenate([w_router, jnp.zeros((D, LANES - n_exp), F32)], axis=1).astype(BF16)
    b_r = jnp.concatenate([b_router, jnp.full((LANES - n_exp,), NEG, F32)])[None, :]
    h_f, h_p, logits = _merge(
        o_cmp.reshape(T, -1), o_slc.reshape(T, -1), o_win.reshape(T, -1), main.reshape(T, n_main),
        t_gl, 0, 1, dil_o, dil_lse, x.reshape(T, D),
        w_br_nsa.astype(BF16), w_br_dil.astype(BF16), w_out.astype(BF16),
        ln1_g[None, :], ln1_b[None, :], w_r, b_r, alpha, min(256, T))

    meta, cnt = _router(logits, min(512, T))
    top_idx = meta[:, 0:TOP_K].astype(jnp.int32)
    gates = meta[:, TOP_K:2 * TOP_K]
    rank = meta[:, 2 * TOP_K:3 * TOP_K].astype(jnp.int32)

    tm = 512 if T * TOP_K >= 512 * n_exp else 128
    counts = cnt[0, :n_exp].astype(jnp.int32)
    padded = (counts + tm - 1) // tm * tm
    pad_end = jnp.cumsum(padded)
    pad_start = pad_end - padded
    dest = (pad_start[top_idx] + rank).reshape(T * TOP_K)
    n_rows = T * TOP_K + n_exp * tm
    n_blk = n_rows // tm
    blk_start = jnp.arange(n_blk, dtype=jnp.int32) * tm
    blk_expert = jnp.minimum((pad_end[None, :] <= blk_start[:, None]).sum(-1), n_exp - 1).astype(jnp.int32)
    n_used = (pad_end[-1:] // tm).astype(jnp.int32)

    xs = _dispatch(h_p, dest, n_rows, min(512, T))
    y = _experts(xs, blk_expert, n_used,
                 w_up.astype(BF16), b_up[:, None, :], w_down.astype(BF16), b_down[:, None, :], tm, 1024)
    out = _combine_ln(y, dest, meta, h_f, ln2_g[None, :], ln2_b[None, :], alpha, min(256, T))
    return out.reshape(B, S, D)


def kernel(x, w_in, b_in, cmp_pos_k, cmp_pos_v, cmp_k_w1, cmp_k_w2, cmp_v_w1, cmp_v_w2, w_br_nsa, w_br_dil,
           w_out, ln1_g, ln1_b, w_router, b_router, w_up, b_up, w_down, b_down, ln2_g, ln2_b):
    depth = w_in.shape[0]
    alpha = (2.0 * depth) ** 0.25
    h = x
    for l in range(depth):
        h = _layer(h, w_in[l], b_in[l], cmp_pos_k[l], cmp_pos_v[l], cmp_k_w1[l], cmp_k_w2[l],
                   cmp_v_w1[l], cmp_v_w2[l], w_br_nsa[l], w_br_dil[l], w_out[l], ln1_g[l], ln1_b[l],
                   w_router[l], b_router[l], w_up[l], b_up[l], w_down[l], b_down[l], ln2_g[l], ln2_b[l], alpha)
    return h
```

```python
import functools

import jax
import jax.numpy as jnp
from jax import lax
from jax.experimental import pallas as pl
from jax.experimental.pallas import tpu as pltpu

F32 = jnp.float32
BF16 = jnp.bfloat16

HEAD_DIM = 128
LANES = 128
ROPE_THETA = 10000.0
NSA_HEADS = 8
NSA_KV_GROUPS = 2
NSA_REP = NSA_HEADS // NSA_KV_GROUPS
CMP_LEN = 32
CMP_STRIDE = 16
SEL_LEN = 64
SEL_TOPK = 16
WIN_LEN = 512
DIL_CONFIGS = ((128, 1), (512, 4), (2048, 16))
N_DIL = len(DIL_CONFIGS)
DIL_HEADS = 4
TOP_K = 4
SWIGLU_LIMIT = 7.0
SWIGLU_ALPHA = 1.702
LN_EPS = 1e-5
NEG = -1e30
FORCE = 1e9
SCALE = HEAD_DIM ** -0.5
LOG2E = 1.4426950408889634
Q_SCALE = SCALE * LOG2E
PLAIN, ROPE, ROPE_Q = 0, 1, 2

GROUP_W = NSA_REP * HEAD_DIM
DIL_W = DIL_HEADS * HEAD_DIM
VMEM_LIMIT = 56 * 1024 * 1024


def _cparams(sem, vmem=None):
    return pltpu.CompilerParams(dimension_semantics=sem, vmem_limit_bytes=vmem)


def _masked_softmax(s, mask):
    s = jnp.where(mask, s, NEG)
    m = s.max(-1, keepdims=True)
    p = jnp.where(mask, jnp.exp2(s - m), 0.0)
    den = p.sum(-1, keepdims=True)
    safe = jnp.where(den > 0, den, 1.0)
    return p, m, safe


def _qk(q, k):
    return lax.dot_general(q, k, (((1,), (1,)), ((), ())), preferred_element_type=F32)


def _pack_pairs(xb):
    m = xb.shape[1] // 2
    lo = lax.bitcast_convert_type(xb[:, :m].astype(F32), jnp.uint32)
    hi = lax.bitcast_convert_type(xb[:, m:].astype(F32), jnp.uint32)
    return (lo >> 16) | (hi & jnp.uint32(0xFFFF0000))


def _unpack_pairs(w):
    lo = lax.bitcast_convert_type(w << 16, F32)
    hi = lax.bitcast_convert_type(w & jnp.uint32(0xFFFF0000), F32)
    return lo, hi


def _proj_kernel(flags_ref, x_ref, w_ref, b_ref, cos_ref, sin_ref, o_ref, xb_ref, *xcol_ref):
    j = pl.program_id(2)
    d, sub, tn = o_ref.shape
    n_sub = tn // LANES

    @pl.when(j == 0)
    def _():
        if d == 1:
            xb_ref[...] = x_ref[...].astype(BF16)
        else:
            xcol, = xcol_ref
            for c in range(xcol.shape[0]):
                xcol[c] = x_ref[:, c * LANES:(c + 1) * LANES]
            for c in range(xcol.shape[0]):
                for r in range(d):
                    xb_ref[r * sub:(r + 1) * sub, c * LANES:(c + 1) * LANES] = (
                        xcol[c, pl.ds(r, sub, stride=d), :].astype(BF16))

    acc = jnp.dot(xb_ref[...], w_ref[...], preferred_element_type=F32) + b_ref[...]
    for u in range(n_sub):
        a = acc[:, u * LANES:(u + 1) * LANES]
        roped = a * cos_ref[...] + pltpu.roll(a, HEAD_DIM // 2, 1) * sin_ref[...]
        flag = flags_ref[j * n_sub + u]
        mult = jnp.where(flag == ROPE_Q, Q_SCALE, 1.0).astype(F32)
        res = (jnp.where(flag == PLAIN, a, roped) * mult).astype(o_ref.dtype)
        for r in range(d):
            o_ref[r, :, u * LANES:(u + 1) * LANES] = res[r * sub:(r + 1) * sub]


def _project(x, w, b, flags, cosx, sinx, d, tm, tn):
    B, S, D = x.shape
    N = w.shape[1]
    L = S // d
    tm = min(tm, S)
    sub = tm // d

    def regroup(tab):
        return tab.reshape(S // tm, sub, d, HEAD_DIM).transpose(0, 2, 1, 3).reshape(S, HEAD_DIM)

    grid_spec = pltpu.PrefetchScalarGridSpec(
        num_scalar_prefetch=1,
        grid=(B, S // tm, N // tn),
        in_specs=[
            pl.BlockSpec((None, tm, D), lambda bb, i, j, f: (bb, i, 0)),
            pl.BlockSpec((D, tn), lambda bb, i, j, f: (0, j)),
            pl.BlockSpec((1, tn), lambda bb, i, j, f: (0, j)),
            pl.BlockSpec((tm, HEAD_DIM), lambda bb, i, j, f: (i, 0)),
            pl.BlockSpec((tm, HEAD_DIM), lambda bb, i, j, f: (i, 0)),
        ],
        out_specs=pl.BlockSpec((None, d, sub, tn), lambda bb, i, j, f: (bb, 0, i, j)),
        scratch_shapes=[pltpu.VMEM((tm, D), BF16)] + ([pltpu.VMEM((D // LANES, tm, LANES), F32)] if d > 1 else []),
    )
    return pl.pallas_call(
        _proj_kernel,
        out_shape=jax.ShapeDtypeStruct((B, d, L, N), BF16),
        grid_spec=grid_spec,
        compiler_params=_cparams(("parallel", "arbitrary", "arbitrary"), VMEM_LIMIT),
        name="proj_rope",
    )(flags, x, w, b, regroup(cosx), regroup(sinx))


def _gelu_tanh(x):
    return 0.5 * x * (1.0 + jnp.tanh(0.7978845608028654 * (x + 0.044715 * x * x * x)))


def _compress_kernel(x_ref, pos_ref, w1_ref, w2_ref, o_ref):
    half = x_ref.shape[-1]
    nc = x_ref.shape[0]
    x = x_ref[...].astype(F32)
    lo = (x + pos_ref[0:1, :]).astype(BF16)
    hi = (x + pos_ref[1:2, :]).astype(BF16)
    y_lo = jnp.dot(lo, w1_ref[0:half, :], preferred_element_type=F32)
    y_hi = jnp.dot(hi, w1_ref[half:2 * half, :], preferred_element_type=F32)
    h = y_lo + pltpu.roll(y_hi, nc - 1, 0)
    g = _gelu_tanh(h).astype(BF16)
    o_ref[...] = jnp.dot(g, w2_ref[...], preferred_element_type=F32).astype(o_ref.dtype)


def _compress(xkv, pos, w1, w2):
    _, B, G, nC, half = xkv.shape
    hid = w1.shape[-1]
    return pl.pallas_call(
        _compress_kernel,
        out_shape=jax.ShapeDtypeStruct((2, B, G, nC, HEAD_DIM), BF16),
        grid=(2, B, G),
        in_specs=[
            pl.BlockSpec((None, None, None, nC, half), lambda a, bb, g: (a, bb, g, 0, 0)),
            pl.BlockSpec((None, 2, half), lambda a, bb, g: (a, 0, 0)),
            pl.BlockSpec((None, 2 * half, hid), lambda a, bb, g: (a, 0, 0)),
            pl.BlockSpec((None, hid, HEAD_DIM), lambda a, bb, g: (a, 0, 0)),
        ],
        out_specs=pl.BlockSpec((None, None, None, nC, HEAD_DIM), lambda a, bb, g: (a, bb, g, 0, 0)),
        compiler_params=_cparams(("arbitrary", "arbitrary", "arbitrary"), VMEM_LIMIT),
        name="compress_mlp",
    )(xkv, pos, w1, w2)


def _cmp_attn_kernel(q_ref, kc_ref, vc_ref, ov_ref, o_ref, sel_ref, *, n_slc, n_sel):
    qi = pl.program_id(2)
    tq = q_ref.shape[0]
    nc = kc_ref.shape[0]
    t = qi * tq + lax.broadcasted_iota(jnp.int32, (tq, nc), 0)
    c = lax.broadcasted_iota(jnp.int32, (tq, nc), 1)
    mask = (c * CMP_STRIDE + (CMP_LEN - 1)) <= t
    kc = kc_ref[...]
    vc = vc_ref[...]
    ps = jnp.zeros((tq, nc), F32)
    for h in range(NSA_REP):
        s = _qk(q_ref[:, h * HEAD_DIM:(h + 1) * HEAD_DIM], kc)
        p, _, safe = _masked_softmax(s, mask)
        p = p / safe
        o = jnp.dot(p.astype(BF16), vc, preferred_element_type=F32)
        o_ref[:, h * HEAD_DIM:(h + 1) * HEAD_DIM] = o.astype(o_ref.dtype)
        ps = ps + p
    imp = jnp.dot(ps.astype(BF16), ov_ref[...], preferred_element_type=F32)
    tj = qi * tq + lax.broadcasted_iota(jnp.int32, (tq, LANES), 0)
    j = lax.broadcasted_iota(jnp.int32, (tq, LANES), 1)
    cur = tj // SEL_LEN
    forced = (j == 0) | (j == cur) | (j == cur - 1)
    imp = jnp.where(forced, FORCE, jnp.where(j > cur, NEG, imp))
    imp_t = imp.T[0:n_slc, :]
    jj = lax.broadcasted_iota(jnp.int32, (n_slc, tq), 0)
    rank = jnp.zeros((n_slc, tq), jnp.int32)
    for j2 in range(n_slc):
        row = imp_t[j2:j2 + 1, :]
        ahead = (row > imp_t) | ((row == imp_t) & (jj > j2))
        rank = rank + ahead.astype(jnp.int32)
    sel_t = jnp.where(rank < n_sel, 0.0, NEG)
    if n_slc < LANES:
        sel_t = jnp.concatenate([sel_t, jnp.zeros((LANES - n_slc, tq), F32)], axis=0)
    sel_ref[...] = sel_t.T.astype(sel_ref.dtype)


def _cmp_attention(main, kvc, overlap, q_blk0, tq):
    B, _, S, _ = main.shape
    G = NSA_KV_GROUPS
    nC = kvc.shape[3]
    n_slc = S // SEL_LEN
    n_sel = min(SEL_TOPK, n_slc)
    kern = functools.partial(_cmp_attn_kernel, n_slc=n_slc, n_sel=n_sel)
    return pl.pallas_call(
        kern,
        out_shape=(jax.ShapeDtypeStruct((B, S, NSA_HEADS * HEAD_DIM), BF16),
                   jax.ShapeDtypeStruct((B, G, S, LANES), BF16)),
        grid=(B, G, S // tq),
        in_specs=[
            pl.BlockSpec((None, None, tq, GROUP_W), lambda bb, g, i: (bb, 0, i, q_blk0 + g)),
            pl.BlockSpec((None, None, None, nC, HEAD_DIM), lambda bb, g, i: (0, bb, g, 0, 0)),
            pl.BlockSpec((None, None, None, nC, HEAD_DIM), lambda bb, g, i: (1, bb, g, 0, 0)),
            pl.BlockSpec((nC, LANES), lambda bb, g, i: (0, 0)),
        ],
        out_specs=(pl.BlockSpec((None, tq, GROUP_W), lambda bb, g, i: (bb, i, g)),
                   pl.BlockSpec((None, None, tq, LANES), lambda bb, g, i: (bb, g, i, 0))),
        compiler_params=_cparams(("parallel", "arbitrary", "arbitrary"), VMEM_LIMIT),
        name="cmp_attn_select",
    )(main, kvc, kvc, overlap)


def _fold_lanes(x, op):
    out = x[:, 0:LANES]
    for u in range(1, x.shape[1] // LANES):
        out = op(out, x[:, u * LANES:(u + 1) * LANES])
    return out


def _sel_attn_kernel(q_ref, k_ref, v_ref, bias_ref, et_ref, o_ref, qx_sc, s_sc, m_sc, l_sc, acc_sc):
    qi = pl.program_id(2)
    t = q_ref.shape[0]
    tc = s_sc.shape[2]
    rows = NSA_REP * t
    for h in range(NSA_REP):
        qx_sc[h * t:(h + 1) * t, 0:HEAD_DIM] = q_ref[:, h * HEAD_DIM:(h + 1) * HEAD_DIM]
        qx_sc[h * t:(h + 1) * t, HEAD_DIM:2 * HEAD_DIM] = bias_ref[...]

    def scores(c):
        start = pl.multiple_of(c * tc, tc)
        kx = jnp.concatenate([k_ref[pl.ds(start, tc), :], et_ref[pl.ds(start, tc), :]], axis=1)
        return _qk(qx_sc[...], kx)

    m_sc[...] = jnp.full(m_sc.shape, NEG, F32)

    def max_pass(c, carry):
        s = scores(c)
        s_sc[c] = s
        m_sc[...] = jnp.maximum(m_sc[...], _fold_lanes(s, jnp.maximum))
        return carry

    n_full = (qi * t) // tc
    lax.fori_loop(0, n_full, max_pass, 0)
    qpos = qi * t + lax.broadcasted_iota(jnp.int32, (rows, tc), 0) % t
    kpos = n_full * tc + lax.broadcasted_iota(jnp.int32, (rows, tc), 1)
    s = jnp.where(kpos <= qpos, scores(n_full), NEG)
    s_sc[n_full] = s
    m = jnp.maximum(m_sc[...], _fold_lanes(s, jnp.maximum)).max(-1, keepdims=True)
    m_sc[...] = jnp.broadcast_to(m, m_sc.shape)
    l_sc[...] = jnp.zeros(l_sc.shape, F32)
    acc_sc[...] = jnp.zeros(acc_sc.shape, F32)

    def exp_pass(c, carry):
        mb = m_sc[...]
        sc = s_sc[c]
        p = jnp.concatenate([jnp.exp2(sc[:, u * LANES:(u + 1) * LANES] - mb) for u in range(tc // LANES)], axis=1)
        l_sc[...] += _fold_lanes(p, jnp.add)
        v = v_ref[pl.ds(pl.multiple_of(c * tc, tc), tc), :]
        acc_sc[...] += jnp.dot(p.astype(BF16), v, preferred_element_type=F32)
        return carry

    lax.fori_loop(0, n_full + 1, exp_pass, 0)
    l = l_sc[...].sum(-1, keepdims=True)
    o = acc_sc[...] / jnp.where(l > 0, l, 1.0)
    for h in range(NSA_REP):
        o_ref[:, h * HEAD_DIM:(h + 1) * HEAD_DIM] = o[h * t:(h + 1) * t].astype(o_ref.dtype)


def _sel_attention(main, bias, onehot_t, q_blk0, k_tile0, v_tile0, t, tc):
    B, _, S, _ = main.shape
    G = NSA_KV_GROUPS
    t = min(t, S)
    tc = min(tc, S)
    rows = NSA_REP * t
    return pl.pallas_call(
        _sel_attn_kernel,
        out_shape=jax.ShapeDtypeStruct((B, S, NSA_HEADS * HEAD_DIM), BF16),
        grid=(B, G, S // t),
        in_specs=[
            pl.BlockSpec((None, None, t, GROUP_W), lambda bb, g, i: (bb, 0, i, q_blk0 + g)),
            pl.BlockSpec((None, None, S, HEAD_DIM), lambda bb, g, i: (bb, 0, 0, k_tile0 + g)),
            pl.BlockSpec((None, None, S, HEAD_DIM), lambda bb, g, i: (bb, 0, 0, v_tile0 + g)),
            pl.BlockSpec((None, None, t, LANES), lambda bb, g, i: (bb, g, i, 0)),
            pl.BlockSpec((S, LANES), lambda bb, g, i: (0, 0)),
        ],
        out_specs=pl.BlockSpec((None, t, GROUP_W), lambda bb, g, i: (bb, i, g)),
        scratch_shapes=[pltpu.VMEM((rows, 2 * HEAD_DIM), BF16), pltpu.VMEM((S // tc, rows, tc), F32),
                        pltpu.VMEM((rows, LANES), F32), pltpu.VMEM((rows, LANES), F32),
                        pltpu.VMEM((rows, HEAD_DIM), F32)],
        compiler_params=_cparams(("parallel", "arbitrary", "arbitrary"), VMEM_LIMIT),
        name="selected_attn",
    )(main, main, main, bias, onehot_t)


def _band_kernel(*refs, n_kt, max_dist, kv_heads, with_lse):
    q_ref = refs[0]
    k_refs = refs[1:1 + n_kt]
    v_refs = refs[1 + n_kt:1 + 2 * n_kt]
    o_ref = refs[1 + 2 * n_kt]
    qi = pl.program_id(2)
    tq = q_ref.shape[0]
    tk = k_refs[0].shape[0]
    nk = n_kt * tk
    qpos = qi * tq + lax.broadcasted_iota(jnp.int32, (tq, nk), 0)
    kpos = (qi - (n_kt - 1)) * tk + lax.broadcasted_iota(jnp.int32, (tq, nk), 1)
    diff = qpos - kpos
    mask = (kpos >= 0) & (diff >= 0) & (diff <= max_dist)
    k = jnp.concatenate([r[...] for r in k_refs], axis=0)
    v = jnp.concatenate([r[...] for r in v_refs], axis=0)
    n_heads = q_ref.shape[1] // HEAD_DIM
    lane = lax.broadcasted_iota(jnp.int32, (tq, LANES), 1)
    lse = jnp.zeros((tq, LANES), F32)
    for h in range(n_heads):
        hk = h if kv_heads > 1 else 0
        s = _qk(q_ref[:, h * HEAD_DIM:(h + 1) * HEAD_DIM], k[:, hk * HEAD_DIM:(hk + 1) * HEAD_DIM])
        p, m, safe = _masked_softmax(s, mask)
        o = jnp.dot(p.astype(BF16), v[:, hk * HEAD_DIM:(hk + 1) * HEAD_DIM], preferred_element_type=F32)
        o_ref[:, h * HEAD_DIM:(h + 1) * HEAD_DIM] = (o / safe).astype(o_ref.dtype)
        lse = jnp.where(lane == h, m + jnp.log2(safe), lse)
    if with_lse:
        refs[2 + 2 * n_kt][...] = lse


def _band_attention(src, grid, q_map, k_map, v_map, kv_width, out_shape, o_map, t, max_dist, with_lse):
    n_kt = -(-max_dist // t) + 1
    kern = functools.partial(_band_kernel, n_kt=n_kt, max_dist=max_dist,
                             kv_heads=kv_width // HEAD_DIM, with_lse=with_lse)

    def shifted(fn, u):
        def index_map(bb, a, i):
            return fn(bb, a, jnp.maximum(i - (n_kt - 1) + u, 0))
        return index_map

    in_specs = [pl.BlockSpec((None, None, t, GROUP_W), q_map)]
    in_specs += [pl.BlockSpec((None, None, t, kv_width), shifted(k_map, u)) for u in range(n_kt)]
    in_specs += [pl.BlockSpec((None, None, t, kv_width), shifted(v_map, u)) for u in range(n_kt)]
    o_spec = pl.BlockSpec((None, t, GROUP_W), o_map)
    if with_lse:
        lse_shape = out_shape[:-1] + (out_shape[-1] // GROUP_W * LANES,)
        out_shapes = (jax.ShapeDtypeStruct(out_shape, BF16), jax.ShapeDtypeStruct(lse_shape, F32))
        out_specs = (o_spec, pl.BlockSpec((None, t, LANES), o_map))
    else:
        out_shapes = jax.ShapeDtypeStruct(out_shape, BF16)
        out_specs = o_spec
    return pl.pallas_call(
        kern, out_shape=out_shapes, grid=grid, in_specs=in_specs, out_specs=out_specs,
        compiler_params=_cparams(("parallel", "arbitrary", "arbitrary"), VMEM_LIMIT),
        name="band_attn",
    )(*([src] * (1 + 2 * n_kt)))


def _layer_norm(z, g, b):
    mu = z.mean(-1, keepdims=True)
    zc = z - mu
    var = (zc * zc).mean(-1, keepdims=True)
    return zc * lax.rsqrt(var + LN_EPS) * g + b


def _merge_kernel(ocmp_ref, oslc_ref, owin_ref, gl_ref, ga_ref, gb_ref,
                  d0_ref, d1_ref, d2_ref, l0_ref, l1_ref, l2_ref, x_ref,
                  wa_ref, wb_ref, wo_ref, g_ref, b_ref, wr_ref, br_ref, hf_ref, hp_ref, lg_ref, *, alpha):
    tt = x_ref.shape[0]
    gates = jax.nn.sigmoid(gl_ref[...].astype(F32))
    parts = []
    for h in range(NSA_HEADS):
        sl = slice(h * HEAD_DIM, (h + 1) * HEAD_DIM)
        acc = jnp.zeros((tt, HEAD_DIM), F32)
        for br, ref in enumerate((ocmp_ref, oslc_ref, owin_ref)):
            gcol = gates[:, 3 * h + br:3 * h + br + 1]
            acc = acc + gcol * ref[:, sl].astype(F32)
        parts.append(acc.astype(BF16))
    o_nsa = jnp.concatenate(parts, axis=1)
    l0, l1, l2 = l0_ref[...], l1_ref[...], l2_ref[...]
    lm = jnp.maximum(jnp.maximum(l0, l1), l2)
    e0, e1, e2 = jnp.exp2(l0 - lm), jnp.exp2(l1 - lm), jnp.exp2(l2 - lm)
    inv = 1.0 / (e0 + e1 + e2)
    w0, w1, w2 = e0 * inv, e1 * inv, e2 * inv
    parts = []
    for h in range(DIL_HEADS):
        sl = slice(h * HEAD_DIM, (h + 1) * HEAD_DIM)
        parts.append(w0[:, h:h + 1] * d0_ref[:, sl].astype(F32) + w1[:, h:h + 1] * d1_ref[:, sl].astype(F32)
                     + w2[:, h:h + 1] * d2_ref[:, sl].astype(F32))
    o_dil = jnp.concatenate(parts, axis=1)
    y_a = jnp.dot(o_nsa, wa_ref[...], preferred_element_type=F32)
    y_b = jnp.dot(o_dil.astype(BF16), wb_ref[...], preferred_element_type=F32)
    merged = (jax.nn.sigmoid(ga_ref[...].astype(F32)) * y_a
              + jax.nn.sigmoid(gb_ref[...].astype(F32)) * y_b)
    mix = jnp.dot(merged.astype(BF16), wo_ref[...], preferred_element_type=F32)
    h = _layer_norm(alpha * x_ref[...] + mix, g_ref[...], b_ref[...])
    hf_ref[...] = h
    hb = h.astype(BF16)
    hp_ref[...] = _pack_pairs(hb)
    lg_ref[...] = jnp.dot(hb, wr_ref[...], preferred_element_type=F32) + br_ref[...]


def _merge(o_cmp, o_slc, o_win, main2d, gl_tile, ga_blk, gb_blk, dil_o, dil_lse, x2d,
           w_a, w_b, w_o, ln_g, ln_b, w_r, b_r, alpha, tt):
    T, D = x2d.shape
    nsa_w = NSA_HEADS * HEAD_DIM
    row = lambda i: (i, 0)
    const = lambda i: (0, 0)
    in_specs = [
        pl.BlockSpec((tt, nsa_w), row), pl.BlockSpec((tt, nsa_w), row), pl.BlockSpec((tt, nsa_w), row),
        pl.BlockSpec((tt, LANES), lambda i: (i, gl_tile)),
        pl.BlockSpec((tt, D), lambda i: (i, ga_blk)),
        pl.BlockSpec((tt, D), lambda i: (i, gb_blk)),
    ]
    in_specs += [pl.BlockSpec((tt, DIL_W), row)] * 3 + [pl.BlockSpec((tt, LANES), row)] * 3
    in_specs += [
        pl.BlockSpec((tt, D), row),
        pl.BlockSpec((nsa_w, D), const), pl.BlockSpec((DIL_W, D), const), pl.BlockSpec((D, D), const),
        pl.BlockSpec((1, D), const), pl.BlockSpec((1, D), const),
        pl.BlockSpec((D, LANES), const), pl.BlockSpec((1, LANES), const),
    ]
    return pl.pallas_call(
        functools.partial(_merge_kernel, alpha=alpha),
        out_shape=(jax.ShapeDtypeStruct((T, D), F32), jax.ShapeDtypeStruct((T, D // 2), jnp.uint32),
                   jax.ShapeDtypeStruct((T, LANES), F32)),
        grid=(T // tt,),
        in_specs=in_specs,
        out_specs=(pl.BlockSpec((tt, D), row), pl.BlockSpec((tt, D // 2), row), pl.BlockSpec((tt, LANES), row)),
        compiler_params=_cparams(("parallel",), VMEM_LIMIT),
        name="merge_ln1",
    )(o_cmp, o_slc, o_win, main2d, main2d, main2d, *dil_o, *dil_lse, x2d, w_a, w_b, w_o, ln_g, ln_b, w_r, b_r)


def _router_kernel(lg_ref, tri_ref, meta_ref, cnt_ref, carry_sc):
    i = pl.program_id(0)
    tt = lg_ref.shape[0]

    @pl.when(i == 0)
    def _():
        carry_sc[...] = jnp.zeros(carry_sc.shape, F32)

    logits = lg_ref[...]
    lane = lax.broadcasted_iota(jnp.int32, (tt, LANES), 1)
    v = logits
    onehot = jnp.zeros((tt, LANES), F32)
    vals, idxs = [], []
    for _ in range(TOP_K):
        m = v.max(-1, keepdims=True)
        idx = jnp.where(v == m, lane, LANES).min(-1, keepdims=True)
        hit = lane == idx
        vals.append(m)
        idxs.append(idx)
        onehot = onehot + hit.astype(F32)
        v = jnp.where(hit, -jnp.inf, v)
    exps = [jnp.exp(vk - vals[0]) for vk in vals]
    den = exps[0] + exps[1] + exps[2] + exps[3]
    before = jnp.dot(tri_ref[...], onehot.astype(BF16), preferred_element_type=F32) + carry_sc[0:1, :]
    meta = jnp.zeros((tt, LANES), F32)
    for k in range(TOP_K):
        rank = jnp.where(lane == idxs[k], before, 0.0).sum(-1, keepdims=True)
        meta = jnp.where(lane == k, idxs[k].astype(F32), meta)
        meta = jnp.where(lane == TOP_K + k, exps[k] / den, meta)
        meta = jnp.where(lane == 2 * TOP_K + k, rank, meta)
    meta_ref[...] = meta
    carry_sc[...] = carry_sc[...] + jnp.broadcast_to(onehot.sum(0, keepdims=True), carry_sc.shape)
    cnt_ref[...] = carry_sc[...]


def _router(logits, tt):
    T = logits.shape[0]
    tri = (jnp.arange(tt)[:, None] > jnp.arange(tt)[None, :]).astype(BF16)
    return pl.pallas_call(
        _router_kernel,
        out_shape=(jax.ShapeDtypeStruct((T, LANES), F32), jax.ShapeDtypeStruct((8, LANES), F32)),
        grid=(T // tt,),
        in_specs=[
            pl.BlockSpec((tt, LANES), lambda i: (i, 0)),
            pl.BlockSpec((tt, tt), lambda i: (0, 0)),
        ],
        out_specs=(pl.BlockSpec((tt, LANES), lambda i: (i, 0)), pl.BlockSpec((8, LANES), lambda i: (0, 0))),
        scratch_shapes=[pltpu.VMEM((8, LANES), F32)],
        compiler_params=_cparams(("arbitrary",), VMEM_LIMIT),
        name="router_top4",
    )(logits, tri)


def _dispatch_kernel(dest_ref, h_ref, xs_in_hbm, xs_hbm, sem):
    del xs_in_hbm
    n = h_ref.shape[0]

    def row_copy(t, d):
        return pltpu.make_async_copy(h_ref.at[pl.ds(t, 1), :], xs_hbm.at[pl.ds(d, 1), :], sem)

    def issue(t, c):
        for k in range(TOP_K):
            row_copy(t, dest_ref[t * TOP_K + k]).start(priority=k % 2)
        return c

    lax.fori_loop(0, n, issue, 0)

    def drain(t, c):
        for k in range(TOP_K):
            row_copy(0, 0).wait()
        return c

    lax.fori_loop(0, n, drain, 0)


def _dispatch(hp, dest_flat, n_rows, tt):
    T, width = hp.shape
    xs0 = jnp.zeros((n_rows, width), hp.dtype)
    return pl.pallas_call(
        _dispatch_kernel,
        out_shape=jax.ShapeDtypeStruct(xs0.shape, xs0.dtype),
        grid=(T // tt,),
        in_specs=[
            pl.BlockSpec((tt * TOP_K,), lambda i: (i,), memory_space=pltpu.SMEM),
            pl.BlockSpec((tt, width), lambda i: (i, 0)),
            pl.BlockSpec(memory_space=pl.ANY),
        ],
        out_specs=pl.BlockSpec(memory_space=pl.ANY),
        scratch_shapes=[pltpu.SemaphoreType.DMA(())],
        input_output_aliases={2: 0},
        compiler_params=_cparams(("arbitrary",)),
        name="moe_dispatch",
    )(dest_flat, hp, xs0)


def _expert_kernel(be_ref, nu_ref, x_ref, wg_ref, wl_ref, bg_ref, bl_ref, wd_ref, bd_ref, y_ref, xb_sc, *, th):
    i = pl.program_id(0)
    half = x_ref.shape[1]
    dh = wg_ref.shape[1]

    @pl.when(i < nu_ref[0])
    def _():
        lo, hi = _unpack_pairs(x_ref[...])
        xb_sc[:, :half] = lo.astype(BF16)
        xb_sc[:, half:] = hi.astype(BF16)
        x = xb_sc[...]
        y = bd_ref[...]
        for c in range(dh // th):
            sl = slice(c * th, (c + 1) * th)
            glu = jnp.dot(x, wg_ref[:, sl], preferred_element_type=F32) + bg_ref[:, sl]
            lin = jnp.dot(x, wl_ref[:, sl], preferred_element_type=F32) + bl_ref[:, sl]
            glu = jnp.minimum(glu, SWIGLU_LIMIT)
            lin = jnp.clip(lin, -SWIGLU_LIMIT, SWIGLU_LIMIT)
            act = glu * jax.nn.sigmoid(SWIGLU_ALPHA * glu) * (lin + 1.0)
            y = y + jnp.dot(act.astype(BF16), wd_ref[sl, :], preferred_element_type=F32)
        y_ref[...] = _pack_pairs(y.astype(BF16))


def _experts(xs, blk_expert, n_used, w_up, b_up, w_down, b_down, tm, th):
    P = xs.shape[0]
    E, D, two_dh = w_up.shape
    dh = two_dh // 2
    th = min(th, dh)
    once = pl.Buffered(1)

    def row(i, be, nu):
        return (jnp.minimum(i, nu[0] - 1), 0)

    grid_spec = pltpu.PrefetchScalarGridSpec(
        num_scalar_prefetch=2,
        grid=(P // tm,),
        in_specs=[
            pl.BlockSpec((tm, D // 2), row),
            pl.BlockSpec((None, D, dh), lambda i, be, nu: (be[i], 0, 0), pipeline_mode=once),
            pl.BlockSpec((None, D, dh), lambda i, be, nu: (be[i], 0, 1), pipeline_mode=once),
            pl.BlockSpec((None, 1, dh), lambda i, be, nu: (be[i], 0, 0)),
            pl.BlockSpec((None, 1, dh), lambda i, be, nu: (be[i], 0, 1)),
            pl.BlockSpec((None, dh, D), lambda i, be, nu: (be[i], 0, 0), pipeline_mode=once),
            pl.BlockSpec((None, 1, D), lambda i, be, nu: (be[i], 0, 0)),
        ],
        out_specs=pl.BlockSpec((tm, D // 2), row),
        scratch_shapes=[pltpu.VMEM((tm, D), BF16)],
    )
    return pl.pallas_call(
        functools.partial(_expert_kernel, th=th),
        out_shape=jax.ShapeDtypeStruct((P, D // 2), jnp.uint32),
        grid_spec=grid_spec,
        compiler_params=_cparams(("arbitrary",), VMEM_LIMIT),
        name="moe_experts",
    )(blk_expert, n_used, xs, w_up, w_up, b_up, b_up, w_down, b_down)


def _combine_kernel(dest_ref, meta_ref, h_ref, g_ref, b_ref, y_hbm, o_ref, buf, sem, *, alpha):
    n = o_ref.shape[0]

    def row_copy(d, r):
        return pltpu.make_async_copy(y_hbm.at[pl.ds(d, 1), :], buf.at[pl.ds(r, 1), :], sem)

    def issue(t, c):
        for k in range(TOP_K):
            row_copy(dest_ref[t * TOP_K + k], k * n + t).start(priority=k % 2)
        return c

    lax.fori_loop(0, n, issue, 0)

    def drain(t, c):
        for k in range(TOP_K):
            row_copy(0, 0).wait()
        return c

    lax.fori_loop(0, n, drain, 0)

    ffn_lo = ffn_hi = None
    for k in range(TOP_K):
        lo, hi = _unpack_pairs(buf[k * n:(k + 1) * n, :])
        gate = meta_ref[:, TOP_K + k:TOP_K + k + 1]
        ffn_lo = gate * lo if k == 0 else ffn_lo + gate * lo
        ffn_hi = gate * hi if k == 0 else ffn_hi + gate * hi
    ffn = jnp.concatenate([ffn_lo, ffn_hi], axis=1)
    o_ref[...] = _layer_norm(alpha * h_ref[...] + ffn, g_ref[...], b_ref[...])


def _combine_ln(y, dest_flat, meta, h, g, b, alpha, tt):
    T, D = h.shape
    return pl.pallas_call(
        functools.partial(_combine_kernel, alpha=alpha),
        out_shape=jax.ShapeDtypeStruct((T, D), F32),
        grid=(T // tt,),
        in_specs=[
            pl.BlockSpec((tt * TOP_K,), lambda i: (i,), memory_space=pltpu.SMEM),
            pl.BlockSpec((tt, LANES), lambda i: (i, 0)),
            pl.BlockSpec((tt, D), lambda i: (i, 0)),
            pl.BlockSpec((1, D), lambda i: (0, 0)),
            pl.BlockSpec((1, D), lambda i: (0, 0)),
            pl.BlockSpec(memory_space=pl.ANY),
        ],
        out_specs=pl.BlockSpec((tt, D), lambda i: (i, 0)),
        scratch_shapes=[pltpu.VMEM((TOP_K * tt, D // 2), y.dtype), pltpu.SemaphoreType.DMA(())],
        compiler_params=_cparams(("arbitrary",), VMEM_LIMIT),
        name="moe_combine_ln2",
    )(dest_flat, meta, h, g, b, y)


def _layer(x, w_in, b_in, pos_k, pos_v, ck_w1, ck_w2, cv_w1, cv_w2, w_br_nsa, w_br_dil, w_out,
           ln1_g, ln1_b, w_router, b_router, w_up, b_up, w_down, b_down, ln2_g, ln2_b, alpha):
    B, S, D = x.shape
    T = B * S
    nd = D // LANES
    G = NSA_KV_GROUPS
    kvw = G * HEAD_DIM
    n_exp = w_router.shape[1]

    o_q = 0
    o_kv = NSA_HEADS * HEAD_DIM
    o_gl = o_kv + 6 * kvw
    o_dil = o_gl + 3 * NSA_HEADS
    o_ga = o_dil + 3 * N_DIL * DIL_W
    o_gb = o_ga + D

    def wcols(a, n):
        return w_in[:, a:a + n], b_in[a:a + n]

    def kv(i):
        return wcols(o_kv + i * kvw, kvw)

    tn = 1024
    gl_w, gl_b = wcols(o_gl, 3 * NSA_HEADS)
    pieces = [wcols(o_ga, D), wcols(o_gb, D), wcols(o_q, NSA_HEADS * HEAD_DIM),
              kv(0), kv(2), kv(4), kv(1), kv(3), kv(5), (gl_w, gl_b)]
    used = sum(p[0].shape[1] for p in pieces)
    n_main = -(-used // tn) * tn
    pieces.append((jnp.zeros((D, n_main - used), F32), jnp.zeros((n_main - used,), F32)))
    w_main = jnp.concatenate([p[0] for p in pieces], axis=1).astype(BF16)
    b_main = jnp.concatenate([p[1] for p in pieces])[None, :]
    t_q = 2 * nd
    t_kc, t_ks, t_kw = t_q + 8, t_q + 10, t_q + 12
    t_vc, t_vs, t_vw = t_q + 14, t_q + 16, t_q + 18
    t_gl = t_q + 20
    tile_id = jnp.arange(n_main // LANES)
    flags_main = jnp.where((tile_id >= t_q) & (tile_id < t_kc), ROPE_Q,
                           jnp.where((tile_id >= t_kc) & (tile_id < t_vc), ROPE, PLAIN)).astype(jnp.int32)
    q_blk0 = t_q // NSA_REP

    pos = jnp.arange(S, dtype=F32)
    inv = ROPE_THETA ** (-jnp.arange(0, HEAD_DIM, 2, dtype=F32) / HEAD_DIM)
    ang = pos[:, None] * inv[None, :]
    cosx = jnp.concatenate([jnp.cos(ang), jnp.cos(ang)], axis=-1)
    sinx = jnp.concatenate([-jnp.sin(ang), jnp.sin(ang)], axis=-1)

    main = _project(x, w_main, b_main, flags_main, cosx, sinx, 1, 1024, tn)

    nC = S // CMP_STRIDE
    half = CMP_STRIDE * HEAD_DIM

    def cmp_in(tile):
        a = main[:, 0, :, tile * LANES:(tile + G) * LANES].reshape(B, S, G, HEAD_DIM)
        return a.transpose(0, 2, 1, 3).reshape(B, G, nC, half)

    xkv = jnp.stack([cmp_in(t_kc), cmp_in(t_vc)])
    pos_kv = jnp.stack([pos_k.reshape(2, half), pos_v.reshape(2, half)])
    w1 = jnp.stack([ck_w1, cv_w1]).astype(BF16)
    w2 = jnp.stack([ck_w2, cv_w2]).astype(BF16)
    kvc = _compress(xkv, pos_kv, w1, w2)

    n_slc = S // SEL_LEN
    assert n_slc <= LANES
    c_start = jnp.arange(nC) * CMP_STRIDE
    jb = jnp.arange(LANES)
    overlap = ((c_start[:, None] < (jb[None, :] + 1) * SEL_LEN) & (c_start[:, None] + CMP_LEN > jb[None, :] * SEL_LEN)
               & (jb[None, :] < n_slc) & (c_start[:, None] + CMP_LEN <= S)).astype(BF16)
    tq = min(256, S)
    o_cmp, sel = _cmp_attention(main, kvc, overlap, q_blk0, tq)

    onehot_t = (jnp.arange(S)[:, None] // SEL_LEN == jnp.arange(LANES)[None, :]).astype(BF16)
    o_slc = _sel_attention(main, sel, onehot_t, q_blk0, t_ks, t_vs, 256, 512)

    tw = min(256, S)
    o_win = _band_attention(
        main, (B, G, S // tw),
        lambda bb, g, i: (bb, 0, i, q_blk0 + g),
        lambda bb, g, i: (bb, 0, i, t_kw + g),
        lambda bb, g, i: (bb, 0, i, t_vw + g),
        HEAD_DIM, (B, S, NSA_HEADS * HEAD_DIM), lambda bb, g, i: (bb, i, g), tw, WIN_LEN - 1, False)

    flags_dil = jnp.array([ROPE_Q] * DIL_HEADS + [ROPE] * DIL_HEADS + [PLAIN] * DIL_HEADS, jnp.int32)
    dil_o, dil_lse = [], []
    for gi, (w, d) in enumerate(DIL_CONFIGS):
        wd, bd = wcols(o_dil + gi * 3 * DIL_W, 3 * DIL_W)
        sub = _project(x, wd.astype(BF16), bd[None, :], flags_dil, cosx, sinx, d, 512, 3 * DIL_W)
        L = S // d
        td = min(128, L)
        o_g, lse_g = _band_attention(
            sub, (B, d, L // td),
            lambda bb, r, i: (bb, r, i, 0),
            lambda bb, r, i: (bb, r, i, 1),
            lambda bb, r, i: (bb, r, i, 2),
            DIL_W, (B, L, d * DIL_W), lambda bb, r, i: (bb, i, r), td, w // d, True)
        dil_o.append(o_g.reshape(T, DIL_W))
        dil_lse.append(lse_g.reshape(T, LANES))

    w_r = jnp.concatenate([w_router, jnp.zeros((D, LANES - n_exp), F32)], axis=1).astype(BF16)
    b_r = jnp.concatenate([b_router, jnp.full((LANES - n_exp,), NEG, F32)])[None, :]
    h_f, h_p, logits = _merge(
        o_cmp.reshape(T, -1), o_slc.reshape(T, -1), o_win.reshape(T, -1), main.reshape(T, n_main),
        t_gl, 0, 1, dil_o, dil_lse, x.reshape(T, D),
        w_br_nsa.astype(BF16), w_br_dil.astype(BF16), w_out.astype(BF16),
        ln1_g[None, :], ln1_b[None, :], w_r, b_r, alpha, min(256, T))

    meta, cnt = _router(logits, min(512, T))
    top_idx = meta[:, 0:TOP_K].astype(jnp.int32)
    gates = meta[:, TOP_K:2 * TOP_K]
    rank = meta[:, 2 * TOP_K:3 * TOP_K].astype(jnp.int32)

    tm = 512 if T * TOP_K >= 512 * n_exp else 128
    counts = cnt[0, :n_exp].astype(jnp.int32)
    padded = (counts + tm - 1) // tm * tm
    pad_end = jnp.cumsum(padded)
    pad_start = pad_end - padded
    dest = (pad_start[top_idx] + rank).reshape(T * TOP_K)
    n_rows = T * TOP_K + n_exp * tm
    n_blk = n_rows // tm
    blk_start = jnp.arange(n_blk, dtype=jnp.int32) * tm
    blk_expert = jnp.minimum((pad_end[None, :] <= blk_start[:, None]).sum(-1), n_exp - 1).astype(jnp.int32)
    n_used = (pad_end[-1:] // tm).astype(jnp.int32)

    xs = _dispatch(h_p, dest, n_rows, min(512, T))
    y = _experts(xs, blk_expert, n_used,
                 w_up.astype(BF16), b_up[:, None, :], w_down.astype(BF16), b_down[:, None, :], tm, 512)
    out = _combine_ln(y, dest, meta, h_f, ln2_g[None, :], ln2_b[None, :], alpha, min(256, T))
    return out.reshape(B, S, D)


def kernel(x, w_in, b_in, cmp_pos_k, cmp_pos_v, cmp_k_w1, cmp_k_w2, cmp_v_w1, cmp_v_w2, w_br_nsa, w_br_dil,
           w_out, ln1_g, ln1_b, w_router, b_router, w_up, b_up, w_down, b_down, ln2_g, ln2_b):
    depth = w_in.shape[0]
    alpha = (2.0 * depth) ** 0.25
    h = x
    for l in range(depth):
        h = _layer(h, w_in[l], b_in[l], cmp_pos_k[l], cmp_pos_v[l], cmp_k_w1[l], cmp_k_w2[l],
                   cmp_v_w1[l], cmp_v_w2[l], w_br_nsa[l], w_br_dil[l], w_out[l], ln1_g[l], ln1_b[l],
                   w_router[l], b_router[l], w_up[l], b_up[l], w_down[l], b_down[l], ln2_g[l], ln2_b[l], alpha)
    return h
```

```python
import functools

import jax
import jax.numpy as jnp
from jax import lax
from jax.experimental import pallas as pl
from jax.experimental.pallas import tpu as pltpu

F32 = jnp.float32
BF16 = jnp.bfloat16

HEAD_DIM = 128
LANES = 128
ROPE_THETA = 10000.0
NSA_HEADS = 8
NSA_KV_GROUPS = 2
NSA_REP = NSA_HEADS // NSA_KV_GROUPS
CMP_LEN = 32
CMP_STRIDE = 16
SEL_LEN = 64
SEL_TOPK = 16
WIN_LEN = 512
DIL_CONFIGS = ((128, 1), (512, 4), (2048, 16))
N_DIL = len(DIL_CONFIGS)
DIL_HEADS = 4
TOP_K = 4
SWIGLU_LIMIT = 7.0
SWIGLU_ALPHA = 1.702
LN_EPS = 1e-5
NEG = -1e30
FORCE = 1e9
SCALE = HEAD_DIM ** -0.5
LOG2E = 1.4426950408889634
Q_SCALE = SCALE * LOG2E
PLAIN, ROPE, ROPE_Q = 0, 1, 2

GROUP_W = NSA_REP * HEAD_DIM
DIL_W = DIL_HEADS * HEAD_DIM
VMEM_LIMIT = 56 * 1024 * 1024


def _cparams(sem, vmem=None):
    return pltpu.CompilerParams(dimension_semantics=sem, vmem_limit_bytes=vmem)


def _masked_softmax(s, mask):
    s = jnp.where(mask, s, NEG)
    m = s.max(-1, keepdims=True)
    p = jnp.where(mask, jnp.exp2(s - m), 0.0)
    den = p.sum(-1, keepdims=True)
    safe = jnp.where(den > 0, den, 1.0)
    return p, m, safe


def _qk(q, k):
    return lax.dot_general(q, k, (((1,), (1,)), ((), ())), preferred_element_type=F32)


def _pack_pairs(xb):
    m = xb.shape[1] // 2
    lo = lax.bitcast_convert_type(xb[:, :m].astype(F32), jnp.uint32)
    hi = lax.bitcast_convert_type(xb[:, m:].astype(F32), jnp.uint32)
    return (lo >> 16) | (hi & jnp.uint32(0xFFFF0000))


def _unpack_pairs(w):
    lo = lax.bitcast_convert_type(w << 16, F32)
    hi = lax.bitcast_convert_type(w & jnp.uint32(0xFFFF0000), F32)
    return lo, hi


def _proj_kernel(flags_ref, x_ref, w_ref, b_ref, cos_ref, sin_ref, o_ref, xb_ref, *xcol_ref):
    j = pl.program_id(2)
    d, sub, tn = o_ref.shape
    n_sub = tn // LANES

    @pl.when(j == 0)
    def _():
        if d == 1:
            xb_ref[...] = x_ref[...].astype(BF16)
        else:
            xcol, = xcol_ref
            for c in range(xcol.shape[0]):
                xcol[c] = x_ref[:, c * LANES:(c + 1) * LANES]
            for c in range(xcol.shape[0]):
                for r in range(d):
                    xb_ref[r * sub:(r + 1) * sub, c * LANES:(c + 1) * LANES] = (
                        xcol[c, pl.ds(r, sub, stride=d), :].astype(BF16))

    acc = jnp.dot(xb_ref[...], w_ref[...], preferred_element_type=F32) + b_ref[...]
    for u in range(n_sub):
        a = acc[:, u * LANES:(u + 1) * LANES]
        roped = a * cos_ref[...] + pltpu.roll(a, HEAD_DIM // 2, 1) * sin_ref[...]
        flag = flags_ref[j * n_sub + u]
        mult = jnp.where(flag == ROPE_Q, Q_SCALE, 1.0).astype(F32)
        res = (jnp.where(flag == PLAIN, a, roped) * mult).astype(o_ref.dtype)
        for r in range(d):
            o_ref[r, :, u * LANES:(u + 1) * LANES] = res[r * sub:(r + 1) * sub]


def _project(x, w, b, flags, cosx, sinx, d, tm, tn):
    B, S, D = x.shape
    N = w.shape[1]
    L = S // d
    tm = min(tm, S)
    sub = tm // d

    def regroup(tab):
        return tab.reshape(S // tm, sub, d, HEAD_DIM).transpose(0, 2, 1, 3).reshape(S, HEAD_DIM)

    grid_spec = pltpu.PrefetchScalarGridSpec(
        num_scalar_prefetch=1,
        grid=(B, S // tm, N // tn),
        in_specs=[
            pl.BlockSpec((None, tm, D), lambda bb, i, j, f: (bb, i, 0)),
            pl.BlockSpec((D, tn), lambda bb, i, j, f: (0, j)),
            pl.BlockSpec((1, tn), lambda bb, i, j, f: (0, j)),
            pl.BlockSpec((tm, HEAD_DIM), lambda bb, i, j, f: (i, 0)),
            pl.BlockSpec((tm, HEAD_DIM), lambda bb, i, j, f: (i, 0)),
        ],
        out_specs=pl.BlockSpec((None, d, sub, tn), lambda bb, i, j, f: (bb, 0, i, j)),
        scratch_shapes=[pltpu.VMEM((tm, D), BF16)] + ([pltpu.VMEM((D // LANES, tm, LANES), F32)] if d > 1 else []),
    )
    return pl.pallas_call(
        _proj_kernel,
        out_shape=jax.ShapeDtypeStruct((B, d, L, N), BF16),
        grid_spec=grid_spec,
        compiler_params=_cparams(("parallel", "arbitrary", "arbitrary"), VMEM_LIMIT),
        name="proj_rope",
    )(flags, x, w, b, regroup(cosx), regroup(sinx))


def _gelu_tanh(x):
    return 0.5 * x * (1.0 + jnp.tanh(0.7978845608028654 * (x + 0.044715 * x * x * x)))


def _compress_kernel(x_ref, pos_ref, w1_ref, w2_ref, o_ref):
    half = x_ref.shape[-1]
    nc = x_ref.shape[0]
    x = x_ref[...].astype(F32)
    lo = (x + pos_ref[0:1, :]).astype(BF16)
    hi = (x + pos_ref[1:2, :]).astype(BF16)
    y_lo = jnp.dot(lo, w1_ref[0:half, :], preferred_element_type=F32)
    y_hi = jnp.dot(hi, w1_ref[half:2 * half, :], preferred_element_type=F32)
    h = y_lo + pltpu.roll(y_hi, nc - 1, 0)
    g = _gelu_tanh(h).astype(BF16)
    o_ref[...] = jnp.dot(g, w2_ref[...], preferred_element_type=F32).astype(o_ref.dtype)


def _compress(xkv, pos, w1, w2):
    _, B, G, nC, half = xkv.shape
    hid = w1.shape[-1]
    return pl.pallas_call(
        _compress_kernel,
        out_shape=jax.ShapeDtypeStruct((2, B, G, nC, HEAD_DIM), BF16),
        grid=(2, B, G),
        in_specs=[
            pl.BlockSpec((None, None, None, nC, half), lambda a, bb, g: (a, bb, g, 0, 0)),
            pl.BlockSpec((None, 2, half), lambda a, bb, g: (a, 0, 0)),
            pl.BlockSpec((None, 2 * half, hid), lambda a, bb, g: (a, 0, 0)),
            pl.BlockSpec((None, hid, HEAD_DIM), lambda a, bb, g: (a, 0, 0)),
        ],
        out_specs=pl.BlockSpec((None, None, None, nC, HEAD_DIM), lambda a, bb, g: (a, bb, g, 0, 0)),
        compiler_params=_cparams(("arbitrary", "arbitrary", "arbitrary"), VMEM_LIMIT),
        name="compress_mlp",
    )(xkv, pos, w1, w2)


def _cmp_attn_kernel(q_ref, kc_ref, vc_ref, ov_ref, o_ref, sel_ref, *, n_slc, n_sel):
    qi = pl.program_id(2)
    tq = q_ref.shape[0]
    nc = kc_ref.shape[0]
    t = qi * tq + lax.broadcasted_iota(jnp.int32, (tq, nc), 0)
    c = lax.broadcasted_iota(jnp.int32, (tq, nc), 1)
    mask = (c * CMP_STRIDE + (CMP_LEN - 1)) <= t
    kc = kc_ref[...]
    vc = vc_ref[...]
    ps = jnp.zeros((tq, nc), F32)
    for h in range(NSA_REP):
        s = _qk(q_ref[:, h * HEAD_DIM:(h + 1) * HEAD_DIM], kc)
        p, _, safe = _masked_softmax(s, mask)
        p = p / safe
        o = jnp.dot(p.astype(BF16), vc, preferred_element_type=F32)
        o_ref[:, h * HEAD_DIM:(h + 1) * HEAD_DIM] = o.astype(o_ref.dtype)
        ps = ps + p
    imp = jnp.dot(ps.astype(BF16), ov_ref[...], preferred_element_type=F32)
    tj = qi * tq + lax.broadcasted_iota(jnp.int32, (tq, LANES), 0)
    j = lax.broadcasted_iota(jnp.int32, (tq, LANES), 1)
    cur = tj // SEL_LEN
    forced = (j == 0) | (j == cur) | (j == cur - 1)
    imp = jnp.where(forced, FORCE, jnp.where(j > cur, NEG, imp))
    imp_t = imp.T[0:n_slc, :]
    jj = lax.broadcasted_iota(jnp.int32, (n_slc, tq), 0)
    rank = jnp.zeros((n_slc, tq), jnp.int32)
    for j2 in range(n_slc):
        row = imp_t[j2:j2 + 1, :]
        ahead = (row > imp_t) | ((row == imp_t) & (jj > j2))
        rank = rank + ahead.astype(jnp.int32)
    sel_t = jnp.where(rank < n_sel, 0.0, NEG)
    if n_slc < LANES:
        sel_t = jnp.concatenate([sel_t, jnp.zeros((LANES - n_slc, tq), F32)], axis=0)
    sel_ref[...] = sel_t.T.astype(sel_ref.dtype)


def _cmp_attention(main, kvc, overlap, q_blk0, tq):
    B, _, S, _ = main.shape
    G = NSA_KV_GROUPS
    nC = kvc.shape[3]
    n_slc = S // SEL_LEN
    n_sel = min(SEL_TOPK, n_slc)
    kern = functools.partial(_cmp_attn_kernel, n_slc=n_slc, n_sel=n_sel)
    return pl.pallas_call(
        kern,
        out_shape=(jax.ShapeDtypeStruct((B, S, NSA_HEADS * HEAD_DIM), BF16),
                   jax.ShapeDtypeStruct((B, G, S, LANES), BF16)),
        grid=(B, G, S // tq),
        in_specs=[
            pl.BlockSpec((None, None, tq, GROUP_W), lambda bb, g, i: (bb, 0, i, q_blk0 + g)),
            pl.BlockSpec((None, None, None, nC, HEAD_DIM), lambda bb, g, i: (0, bb, g, 0, 0)),
            pl.BlockSpec((None, None, None, nC, HEAD_DIM), lambda bb, g, i: (1, bb, g, 0, 0)),
            pl.BlockSpec((nC, LANES), lambda bb, g, i: (0, 0)),
        ],
        out_specs=(pl.BlockSpec((None, tq, GROUP_W), lambda bb, g, i: (bb, i, g)),
                   pl.BlockSpec((None, None, tq, LANES), lambda bb, g, i: (bb, g, i, 0))),
        compiler_params=_cparams(("parallel", "arbitrary", "arbitrary"), VMEM_LIMIT),
        name="cmp_attn_select",
    )(main, kvc, kvc, overlap)


def _fold_lanes(x, op):
    out = x[:, 0:LANES]
    for u in range(1, x.shape[1] // LANES):
        out = op(out, x[:, u * LANES:(u + 1) * LANES])
    return out


def _sel_attn_kernel(q_ref, k_ref, v_ref, bias_ref, et_ref, o_ref, qx_sc, s_sc, m_sc, l_sc, acc_sc):
    qi = pl.program_id(2)
    t = q_ref.shape[0]
    tc = s_sc.shape[2]
    rows = NSA_REP * t
    for h in range(NSA_REP):
        qx_sc[h * t:(h + 1) * t, 0:HEAD_DIM] = q_ref[:, h * HEAD_DIM:(h + 1) * HEAD_DIM]
        qx_sc[h * t:(h + 1) * t, HEAD_DIM:2 * HEAD_DIM] = bias_ref[...]

    def scores(c):
        start = pl.multiple_of(c * tc, tc)
        kx = jnp.concatenate([k_ref[pl.ds(start, tc), :], et_ref[pl.ds(start, tc), :]], axis=1)
        return _qk(qx_sc[...], kx)

    m_sc[...] = jnp.full(m_sc.shape, NEG, F32)

    def max_pass(c, carry):
        s = scores(c)
        s_sc[c] = s
        m_sc[...] = jnp.maximum(m_sc[...], _fold_lanes(s, jnp.maximum))
        return carry

    n_full = (qi * t) // tc
    lax.fori_loop(0, n_full, max_pass, 0)
    qpos = qi * t + lax.broadcasted_iota(jnp.int32, (rows, tc), 0) % t
    kpos = n_full * tc + lax.broadcasted_iota(jnp.int32, (rows, tc), 1)
    s = jnp.where(kpos <= qpos, scores(n_full), NEG)
    s_sc[n_full] = s
    m = jnp.maximum(m_sc[...], _fold_lanes(s, jnp.maximum)).max(-1, keepdims=True)
    m_sc[...] = jnp.broadcast_to(m, m_sc.shape)
    l_sc[...] = jnp.zeros(l_sc.shape, F32)
    acc_sc[...] = jnp.zeros(acc_sc.shape, F32)

    def exp_pass(c, carry):
        mb = m_sc[...]
        sc = s_sc[c]
        p = jnp.concatenate([jnp.exp2(sc[:, u * LANES:(u + 1) * LANES] - mb) for u in range(tc // LANES)], axis=1)
        l_sc[...] += _fold_lanes(p, jnp.add)
        v = v_ref[pl.ds(pl.multiple_of(c * tc, tc), tc), :]
        acc_sc[...] += jnp.dot(p.astype(BF16), v, preferred_element_type=F32)
        return carry

    lax.fori_loop(0, n_full + 1, exp_pass, 0)
    l = l_sc[...].sum(-1, keepdims=True)
    o = acc_sc[...] / jnp.where(l > 0, l, 1.0)
    for h in range(NSA_REP):
        o_ref[:, h * HEAD_DIM:(h + 1) * HEAD_DIM] = o[h * t:(h + 1) * t].astype(o_ref.dtype)


def _sel_attention(main, bias, onehot_t, q_blk0, k_tile0, v_tile0, t, tc):
    B, _, S, _ = main.shape
    G = NSA_KV_GROUPS
    t = min(t, S)
    tc = min(tc, S)
    rows = NSA_REP * t
    return pl.pallas_call(
        _sel_attn_kernel,
        out_shape=jax.ShapeDtypeStruct((B, S, NSA_HEADS * HEAD_DIM), BF16),
        grid=(B, G, S // t),
        in_specs=[
            pl.BlockSpec((None, None, t, GROUP_W), lambda bb, g, i: (bb, 0, i, q_blk0 + g)),
            pl.BlockSpec((None, None, S, HEAD_DIM), lambda bb, g, i: (bb, 0, 0, k_tile0 + g)),
            pl.BlockSpec((None, None, S, HEAD_DIM), lambda bb, g, i: (bb, 0, 0, v_tile0 + g)),
            pl.BlockSpec((None, None, t, LANES), lambda bb, g, i: (bb, g, i, 0)),
            pl.BlockSpec((S, LANES), lambda bb, g, i: (0, 0)),
        ],
        out_specs=pl.BlockSpec((None, t, GROUP_W), lambda bb, g, i: (bb, i, g)),
        scratch_shapes=[pltpu.VMEM((rows, 2 * HEAD_DIM), BF16), pltpu.VMEM((S // tc, rows, tc), F32),
                        pltpu.VMEM((rows, LANES), F32), pltpu.VMEM((rows, LANES), F32),
                        pltpu.VMEM((rows, HEAD_DIM), F32)],
        compiler_params=_cparams(("parallel", "arbitrary", "arbitrary"), VMEM_LIMIT),
        name="selected_attn",
    )(main, main, main, bias, onehot_t)


def _band_kernel(*refs, n_kt, max_dist, kv_heads, with_lse):
    q_ref = refs[0]
    k_refs = refs[1:1 + n_kt]
    v_refs = refs[1 + n_kt:1 + 2 * n_kt]
    o_ref = refs[1 + 2 * n_kt]
    qi = pl.program_id(2)
    tq = q_ref.shape[0]
    tk = k_refs[0].shape[0]
    nk = n_kt * tk
    qpos = qi * tq + lax.broadcasted_iota(jnp.int32, (tq, nk), 0)
    kpos = (qi - (n_kt - 1)) * tk + lax.broadcasted_iota(jnp.int32, (tq, nk), 1)
    diff = qpos - kpos
    mask = (kpos >= 0) & (diff >= 0) & (diff <= max_dist)
    k = jnp.concatenate([r[...] for r in k_refs], axis=0)
    v = jnp.concatenate([r[...] for r in v_refs], axis=0)
    n_heads = q_ref.shape[1] // HEAD_DIM
    lane = lax.broadcasted_iota(jnp.int32, (tq, LANES), 1)
    lse = jnp.zeros((tq, LANES), F32)
    for h in range(n_heads):
        hk = h if kv_heads > 1 else 0
        s = _qk(q_ref[:, h * HEAD_DIM:(h + 1) * HEAD_DIM], k[:, hk * HEAD_DIM:(hk + 1) * HEAD_DIM])
        p, m, safe = _masked_softmax(s, mask)
        o = jnp.dot(p.astype(BF16), v[:, hk * HEAD_DIM:(hk + 1) * HEAD_DIM], preferred_element_type=F32)
        o_ref[:, h * HEAD_DIM:(h + 1) * HEAD_DIM] = (o / safe).astype(o_ref.dtype)
        lse = jnp.where(lane == h, m + jnp.log2(safe), lse)
    if with_lse:
        refs[2 + 2 * n_kt][...] = lse


def _band_attention(src, grid, q_map, k_map, v_map, kv_width, out_shape, o_map, t, max_dist, with_lse):
    n_kt = -(-max_dist // t) + 1
    kern = functools.partial(_band_kernel, n_kt=n_kt, max_dist=max_dist,
                             kv_heads=kv_width // HEAD_DIM, with_lse=with_lse)

    def shifted(fn, u):
        def index_map(bb, a, i):
            return fn(bb, a, jnp.maximum(i - (n_kt - 1) + u, 0))
        return index_map

    in_specs = [pl.BlockSpec((None, None, t, GROUP_W), q_map)]
    in_specs += [pl.BlockSpec((None, None, t, kv_width), shifted(k_map, u)) for u in range(n_kt)]
    in_specs += [pl.BlockSpec((None, None, t, kv_width), shifted(v_map, u)) for u in range(n_kt)]
    o_spec = pl.BlockSpec((None, t, GROUP_W), o_map)
    if with_lse:
        lse_shape = out_shape[:-1] + (out_shape[-1] // GROUP_W * LANES,)
        out_shapes = (jax.ShapeDtypeStruct(out_shape, BF16), jax.ShapeDtypeStruct(lse_shape, F32))
        out_specs = (o_spec, pl.BlockSpec((None, t, LANES), o_map))
    else:
        out_shapes = jax.ShapeDtypeStruct(out_shape, BF16)
        out_specs = o_spec
    return pl.pallas_call(
        kern, out_shape=out_shapes, grid=grid, in_specs=in_specs, out_specs=out_specs,
        compiler_params=_cparams(("parallel", "arbitrary", "arbitrary"), VMEM_LIMIT),
        name="band_attn",
    )(*([src] * (1 + 2 * n_kt)))


def _layer_norm(z, g, b):
    mu = z.mean(-1, keepdims=True)
    zc = z - mu
    var = (zc * zc).mean(-1, keepdims=True)
    return zc * lax.rsqrt(var + LN_EPS) * g + b


def _merge_kernel(ocmp_ref, oslc_ref, owin_ref, gl_ref, ga_ref, gb_ref,
                  d0_ref, d1_ref, d2_ref, l0_ref, l1_ref, l2_ref, x_ref,
                  wa_ref, wb_ref, wo_ref, g_ref, b_ref, wr_ref, br_ref, hf_ref, hp_ref, lg_ref, *, alpha):
    tt = x_ref.shape[0]
    gates = jax.nn.sigmoid(gl_ref[...].astype(F32))
    parts = []
    for h in range(NSA_HEADS):
        sl = slice(h * HEAD_DIM, (h + 1) * HEAD_DIM)
        acc = jnp.zeros((tt, HEAD_DIM), F32)
        for br, ref in enumerate((ocmp_ref, oslc_ref, owin_ref)):
            gcol = gates[:, 3 * h + br:3 * h + br + 1]
            acc = acc + gcol * ref[:, sl].astype(F32)
        parts.append(acc.astype(BF16))
    o_nsa = jnp.concatenate(parts, axis=1)
    l0, l1, l2 = l0_ref[...], l1_ref[...], l2_ref[...]
    lm = jnp.maximum(jnp.maximum(l0, l1), l2)
    e0, e1, e2 = jnp.exp2(l0 - lm), jnp.exp2(l1 - lm), jnp.exp2(l2 - lm)
    inv = 1.0 / (e0 + e1 + e2)
    w0, w1, w2 = e0 * inv, e1 * inv, e2 * inv
    parts = []
    for h in range(DIL_HEADS):
        sl = slice(h * HEAD_DIM, (h + 1) * HEAD_DIM)
        parts.append(w0[:, h:h + 1] * d0_ref[:, sl].astype(F32) + w1[:, h:h + 1] * d1_ref[:, sl].astype(F32)
                     + w2[:, h:h + 1] * d2_ref[:, sl].astype(F32))
    o_dil = jnp.concatenate(parts, axis=1)
    y_a = jnp.dot(o_nsa, wa_ref[...], preferred_element_type=F32)
    y_b = jnp.dot(o_dil.astype(BF16), wb_ref[...], preferred_element_type=F32)
    merged = (jax.nn.sigmoid(ga_ref[...].astype(F32)) * y_a
              + jax.nn.sigmoid(gb_ref[...].astype(F32)) * y_b)
    mix = jnp.dot(merged.astype(BF16), wo_ref[...], preferred_element_type=F32)
    h = _layer_norm(alpha * x_ref[...] + mix, g_ref[...], b_ref[...])
    hf_ref[...] = h
    hb = h.astype(BF16)
    hp_ref[...] = _pack_pairs(hb)
    lg_ref[...] = jnp.dot(hb, wr_ref[...], preferred_element_type=F32) + br_ref[...]


def _merge(o_cmp, o_slc, o_win, main2d, gl_tile, ga_blk, gb_blk, dil_o, dil_lse, x2d,
           w_a, w_b, w_o, ln_g, ln_b, w_r, b_r, alpha, tt):
    T, D = x2d.shape
    nsa_w = NSA_HEADS * HEAD_DIM
    row = lambda i: (i, 0)
    const = lambda i: (0, 0)
    in_specs = [
        pl.BlockSpec((tt, nsa_w), row), pl.BlockSpec((tt, nsa_w), row), pl.BlockSpec((tt, nsa_w), row),
        pl.BlockSpec((tt, LANES), lambda i: (i, gl_tile)),
        pl.BlockSpec((tt, D), lambda i: (i, ga_blk)),
        pl.BlockSpec((tt, D), lambda i: (i, gb_blk)),
    ]
    in_specs += [pl.BlockSpec((tt, DIL_W), row)] * 3 + [pl.BlockSpec((tt, LANES), row)] * 3
    in_specs += [
        pl.BlockSpec((tt, D), row),
        pl.BlockSpec((nsa_w, D), const), pl.BlockSpec((DIL_W, D), const), pl.BlockSpec((D, D), const),
        pl.BlockSpec((1, D), const), pl.BlockSpec((1, D), const),
        pl.BlockSpec((D, LANES), const), pl.BlockSpec((1, LANES), const),
    ]
    return pl.pallas_call(
        functools.partial(_merge_kernel, alpha=alpha),
        out_shape=(jax.ShapeDtypeStruct((T, D), F32), jax.ShapeDtypeStruct((T, D // 2), jnp.uint32),
                   jax.ShapeDtypeStruct((T, LANES), F32)),
        grid=(T // tt,),
        in_specs=in_specs,
        out_specs=(pl.BlockSpec((tt, D), row), pl.BlockSpec((tt, D // 2), row), pl.BlockSpec((tt, LANES), row)),
        compiler_params=_cparams(("parallel",), VMEM_LIMIT),
        name="merge_ln1",
    )(o_cmp, o_slc, o_win, main2d, main2d, main2d, *dil_o, *dil_lse, x2d, w_a, w_b, w_o, ln_g, ln_b, w_r, b_r)


def _router_kernel(lg_ref, tri_ref, meta_ref, cnt_ref, carry_sc):
    i = pl.program_id(0)
    tt = lg_ref.shape[0]

    @pl.when(i == 0)
    def _():
        carry_sc[...] = jnp.zeros(carry_sc.shape, F32)

    logits = lg_ref[...]
    lane = lax.broadcasted_iota(jnp.int32, (tt, LANES), 1)
    v = logits
    onehot = jnp.zeros((tt, LANES), F32)
    vals, idxs = [], []
    for _ in range(TOP_K):
        m = v.max(-1, keepdims=True)
        idx = jnp.where(v == m, lane, LANES).min(-1, keepdims=True)
        hit = lane == idx
        vals.append(m)
        idxs.append(idx)
        onehot = onehot + hit.astype(F32)
        v = jnp.where(hit, -jnp.inf, v)
    exps = [jnp.exp(vk - vals[0]) for vk in vals]
    den = exps[0] + exps[1] + exps[2] + exps[3]
    before = jnp.dot(tri_ref[...], onehot.astype(BF16), preferred_element_type=F32) + carry_sc[0:1, :]
    meta = jnp.zeros((tt, LANES), F32)
    for k in range(TOP_K):
        rank = jnp.where(lane == idxs[k], before, 0.0).sum(-1, keepdims=True)
        meta = jnp.where(lane == k, idxs[k].astype(F32), meta)
        meta = jnp.where(lane == TOP_K + k, exps[k] / den, meta)
        meta = jnp.where(lane == 2 * TOP_K + k, rank, meta)
    meta_ref[...] = meta
    carry_sc[...] = carry_sc[...] + jnp.broadcast_to(onehot.sum(0, keepdims=True), carry_sc.shape)
    cnt_ref[...] = carry_sc[...]


def _router(logits, tt):
    T = logits.shape[0]
    tri = (jnp.arange(tt)[:, None] > jnp.arange(tt)[None, :]).astype(BF16)
    return pl.pallas_call(
        _router_kernel,
        out_shape=(jax.ShapeDtypeStruct((T, LANES), F32), jax.ShapeDtypeStruct((8, LANES), F32)),
        grid=(T // tt,),
        in_specs=[
            pl.BlockSpec((tt, LANES), lambda i: (i, 0)),
            pl.BlockSpec((tt, tt), lambda i: (0, 0)),
        ],
        out_specs=(pl.BlockSpec((tt, LANES), lambda i: (i, 0)), pl.BlockSpec((8, LANES), lambda i: (0, 0))),
        scratch_shapes=[pltpu.VMEM((8, LANES), F32)],
        compiler_params=_cparams(("arbitrary",), VMEM_LIMIT),
        name="router_top4",
    )(logits, tri)


def _expert_kernel(be_ref, nu_ref, src_cur, src_next, dst_prev, dst_cur, h_hbm,
                   wg_ref, wl_ref, bg_ref, bl_ref, wd_ref, bd_ref, y_hbm,
                   xg, yb, xb_sc, gsem, ssem, *, th):
    i = pl.program_id(0)
    nu = nu_ref[0]
    tm, half = xb_sc.shape[0], xg.shape[2]
    dh = wg_ref.shape[1]
    slot = i % 2
    other = 1 - slot

    def gather_row(src_ref, r, s):
        return pltpu.make_async_copy(h_hbm.at[pl.ds(src_ref[r], 1), :], xg.at[s, pl.ds(r, 1), :], gsem.at[s])

    def scatter_row(dst_ref, r, s):
        return pltpu.make_async_copy(yb.at[s, pl.ds(r, 1), :], y_hbm.at[pl.ds(dst_ref[r], 1), :], ssem.at[s])

    def wait_gather(s):
        pltpu.make_async_copy(h_hbm.at[pl.ds(0, tm), :], xg.at[s], gsem.at[s]).wait()

    def wait_scatter(s):
        pltpu.make_async_copy(yb.at[s], y_hbm.at[pl.ds(0, tm), :], ssem.at[s]).wait()

    @pl.when(i == 0)
    def _():
        yb[1] = jnp.zeros(yb.shape[1:], yb.dtype)
        for r in range(tm):
            gather_row(src_cur, r, 0).start(priority=r % 2)

    @pl.when(i < nu)
    def _():
        wait_gather(slot)

        @pl.when(i >= 1)
        def _():
            wait_scatter(slot)

        for r in range(tm):
            gather_row(src_next, r, other).start(priority=r % 2)
            scatter_row(dst_prev, r, other).start(priority=(r + 1) % 2)
        lo, hi = _unpack_pairs(xg[slot])
        xb_sc[:, :half] = lo.astype(BF16)
        xb_sc[:, half:] = hi.astype(BF16)
        x = xb_sc[...]
        y = bd_ref[...]
        for c in range(dh // th):
            sl = slice(c * th, (c + 1) * th)
            glu = jnp.dot(x, wg_ref[:, sl], preferred_element_type=F32) + bg_ref[:, sl]
            lin = jnp.dot(x, wl_ref[:, sl], preferred_element_type=F32) + bl_ref[:, sl]
            glu = jnp.minimum(glu, SWIGLU_LIMIT)
            lin = jnp.clip(lin, -SWIGLU_LIMIT, SWIGLU_LIMIT)
            act = glu * jax.nn.sigmoid(SWIGLU_ALPHA * glu) * (lin + 1.0)
            y = y + jnp.dot(act.astype(BF16), wd_ref[sl, :], preferred_element_type=F32)
        yb[slot] = _pack_pairs(y.astype(BF16))

    @pl.when(i == nu - 1)
    def _():
        for r in range(tm):
            scatter_row(dst_cur, r, slot).start(priority=r % 2)
        wait_gather(other)
        wait_scatter(other)
        wait_scatter(slot)


def _experts(h_packed, src_row, dst_row, blk_expert, n_used, w_up, b_up, w_down, b_down, tm, th):
    n_blk = src_row.shape[0] // tm
    n_out = dst_row.shape[0]
    E, D, two_dh = w_up.shape
    dh = two_dh // 2
    th = min(th, dh)
    once = pl.Buffered(1)
    smem = pltpu.SMEM

    grid_spec = pltpu.PrefetchScalarGridSpec(
        num_scalar_prefetch=2,
        grid=(n_blk,),
        in_specs=[
            pl.BlockSpec((tm,), lambda i, be, nu: (0,), memory_space=smem),
            pl.BlockSpec((tm,), lambda i, be, nu: (jnp.minimum(i + 1, nu[0] - 1),), memory_space=smem),
            pl.BlockSpec((tm,), lambda i, be, nu: (i,), memory_space=smem),
            pl.BlockSpec((tm,), lambda i, be, nu: (i + 1,), memory_space=smem),
            pl.BlockSpec(memory_space=pl.ANY),
            pl.BlockSpec((None, D, dh), lambda i, be, nu: (be[i], 0, 0), pipeline_mode=once),
            pl.BlockSpec((None, D, dh), lambda i, be, nu: (be[i], 0, 1), pipeline_mode=once),
            pl.BlockSpec((None, 1, dh), lambda i, be, nu: (be[i], 0, 0)),
            pl.BlockSpec((None, 1, dh), lambda i, be, nu: (be[i], 0, 1)),
            pl.BlockSpec((None, dh, D), lambda i, be, nu: (be[i], 0, 0), pipeline_mode=once),
            pl.BlockSpec((None, 1, D), lambda i, be, nu: (be[i], 0, 0)),
        ],
        out_specs=pl.BlockSpec(memory_space=pl.ANY),
        scratch_shapes=[pltpu.VMEM((2, tm, D // 2), jnp.uint32), pltpu.VMEM((2, tm, D // 2), jnp.uint32),
                        pltpu.VMEM((tm, D), BF16),
                        pltpu.SemaphoreType.DMA((2,)), pltpu.SemaphoreType.DMA((2,))],
    )
    return pl.pallas_call(
        functools.partial(_expert_kernel, th=th),
        out_shape=jax.ShapeDtypeStruct((n_out, D // 2), jnp.uint32),
        grid_spec=grid_spec,
        compiler_params=_cparams(("arbitrary",), VMEM_LIMIT),
        name="moe_experts",
    )(blk_expert, n_used, src_row, src_row, dst_row, dst_row, h_packed, w_up, w_up, b_up, b_up, w_down, b_down)


def _combine_kernel(y0_ref, y1_ref, y2_ref, y3_ref, meta_ref, h_ref, g_ref, b_ref, o_ref, *, alpha):
    ffn_lo = ffn_hi = None
    for k, y_ref in enumerate((y0_ref, y1_ref, y2_ref, y3_ref)):
        lo, hi = _unpack_pairs(y_ref[...])
        gate = meta_ref[:, TOP_K + k:TOP_K + k + 1]
        ffn_lo = gate * lo if k == 0 else ffn_lo + gate * lo
        ffn_hi = gate * hi if k == 0 else ffn_hi + gate * hi
    ffn = jnp.concatenate([ffn_lo, ffn_hi], axis=1)
    o_ref[...] = _layer_norm(alpha * h_ref[...] + ffn, g_ref[...], b_ref[...])


def _combine_ln(y, meta, h, g, b, alpha, tt):
    T, D = h.shape
    nt = T // tt

    def plane(k):
        return pl.BlockSpec((tt, D // 2), lambda i: (k * nt + i, 0))

    return pl.pallas_call(
        functools.partial(_combine_kernel, alpha=alpha),
        out_shape=jax.ShapeDtypeStruct((T, D), F32),
        grid=(nt,),
        in_specs=[plane(k) for k in range(TOP_K)] + [
            pl.BlockSpec((tt, LANES), lambda i: (i, 0)),
            pl.BlockSpec((tt, D), lambda i: (i, 0)),
            pl.BlockSpec((1, D), lambda i: (0, 0)),
            pl.BlockSpec((1, D), lambda i: (0, 0)),
        ],
        out_specs=pl.BlockSpec((tt, D), lambda i: (i, 0)),
        compiler_params=_cparams(("parallel",), VMEM_LIMIT),
        name="moe_combine_ln2",
    )(y, y, y, y, meta, h, g, b)


def _layer(x, w_in, b_in, pos_k, pos_v, ck_w1, ck_w2, cv_w1, cv_w2, w_br_nsa, w_br_dil, w_out,
           ln1_g, ln1_b, w_router, b_router, w_up, b_up, w_down, b_down, ln2_g, ln2_b, alpha):
    B, S, D = x.shape
    T = B * S
    nd = D // LANES
    G = NSA_KV_GROUPS
    kvw = G * HEAD_DIM
    n_exp = w_router.shape[1]

    o_q = 0
    o_kv = NSA_HEADS * HEAD_DIM
    o_gl = o_kv + 6 * kvw
    o_dil = o_gl + 3 * NSA_HEADS
    o_ga = o_dil + 3 * N_DIL * DIL_W
    o_gb = o_ga + D

    def wcols(a, n):
        return w_in[:, a:a + n], b_in[a:a + n]

    def kv(i):
        return wcols(o_kv + i * kvw, kvw)

    tn = 1024
    gl_w, gl_b = wcols(o_gl, 3 * NSA_HEADS)
    pieces = [wcols(o_ga, D), wcols(o_gb, D), wcols(o_q, NSA_HEADS * HEAD_DIM),
              kv(0), kv(2), kv(4), kv(1), kv(3), kv(5), (gl_w, gl_b)]
    used = sum(p[0].shape[1] for p in pieces)
    n_main = -(-used // tn) * tn
    pieces.append((jnp.zeros((D, n_main - used), F32), jnp.zeros((n_main - used,), F32)))
    w_main = jnp.concatenate([p[0] for p in pieces], axis=1).astype(BF16)
    b_main = jnp.concatenate([p[1] for p in pieces])[None, :]
    t_q = 2 * nd
    t_kc, t_ks, t_kw = t_q + 8, t_q + 10, t_q + 12
    t_vc, t_vs, t_vw = t_q + 14, t_q + 16, t_q + 18
    t_gl = t_q + 20
    tile_id = jnp.arange(n_main // LANES)
    flags_main = jnp.where((tile_id >= t_q) & (tile_id < t_kc), ROPE_Q,
                           jnp.where((tile_id >= t_kc) & (tile_id < t_vc), ROPE, PLAIN)).astype(jnp.int32)
    q_blk0 = t_q // NSA_REP

    pos = jnp.arange(S, dtype=F32)
    inv = ROPE_THETA ** (-jnp.arange(0, HEAD_DIM, 2, dtype=F32) / HEAD_DIM)
    ang = pos[:, None] * inv[None, :]
    cosx = jnp.concatenate([jnp.cos(ang), jnp.cos(ang)], axis=-1)
    sinx = jnp.concatenate([-jnp.sin(ang), jnp.sin(ang)], axis=-1)

    main = _project(x, w_main, b_main, flags_main, cosx, sinx, 1, 1024, tn)

    nC = S // CMP_STRIDE
    half = CMP_STRIDE * HEAD_DIM

    def cmp_in(tile):
        a = main[:, 0, :, tile * LANES:(tile + G) * LANES].reshape(B, S, G, HEAD_DIM)
        return a.transpose(0, 2, 1, 3).reshape(B, G, nC, half)

    xkv = jnp.stack([cmp_in(t_kc), cmp_in(t_vc)])
    pos_kv = jnp.stack([pos_k.reshape(2, half), pos_v.reshape(2, half)])
    w1 = jnp.stack([ck_w1, cv_w1]).astype(BF16)
    w2 = jnp.stack([ck_w2, cv_w2]).astype(BF16)
    kvc = _compress(xkv, pos_kv, w1, w2)

    n_slc = S // SEL_LEN
    assert n_slc <= LANES
    c_start = jnp.arange(nC) * CMP_STRIDE
    jb = jnp.arange(LANES)
    overlap = ((c_start[:, None] < (jb[None, :] + 1) * SEL_LEN) & (c_start[:, None] + CMP_LEN > jb[None, :] * SEL_LEN)
               & (jb[None, :] < n_slc) & (c_start[:, None] + CMP_LEN <= S)).astype(BF16)
    tq = min(256, S)
    o_cmp, sel = _cmp_attention(main, kvc, overlap, q_blk0, tq)

    onehot_t = (jnp.arange(S)[:, None] // SEL_LEN == jnp.arange(LANES)[None, :]).astype(BF16)
    o_slc = _sel_attention(main, sel, onehot_t, q_blk0, t_ks, t_vs, 256, 512)

    tw = min(256, S)
    o_win = _band_attention(
        main, (B, G, S // tw),
        lambda bb, g, i: (bb, 0, i, q_blk0 + g),
        lambda bb, g, i: (bb, 0, i, t_kw + g),
        lambda bb, g, i: (bb, 0, i, t_vw + g),
        HEAD_DIM, (B, S, NSA_HEADS * HEAD_DIM), lambda bb, g, i: (bb, i, g), tw, WIN_LEN - 1, False)

    flags_dil = jnp.array([ROPE_Q] * DIL_HEADS + [ROPE] * DIL_HEADS + [PLAIN] * DIL_HEADS, jnp.int32)
    dil_o, dil_lse = [], []
    for gi, (w, d) in enumerate(DIL_CONFIGS):
        wd, bd = wcols(o_dil + gi * 3 * DIL_W, 3 * DIL_W)
        sub = _project(x, wd.astype(BF16), bd[None, :], flags_dil, cosx, sinx, d, 512, 3 * DIL_W)
        L = S // d
        td = min(128, L)
        o_g, lse_g = _band_attention(
            sub, (B, d, L // td),
            lambda bb, r, i: (bb, r, i, 0),
            lambda bb, r, i: (bb, r, i, 1),
            lambda bb, r, i: (bb, r, i, 2),
            DIL_W, (B, L, d * DIL_W), lambda bb, r, i: (bb, i, r), td, w // d, True)
        dil_o.append(o_g.reshape(T, DIL_W))
        dil_lse.append(lse_g.reshape(T, LANES))

    w_r = jnp.concatenate([w_router, jnp.zeros((D, LANES - n_exp), F32)], axis=1).astype(BF16)
    b_r = jnp.concatenate([b_router, jnp.full((LANES - n_exp,), NEG, F32)])[None, :]
    h_f, h_p, logits = _merge(
        o_cmp.reshape(T, -1), o_slc.reshape(T, -1), o_win.reshape(T, -1), main.reshape(T, n_main),
        t_gl, 0, 1, dil_o, dil_lse, x.reshape(T, D),
        w_br_nsa.astype(BF16), w_br_dil.astype(BF16), w_out.astype(BF16),
        ln1_g[None, :], ln1_b[None, :], w_r, b_r, alpha, min(256, T))

    meta, cnt = _router(logits, min(512, T))
    top_idx = meta[:, 0:TOP_K].astype(jnp.int32)
    gates = meta[:, TOP_K:2 * TOP_K]
    rank = meta[:, 2 * TOP_K:3 * TOP_K].astype(jnp.int32)

    tm = 512 if T * TOP_K >= 512 * n_exp else 128
    counts = cnt[0, :n_exp].astype(jnp.int32)
    padded = (counts + tm - 1) // tm * tm
    pad_end = jnp.cumsum(padded)
    pad_start = pad_end - padded
    dest = (pad_start[top_idx] + rank).reshape(T * TOP_K)
    n_rows = T * TOP_K + n_exp * tm
    n_blk = n_rows // tm
    blk_start = jnp.arange(n_blk, dtype=jnp.int32) * tm
    blk_expert = jnp.minimum((pad_end[None, :] <= blk_start[:, None]).sum(-1), n_exp - 1).astype(jnp.int32)
    n_used = (pad_end[-1:] // tm).astype(jnp.int32)

    n_asg = T * TOP_K
    asg_of_row = jnp.full((n_rows,), -1, jnp.int32).at[dest].set(jnp.arange(n_asg, dtype=jnp.int32))
    is_pad = asg_of_row < 0
    spare = n_asg + tm + jnp.cumsum(is_pad.astype(jnp.int32)) - 1
    src_row = jnp.where(is_pad, 0, asg_of_row // TOP_K)
    dst_row = jnp.where(is_pad, spare, (asg_of_row % TOP_K) * T + asg_of_row // TOP_K)
    dst_row = jnp.concatenate([n_asg + jnp.arange(tm, dtype=jnp.int32), dst_row])

    y = _experts(h_p, src_row, dst_row, blk_expert, n_used,
                 w_up.astype(BF16), b_up[:, None, :], w_down.astype(BF16), b_down[:, None, :], tm, 512)
    out = _combine_ln(y, meta, h_f, ln2_g[None, :], ln2_b[None, :], alpha, min(512, T))
    return out.reshape(B, S, D)


def kernel(x, w_in, b_in, cmp_pos_k, cmp_pos_v, cmp_k_w1, cmp_k_w2, cmp_v_w1, cmp_v_w2, w_br_nsa, w_br_dil,
           w_out, ln1_g, ln1_b, w_router, b_router, w_up, b_up, w_down, b_down, ln2_g, ln2_b):
    depth = w_in.shape[0]
    alpha = (2.0 * depth) ** 0.25
    h = x
    for l in range(depth):
        h = _layer(h, w_in[l], b_in[l], cmp_pos_k[l], cmp_pos_v[l], cmp_k_w1[l], cmp_k_w2[l],
                   cmp_v_w1[l], cmp_v_w2[l], w_br_nsa[l], w_br_dil[l], w_out[l], ln1_g[l], ln1_b[l],
                   w_router[l], b_router[l], w_up[l], b_up[l], w_down[l], b_down[l], ln2_g[l], ln2_b[l], alpha)
    return h
```

```python
import functools

import jax
import jax.numpy as jnp
from jax import lax
from jax.experimental import pallas as pl
from jax.experimental.pallas import tpu as pltpu

F32 = jnp.float32
BF16 = jnp.bfloat16

HEAD_DIM = 128
LANES = 128
ROPE_THETA = 10000.0
NSA_HEADS = 8
NSA_KV_GROUPS = 2
NSA_REP = NSA_HEADS // NSA_KV_GROUPS
CMP_LEN = 32
CMP_STRIDE = 16
SEL_LEN = 64
SEL_TOPK = 16
WIN_LEN = 512
DIL_CONFIGS = ((128, 1), (512, 4), (2048, 16))
N_DIL = len(DIL_CONFIGS)
DIL_HEADS = 4
TOP_K = 4
SWIGLU_LIMIT = 7.0
SWIGLU_ALPHA = 1.702
LN_EPS = 1e-5
NEG = -1e30
FORCE = 1e9
SCALE = HEAD_DIM ** -0.5
LOG2E = 1.4426950408889634
Q_SCALE = SCALE * LOG2E
PLAIN, ROPE, ROPE_Q = 0, 1, 2

GROUP_W = NSA_REP * HEAD_DIM
DIL_W = DIL_HEADS * HEAD_DIM
VMEM_LIMIT = 56 * 1024 * 1024


def _cparams(sem, vmem=None):
    return pltpu.CompilerParams(dimension_semantics=sem, vmem_limit_bytes=vmem)


def _masked_softmax(s, mask):
    s = jnp.where(mask, s, NEG)
    m = s.max(-1, keepdims=True)
    p = jnp.where(mask, jnp.exp2(s - m), 0.0)
    den = p.sum(-1, keepdims=True)
    safe = jnp.where(den > 0, den, 1.0)
    return p, m, safe


def _qk(q, k):
    return lax.dot_general(q, k, (((1,), (1,)), ((), ())), preferred_element_type=F32)


def _pack_pairs(xb):
    m = xb.shape[1] // 2
    lo = lax.bitcast_convert_type(xb[:, :m].astype(F32), jnp.uint32)
    hi = lax.bitcast_convert_type(xb[:, m:].astype(F32), jnp.uint32)
    return (lo >> 16) | (hi & jnp.uint32(0xFFFF0000))


def _unpack_pairs(w):
    lo = lax.bitcast_convert_type(w << 16, F32)
    hi = lax.bitcast_convert_type(w & jnp.uint32(0xFFFF0000), F32)
    return lo, hi


def _proj_kernel(flags_ref, x_ref, w_ref, b_ref, cos_ref, sin_ref, o_ref, xb_ref, *xcol_ref):
    j = pl.program_id(2)
    d, sub, tn = o_ref.shape
    n_sub = tn // LANES

    @pl.when(j == 0)
    def _():
        if d == 1:
            xb_ref[...] = x_ref[...].astype(BF16)
        else:
            xcol, = xcol_ref
            for c in range(xcol.shape[0]):
                xcol[c] = x_ref[:, c * LANES:(c + 1) * LANES]
            for c in range(xcol.shape[0]):
                for r in range(d):
                    xb_ref[r * sub:(r + 1) * sub, c * LANES:(c + 1) * LANES] = (
                        xcol[c, pl.ds(r, sub, stride=d), :].astype(BF16))

    acc = jnp.dot(xb_ref[...], w_ref[...], preferred_element_type=F32) + b_ref[...]
    for u in range(n_sub):
        a = acc[:, u * LANES:(u + 1) * LANES]
        roped = a * cos_ref[...] + pltpu.roll(a, HEAD_DIM // 2, 1) * sin_ref[...]
        flag = flags_ref[j * n_sub + u]
        mult = jnp.where(flag == ROPE_Q, Q_SCALE, 1.0).astype(F32)
        res = (jnp.where(flag == PLAIN, a, roped) * mult).astype(o_ref.dtype)
        for r in range(d):
            o_ref[r, :, u * LANES:(u + 1) * LANES] = res[r * sub:(r + 1) * sub]


def _project(x, w, b, flags, cosx, sinx, d, tm, tn):
    B, S, D = x.shape
    N = w.shape[1]
    L = S // d
    tm = min(tm, S)
    sub = tm // d

    def regroup(tab):
        return tab.reshape(S // tm, sub, d, HEAD_DIM).transpose(0, 2, 1, 3).reshape(S, HEAD_DIM)

    grid_spec = pltpu.PrefetchScalarGridSpec(
        num_scalar_prefetch=1,
        grid=(B, S // tm, N // tn),
        in_specs=[
            pl.BlockSpec((None, tm, D), lambda bb, i, j, f: (bb, i, 0)),
            pl.BlockSpec((D, tn), lambda bb, i, j, f: (0, j)),
            pl.BlockSpec((1, tn), lambda bb, i, j, f: (0, j)),
            pl.BlockSpec((tm, HEAD_DIM), lambda bb, i, j, f: (i, 0)),
            pl.BlockSpec((tm, HEAD_DIM), lambda bb, i, j, f: (i, 0)),
        ],
        out_specs=pl.BlockSpec((None, d, sub, tn), lambda bb, i, j, f: (bb, 0, i, j)),
        scratch_shapes=[pltpu.VMEM((tm, D), BF16)] + ([pltpu.VMEM((D // LANES, tm, LANES), F32)] if d > 1 else []),
    )
    return pl.pallas_call(
        _proj_kernel,
        out_shape=jax.ShapeDtypeStruct((B, d, L, N), BF16),
        grid_spec=grid_spec,
        compiler_params=_cparams(("parallel", "arbitrary", "arbitrary"), VMEM_LIMIT),
        name="proj_rope",
    )(flags, x, w, b, regroup(cosx), regroup(sinx))


def _gelu_tanh(x):
    return 0.5 * x * (1.0 + jnp.tanh(0.7978845608028654 * (x + 0.044715 * x * x * x)))


def _compress_kernel(x_ref, pos_ref, w1_ref, w2_ref, o_ref):
    half = x_ref.shape[-1]
    nc = x_ref.shape[0]
    x = x_ref[...].astype(F32)
    lo = (x + pos_ref[0:1, :]).astype(BF16)
    hi = (x + pos_ref[1:2, :]).astype(BF16)
    y_lo = jnp.dot(lo, w1_ref[0:half, :], preferred_element_type=F32)
    y_hi = jnp.dot(hi, w1_ref[half:2 * half, :], preferred_element_type=F32)
    h = y_lo + pltpu.roll(y_hi, nc - 1, 0)
    g = _gelu_tanh(h).astype(BF16)
    o_ref[...] = jnp.dot(g, w2_ref[...], preferred_element_type=F32).astype(o_ref.dtype)


def _compress(xkv, pos, w1, w2):
    _, B, G, nC, half = xkv.shape
    hid = w1.shape[-1]
    return pl.pallas_call(
        _compress_kernel,
        out_shape=jax.ShapeDtypeStruct((2, B, G, nC, HEAD_DIM), BF16),
        grid=(2, B, G),
        in_specs=[
            pl.BlockSpec((None, None, None, nC, half), lambda a, bb, g: (a, bb, g, 0, 0)),
            pl.BlockSpec((None, 2, half), lambda a, bb, g: (a, 0, 0)),
            pl.BlockSpec((None, 2 * half, hid), lambda a, bb, g: (a, 0, 0)),
            pl.BlockSpec((None, hid, HEAD_DIM), lambda a, bb, g: (a, 0, 0)),
        ],
        out_specs=pl.BlockSpec((None, None, None, nC, HEAD_DIM), lambda a, bb, g: (a, bb, g, 0, 0)),
        compiler_params=_cparams(("arbitrary", "arbitrary", "arbitrary"), VMEM_LIMIT),
        name="compress_mlp",
    )(xkv, pos, w1, w2)


def _cmp_attn_kernel(q_ref, kc_ref, vc_ref, ov_ref, o_ref, sel_ref, *, n_slc, n_sel):
    qi = pl.program_id(2)
    tq = q_ref.shape[0]
    nc = kc_ref.shape[0]
    t = qi * tq + lax.broadcasted_iota(jnp.int32, (tq, nc), 0)
    c = lax.broadcasted_iota(jnp.int32, (tq, nc), 1)
    mask = (c * CMP_STRIDE + (CMP_LEN - 1)) <= t
    kc = kc_ref[...]
    vc = vc_ref[...]
    ps = jnp.zeros((tq, nc), F32)
    for h in range(NSA_REP):
        s = _qk(q_ref[:, h * HEAD_DIM:(h + 1) * HEAD_DIM], kc)
        p, _, safe = _masked_softmax(s, mask)
        p = p / safe
        o = jnp.dot(p.astype(BF16), vc, preferred_element_type=F32)
        o_ref[:, h * HEAD_DIM:(h + 1) * HEAD_DIM] = o.astype(o_ref.dtype)
        ps = ps + p
    imp = jnp.dot(ps.astype(BF16), ov_ref[...], preferred_element_type=F32)
    tj = qi * tq + lax.broadcasted_iota(jnp.int32, (tq, LANES), 0)
    j = lax.broadcasted_iota(jnp.int32, (tq, LANES), 1)
    cur = tj // SEL_LEN
    forced = (j == 0) | (j == cur) | (j == cur - 1)
    imp = jnp.where(forced, FORCE, jnp.where(j > cur, NEG, imp))
    imp_t = imp.T[0:n_slc, :]
    jj = lax.broadcasted_iota(jnp.int32, (n_slc, tq), 0)
    rank = jnp.zeros((n_slc, tq), jnp.int32)
    for j2 in range(n_slc):
        row = imp_t[j2:j2 + 1, :]
        ahead = (row > imp_t) | ((row == imp_t) & (jj > j2))
        rank = rank + ahead.astype(jnp.int32)
    sel_t = jnp.where(rank < n_sel, 0.0, NEG)
    if n_slc < LANES:
        sel_t = jnp.concatenate([sel_t, jnp.zeros((LANES - n_slc, tq), F32)], axis=0)
    sel_ref[...] = sel_t.T.astype(sel_ref.dtype)


def _cmp_attention(main, kvc, overlap, q_blk0, tq):
    B, _, S, _ = main.shape
    G = NSA_KV_GROUPS
    nC = kvc.shape[3]
    n_slc = S // SEL_LEN
    n_sel = min(SEL_TOPK, n_slc)
    kern = functools.partial(_cmp_attn_kernel, n_slc=n_slc, n_sel=n_sel)
    return pl.pallas_call(
        kern,
        out_shape=(jax.ShapeDtypeStruct((B, S, NSA_HEADS * HEAD_DIM), BF16),
                   jax.ShapeDtypeStruct((B, G, S, LANES), BF16)),
        grid=(B, G, S // tq),
        in_specs=[
            pl.BlockSpec((None, None, tq, GROUP_W), lambda bb, g, i: (bb, 0, i, q_blk0 + g)),
            pl.BlockSpec((None, None, None, nC, HEAD_DIM), lambda bb, g, i: (0, bb, g, 0, 0)),
            pl.BlockSpec((None, None, None, nC, HEAD_DIM), lambda bb, g, i: (1, bb, g, 0, 0)),
            pl.BlockSpec((nC, LANES), lambda bb, g, i: (0, 0)),
        ],
        out_specs=(pl.BlockSpec((None, tq, GROUP_W), lambda bb, g, i: (bb, i, g)),
                   pl.BlockSpec((None, None, tq, LANES), lambda bb, g, i: (bb, g, i, 0))),
        compiler_params=_cparams(("parallel", "arbitrary", "arbitrary"), VMEM_LIMIT),
        name="cmp_attn_select",
    )(main, kvc, kvc, overlap)


def _fold_lanes(x, op):
    out = x[:, 0:LANES]
    for u in range(1, x.shape[1] // LANES):
        out = op(out, x[:, u * LANES:(u + 1) * LANES])
    return out


def _sel_attn_kernel(q_ref, k_ref, v_ref, bias_ref, et_ref, o_ref, qx_sc, s_sc, m_sc, l_sc, acc_sc):
    qi = pl.program_id(2)
    t = q_ref.shape[0]
    tc = s_sc.shape[2]
    rows = NSA_REP * t
    for h in range(NSA_REP):
        qx_sc[h * t:(h + 1) * t, 0:HEAD_DIM] = q_ref[:, h * HEAD_DIM:(h + 1) * HEAD_DIM]
        qx_sc[h * t:(h + 1) * t, HEAD_DIM:2 * HEAD_DIM] = bias_ref[...]

    def scores(c):
        start = pl.multiple_of(c * tc, tc)
        kx = jnp.concatenate([k_ref[pl.ds(start, tc), :], et_ref[pl.ds(start, tc), :]], axis=1)
        return _qk(qx_sc[...], kx)

    m_sc[...] = jnp.full(m_sc.shape, NEG, F32)

    def max_pass(c, carry):
        s = scores(c)
        s_sc[c] = s
        m_sc[...] = jnp.maximum(m_sc[...], _fold_lanes(s, jnp.maximum))
        return carry

    n_full = (qi * t) // tc
    lax.fori_loop(0, n_full, max_pass, 0)
    qpos = qi * t + lax.broadcasted_iota(jnp.int32, (rows, tc), 0) % t
    kpos = n_full * tc + lax.broadcasted_iota(jnp.int32, (rows, tc), 1)
    s = jnp.where(kpos <= qpos, scores(n_full), NEG)
    s_sc[n_full] = s
    m = jnp.maximum(m_sc[...], _fold_lanes(s, jnp.maximum)).max(-1, keepdims=True)
    m_sc[...] = jnp.broadcast_to(m, m_sc.shape)
    l_sc[...] = jnp.zeros(l_sc.shape, F32)
    acc_sc[...] = jnp.zeros(acc_sc.shape, F32)

    def exp_pass(c, carry):
        mb = m_sc[...]
        sc = s_sc[c]
        p = jnp.concatenate([jnp.exp2(sc[:, u * LANES:(u + 1) * LANES] - mb) for u in range(tc // LANES)], axis=1)
        l_sc[...] += _fold_lanes(p, jnp.add)
        v = v_ref[pl.ds(pl.multiple_of(c * tc, tc), tc), :]
        acc_sc[...] += jnp.dot(p.astype(BF16), v, preferred_element_type=F32)
        return carry

    lax.fori_loop(0, n_full + 1, exp_pass, 0)
    l = l_sc[...].sum(-1, keepdims=True)
    o = acc_sc[...] / jnp.where(l > 0, l, 1.0)
    for h in range(NSA_REP):
        o_ref[:, h * HEAD_DIM:(h + 1) * HEAD_DIM] = o[h * t:(h + 1) * t].astype(o_ref.dtype)


def _sel_attention(main, bias, onehot_t, q_blk0, k_tile0, v_tile0, t, tc):
    B, _, S, _ = main.shape
    G = NSA_KV_GROUPS
    t = min(t, S)
    tc = min(tc, S)
    rows = NSA_REP * t
    return pl.pallas_call(
        _sel_attn_kernel,
        out_shape=jax.ShapeDtypeStruct((B, S, NSA_HEADS * HEAD_DIM), BF16),
        grid=(B, G, S // t),
        in_specs=[
            pl.BlockSpec((None, None, t, GROUP_W), lambda bb, g, i: (bb, 0, i, q_blk0 + g)),
            pl.BlockSpec((None, None, S, HEAD_DIM), lambda bb, g, i: (bb, 0, 0, k_tile0 + g)),
            pl.BlockSpec((None, None, S, HEAD_DIM), lambda bb, g, i: (bb, 0, 0, v_tile0 + g)),
            pl.BlockSpec((None, None, t, LANES), lambda bb, g, i: (bb, g, i, 0)),
            pl.BlockSpec((S, LANES), lambda bb, g, i: (0, 0)),
        ],
        out_specs=pl.BlockSpec((None, t, GROUP_W), lambda bb, g, i: (bb, i, g)),
        scratch_shapes=[pltpu.VMEM((rows, 2 * HEAD_DIM), BF16), pltpu.VMEM((S // tc, rows, tc), F32),
                        pltpu.VMEM((rows, LANES), F32), pltpu.VMEM((rows, LANES), F32),
                        pltpu.VMEM((rows, HEAD_DIM), F32)],
        compiler_params=_cparams(("parallel", "arbitrary", "arbitrary"), VMEM_LIMIT),
        name="selected_attn",
    )(main, main, main, bias, onehot_t)


def _band_kernel(*refs, n_prev, n_sub, max_dist, kv_heads, with_lse):
    q_ref = refs[0]
    k_refs = refs[1:2 + n_prev]
    v_refs = refs[2 + n_prev:3 + 2 * n_prev]
    o_ref = refs[3 + 2 * n_prev]
    qi = pl.program_id(2)
    t = k_refs[0].shape[0]
    nk = (n_prev + 1) * t
    n_heads = q_ref.shape[1] // HEAD_DIM
    k_all = jnp.concatenate([r[...] for r in k_refs], axis=0)
    v_all = jnp.concatenate([r[...] for r in v_refs], axis=0)
    kcol = lax.broadcasted_iota(jnp.int32, (t, nk), 1)
    diff = n_prev * t + lax.broadcasted_iota(jnp.int32, (t, nk), 0) - kcol
    in_band = (diff >= 0) & (diff <= max_dist)
    lane = lax.broadcasted_iota(jnp.int32, (t, LANES), 1)
    for u in range(n_sub):
        first_key = (qi * n_sub + u - n_prev) * t
        bias = jnp.where(in_band & (kcol + first_key >= 0), 0.0, NEG)
        rows = slice(u * t, (u + 1) * t)
        k_u = k_all[u * t:u * t + nk]
        v_u = v_all[u * t:u * t + nk]
        lse = jnp.zeros((t, LANES), F32)
        if kv_heads == 1:
            q = jnp.concatenate([q_ref[rows, h * HEAD_DIM:(h + 1) * HEAD_DIM] for h in range(n_heads)], axis=0)
            s = _qk(q, k_u).reshape(n_heads, t, nk) + bias[None]
            m = s.max(-1, keepdims=True)
            p = jnp.exp2(s - m)
            l = p.sum(-1, keepdims=True)
            o = jnp.dot(p.astype(BF16).reshape(n_heads * t, nk), v_u, preferred_element_type=F32)
            o = o.reshape(n_heads, t, HEAD_DIM) / l
            for h in range(n_heads):
                o_ref[rows, h * HEAD_DIM:(h + 1) * HEAD_DIM] = o[h].astype(o_ref.dtype)
                lse = jnp.where(lane == h, m[h] + jnp.log2(l[h]), lse)
        else:
            for h in range(n_heads):
                cols = slice(h * HEAD_DIM, (h + 1) * HEAD_DIM)
                s = _qk(q_ref[rows, cols], k_u[:, cols]) + bias
                m = s.max(-1, keepdims=True)
                p = jnp.exp2(s - m)
                l = p.sum(-1, keepdims=True)
                o = jnp.dot(p.astype(BF16), v_u[:, cols], preferred_element_type=F32)
                o_ref[rows, cols] = (o / l).astype(o_ref.dtype)
                lse = jnp.where(lane == h, m + jnp.log2(l), lse)
        if with_lse:
            refs[4 + 2 * n_prev][rows, :] = lse


def _band_attention(src, lead_grid, length, q_map, k_map, v_map, kv_width, out_shape, o_map,
                    t, n_sub, max_dist, with_lse):
    t = min(t, length)
    n_prev = -(-max_dist // t)
    n_sub = min(n_sub, length // t)
    tile = n_sub * t
    kern = functools.partial(_band_kernel, n_prev=n_prev, n_sub=n_sub, max_dist=max_dist,
                             kv_heads=kv_width // HEAD_DIM, with_lse=with_lse)

    def preceding(fn, j):
        def index_map(bb, a, i):
            return fn(bb, a, jnp.maximum(i * n_sub - n_prev + j, 0))
        return index_map

    def kv_specs(fn):
        return ([pl.BlockSpec((None, None, t, kv_width), preceding(fn, j)) for j in range(n_prev)]
                + [pl.BlockSpec((None, None, tile, kv_width), fn)])

    in_specs = [pl.BlockSpec((None, None, tile, GROUP_W), q_map)] + kv_specs(k_map) + kv_specs(v_map)
    o_spec = pl.BlockSpec((None, tile, GROUP_W), o_map)
    if with_lse:
        lse_shape = out_shape[:-1] + (out_shape[-1] // GROUP_W * LANES,)
        out_shapes = (jax.ShapeDtypeStruct(out_shape, BF16), jax.ShapeDtypeStruct(lse_shape, F32))
        out_specs = (o_spec, pl.BlockSpec((None, tile, LANES), o_map))
    else:
        out_shapes = jax.ShapeDtypeStruct(out_shape, BF16)
        out_specs = o_spec
    return pl.pallas_call(
        kern, out_shape=out_shapes, grid=lead_grid + (length // tile,), in_specs=in_specs, out_specs=out_specs,
        compiler_params=_cparams(("parallel", "arbitrary", "arbitrary"), VMEM_LIMIT),
        name="band_attn",
    )(*([src] * (3 + 2 * n_prev)))


def _layer_norm(z, g, b):
    mu = z.mean(-1, keepdims=True)
    zc = z - mu
    var = (zc * zc).mean(-1, keepdims=True)
    return zc * lax.rsqrt(var + LN_EPS) * g + b


def _merge_kernel(ocmp_ref, oslc_ref, owin_ref, gl_ref, ga_ref, gb_ref,
                  d0_ref, d1_ref, d2_ref, l0_ref, l1_ref, l2_ref, x_ref,
                  wa_ref, wb_ref, wo_ref, g_ref, b_ref, wr_ref, br_ref, hf_ref, hp_ref, lg_ref, *, alpha):
    tt = x_ref.shape[0]
    gates = jax.nn.sigmoid(gl_ref[...].astype(F32))
    parts = []
    for h in range(NSA_HEADS):
        sl = slice(h * HEAD_DIM, (h + 1) * HEAD_DIM)
        acc = jnp.zeros((tt, HEAD_DIM), F32)
        for br, ref in enumerate((ocmp_ref, oslc_ref, owin_ref)):
            gcol = gates[:, 3 * h + br:3 * h + br + 1]
            acc = acc + gcol * ref[:, sl].astype(F32)
        parts.append(acc.astype(BF16))
    o_nsa = jnp.concatenate(parts, axis=1)
    l0, l1, l2 = l0_ref[...], l1_ref[...], l2_ref[...]
    lm = jnp.maximum(jnp.maximum(l0, l1), l2)
    e0, e1, e2 = jnp.exp2(l0 - lm), jnp.exp2(l1 - lm), jnp.exp2(l2 - lm)
    inv = 1.0 / (e0 + e1 + e2)
    w0, w1, w2 = e0 * inv, e1 * inv, e2 * inv
    parts = []
    for h in range(DIL_HEADS):
        sl = slice(h * HEAD_DIM, (h + 1) * HEAD_DIM)
        parts.append(w0[:, h:h + 1] * d0_ref[:, sl].astype(F32) + w1[:, h:h + 1] * d1_ref[:, sl].astype(F32)
                     + w2[:, h:h + 1] * d2_ref[:, sl].astype(F32))
    o_dil = jnp.concatenate(parts, axis=1)
    y_a = jnp.dot(o_nsa, wa_ref[...], preferred_element_type=F32)
    y_b = jnp.dot(o_dil.astype(BF16), wb_ref[...], preferred_element_type=F32)
    merged = (jax.nn.sigmoid(ga_ref[...].astype(F32)) * y_a
              + jax.nn.sigmoid(gb_ref[...].astype(F32)) * y_b)
    mix = jnp.dot(merged.astype(BF16), wo_ref[...], preferred_element_type=F32)
    h = _layer_norm(alpha * x_ref[...] + mix, g_ref[...], b_ref[...])
    hf_ref[...] = h
    hb = h.astype(BF16)
    hp_ref[...] = _pack_pairs(hb)
    lg_ref[...] = jnp.dot(hb, wr_ref[...], preferred_element_type=F32) + br_ref[...]


def _merge(o_cmp, o_slc, o_win, main2d, gl_tile, ga_blk, gb_blk, dil_o, dil_lse, x2d,
           w_a, w_b, w_o, ln_g, ln_b, w_r, b_r, alpha, tt):
    T, D = x2d.shape
    nsa_w = NSA_HEADS * HEAD_DIM
    row = lambda i: (i, 0)
    const = lambda i: (0, 0)
    in_specs = [
        pl.BlockSpec((tt, nsa_w), row), pl.BlockSpec((tt, nsa_w), row), pl.BlockSpec((tt, nsa_w), row),
        pl.BlockSpec((tt, LANES), lambda i: (i, gl_tile)),
        pl.BlockSpec((tt, D), lambda i: (i, ga_blk)),
        pl.BlockSpec((tt, D), lambda i: (i, gb_blk)),
    ]
    in_specs += [pl.BlockSpec((tt, DIL_W), row)] * 3 + [pl.BlockSpec((tt, LANES), row)] * 3
    in_specs += [
        pl.BlockSpec((tt, D), row),
        pl.BlockSpec((nsa_w, D), const), pl.BlockSpec((DIL_W, D), const), pl.BlockSpec((D, D), const),
        pl.BlockSpec((1, D), const), pl.BlockSpec((1, D), const),
        pl.BlockSpec((D, LANES), const), pl.BlockSpec((1, LANES), const),
    ]
    return pl.pallas_call(
        functools.partial(_merge_kernel, alpha=alpha),
        out_shape=(jax.ShapeDtypeStruct((T, D), F32), jax.ShapeDtypeStruct((T, D // 2), jnp.uint32),
                   jax.ShapeDtypeStruct((T, LANES), F32)),
        grid=(T // tt,),
        in_specs=in_specs,
        out_specs=(pl.BlockSpec((tt, D), row), pl.BlockSpec((tt, D // 2), row), pl.BlockSpec((tt, LANES), row)),
        compiler_params=_cparams(("parallel",), VMEM_LIMIT),
        name="merge_ln1",
    )(o_cmp, o_slc, o_win, main2d, main2d, main2d, *dil_o, *dil_lse, x2d, w_a, w_b, w_o, ln_g, ln_b, w_r, b_r)


def _router_kernel(lg_ref, tri_ref, meta_ref, cnt_ref, carry_sc):
    i = pl.program_id(0)
    tt = lg_ref.shape[0]

    @pl.when(i == 0)
    def _():
        carry_sc[...] = jnp.zeros(carry_sc.shape, F32)

    logits = lg_ref[...]
    lane = lax.broadcasted_iota(jnp.int32, (tt, LANES), 1)
    v = logits
    onehot = jnp.zeros((tt, LANES), F32)
    vals, idxs = [], []
    for _ in range(TOP_K):
        m = v.max(-1, keepdims=True)
        idx = jnp.where(v == m, lane, LANES).min(-1, keepdims=True)
        hit = lane == idx
        vals.append(m)
        idxs.append(idx)
        onehot = onehot + hit.astype(F32)
        v = jnp.where(hit, -jnp.inf, v)
    exps = [jnp.exp(vk - vals[0]) for vk in vals]
    den = exps[0] + exps[1] + exps[2] + exps[3]
    before = jnp.dot(tri_ref[...], onehot.astype(BF16), preferred_element_type=F32) + carry_sc[0:1, :]
    meta = jnp.zeros((tt, LANES), F32)
    for k in range(TOP_K):
        rank = jnp.where(lane == idxs[k], before, 0.0).sum(-1, keepdims=True)
        meta = jnp.where(lane == k, idxs[k].astype(F32), meta)
        meta = jnp.where(lane == TOP_K + k, exps[k] / den, meta)
        meta = jnp.where(lane == 2 * TOP_K + k, rank, meta)
    meta_ref[...] = meta
    carry_sc[...] = carry_sc[...] + jnp.broadcast_to(onehot.sum(0, keepdims=True), carry_sc.shape)
    cnt_ref[...] = carry_sc[...]


def _router(logits, tt):
    T = logits.shape[0]
    tri = (jnp.arange(tt)[:, None] > jnp.arange(tt)[None, :]).astype(BF16)
    return pl.pallas_call(
        _router_kernel,
        out_shape=(jax.ShapeDtypeStruct((T, LANES), F32), jax.ShapeDtypeStruct((8, LANES), F32)),
        grid=(T // tt,),
        in_specs=[
            pl.BlockSpec((tt, LANES), lambda i: (i, 0)),
            pl.BlockSpec((tt, tt), lambda i: (0, 0)),
        ],
        out_specs=(pl.BlockSpec((tt, LANES), lambda i: (i, 0)), pl.BlockSpec((8, LANES), lambda i: (0, 0))),
        scratch_shapes=[pltpu.VMEM((8, LANES), F32)],
        compiler_params=_cparams(("arbitrary",), VMEM_LIMIT),
        name="router_top4",
    )(logits, tri)


def _expert_kernel(be_ref, nu_ref, src_cur, src_next, dst_prev, dst_cur, h_hbm,
                   wg_ref, wl_ref, bg_ref, bl_ref, wd_ref, bd_ref, y_hbm,
                   xg, yb, xb_sc, gsem, ssem, *, th):
    i = pl.program_id(0)
    nu = nu_ref[0]
    tm, half = xb_sc.shape[0], xg.shape[2]
    dh = wg_ref.shape[1]
    slot = i % 2
    other = 1 - slot

    def gather_row(src_ref, r, s):
        return pltpu.make_async_copy(h_hbm.at[pl.ds(src_ref[r], 1), :], xg.at[s, pl.ds(r, 1), :], gsem.at[s])

    def scatter_row(dst_ref, r, s):
        return pltpu.make_async_copy(yb.at[s, pl.ds(r, 1), :], y_hbm.at[pl.ds(dst_ref[r], 1), :], ssem.at[s])

    def wait_gather(s):
        pltpu.make_async_copy(h_hbm.at[pl.ds(0, tm), :], xg.at[s], gsem.at[s]).wait()

    def wait_scatter(s):
        pltpu.make_async_copy(yb.at[s], y_hbm.at[pl.ds(0, tm), :], ssem.at[s]).wait()

    @pl.when(i == 0)
    def _():
        yb[1] = jnp.zeros(yb.shape[1:], yb.dtype)
        for r in range(tm):
            gather_row(src_cur, r, 0).start(priority=r % 2)

    @pl.when(i < nu)
    def _():
        wait_gather(slot)

        @pl.when(i >= 1)
        def _():
            wait_scatter(slot)

        for r in range(tm):
            gather_row(src_next, r, other).start(priority=r % 2)
            scatter_row(dst_prev, r, other).start(priority=(r + 1) % 2)
        lo, hi = _unpack_pairs(xg[slot])
        xb_sc[:, :half] = lo.astype(BF16)
        xb_sc[:, half:] = hi.astype(BF16)
        x = xb_sc[...]
        y = bd_ref[...]
        for c in range(dh // th):
            sl = slice(c * th, (c + 1) * th)
            glu = jnp.dot(x, wg_ref[:, sl], preferred_element_type=F32) + bg_ref[:, sl]
            lin = jnp.dot(x, wl_ref[:, sl], preferred_element_type=F32) + bl_ref[:, sl]
            glu = jnp.minimum(glu, SWIGLU_LIMIT)
            lin = jnp.clip(lin, -SWIGLU_LIMIT, SWIGLU_LIMIT)
            act = glu * jax.nn.sigmoid(SWIGLU_ALPHA * glu) * (lin + 1.0)
            y = y + jnp.dot(act.astype(BF16), wd_ref[sl, :], preferred_element_type=F32)
        yb[slot] = _pack_pairs(y.astype(BF16))

    @pl.when(i == nu - 1)
    def _():
        for r in range(tm):
            scatter_row(dst_cur, r, slot).start(priority=r % 2)
        wait_gather(other)
        wait_scatter(other)
        wait_scatter(slot)


def _experts(h_packed, src_row, dst_row, blk_expert, n_used, w_up, b_up, w_down, b_down, tm, th):
    n_blk = src_row.shape[0] // tm
    n_out = dst_row.shape[0]
    E, D, two_dh = w_up.shape
    dh = two_dh // 2
    th = min(th, dh)
    once = pl.Buffered(1)
    smem = pltpu.SMEM

    grid_spec = pltpu.PrefetchScalarGridSpec(
        num_scalar_prefetch=2,
        grid=(n_blk,),
        in_specs=[
            pl.BlockSpec((tm,), lambda i, be, nu: (0,), memory_space=smem),
            pl.BlockSpec((tm,), lambda i, be, nu: (jnp.minimum(i + 1, nu[0] - 1),), memory_space=smem),
            pl.BlockSpec((tm,), lambda i, be, nu: (i,), memory_space=smem),
            pl.BlockSpec((tm,), lambda i, be, nu: (i + 1,), memory_space=smem),
            pl.BlockSpec(memory_space=pl.ANY),
            pl.BlockSpec((None, D, dh), lambda i, be, nu: (be[i], 0, 0), pipeline_mode=once),
            pl.BlockSpec((None, D, dh), lambda i, be, nu: (be[i], 0, 1), pipeline_mode=once),
            pl.BlockSpec((None, 1, dh), lambda i, be, nu: (be[i], 0, 0)),
            pl.BlockSpec((None, 1, dh), lambda i, be, nu: (be[i], 0, 1)),
            pl.BlockSpec((None, dh, D), lambda i, be, nu: (be[i], 0, 0), pipeline_mode=once),
            pl.BlockSpec((None, 1, D), lambda i, be, nu: (be[i], 0, 0)),
        ],
        out_specs=pl.BlockSpec(memory_space=pl.ANY),
        scratch_shapes=[pltpu.VMEM((2, tm, D // 2), jnp.uint32), pltpu.VMEM((2, tm, D // 2), jnp.uint32),
                        pltpu.VMEM((tm, D), BF16),
                        pltpu.SemaphoreType.DMA((2,)), pltpu.SemaphoreType.DMA((2,))],
    )
    return pl.pallas_call(
        functools.partial(_expert_kernel, th=th),
        out_shape=jax.ShapeDtypeStruct((n_out, D // 2), jnp.uint32),
        grid_spec=grid_spec,
        compiler_params=_cparams(("arbitrary",), VMEM_LIMIT),
        name="moe_experts",
    )(blk_expert, n_used, src_row, src_row, dst_row, dst_row, h_packed, w_up, w_up, b_up, b_up, w_down, b_down)


def _combine_kernel(y0_ref, y1_ref, y2_ref, y3_ref, meta_ref, h_ref, g_ref, b_ref, o_ref, *, alpha):
    ffn_lo = ffn_hi = None
    for k, y_ref in enumerate((y0_ref, y1_ref, y2_ref, y3_ref)):
        lo, hi = _unpack_pairs(y_ref[...])
        gate = meta_ref[:, TOP_K + k:TOP_K + k + 1]
        ffn_lo = gate * lo if k == 0 else ffn_lo + gate * lo
        ffn_hi = gate * hi if k == 0 else ffn_hi + gate * hi
    ffn = jnp.concatenate([ffn_lo, ffn_hi], axis=1)
    o_ref[...] = _layer_norm(alpha * h_ref[...] + ffn, g_ref[...], b_ref[...])


def _combine_ln(y, meta, h, g, b, alpha, tt):
    T, D = h.shape
    nt = T // tt

    def plane(k):
        return pl.BlockSpec((tt, D // 2), lambda i: (k * nt + i, 0))

    return pl.pallas_call(
        functools.partial(_combine_kernel, alpha=alpha),
        out_shape=jax.ShapeDtypeStruct((T, D), F32),
        grid=(nt,),
        in_specs=[plane(k) for k in range(TOP_K)] + [
            pl.BlockSpec((tt, LANES), lambda i: (i, 0)),
            pl.BlockSpec((tt, D), lambda i: (i, 0)),
            pl.BlockSpec((1, D), lambda i: (0, 0)),
            pl.BlockSpec((1, D), lambda i: (0, 0)),
        ],
        out_specs=pl.BlockSpec((tt, D), lambda i: (i, 0)),
        compiler_params=_cparams(("parallel",), VMEM_LIMIT),
        name="moe_combine_ln2",
    )(y, y, y, y, meta, h, g, b)


def _layer(x, w_in, b_in, pos_k, pos_v, ck_w1, ck_w2, cv_w1, cv_w2, w_br_nsa, w_br_dil, w_out,
           ln1_g, ln1_b, w_router, b_router, w_up, b_up, w_down, b_down, ln2_g, ln2_b, alpha):
    B, S, D = x.shape
    T = B * S
    nd = D // LANES
    G = NSA_KV_GROUPS
    kvw = G * HEAD_DIM
    n_exp = w_router.shape[1]

    o_q = 0
    o_kv = NSA_HEADS * HEAD_DIM
    o_gl = o_kv + 6 * kvw
    o_dil = o_gl + 3 * NSA_HEADS
    o_ga = o_dil + 3 * N_DIL * DIL_W
    o_gb = o_ga + D

    def wcols(a, n):
        return w_in[:, a:a + n], b_in[a:a + n]

    def kv(i):
        return wcols(o_kv + i * kvw, kvw)

    tn = 1024
    gl_w, gl_b = wcols(o_gl, 3 * NSA_HEADS)
    pieces = [wcols(o_ga, D), wcols(o_gb, D), wcols(o_q, NSA_HEADS * HEAD_DIM),
              kv(0), kv(2), kv(4), kv(1), kv(3), kv(5), (gl_w, gl_b)]
    used = sum(p[0].shape[1] for p in pieces)
    n_main = -(-used // tn) * tn
    pieces.append((jnp.zeros((D, n_main - used), F32), jnp.zeros((n_main - used,), F32)))
    w_main = jnp.concatenate([p[0] for p in pieces], axis=1).astype(BF16)
    b_main = jnp.concatenate([p[1] for p in pieces])[None, :]
    t_q = 2 * nd
    t_kc, t_ks, t_kw = t_q + 8, t_q + 10, t_q + 12
    t_vc, t_vs, t_vw = t_q + 14, t_q + 16, t_q + 18
    t_gl = t_q + 20
    tile_id = jnp.arange(n_main // LANES)
    flags_main = jnp.where((tile_id >= t_q) & (tile_id < t_kc), ROPE_Q,
                           jnp.where((tile_id >= t_kc) & (tile_id < t_vc), ROPE, PLAIN)).astype(jnp.int32)
    q_blk0 = t_q // NSA_REP

    pos = jnp.arange(S, dtype=F32)
    inv = ROPE_THETA ** (-jnp.arange(0, HEAD_DIM, 2, dtype=F32) / HEAD_DIM)
    ang = pos[:, None] * inv[None, :]
    cosx = jnp.concatenate([jnp.cos(ang), jnp.cos(ang)], axis=-1)
    sinx = jnp.concatenate([-jnp.sin(ang), jnp.sin(ang)], axis=-1)

    main = _project(x, w_main, b_main, flags_main, cosx, sinx, 1, 1024, tn)

    nC = S // CMP_STRIDE
    half = CMP_STRIDE * HEAD_DIM

    def cmp_in(tile):
        a = main[:, 0, :, tile * LANES:(tile + G) * LANES].reshape(B, S, G, HEAD_DIM)
        return a.transpose(0, 2, 1, 3).reshape(B, G, nC, half)

    xkv = jnp.stack([cmp_in(t_kc), cmp_in(t_vc)])
    pos_kv = jnp.stack([pos_k.reshape(2, half), pos_v.reshape(2, half)])
    w1 = jnp.stack([ck_w1, cv_w1]).astype(BF16)
    w2 = jnp.stack([ck_w2, cv_w2]).astype(BF16)
    kvc = _compress(xkv, pos_kv, w1, w2)

    n_slc = S // SEL_LEN
    assert n_slc <= LANES
    c_start = jnp.arange(nC) * CMP_STRIDE
    jb = jnp.arange(LANES)
    overlap = ((c_start[:, None] < (jb[None, :] + 1) * SEL_LEN) & (c_start[:, None] + CMP_LEN > jb[None, :] * SEL_LEN)
               & (jb[None, :] < n_slc) & (c_start[:, None] + CMP_LEN <= S)).astype(BF16)
    tq = min(256, S)
    o_cmp, sel = _cmp_attention(main, kvc, overlap, q_blk0, tq)

    onehot_t = (jnp.arange(S)[:, None] // SEL_LEN == jnp.arange(LANES)[None, :]).astype(BF16)
    o_slc = _sel_attention(main, sel, onehot_t, q_blk0, t_ks, t_vs, 256, 512)

    o_win = _band_attention(
        main, (B, G), S,
        lambda bb, g, i: (bb, 0, i, q_blk0 + g),
        lambda bb, g, i: (bb, 0, i, t_kw + g),
        lambda bb, g, i: (bb, 0, i, t_vw + g),
        HEAD_DIM, (B, S, NSA_HEADS * HEAD_DIM), lambda bb, g, i: (bb, i, g), 256, 2, WIN_LEN - 1, False)

    flags_dil = jnp.array([ROPE_Q] * DIL_HEADS + [ROPE] * DIL_HEADS + [PLAIN] * DIL_HEADS, jnp.int32)
    dil_o, dil_lse = [], []
    for gi, (w, d) in enumerate(DIL_CONFIGS):
        wd, bd = wcols(o_dil + gi * 3 * DIL_W, 3 * DIL_W)
        sub = _project(x, wd.astype(BF16), bd[None, :], flags_dil, cosx, sinx, d, 512, 3 * DIL_W)
        L = S // d
        o_g, lse_g = _band_attention(
            sub, (B, d), L,
            lambda bb, r, i: (bb, r, i, 0),
            lambda bb, r, i: (bb, r, i, 1),
            lambda bb, r, i: (bb, r, i, 2),
            DIL_W, (B, L, d * DIL_W), lambda bb, r, i: (bb, i, r), 128, 4, w // d, True)
        dil_o.append(o_g.reshape(T, DIL_W))
        dil_lse.append(lse_g.reshape(T, LANES))

    w_r = jnp.concatenate([w_router, jnp.zeros((D, LANES - n_exp), F32)], axis=1).astype(BF16)
    b_r = jnp.concatenate([b_router, jnp.full((LANES - n_exp,), NEG, F32)])[None, :]
    h_f, h_p, logits = _merge(
        o_cmp.reshape(T, -1), o_slc.reshape(T, -1), o_win.reshape(T, -1), main.reshape(T, n_main),
        t_gl, 0, 1, dil_o, dil_lse, x.reshape(T, D),
        w_br_nsa.astype(BF16), w_br_dil.astype(BF16), w_out.astype(BF16),
        ln1_g[None, :], ln1_b[None, :], w_r, b_r, alpha, min(256, T))

    meta, cnt = _router(logits, min(512, T))
    top_idx = meta[:, 0:TOP_K].astype(jnp.int32)
    gates = meta[:, TOP_K:2 * TOP_K]
    rank = meta[:, 2 * TOP_K:3 * TOP_K].astype(jnp.int32)

    tm = 512 if T * TOP_K >= 512 * n_exp else 128
    counts = cnt[0, :n_exp].astype(jnp.int32)
    padded = (counts + tm - 1) // tm * tm
    pad_end = jnp.cumsum(padded)
    pad_start = pad_end - padded
    dest = (pad_start[top_idx] + rank).reshape(T * TOP_K)
    n_rows = T * TOP_K + n_exp * tm
    n_blk = n_rows // tm
    blk_start = jnp.arange(n_blk, dtype=jnp.int32) * tm
    blk_expert = jnp.minimum((pad_end[None, :] <= blk_start[:, None]).sum(-1), n_exp - 1).astype(jnp.int32)
    n_used = (pad_end[-1:] // tm).astype(jnp.int32)

    n_asg = T * TOP_K
    asg_of_row = jnp.full((n_rows,), -1, jnp.int32).at[dest].set(jnp.arange(n_asg, dtype=jnp.int32))
    is_pad = asg_of_row < 0
    spare = n_asg + tm + jnp.cumsum(is_pad.astype(jnp.int32)) - 1
    src_row = jnp.where(is_pad, 0, asg_of_row // TOP_K)
    dst_row = jnp.where(is_pad, spare, (asg_of_row % TOP_K) * T + asg_of_row // TOP_K)
    dst_row = jnp.concatenate([n_asg + jnp.arange(tm, dtype=jnp.int32), dst_row])

    y = _experts(h_p, src_row, dst_row, blk_expert, n_used,
                 w_up.astype(BF16), b_up[:, None, :], w_down.astype(BF16), b_down[:, None, :], tm, 512)
    out = _combine_ln(y, meta, h_f, ln2_g[None, :], ln2_b[None, :], alpha, min(512, T))
    return out.reshape(B, S, D)


def kernel(x, w_in, b_in, cmp_pos_k, cmp_pos_v, cmp_k_w1, cmp_k_w2, cmp_v_w1, cmp_v_w2, w_br_nsa, w_br_dil,
           w_out, ln1_g, ln1_b, w_router, b_router, w_up, b_up, w_down, b_down, ln2_g, ln2_b):
    depth = w_in.shape[0]
    alpha = (2.0 * depth) ** 0.25
    h = x
    for l in range(depth):
        h = _layer(h, w_in[l], b_in[l], cmp_pos_k[l], cmp_pos_v[l], cmp_k_w1[l], cmp_k_w2[l],
                   cmp_v_w1[l], cmp_v_w2[l], w_br_nsa[l], w_br_dil[l], w_out[l], ln1_g[l], ln1_b[l],
                   w_router[l], b_router[l], w_up[l], b_up[l], w_down[l], b_down[l], ln2_g[l], ln2_b[l], alpha)
    return h
```

```python
import functools

import jax
import jax.numpy as jnp
from jax import lax
from jax.experimental import pallas as pl
from jax.experimental.pallas import tpu as pltpu

F32 = jnp.float32
BF16 = jnp.bfloat16

HEAD_DIM = 128
LANES = 128
ROPE_THETA = 10000.0
NSA_HEADS = 8
NSA_KV_GROUPS = 2
NSA_REP = NSA_HEADS // NSA_KV_GROUPS
CMP_LEN = 32
CMP_STRIDE = 16
SEL_LEN = 64
SEL_TOPK = 16
WIN_LEN = 512
DIL_CONFIGS = ((128, 1), (512, 4), (2048, 16))
N_DIL = len(DIL_CONFIGS)
DIL_HEADS = 4
TOP_K = 4
SWIGLU_LIMIT = 7.0
SWIGLU_ALPHA = 1.702
LN_EPS = 1e-5
NEG = -1e30
FORCE = 1e9
SCALE = HEAD_DIM ** -0.5
LOG2E = 1.4426950408889634
Q_SCALE = SCALE * LOG2E
PLAIN, ROPE, ROPE_Q = 0, 1, 2

GROUP_W = NSA_REP * HEAD_DIM
DIL_W = DIL_HEADS * HEAD_DIM
VMEM_LIMIT = 56 * 1024 * 1024


def _cparams(sem, vmem=None):
    return pltpu.CompilerParams(dimension_semantics=sem, vmem_limit_bytes=vmem)


def _masked_softmax(s, mask):
    s = jnp.where(mask, s, NEG)
    m = s.max(-1, keepdims=True)
    p = jnp.where(mask, jnp.exp2(s - m), 0.0)
    den = p.sum(-1, keepdims=True)
    safe = jnp.where(den > 0, den, 1.0)
    return p, m, safe


def _qk(q, k):
    return lax.dot_general(q, k, (((1,), (1,)), ((), ())), preferred_element_type=F32)


def _pack_pairs(xb):
    m = xb.shape[1] // 2
    lo = lax.bitcast_convert_type(xb[:, :m].astype(F32), jnp.uint32)
    hi = lax.bitcast_convert_type(xb[:, m:].astype(F32), jnp.uint32)
    return (lo >> 16) | (hi & jnp.uint32(0xFFFF0000))


def _unpack_pairs(w):
    lo = lax.bitcast_convert_type(w << 16, F32)
    hi = lax.bitcast_convert_type(w & jnp.uint32(0xFFFF0000), F32)
    return lo, hi


def _proj_kernel(flags_ref, x_ref, w_ref, b_ref, cos_ref, sin_ref, o_ref, xb_ref, *xcol_ref):
    j = pl.program_id(2)
    d, sub, tn = o_ref.shape
    n_sub = tn // LANES

    @pl.when(j == 0)
    def _():
        if d == 1:
            xb_ref[...] = x_ref[...].astype(BF16)
        else:
            xcol, = xcol_ref
            for c in range(xcol.shape[0]):
                xcol[c] = x_ref[:, c * LANES:(c + 1) * LANES]
            for c in range(xcol.shape[0]):
                for r in range(d):
                    xb_ref[r * sub:(r + 1) * sub, c * LANES:(c + 1) * LANES] = (
                        xcol[c, pl.ds(r, sub, stride=d), :].astype(BF16))

    acc = jnp.dot(xb_ref[...], w_ref[...], preferred_element_type=F32) + b_ref[...]
    for u in range(n_sub):
        a = acc[:, u * LANES:(u + 1) * LANES]
        roped = a * cos_ref[...] + pltpu.roll(a, HEAD_DIM // 2, 1) * sin_ref[...]
        flag = flags_ref[j * n_sub + u]
        mult = jnp.where(flag == ROPE_Q, Q_SCALE, 1.0).astype(F32)
        res = (jnp.where(flag == PLAIN, a, roped) * mult).astype(o_ref.dtype)
        for r in range(d):
            o_ref[r, :, u * LANES:(u + 1) * LANES] = res[r * sub:(r + 1) * sub]


def _project(x, w, b, flags, cosx, sinx, d, tm, tn):
    B, S, D = x.shape
    N = w.shape[1]
    L = S // d
    tm = min(tm, S)
    sub = tm // d

    def regroup(tab):
        return tab.reshape(S // tm, sub, d, HEAD_DIM).transpose(0, 2, 1, 3).reshape(S, HEAD_DIM)

    grid_spec = pltpu.PrefetchScalarGridSpec(
        num_scalar_prefetch=1,
        grid=(B, S // tm, N // tn),
        in_specs=[
            pl.BlockSpec((None, tm, D), lambda bb, i, j, f: (bb, i, 0)),
            pl.BlockSpec((D, tn), lambda bb, i, j, f: (0, j)),
            pl.BlockSpec((1, tn), lambda bb, i, j, f: (0, j)),
            pl.BlockSpec((tm, HEAD_DIM), lambda bb, i, j, f: (i, 0)),
            pl.BlockSpec((tm, HEAD_DIM), lambda bb, i, j, f: (i, 0)),
        ],
        out_specs=pl.BlockSpec((None, d, sub, tn), lambda bb, i, j, f: (bb, 0, i, j)),
        scratch_shapes=[pltpu.VMEM((tm, D), BF16)] + ([pltpu.VMEM((D // LANES, tm, LANES), F32)] if d > 1 else []),
    )
    return pl.pallas_call(
        _proj_kernel,
        out_shape=jax.ShapeDtypeStruct((B, d, L, N), BF16),
        grid_spec=grid_spec,
        compiler_params=_cparams(("parallel", "arbitrary", "arbitrary"), VMEM_LIMIT),
        name="proj_rope",
    )(flags, x, w, b, regroup(cosx), regroup(sinx))


def _gelu_tanh(x):
    return 0.5 * x * (1.0 + jnp.tanh(0.7978845608028654 * (x + 0.044715 * x * x * x)))


def _compress_kernel(x_ref, pos_ref, w1_ref, w2_ref, o_ref):
    half = x_ref.shape[-1]
    nc = x_ref.shape[0]
    x = x_ref[...].astype(F32)
    lo = (x + pos_ref[0:1, :]).astype(BF16)
    hi = (x + pos_ref[1:2, :]).astype(BF16)
    y_lo = jnp.dot(lo, w1_ref[0:half, :], preferred_element_type=F32)
    y_hi = jnp.dot(hi, w1_ref[half:2 * half, :], preferred_element_type=F32)
    h = y_lo + pltpu.roll(y_hi, nc - 1, 0)
    g = _gelu_tanh(h).astype(BF16)
    o_ref[...] = jnp.dot(g, w2_ref[...], preferred_element_type=F32).astype(o_ref.dtype)


def _compress(xkv, pos, w1, w2):
    _, B, G, nC, half = xkv.shape
    hid = w1.shape[-1]
    return pl.pallas_call(
        _compress_kernel,
        out_shape=jax.ShapeDtypeStruct((2, B, G, nC, HEAD_DIM), BF16),
        grid=(2, B, G),
        in_specs=[
            pl.BlockSpec((None, None, None, nC, half), lambda a, bb, g: (a, bb, g, 0, 0)),
            pl.BlockSpec((None, 2, half), lambda a, bb, g: (a, 0, 0)),
            pl.BlockSpec((None, 2 * half, hid), lambda a, bb, g: (a, 0, 0)),
            pl.BlockSpec((None, hid, HEAD_DIM), lambda a, bb, g: (a, 0, 0)),
        ],
        out_specs=pl.BlockSpec((None, None, None, nC, HEAD_DIM), lambda a, bb, g: (a, bb, g, 0, 0)),
        compiler_params=_cparams(("arbitrary", "arbitrary", "arbitrary"), VMEM_LIMIT),
        name="compress_mlp",
    )(xkv, pos, w1, w2)


def _cmp_attn_kernel(q_ref, kc_ref, vc_ref, ov_ref, o_ref, sel_ref, *, n_slc, n_sel):
    qi = pl.program_id(2)
    tq = q_ref.shape[0]
    nc = kc_ref.shape[0]
    t = qi * tq + lax.broadcasted_iota(jnp.int32, (tq, nc), 0)
    c = lax.broadcasted_iota(jnp.int32, (tq, nc), 1)
    mask = (c * CMP_STRIDE + (CMP_LEN - 1)) <= t
    kc = kc_ref[...]
    vc = vc_ref[...]
    ps = jnp.zeros((tq, nc), F32)
    for h in range(NSA_REP):
        s = _qk(q_ref[:, h * HEAD_DIM:(h + 1) * HEAD_DIM], kc)
        p, _, safe = _masked_softmax(s, mask)
        p = p / safe
        o = jnp.dot(p.astype(BF16), vc, preferred_element_type=F32)
        o_ref[:, h * HEAD_DIM:(h + 1) * HEAD_DIM] = o.astype(o_ref.dtype)
        ps = ps + p
    imp = jnp.dot(ps.astype(BF16), ov_ref[...], preferred_element_type=F32)
    tj = qi * tq + lax.broadcasted_iota(jnp.int32, (tq, LANES), 0)
    j = lax.broadcasted_iota(jnp.int32, (tq, LANES), 1)
    cur = tj // SEL_LEN
    forced = (j == 0) | (j == cur) | (j == cur - 1)
    imp = jnp.where(forced, FORCE, jnp.where(j > cur, NEG, imp))
    imp_t = imp.T[0:n_slc, :]
    grp = 8
    groups = [imp_t[a:a + grp, :] for a in range(0, n_slc, grp)]
    ranks = [jnp.zeros(gv.shape, jnp.int32) for gv in groups]
    for j2 in range(n_slc):
        row = imp_t[j2:j2 + 1, :]
        for gi, gv in enumerate(groups):
            lo = gi * grp
            if lo > j2:
                ahead = row >= gv
            elif lo + gv.shape[0] - 1 <= j2:
                ahead = row > gv
            else:
                later = lax.broadcasted_iota(jnp.int32, gv.shape, 0) + lo > j2
                ahead = (row > gv) | ((row == gv) & later)
            ranks[gi] = jnp.where(ahead, ranks[gi] + 1, ranks[gi])
    rank = jnp.concatenate(ranks, axis=0)
    sel_t = jnp.where(rank < n_sel, 0.0, NEG)
    if n_slc < LANES:
        sel_t = jnp.concatenate([sel_t, jnp.zeros((LANES - n_slc, tq), F32)], axis=0)
    sel_ref[...] = sel_t.T.astype(sel_ref.dtype)


def _cmp_attention(main, kvc, overlap, q_blk0, tq):
    B, _, S, _ = main.shape
    G = NSA_KV_GROUPS
    nC = kvc.shape[3]
    n_slc = S // SEL_LEN
    n_sel = min(SEL_TOPK, n_slc)
    kern = functools.partial(_cmp_attn_kernel, n_slc=n_slc, n_sel=n_sel)
    return pl.pallas_call(
        kern,
        out_shape=(jax.ShapeDtypeStruct((B, S, NSA_HEADS * HEAD_DIM), BF16),
                   jax.ShapeDtypeStruct((B, G, S, LANES), BF16)),
        grid=(B, G, S // tq),
        in_specs=[
            pl.BlockSpec((None, None, tq, GROUP_W), lambda bb, g, i: (bb, 0, i, q_blk0 + g)),
            pl.BlockSpec((None, None, None, nC, HEAD_DIM), lambda bb, g, i: (0, bb, g, 0, 0)),
            pl.BlockSpec((None, None, None, nC, HEAD_DIM), lambda bb, g, i: (1, bb, g, 0, 0)),
            pl.BlockSpec((nC, LANES), lambda bb, g, i: (0, 0)),
        ],
        out_specs=(pl.BlockSpec((None, tq, GROUP_W), lambda bb, g, i: (bb, i, g)),
                   pl.BlockSpec((None, None, tq, LANES), lambda bb, g, i: (bb, g, i, 0))),
        compiler_params=_cparams(("parallel", "arbitrary", "arbitrary"), VMEM_LIMIT),
        name="cmp_attn_select",
    )(main, kvc, kvc, overlap)


def _fold_lanes(x, op):
    out = x[:, 0:LANES]
    for u in range(1, x.shape[1] // LANES):
        out = op(out, x[:, u * LANES:(u + 1) * LANES])
    return out


def _sel_attn_kernel(q_ref, k_ref, v_ref, bias_ref, et_ref, wu_ref, wd_ref,
                     o_ref, wu_out, wd_out, qx_sc, s_sc, m_sc, l_sc, acc_sc):
    wu_out[...] = wu_ref[...].astype(wu_out.dtype)
    wd_out[...] = wd_ref[...].astype(wd_out.dtype)
    qi = pl.program_id(2)
    t = q_ref.shape[0]
    tc = s_sc.shape[2]
    rows = NSA_REP * t
    for h in range(NSA_REP):
        qx_sc[h * t:(h + 1) * t, 0:HEAD_DIM] = q_ref[:, h * HEAD_DIM:(h + 1) * HEAD_DIM]
        qx_sc[h * t:(h + 1) * t, HEAD_DIM:2 * HEAD_DIM] = bias_ref[...]

    def scores(c):
        start = pl.multiple_of(c * tc, tc)
        kx = jnp.concatenate([k_ref[pl.ds(start, tc), :], et_ref[pl.ds(start, tc), :]], axis=1)
        return _qk(qx_sc[...], kx)

    m_sc[...] = jnp.full(m_sc.shape, NEG, F32)

    def max_pass(c, carry):
        s = scores(c)
        s_sc[c] = s
        m_sc[...] = jnp.maximum(m_sc[...], _fold_lanes(s, jnp.maximum))
        return carry

    n_full = (qi * t) // tc
    lax.fori_loop(0, n_full, max_pass, 0)
    qpos = qi * t + lax.broadcasted_iota(jnp.int32, (rows, tc), 0) % t
    kpos = n_full * tc + lax.broadcasted_iota(jnp.int32, (rows, tc), 1)
    s = jnp.where(kpos <= qpos, scores(n_full), NEG)
    s_sc[n_full] = s
    m = jnp.maximum(m_sc[...], _fold_lanes(s, jnp.maximum)).max(-1, keepdims=True)
    m_sc[...] = jnp.broadcast_to(m, m_sc.shape)
    l_sc[...] = jnp.zeros(l_sc.shape, F32)
    acc_sc[...] = jnp.zeros(acc_sc.shape, F32)

    def exp_pass(c, carry):
        mb = m_sc[...]
        sc = s_sc[c]
        p = jnp.concatenate([jnp.exp2(sc[:, u * LANES:(u + 1) * LANES] - mb) for u in range(tc // LANES)], axis=1)
        l_sc[...] += _fold_lanes(p, jnp.add)
        v = v_ref[pl.ds(pl.multiple_of(c * tc, tc), tc), :]
        acc_sc[...] += jnp.dot(p.astype(BF16), v, preferred_element_type=F32)
        return carry

    lax.fori_loop(0, n_full + 1, exp_pass, 0)
    l = l_sc[...].sum(-1, keepdims=True)
    o = acc_sc[...] / jnp.where(l > 0, l, 1.0)
    for h in range(NSA_REP):
        o_ref[:, h * HEAD_DIM:(h + 1) * HEAD_DIM] = o[h * t:(h + 1) * t].astype(o_ref.dtype)


def _sel_attention(main, bias, onehot_t, w_up, w_down, q_blk0, k_tile0, v_tile0, t, tc):
    B, _, S, _ = main.shape
    G = NSA_KV_GROUPS
    t = min(t, S)
    tc = min(tc, S)
    rows = NSA_REP * t
    nq = S // t
    n_steps = B * G * nq
    wu2 = w_up.reshape(-1, w_up.shape[-1])
    wd2 = w_down.reshape(-1, w_down.shape[-1])
    ru, rd = wu2.shape[0] // n_steps, wd2.shape[0] // n_steps
    assert ru * n_steps == wu2.shape[0] and rd * n_steps == wd2.shape[0]

    def slab(bb, g, i):
        return ((bb * G + g) * nq + i, 0)

    o, wu_b, wd_b = pl.pallas_call(
        _sel_attn_kernel,
        out_shape=(jax.ShapeDtypeStruct((B, S, NSA_HEADS * HEAD_DIM), BF16),
                   jax.ShapeDtypeStruct(wu2.shape, BF16), jax.ShapeDtypeStruct(wd2.shape, BF16)),
        grid=(B, G, nq),
        in_specs=[
            pl.BlockSpec((None, None, t, GROUP_W), lambda bb, g, i: (bb, 0, i, q_blk0 + g)),
            pl.BlockSpec((None, None, S, HEAD_DIM), lambda bb, g, i: (bb, 0, 0, k_tile0 + g)),
            pl.BlockSpec((None, None, S, HEAD_DIM), lambda bb, g, i: (bb, 0, 0, v_tile0 + g)),
            pl.BlockSpec((None, None, t, LANES), lambda bb, g, i: (bb, g, i, 0)),
            pl.BlockSpec((S, LANES), lambda bb, g, i: (0, 0)),
            pl.BlockSpec((ru, wu2.shape[1]), slab),
            pl.BlockSpec((rd, wd2.shape[1]), slab),
        ],
        out_specs=(pl.BlockSpec((None, t, GROUP_W), lambda bb, g, i: (bb, i, g)),
                   pl.BlockSpec((ru, wu2.shape[1]), slab), pl.BlockSpec((rd, wd2.shape[1]), slab)),
        scratch_shapes=[pltpu.VMEM((rows, 2 * HEAD_DIM), BF16), pltpu.VMEM((S // tc, rows, tc), F32),
                        pltpu.VMEM((rows, LANES), F32), pltpu.VMEM((rows, LANES), F32),
                        pltpu.VMEM((rows, HEAD_DIM), F32)],
        compiler_params=_cparams(("parallel", "arbitrary", "arbitrary"), VMEM_LIMIT),
        name="selected_attn",
    )(main, main, main, bias, onehot_t, wu2, wd2)
    return o, wu_b.reshape(w_up.shape), wd_b.reshape(w_down.shape)


def _band_kernel(*refs, n_prev, n_sub, max_dist, kv_heads, with_lse):
    q_ref = refs[0]
    k_refs = refs[1:2 + n_prev]
    v_refs = refs[2 + n_prev:3 + 2 * n_prev]
    o_ref = refs[3 + 2 * n_prev]
    qi = pl.program_id(2)
    t = k_refs[0].shape[0]
    nk = (n_prev + 1) * t
    n_heads = q_ref.shape[1] // HEAD_DIM
    k_all = jnp.concatenate([r[...] for r in k_refs], axis=0)
    v_all = jnp.concatenate([r[...] for r in v_refs], axis=0)
    kcol = lax.broadcasted_iota(jnp.int32, (t, nk), 1)
    diff = n_prev * t + lax.broadcasted_iota(jnp.int32, (t, nk), 0) - kcol
    in_band = (diff >= 0) & (diff <= max_dist)
    lane = lax.broadcasted_iota(jnp.int32, (t, LANES), 1)
    for u in range(n_sub):
        first_key = (qi * n_sub + u - n_prev) * t
        bias = jnp.where(in_band & (kcol + first_key >= 0), 0.0, NEG)
        rows = slice(u * t, (u + 1) * t)
        k_u = k_all[u * t:u * t + nk]
        v_u = v_all[u * t:u * t + nk]
        lse = jnp.zeros((t, LANES), F32)
        if kv_heads == 1:
            q = jnp.concatenate([q_ref[rows, h * HEAD_DIM:(h + 1) * HEAD_DIM] for h in range(n_heads)], axis=0)
            s = _qk(q, k_u).reshape(n_heads, t, nk) + bias[None]
            m = s.max(-1, keepdims=True)
            p = jnp.exp2(s - m)
            l = p.sum(-1, keepdims=True)
            o = jnp.dot(p.astype(BF16).reshape(n_heads * t, nk), v_u, preferred_element_type=F32)
            o = o.reshape(n_heads, t, HEAD_DIM) / l
            for h in range(n_heads):
                o_ref[rows, h * HEAD_DIM:(h + 1) * HEAD_DIM] = o[h].astype(o_ref.dtype)
                lse = jnp.where(lane == h, m[h] + jnp.log2(l[h]), lse)
        else:
            for h in range(n_heads):
                cols = slice(h * HEAD_DIM, (h + 1) * HEAD_DIM)
                s = _qk(q_ref[rows, cols], k_u[:, cols]) + bias
                m = s.max(-1, keepdims=True)
                p = jnp.exp2(s - m)
                l = p.sum(-1, keepdims=True)
                o = jnp.dot(p.astype(BF16), v_u[:, cols], preferred_element_type=F32)
                o_ref[rows, cols] = (o / l).astype(o_ref.dtype)
                lse = jnp.where(lane == h, m + jnp.log2(l), lse)
        if with_lse:
            refs[4 + 2 * n_prev][rows, :] = lse


def _band_attention(src, lead_grid, length, q_map, k_map, v_map, kv_width, out_shape, o_map,
                    t, n_sub, max_dist, with_lse):
    t = min(t, length)
    n_prev = -(-max_dist // t)
    n_sub = min(n_sub, length // t)
    tile = n_sub * t
    kern = functools.partial(_band_kernel, n_prev=n_prev, n_sub=n_sub, max_dist=max_dist,
                             kv_heads=kv_width // HEAD_DIM, with_lse=with_lse)

    def preceding(fn, j):
        def index_map(bb, a, i):
            return fn(bb, a, jnp.maximum(i * n_sub - n_prev + j, 0))
        return index_map

    def kv_specs(fn):
        return ([pl.BlockSpec((None, None, t, kv_width), preceding(fn, j)) for j in range(n_prev)]
                + [pl.BlockSpec((None, None, tile, kv_width), fn)])

    in_specs = [pl.BlockSpec((None, None, tile, GROUP_W), q_map)] + kv_specs(k_map) + kv_specs(v_map)
    o_spec = pl.BlockSpec((None, tile, GROUP_W), o_map)
    if with_lse:
        lse_shape = out_shape[:-1] + (out_shape[-1] // GROUP_W * LANES,)
        out_shapes = (jax.ShapeDtypeStruct(out_shape, BF16), jax.ShapeDtypeStruct(lse_shape, F32))
        out_specs = (o_spec, pl.BlockSpec((None, tile, LANES), o_map))
    else:
        out_shapes = jax.ShapeDtypeStruct(out_shape, BF16)
        out_specs = o_spec
    return pl.pallas_call(
        kern, out_shape=out_shapes, grid=lead_grid + (length // tile,), in_specs=in_specs, out_specs=out_specs,
        compiler_params=_cparams(("parallel", "arbitrary", "arbitrary"), VMEM_LIMIT),
        name="band_attn",
    )(*([src] * (3 + 2 * n_prev)))


def _layer_norm(z, g, b):
    mu = z.mean(-1, keepdims=True)
    zc = z - mu
    var = (zc * zc).mean(-1, keepdims=True)
    return zc * lax.rsqrt(var + LN_EPS) * g + b


def _merge_kernel(ocmp_ref, oslc_ref, owin_ref, gl_ref, ga_ref, gb_ref,
                  d0_ref, d1_ref, d2_ref, l0_ref, l1_ref, l2_ref, x_ref,
                  wa_ref, wb_ref, wo_ref, g_ref, b_ref, wr_ref, br_ref, hf_ref, hp_ref, lg_ref, *, alpha):
    tt = x_ref.shape[0]
    gates = jax.nn.sigmoid(gl_ref[...].astype(F32))
    parts = []
    for h in range(NSA_HEADS):
        sl = slice(h * HEAD_DIM, (h + 1) * HEAD_DIM)
        acc = jnp.zeros((tt, HEAD_DIM), F32)
        for br, ref in enumerate((ocmp_ref, oslc_ref, owin_ref)):
            gcol = gates[:, 3 * h + br:3 * h + br + 1]
            acc = acc + gcol * ref[:, sl].astype(F32)
        parts.append(acc.astype(BF16))
    o_nsa = jnp.concatenate(parts, axis=1)
    l0, l1, l2 = l0_ref[...], l1_ref[...], l2_ref[...]
    lm = jnp.maximum(jnp.maximum(l0, l1), l2)
    e0, e1, e2 = jnp.exp2(l0 - lm), jnp.exp2(l1 - lm), jnp.exp2(l2 - lm)
    inv = 1.0 / (e0 + e1 + e2)
    w0, w1, w2 = e0 * inv, e1 * inv, e2 * inv
    parts = []
    for h in range(DIL_HEADS):
        sl = slice(h * HEAD_DIM, (h + 1) * HEAD_DIM)
        parts.append(w0[:, h:h + 1] * d0_ref[:, sl].astype(F32) + w1[:, h:h + 1] * d1_ref[:, sl].astype(F32)
                     + w2[:, h:h + 1] * d2_ref[:, sl].astype(F32))
    o_dil = jnp.concatenate(parts, axis=1)
    y_a = jnp.dot(o_nsa, wa_ref[...], preferred_element_type=F32)
    y_b = jnp.dot(o_dil.astype(BF16), wb_ref[...], preferred_element_type=F32)
    merged = (jax.nn.sigmoid(ga_ref[...].astype(F32)) * y_a
              + jax.nn.sigmoid(gb_ref[...].astype(F32)) * y_b)
    mix = jnp.dot(merged.astype(BF16), wo_ref[...], preferred_element_type=F32)
    h = _layer_norm(alpha * x_ref[...] + mix, g_ref[...], b_ref[...])
    hf_ref[...] = h
    hb = h.astype(BF16)
    hp_ref[...] = _pack_pairs(hb)
    lg_ref[...] = jnp.dot(hb, wr_ref[...], preferred_element_type=F32) + br_ref[...]


def _merge(o_cmp, o_slc, o_win, main2d, gl_tile, ga_blk, gb_blk, dil_o, dil_lse, x2d,
           w_a, w_b, w_o, ln_g, ln_b, w_r, b_r, alpha, tt):
    T, D = x2d.shape
    nsa_w = NSA_HEADS * HEAD_DIM
    row = lambda i: (i, 0)
    const = lambda i: (0, 0)
    in_specs = [
        pl.BlockSpec((tt, nsa_w), row), pl.BlockSpec((tt, nsa_w), row), pl.BlockSpec((tt, nsa_w), row),
        pl.BlockSpec((tt, LANES), lambda i: (i, gl_tile)),
        pl.BlockSpec((tt, D), lambda i: (i, ga_blk)),
        pl.BlockSpec((tt, D), lambda i: (i, gb_blk)),
    ]
    in_specs += [pl.BlockSpec((tt, DIL_W), row)] * 3 + [pl.BlockSpec((tt, LANES), row)] * 3
    in_specs += [
        pl.BlockSpec((tt, D), row),
        pl.BlockSpec((nsa_w, D), const), pl.BlockSpec((DIL_W, D), const), pl.BlockSpec((D, D), const),
        pl.BlockSpec((1, D), const), pl.BlockSpec((1, D), const),
        pl.BlockSpec((D, LANES), const), pl.BlockSpec((1, LANES), const),
    ]
    return pl.pallas_call(
        functools.partial(_merge_kernel, alpha=alpha),
        out_shape=(jax.ShapeDtypeStruct((T, D), F32), jax.ShapeDtypeStruct((T, D // 2), jnp.uint32),
                   jax.ShapeDtypeStruct((T, LANES), F32)),
        grid=(T // tt,),
        in_specs=in_specs,
        out_specs=(pl.BlockSpec((tt, D), row), pl.BlockSpec((tt, D // 2), row), pl.BlockSpec((tt, LANES), row)),
        compiler_params=_cparams(("parallel",), VMEM_LIMIT),
        name="merge_ln1",
    )(o_cmp, o_slc, o_win, main2d, main2d, main2d, *dil_o, *dil_lse, x2d, w_a, w_b, w_o, ln_g, ln_b, w_r, b_r)


def _router_kernel(lg_ref, tri_ref, meta_ref, cnt_ref, carry_sc):
    i = pl.program_id(0)
    tt = lg_ref.shape[0]

    @pl.when(i == 0)
    def _():
        carry_sc[...] = jnp.zeros(carry_sc.shape, F32)

    logits = lg_ref[...]
    lane = lax.broadcasted_iota(jnp.int32, (tt, LANES), 1)
    v = logits
    onehot = jnp.zeros((tt, LANES), F32)
    vals, idxs = [], []
    for _ in range(TOP_K):
        m = v.max(-1, keepdims=True)
        idx = jnp.where(v == m, lane, LANES).min(-1, keepdims=True)
        hit = lane == idx
        vals.append(m)
        idxs.append(idx)
        onehot = onehot + hit.astype(F32)
        v = jnp.where(hit, -jnp.inf, v)
    exps = [jnp.exp(vk - vals[0]) for vk in vals]
    den = exps[0] + exps[1] + exps[2] + exps[3]
    before = jnp.dot(tri_ref[...], onehot.astype(BF16), preferred_element_type=F32) + carry_sc[0:1, :]
    meta = jnp.zeros((tt, LANES), F32)
    for k in range(TOP_K):
        rank = jnp.where(lane == idxs[k], before, 0.0).sum(-1, keepdims=True)
        meta = jnp.where(lane == k, idxs[k].astype(F32), meta)
        meta = jnp.where(lane == TOP_K + k, exps[k] / den, meta)
        meta = jnp.where(lane == 2 * TOP_K + k, rank, meta)
    meta_ref[...] = meta
    carry_sc[...] = carry_sc[...] + jnp.broadcast_to(onehot.sum(0, keepdims=True), carry_sc.shape)
    cnt_ref[...] = carry_sc[...]


def _router(logits, tt):
    T = logits.shape[0]
    tri = (jnp.arange(tt)[:, None] > jnp.arange(tt)[None, :]).astype(BF16)
    return pl.pallas_call(
        _router_kernel,
        out_shape=(jax.ShapeDtypeStruct((T, LANES), F32), jax.ShapeDtypeStruct((8, LANES), F32)),
        grid=(T // tt,),
        in_specs=[
            pl.BlockSpec((tt, LANES), lambda i: (i, 0)),
            pl.BlockSpec((tt, tt), lambda i: (0, 0)),
        ],
        out_specs=(pl.BlockSpec((tt, LANES), lambda i: (i, 0)), pl.BlockSpec((8, LANES), lambda i: (0, 0))),
        scratch_shapes=[pltpu.VMEM((8, LANES), F32)],
        compiler_params=_cparams(("arbitrary",), VMEM_LIMIT),
        name="router_top4",
    )(logits, tri)


def _expert_kernel(be_ref, nu_ref, src_cur, src_next, dst_prev, dst_cur, h_hbm,
                   wg_ref, wl_ref, bg_ref, bl_ref, wd_ref, bd_ref, y_hbm,
                   xg, yb, xb_sc, gsem, ssem, *, th):
    i = pl.program_id(0)
    nu = nu_ref[0]
    tm, half = xb_sc.shape[0], xg.shape[2]
    dh = wg_ref.shape[1]
    slot = i % 2
    other = 1 - slot

    def gather_row(src_ref, r, s):
        return pltpu.make_async_copy(h_hbm.at[pl.ds(src_ref[r], 1), :], xg.at[s, pl.ds(r, 1), :], gsem.at[s])

    def scatter_row(dst_ref, r, s):
        return pltpu.make_async_copy(yb.at[s, pl.ds(r, 1), :], y_hbm.at[pl.ds(dst_ref[r], 1), :], ssem.at[s])

    def wait_gather(s):
        pltpu.make_async_copy(h_hbm.at[pl.ds(0, tm), :], xg.at[s], gsem.at[s]).wait()

    def wait_scatter(s):
        pltpu.make_async_copy(yb.at[s], y_hbm.at[pl.ds(0, tm), :], ssem.at[s]).wait()

    @pl.when(i == 0)
    def _():
        yb[1] = jnp.zeros(yb.shape[1:], yb.dtype)
        for r in range(tm):
            gather_row(src_cur, r, 0).start(priority=r % 2)

    @pl.when(i < nu)
    def _():
        wait_gather(slot)

        @pl.when(i >= 1)
        def _():
            wait_scatter(slot)

        for r in range(tm):
            gather_row(src_next, r, other).start(priority=r % 2)
            scatter_row(dst_prev, r, other).start(priority=(r + 1) % 2)
        lo, hi = _unpack_pairs(xg[slot])
        xb_sc[:, :half] = lo.astype(BF16)
        xb_sc[:, half:] = hi.astype(BF16)
        x = xb_sc[...]
        y = bd_ref[...]
        for c in range(dh // th):
            sl = slice(c * th, (c + 1) * th)
            glu = jnp.dot(x, wg_ref[:, sl], preferred_element_type=F32) + bg_ref[:, sl]
            lin = jnp.dot(x, wl_ref[:, sl], preferred_element_type=F32) + bl_ref[:, sl]
            glu = jnp.minimum(glu, SWIGLU_LIMIT)
            lin = jnp.clip(lin, -SWIGLU_LIMIT, SWIGLU_LIMIT)
            act = glu * jax.nn.sigmoid(SWIGLU_ALPHA * glu) * (lin + 1.0)
            y = y + jnp.dot(act.astype(BF16), wd_ref[sl, :], preferred_element_type=F32)
        yb[slot] = _pack_pairs(y.astype(BF16))

    @pl.when(i == nu - 1)
    def _():
        for r in range(tm):
            scatter_row(dst_cur, r, slot).start(priority=r % 2)
        wait_gather(other)
        wait_scatter(other)
        wait_scatter(slot)


def _experts(h_packed, src_row, dst_row, blk_expert, n_used, w_up, b_up, w_down, b_down, tm, th):
    n_blk = src_row.shape[0] // tm
    n_out = dst_row.shape[0]
    E, D, two_dh = w_up.shape
    dh = two_dh // 2
    th = min(th, dh)
    once = pl.Buffered(1)
    smem = pltpu.SMEM

    grid_spec = pltpu.PrefetchScalarGridSpec(
        num_scalar_prefetch=2,
        grid=(n_blk,),
        in_specs=[
            pl.BlockSpec((tm,), lambda i, be, nu: (0,), memory_space=smem),
            pl.BlockSpec((tm,), lambda i, be, nu: (jnp.minimum(i + 1, nu[0] - 1),), memory_space=smem),
            pl.BlockSpec((tm,), lambda i, be, nu: (i,), memory_space=smem),
            pl.BlockSpec((tm,), lambda i, be, nu: (i + 1,), memory_space=smem),
            pl.BlockSpec(memory_space=pl.ANY),
            pl.BlockSpec((None, D, dh), lambda i, be, nu: (be[i], 0, 0), pipeline_mode=once),
            pl.BlockSpec((None, D, dh), lambda i, be, nu: (be[i], 0, 1), pipeline_mode=once),
            pl.BlockSpec((None, 1, dh), lambda i, be, nu: (be[i], 0, 0)),
            pl.BlockSpec((None, 1, dh), lambda i, be, nu: (be[i], 0, 1)),
            pl.BlockSpec((None, dh, D), lambda i, be, nu: (be[i], 0, 0), pipeline_mode=once),
            pl.BlockSpec((None, 1, D), lambda i, be, nu: (be[i], 0, 0)),
        ],
        out_specs=pl.BlockSpec(memory_space=pl.ANY),
        scratch_shapes=[pltpu.VMEM((2, tm, D // 2), jnp.uint32), pltpu.VMEM((2, tm, D // 2), jnp.uint32),
                        pltpu.VMEM((tm, D), BF16),
                        pltpu.SemaphoreType.DMA((2,)), pltpu.SemaphoreType.DMA((2,))],
    )
    return pl.pallas_call(
        functools.partial(_expert_kernel, th=th),
        out_shape=jax.ShapeDtypeStruct((n_out, D // 2), jnp.uint32),
        grid_spec=grid_spec,
        compiler_params=_cparams(("arbitrary",), VMEM_LIMIT),
        name="moe_experts",
    )(blk_expert, n_used, src_row, src_row, dst_row, dst_row, h_packed, w_up, w_up, b_up, b_up, w_down, b_down)


def _combine_kernel(y0_ref, y1_ref, y2_ref, y3_ref, meta_ref, h_ref, g_ref, b_ref, o_ref, *, alpha):
    ffn_lo = ffn_hi = None
    for k, y_ref in enumerate((y0_ref, y1_ref, y2_ref, y3_ref)):
        lo, hi = _unpack_pairs(y_ref[...])
        gate = meta_ref[:, TOP_K + k:TOP_K + k + 1]
        ffn_lo = gate * lo if k == 0 else ffn_lo + gate * lo
        ffn_hi = gate * hi if k == 0 else ffn_hi + gate * hi
    ffn = jnp.concatenate([ffn_lo, ffn_hi], axis=1)
    o_ref[...] = _layer_norm(alpha * h_ref[...] + ffn, g_ref[...], b_ref[...])


def _combine_ln(y, meta, h, g, b, alpha, tt):
    T, D = h.shape
    nt = T // tt

    def plane(k):
        return pl.BlockSpec((tt, D // 2), lambda i: (k * nt + i, 0))

    return pl.pallas_call(
        functools.partial(_combine_kernel, alpha=alpha),
        out_shape=jax.ShapeDtypeStruct((T, D), F32),
        grid=(nt,),
        in_specs=[plane(k) for k in range(TOP_K)] + [
            pl.BlockSpec((tt, LANES), lambda i: (i, 0)),
            pl.BlockSpec((tt, D), lambda i: (i, 0)),
            pl.BlockSpec((1, D), lambda i: (0, 0)),
            pl.BlockSpec((1, D), lambda i: (0, 0)),
        ],
        out_specs=pl.BlockSpec((tt, D), lambda i: (i, 0)),
        compiler_params=_cparams(("parallel",), VMEM_LIMIT),
        name="moe_combine_ln2",
    )(y, y, y, y, meta, h, g, b)


def _layer(x, w_in, b_in, pos_k, pos_v, ck_w1, ck_w2, cv_w1, cv_w2, w_br_nsa, w_br_dil, w_out,
           ln1_g, ln1_b, w_router, b_router, w_up, b_up, w_down, b_down, ln2_g, ln2_b, alpha):
    B, S, D = x.shape
    T = B * S
    nd = D // LANES
    G = NSA_KV_GROUPS
    kvw = G * HEAD_DIM
    n_exp = w_router.shape[1]

    o_q = 0
    o_kv = NSA_HEADS * HEAD_DIM
    o_gl = o_kv + 6 * kvw
    o_dil = o_gl + 3 * NSA_HEADS
    o_ga = o_dil + 3 * N_DIL * DIL_W
    o_gb = o_ga + D

    def wcols(a, n):
        return w_in[:, a:a + n], b_in[a:a + n]

    def kv(i):
        return wcols(o_kv + i * kvw, kvw)

    tn = 1024
    gl_w, gl_b = wcols(o_gl, 3 * NSA_HEADS)
    pieces = [wcols(o_ga, D), wcols(o_gb, D), wcols(o_q, NSA_HEADS * HEAD_DIM),
              kv(0), kv(2), kv(4), kv(1), kv(3), kv(5), (gl_w, gl_b)]
    used = sum(p[0].shape[1] for p in pieces)
    n_main = -(-used // tn) * tn
    pieces.append((jnp.zeros((D, n_main - used), F32), jnp.zeros((n_main - used,), F32)))
    w_main = jnp.concatenate([p[0] for p in pieces], axis=1).astype(BF16)
    b_main = jnp.concatenate([p[1] for p in pieces])[None, :]
    t_q = 2 * nd
    t_kc, t_ks, t_kw = t_q + 8, t_q + 10, t_q + 12
    t_vc, t_vs, t_vw = t_q + 14, t_q + 16, t_q + 18
    t_gl = t_q + 20
    tile_id = jnp.arange(n_main // LANES)
    flags_main = jnp.where((tile_id >= t_q) & (tile_id < t_kc), ROPE_Q,
                           jnp.where((tile_id >= t_kc) & (tile_id < t_vc), ROPE, PLAIN)).astype(jnp.int32)
    q_blk0 = t_q // NSA_REP

    pos = jnp.arange(S, dtype=F32)
    inv = ROPE_THETA ** (-jnp.arange(0, HEAD_DIM, 2, dtype=F32) / HEAD_DIM)
    ang = pos[:, None] * inv[None, :]
    cosx = jnp.concatenate([jnp.cos(ang), jnp.cos(ang)], axis=-1)
    sinx = jnp.concatenate([-jnp.sin(ang), jnp.sin(ang)], axis=-1)

    main = _project(x, w_main, b_main, flags_main, cosx, sinx, 1, 1024, tn)

    nC = S // CMP_STRIDE
    half = CMP_STRIDE * HEAD_DIM

    def cmp_in(tile):
        a = main[:, 0, :, tile * LANES:(tile + G) * LANES].reshape(B, S, G, HEAD_DIM)
        return a.transpose(0, 2, 1, 3).reshape(B, G, nC, half)

    xkv = jnp.stack([cmp_in(t_kc), cmp_in(t_vc)])
    pos_kv = jnp.stack([pos_k.reshape(2, half), pos_v.reshape(2, half)])
    w1 = jnp.stack([ck_w1, cv_w1]).astype(BF16)
    w2 = jnp.stack([ck_w2, cv_w2]).astype(BF16)
    kvc = _compress(xkv, pos_kv, w1, w2)

    n_slc = S // SEL_LEN
    assert n_slc <= LANES
    c_start = jnp.arange(nC) * CMP_STRIDE
    jb = jnp.arange(LANES)
    overlap = ((c_start[:, None] < (jb[None, :] + 1) * SEL_LEN) & (c_start[:, None] + CMP_LEN > jb[None, :] * SEL_LEN)
               & (jb[None, :] < n_slc) & (c_start[:, None] + CMP_LEN <= S)).astype(BF16)
    tq = min(256, S)
    o_cmp, sel = _cmp_attention(main, kvc, overlap, q_blk0, tq)

    onehot_t = (jnp.arange(S)[:, None] // SEL_LEN == jnp.arange(LANES)[None, :]).astype(BF16)
    o_slc, w_up_b, w_down_b = _sel_attention(main, sel, onehot_t, w_up, w_down, q_blk0, t_ks, t_vs, 256, 512)

    o_win = _band_attention(
        main, (B, G), S,
        lambda bb, g, i: (bb, 0, i, q_blk0 + g),
        lambda bb, g, i: (bb, 0, i, t_kw + g),
        lambda bb, g, i: (bb, 0, i, t_vw + g),
        HEAD_DIM, (B, S, NSA_HEADS * HEAD_DIM), lambda bb, g, i: (bb, i, g), 256, 2, WIN_LEN - 1, False)

    flags_dil = jnp.array([ROPE_Q] * DIL_HEADS + [ROPE] * DIL_HEADS + [PLAIN] * DIL_HEADS, jnp.int32)
    dil_o, dil_lse = [], []
    for gi, (w, d) in enumerate(DIL_CONFIGS):
        wd, bd = wcols(o_dil + gi * 3 * DIL_W, 3 * DIL_W)
        sub = _project(x, wd.astype(BF16), bd[None, :], flags_dil, cosx, sinx, d, 512, 3 * DIL_W)
        L = S // d
        o_g, lse_g = _band_attention(
            sub, (B, d), L,
            lambda bb, r, i: (bb, r, i, 0),
            lambda bb, r, i: (bb, r, i, 1),
            lambda bb, r, i: (bb, r, i, 2),
            DIL_W, (B, L, d * DIL_W), lambda bb, r, i: (bb, i, r), 128, 4, w // d, True)
        dil_o.append(o_g.reshape(T, DIL_W))
        dil_lse.append(lse_g.reshape(T, LANES))

    w_r = jnp.concatenate([w_router, jnp.zeros((D, LANES - n_exp), F32)], axis=1).astype(BF16)
    b_r = jnp.concatenate([b_router, jnp.full((LANES - n_exp,), NEG, F32)])[None, :]
    h_f, h_p, logits = _merge(
        o_cmp.reshape(T, -1), o_slc.reshape(T, -1), o_win.reshape(T, -1), main.reshape(T, n_main),
        t_gl, 0, 1, dil_o, dil_lse, x.reshape(T, D),
        w_br_nsa.astype(BF16), w_br_dil.astype(BF16), w_out.astype(BF16),
        ln1_g[None, :], ln1_b[None, :], w_r, b_r, alpha, min(256, T))

    meta, cnt = _router(logits, min(512, T))
    top_idx = meta[:, 0:TOP_K].astype(jnp.int32)
    gates = meta[:, TOP_K:2 * TOP_K]
    rank = meta[:, 2 * TOP_K:3 * TOP_K].astype(jnp.int32)

    tm = 512 if T * TOP_K >= 512 * n_exp else 128
    counts = cnt[0, :n_exp].astype(jnp.int32)
    padded = (counts + tm - 1) // tm * tm
    pad_end = jnp.cumsum(padded)
    pad_start = pad_end - padded
    dest = (pad_start[top_idx] + rank).reshape(T * TOP_K)
    n_rows = T * TOP_K + n_exp * tm
    n_blk = n_rows // tm
    blk_start = jnp.arange(n_blk, dtype=jnp.int32) * tm
    blk_expert = jnp.minimum((pad_end[None, :] <= blk_start[:, None]).sum(-1), n_exp - 1).astype(jnp.int32)
    n_used = (pad_end[-1:] // tm).astype(jnp.int32)

    n_asg = T * TOP_K
    asg_of_row = jnp.full((n_rows,), -1, jnp.int32).at[dest].set(jnp.arange(n_asg, dtype=jnp.int32))
    is_pad = asg_of_row < 0
    spare = n_asg + tm + jnp.cumsum(is_pad.astype(jnp.int32)) - 1
    src_row = jnp.where(is_pad, 0, asg_of_row // TOP_K)
    dst_row = jnp.where(is_pad, spare, (asg_of_row % TOP_K) * T + asg_of_row // TOP_K)
    dst_row = jnp.concatenate([n_asg + jnp.arange(tm, dtype=jnp.int32), dst_row])

    y = _experts(h_p, src_row, dst_row, blk_expert, n_used,
                 w_up_b, b_up[:, None, :], w_down_b, b_down[:, None, :], tm, 512)
    out = _combine_ln(y, meta, h_f, ln2_g[None, :], ln2_b[None, :], alpha, min(512, T))
    return out.reshape(B, S, D)


def kernel(x, w_in, b_in, cmp_pos_k, cmp_pos_v, cmp_k_w1, cmp_k_w2, cmp_v_w1, cmp_v_w2, w_br_nsa, w_br_dil,
           w_out, ln1_g, ln1_b, w_router, b_router, w_up, b_up, w_down, b_down, ln2_g, ln2_b):
    depth = w_in.shape[0]
    alpha = (2.0 * depth) ** 0.25
    h = x
    for l in range(depth):
        h = _layer(h, w_in[l], b_in[l], cmp_pos_k[l], cmp_pos_v[l], cmp_k_w1[l], cmp_k_w2[l],
                   cmp_v_w1[l], cmp_v_w2[l], w_br_nsa[l], w_br_dil[l], w_out[l], ln1_g[l], ln1_b[l],
                   w_router[l], b_router[l], w_up[l], b_up[l], w_down[l], b_down[l], ln2_g[l], ln2_b[l], alpha)
    return h
```

```python
import functools

import jax
import jax.numpy as jnp
from jax import lax
from jax.experimental import pallas as pl
from jax.experimental.pallas import tpu as pltpu

F32 = jnp.float32
BF16 = jnp.bfloat16

HEAD_DIM = 128
LANES = 128
ROPE_THETA = 10000.0
NSA_HEADS = 8
NSA_KV_GROUPS = 2
NSA_REP = NSA_HEADS // NSA_KV_GROUPS
CMP_LEN = 32
CMP_STRIDE = 16
SEL_LEN = 64
SEL_TOPK = 16
WIN_LEN = 512
DIL_CONFIGS = ((128, 1), (512, 4), (2048, 16))
N_DIL = len(DIL_CONFIGS)
DIL_HEADS = 4
TOP_K = 4
SWIGLU_LIMIT = 7.0
SWIGLU_ALPHA = 1.702
LN_EPS = 1e-5
NEG = -1e30
FORCE = 1e9
SCALE = HEAD_DIM ** -0.5
LOG2E = 1.4426950408889634
Q_SCALE = SCALE * LOG2E
PLAIN, ROPE, ROPE_Q = 0, 1, 2

GROUP_W = NSA_REP * HEAD_DIM
DIL_W = DIL_HEADS * HEAD_DIM
VMEM_LIMIT = 56 * 1024 * 1024


def _cparams(sem, vmem=None):
    return pltpu.CompilerParams(dimension_semantics=sem, vmem_limit_bytes=vmem)


def _masked_softmax(s, mask):
    s = jnp.where(mask, s, NEG)
    m = s.max(-1, keepdims=True)
    p = jnp.where(mask, jnp.exp2(s - m), 0.0)
    den = p.sum(-1, keepdims=True)
    safe = jnp.where(den > 0, den, 1.0)
    return p, m, safe


def _qk(q, k):
    return lax.dot_general(q, k, (((1,), (1,)), ((), ())), preferred_element_type=F32)


def _pack_pairs(xb):
    m = xb.shape[1] // 2
    lo = lax.bitcast_convert_type(xb[:, :m].astype(F32), jnp.uint32)
    hi = lax.bitcast_convert_type(xb[:, m:].astype(F32), jnp.uint32)
    return (lo >> 16) | (hi & jnp.uint32(0xFFFF0000))


def _unpack_pairs(w):
    lo = lax.bitcast_convert_type(w << 16, F32)
    hi = lax.bitcast_convert_type(w & jnp.uint32(0xFFFF0000), F32)
    return lo, hi


def _store_tile_rows(ref, start, packed, nsub):
    n = packed.shape[0]
    for j in range(nsub):
        ref[pl.ds(start + j, n, stride=nsub), :] = packed[:, j * LANES:(j + 1) * LANES]


def _load_tile_rows(ref, start, n, nsub):
    return jnp.concatenate([ref[pl.ds(start + j, n, stride=nsub), :] for j in range(nsub)], axis=1)


def _proj_kernel(flags_ref, x_ref, w_ref, b_ref, cos_ref, sin_ref, o_ref, xb_ref, *xcol_ref):
    j = pl.program_id(2)
    d, sub, tn = o_ref.shape
    n_sub = tn // LANES

    @pl.when(j == 0)
    def _():
        if d == 1:
            xb_ref[...] = x_ref[...].astype(BF16)
        else:
            xcol, = xcol_ref
            for c in range(xcol.shape[0]):
                xcol[c] = x_ref[:, c * LANES:(c + 1) * LANES]
            for c in range(xcol.shape[0]):
                for r in range(d):
                    xb_ref[r * sub:(r + 1) * sub, c * LANES:(c + 1) * LANES] = (
                        xcol[c, pl.ds(r, sub, stride=d), :].astype(BF16))

    acc = jnp.dot(xb_ref[...], w_ref[...], preferred_element_type=F32) + b_ref[...]
    for u in range(n_sub):
        a = acc[:, u * LANES:(u + 1) * LANES]
        roped = a * cos_ref[...] + pltpu.roll(a, HEAD_DIM // 2, 1) * sin_ref[...]
        flag = flags_ref[j * n_sub + u]
        mult = jnp.where(flag == ROPE_Q, Q_SCALE, 1.0).astype(F32)
        res = (jnp.where(flag == PLAIN, a, roped) * mult).astype(o_ref.dtype)
        for r in range(d):
            o_ref[r, :, u * LANES:(u + 1) * LANES] = res[r * sub:(r + 1) * sub]


def _project(x, w, b, flags, cosx, sinx, d, tm, tn):
    B, S, D = x.shape
    N = w.shape[1]
    L = S // d
    tm = min(tm, S)
    sub = tm // d

    def regroup(tab):
        return tab.reshape(S // tm, sub, d, HEAD_DIM).transpose(0, 2, 1, 3).reshape(S, HEAD_DIM)

    grid_spec = pltpu.PrefetchScalarGridSpec(
        num_scalar_prefetch=1,
        grid=(B, S // tm, N // tn),
        in_specs=[
            pl.BlockSpec((None, tm, D), lambda bb, i, j, f: (bb, i, 0)),
            pl.BlockSpec((D, tn), lambda bb, i, j, f: (0, j)),
            pl.BlockSpec((1, tn), lambda bb, i, j, f: (0, j)),
            pl.BlockSpec((tm, HEAD_DIM), lambda bb, i, j, f: (i, 0)),
            pl.BlockSpec((tm, HEAD_DIM), lambda bb, i, j, f: (i, 0)),
        ],
        out_specs=pl.BlockSpec((None, d, sub, tn), lambda bb, i, j, f: (bb, 0, i, j)),
        scratch_shapes=[pltpu.VMEM((tm, D), BF16)] + ([pltpu.VMEM((D // LANES, tm, LANES), F32)] if d > 1 else []),
    )
    return pl.pallas_call(
        _proj_kernel,
        out_shape=jax.ShapeDtypeStruct((B, d, L, N), BF16),
        grid_spec=grid_spec,
        compiler_params=_cparams(("parallel", "arbitrary", "arbitrary"), VMEM_LIMIT),
        name="proj_rope",
    )(flags, x, w, b, regroup(cosx), regroup(sinx))


def _gelu_tanh(x):
    return 0.5 * x * (1.0 + jnp.tanh(0.7978845608028654 * (x + 0.044715 * x * x * x)))


def _compress_kernel(x_ref, pos_ref, w1_ref, w2_ref, o_ref):
    half = x_ref.shape[-1]
    nc = x_ref.shape[0]
    x = x_ref[...].astype(F32)
    lo = (x + pos_ref[0:1, :]).astype(BF16)
    hi = (x + pos_ref[1:2, :]).astype(BF16)
    y_lo = jnp.dot(lo, w1_ref[0:half, :], preferred_element_type=F32)
    y_hi = jnp.dot(hi, w1_ref[half:2 * half, :], preferred_element_type=F32)
    h = y_lo + pltpu.roll(y_hi, nc - 1, 0)
    g = _gelu_tanh(h).astype(BF16)
    o_ref[...] = jnp.dot(g, w2_ref[...], preferred_element_type=F32).astype(o_ref.dtype)


def _compress(xkv, pos, w1, w2):
    _, B, G, nC, half = xkv.shape
    hid = w1.shape[-1]
    return pl.pallas_call(
        _compress_kernel,
        out_shape=jax.ShapeDtypeStruct((2, B, G, nC, HEAD_DIM), BF16),
        grid=(2, B, G),
        in_specs=[
            pl.BlockSpec((None, None, None, nC, half), lambda a, bb, g: (a, bb, g, 0, 0)),
            pl.BlockSpec((None, 2, half), lambda a, bb, g: (a, 0, 0)),
            pl.BlockSpec((None, 2 * half, hid), lambda a, bb, g: (a, 0, 0)),
            pl.BlockSpec((None, hid, HEAD_DIM), lambda a, bb, g: (a, 0, 0)),
        ],
        out_specs=pl.BlockSpec((None, None, None, nC, HEAD_DIM), lambda a, bb, g: (a, bb, g, 0, 0)),
        compiler_params=_cparams(("arbitrary", "arbitrary", "arbitrary"), VMEM_LIMIT),
        name="compress_mlp",
    )(xkv, pos, w1, w2)


def _cmp_attn_kernel(q_ref, kc_ref, vc_ref, ov_ref, o_ref, sel_ref, *, n_slc, n_sel):
    qi = pl.program_id(2)
    tq = q_ref.shape[0]
    nc = kc_ref.shape[0]
    t = qi * tq + lax.broadcasted_iota(jnp.int32, (tq, nc), 0)
    c = lax.broadcasted_iota(jnp.int32, (tq, nc), 1)
    mask = (c * CMP_STRIDE + (CMP_LEN - 1)) <= t
    kc = kc_ref[...]
    vc = vc_ref[...]
    ps = jnp.zeros((tq, nc), F32)
    for h in range(NSA_REP):
        s = _qk(q_ref[:, h * HEAD_DIM:(h + 1) * HEAD_DIM], kc)
        p, _, safe = _masked_softmax(s, mask)
        p = p / safe
        o = jnp.dot(p.astype(BF16), vc, preferred_element_type=F32)
        o_ref[:, h * HEAD_DIM:(h + 1) * HEAD_DIM] = o.astype(o_ref.dtype)
        ps = ps + p
    imp = jnp.dot(ps.astype(BF16), ov_ref[...], preferred_element_type=F32)
    tj = qi * tq + lax.broadcasted_iota(jnp.int32, (tq, LANES), 0)
    j = lax.broadcasted_iota(jnp.int32, (tq, LANES), 1)
    cur = tj // SEL_LEN
    forced = (j == 0) | (j == cur) | (j == cur - 1)
    imp = jnp.where(forced, FORCE, jnp.where(j > cur, NEG, imp))
    imp_t = imp.T[0:n_slc, :]
    grp = 8
    groups = [imp_t[a:a + grp, :] for a in range(0, n_slc, grp)]
    ranks = [jnp.zeros(gv.shape, jnp.int32) for gv in groups]
    for j2 in range(n_slc):
        row = imp_t[j2:j2 + 1, :]
        for gi, gv in enumerate(groups):
            lo = gi * grp
            if lo > j2:
                ahead = row >= gv
            elif lo + gv.shape[0] - 1 <= j2:
                ahead = row > gv
            else:
                later = lax.broadcasted_iota(jnp.int32, gv.shape, 0) + lo > j2
                ahead = (row > gv) | ((row == gv) & later)
            ranks[gi] = jnp.where(ahead, ranks[gi] + 1, ranks[gi])
    rank = jnp.concatenate(ranks, axis=0)
    sel_t = jnp.where(rank < n_sel, 0.0, NEG)
    if n_slc < LANES:
        sel_t = jnp.concatenate([sel_t, jnp.zeros((LANES - n_slc, tq), F32)], axis=0)
    sel_ref[...] = sel_t.T.astype(sel_ref.dtype)


def _cmp_attention(main, kvc, overlap, q_blk0, tq):
    B, _, S, _ = main.shape
    G = NSA_KV_GROUPS
    nC = kvc.shape[3]
    n_slc = S // SEL_LEN
    n_sel = min(SEL_TOPK, n_slc)
    kern = functools.partial(_cmp_attn_kernel, n_slc=n_slc, n_sel=n_sel)
    return pl.pallas_call(
        kern,
        out_shape=(jax.ShapeDtypeStruct((B, S, NSA_HEADS * HEAD_DIM), BF16),
                   jax.ShapeDtypeStruct((B, G, S, LANES), BF16)),
        grid=(B, G, S // tq),
        in_specs=[
            pl.BlockSpec((None, None, tq, GROUP_W), lambda bb, g, i: (bb, 0, i, q_blk0 + g)),
            pl.BlockSpec((None, None, None, nC, HEAD_DIM), lambda bb, g, i: (0, bb, g, 0, 0)),
            pl.BlockSpec((None, None, None, nC, HEAD_DIM), lambda bb, g, i: (1, bb, g, 0, 0)),
            pl.BlockSpec((nC, LANES), lambda bb, g, i: (0, 0)),
        ],
        out_specs=(pl.BlockSpec((None, tq, GROUP_W), lambda bb, g, i: (bb, i, g)),
                   pl.BlockSpec((None, None, tq, LANES), lambda bb, g, i: (bb, g, i, 0))),
        compiler_params=_cparams(("parallel", "arbitrary", "arbitrary"), VMEM_LIMIT),
        name="cmp_attn_select",
    )(main, kvc, kvc, overlap)


def _fold_lanes(x, op):
    out = x[:, 0:LANES]
    for u in range(1, x.shape[1] // LANES):
        out = op(out, x[:, u * LANES:(u + 1) * LANES])
    return out


def _sel_attn_kernel(q_ref, k_ref, v_ref, bias_ref, et_ref, wu_ref, wd_ref,
                     o_ref, wu_out, wd_out, qx_sc, s_sc, m_sc, l_sc, acc_sc):
    wu_out[...] = wu_ref[...].astype(wu_out.dtype)
    wd_out[...] = wd_ref[...].astype(wd_out.dtype)
    qi = pl.program_id(2)
    t = q_ref.shape[0]
    tc = s_sc.shape[2]
    rows = NSA_REP * t
    for h in range(NSA_REP):
        qx_sc[h * t:(h + 1) * t, 0:HEAD_DIM] = q_ref[:, h * HEAD_DIM:(h + 1) * HEAD_DIM]
        qx_sc[h * t:(h + 1) * t, HEAD_DIM:2 * HEAD_DIM] = bias_ref[...]

    def scores(c):
        start = pl.multiple_of(c * tc, tc)
        kx = jnp.concatenate([k_ref[pl.ds(start, tc), :], et_ref[pl.ds(start, tc), :]], axis=1)
        return _qk(qx_sc[...], kx)

    m_sc[...] = jnp.full(m_sc.shape, NEG, F32)

    def max_pass(c, carry):
        s = scores(c)
        s_sc[c] = s
        m_sc[...] = jnp.maximum(m_sc[...], _fold_lanes(s, jnp.maximum))
        return carry

    n_full = (qi * t) // tc
    lax.fori_loop(0, n_full, max_pass, 0)
    qpos = qi * t + lax.broadcasted_iota(jnp.int32, (rows, tc), 0) % t
    kpos = n_full * tc + lax.broadcasted_iota(jnp.int32, (rows, tc), 1)
    s = jnp.where(kpos <= qpos, scores(n_full), NEG)
    s_sc[n_full] = s
    m = jnp.maximum(m_sc[...], _fold_lanes(s, jnp.maximum)).max(-1, keepdims=True)
    m_sc[...] = jnp.broadcast_to(m, m_sc.shape)
    l_sc[...] = jnp.zeros(l_sc.shape, F32)
    acc_sc[...] = jnp.zeros(acc_sc.shape, F32)

    def exp_pass(c, carry):
        mb = m_sc[...]
        sc = s_sc[c]
        p = jnp.concatenate([jnp.exp2(sc[:, u * LANES:(u + 1) * LANES] - mb) for u in range(tc // LANES)], axis=1)
        l_sc[...] += _fold_lanes(p, jnp.add)
        v = v_ref[pl.ds(pl.multiple_of(c * tc, tc), tc), :]
        acc_sc[...] += jnp.dot(p.astype(BF16), v, preferred_element_type=F32)
        return carry

    lax.fori_loop(0, n_full + 1, exp_pass, 0)
    l = l_sc[...].sum(-1, keepdims=True)
    o = acc_sc[...] / jnp.where(l > 0, l, 1.0)
    for h in range(NSA_REP):
        o_ref[:, h * HEAD_DIM:(h + 1) * HEAD_DIM] = o[h * t:(h + 1) * t].astype(o_ref.dtype)


def _sel_attention(main, bias, onehot_t, w_up, w_down, q_blk0, k_tile0, v_tile0, t, tc):
    B, _, S, _ = main.shape
    G = NSA_KV_GROUPS
    t = min(t, S)
    tc = min(tc, S)
    rows = NSA_REP * t
    nq = S // t
    n_steps = B * G * nq
    wu2 = w_up.reshape(-1, w_up.shape[-1])
    wd2 = w_down.reshape(-1, w_down.shape[-1])
    ru, rd = wu2.shape[0] // n_steps, wd2.shape[0] // n_steps
    assert ru * n_steps == wu2.shape[0] and rd * n_steps == wd2.shape[0]

    def slab(bb, g, i):
        return ((bb * G + g) * nq + i, 0)

    o, wu_b, wd_b = pl.pallas_call(
        _sel_attn_kernel,
        out_shape=(jax.ShapeDtypeStruct((B, S, NSA_HEADS * HEAD_DIM), BF16),
                   jax.ShapeDtypeStruct(wu2.shape, BF16), jax.ShapeDtypeStruct(wd2.shape, BF16)),
        grid=(B, G, nq),
        in_specs=[
            pl.BlockSpec((None, None, t, GROUP_W), lambda bb, g, i: (bb, 0, i, q_blk0 + g)),
            pl.BlockSpec((None, None, S, HEAD_DIM), lambda bb, g, i: (bb, 0, 0, k_tile0 + g)),
            pl.BlockSpec((None, None, S, HEAD_DIM), lambda bb, g, i: (bb, 0, 0, v_tile0 + g)),
            pl.BlockSpec((None, None, t, LANES), lambda bb, g, i: (bb, g, i, 0)),
            pl.BlockSpec((S, LANES), lambda bb, g, i: (0, 0)),
            pl.BlockSpec((ru, wu2.shape[1]), slab),
            pl.BlockSpec((rd, wd2.shape[1]), slab),
        ],
        out_specs=(pl.BlockSpec((None, t, GROUP_W), lambda bb, g, i: (bb, i, g)),
                   pl.BlockSpec((ru, wu2.shape[1]), slab), pl.BlockSpec((rd, wd2.shape[1]), slab)),
        scratch_shapes=[pltpu.VMEM((rows, 2 * HEAD_DIM), BF16), pltpu.VMEM((S // tc, rows, tc), F32),
                        pltpu.VMEM((rows, LANES), F32), pltpu.VMEM((rows, LANES), F32),
                        pltpu.VMEM((rows, HEAD_DIM), F32)],
        compiler_params=_cparams(("parallel", "arbitrary", "arbitrary"), VMEM_LIMIT),
        name="selected_attn",
    )(main, main, main, bias, onehot_t, wu2, wd2)
    return o, wu_b.reshape(w_up.shape), wd_b.reshape(w_down.shape)


def _band_kernel(*refs, n_prev, n_sub, max_dist, kv_heads, with_lse):
    q_ref = refs[0]
    k_refs = refs[1:2 + n_prev]
    v_refs = refs[2 + n_prev:3 + 2 * n_prev]
    o_ref = refs[3 + 2 * n_prev]
    qi = pl.program_id(2)
    t = k_refs[0].shape[0]
    nk = (n_prev + 1) * t
    n_heads = q_ref.shape[1] // HEAD_DIM
    k_all = jnp.concatenate([r[...] for r in k_refs], axis=0)
    v_all = jnp.concatenate([r[...] for r in v_refs], axis=0)
    kcol = lax.broadcasted_iota(jnp.int32, (t, nk), 1)
    diff = n_prev * t + lax.broadcasted_iota(jnp.int32, (t, nk), 0) - kcol
    in_band = (diff >= 0) & (diff <= max_dist)
    lane = lax.broadcasted_iota(jnp.int32, (t, LANES), 1)
    for u in range(n_sub):
        first_key = (qi * n_sub + u - n_prev) * t
        bias = jnp.where(in_band & (kcol + first_key >= 0), 0.0, NEG)
        rows = slice(u * t, (u + 1) * t)
        k_u = k_all[u * t:u * t + nk]
        v_u = v_all[u * t:u * t + nk]
        lse = jnp.zeros((t, LANES), F32)
        if kv_heads == 1:
            q = jnp.concatenate([q_ref[rows, h * HEAD_DIM:(h + 1) * HEAD_DIM] for h in range(n_heads)], axis=0)
            s = _qk(q, k_u).reshape(n_heads, t, nk) + bias[None]
            m = s.max(-1, keepdims=True)
            p = jnp.exp2(s - m)
            l = p.sum(-1, keepdims=True)
            o = jnp.dot(p.astype(BF16).reshape(n_heads * t, nk), v_u, preferred_element_type=F32)
            o = o.reshape(n_heads, t, HEAD_DIM) / l
            for h in range(n_heads):
                o_ref[rows, h * HEAD_DIM:(h + 1) * HEAD_DIM] = o[h].astype(o_ref.dtype)
                lse = jnp.where(lane == h, m[h] + jnp.log2(l[h]), lse)
        else:
            for h in range(n_heads):
                cols = slice(h * HEAD_DIM, (h + 1) * HEAD_DIM)
                s = _qk(q_ref[rows, cols], k_u[:, cols]) + bias
                m = s.max(-1, keepdims=True)
                p = jnp.exp2(s - m)
                l = p.sum(-1, keepdims=True)
                o = jnp.dot(p.astype(BF16), v_u[:, cols], preferred_element_type=F32)
                o_ref[rows, cols] = (o / l).astype(o_ref.dtype)
                lse = jnp.where(lane == h, m + jnp.log2(l), lse)
        if with_lse:
            refs[4 + 2 * n_prev][rows, :] = lse


def _band_attention(src, lead_grid, length, q_map, k_map, v_map, kv_width, out_shape, o_map,
                    t, n_sub, max_dist, with_lse):
    t = min(t, length)
    n_prev = -(-max_dist // t)
    n_sub = min(n_sub, length // t)
    tile = n_sub * t
    kern = functools.partial(_band_kernel, n_prev=n_prev, n_sub=n_sub, max_dist=max_dist,
                             kv_heads=kv_width // HEAD_DIM, with_lse=with_lse)

    def preceding(fn, j):
        def index_map(bb, a, i):
            return fn(bb, a, jnp.maximum(i * n_sub - n_prev + j, 0))
        return index_map

    def kv_specs(fn):
        return ([pl.BlockSpec((None, None, t, kv_width), preceding(fn, j)) for j in range(n_prev)]
                + [pl.BlockSpec((None, None, tile, kv_width), fn)])

    in_specs = [pl.BlockSpec((None, None, tile, GROUP_W), q_map)] + kv_specs(k_map) + kv_specs(v_map)
    o_spec = pl.BlockSpec((None, tile, GROUP_W), o_map)
    if with_lse:
        lse_shape = out_shape[:-1] + (out_shape[-1] // GROUP_W * LANES,)
        out_shapes = (jax.ShapeDtypeStruct(out_shape, BF16), jax.ShapeDtypeStruct(lse_shape, F32))
        out_specs = (o_spec, pl.BlockSpec((None, tile, LANES), o_map))
    else:
        out_shapes = jax.ShapeDtypeStruct(out_shape, BF16)
        out_specs = o_spec
    return pl.pallas_call(
        kern, out_shape=out_shapes, grid=lead_grid + (length // tile,), in_specs=in_specs, out_specs=out_specs,
        compiler_params=_cparams(("parallel", "arbitrary", "arbitrary"), VMEM_LIMIT),
        name="band_attn",
    )(*([src] * (3 + 2 * n_prev)))


def _layer_norm(z, g, b):
    mu = z.mean(-1, keepdims=True)
    zc = z - mu
    var = (zc * zc).mean(-1, keepdims=True)
    return zc * lax.rsqrt(var + LN_EPS) * g + b


def _merge_kernel(ocmp_ref, oslc_ref, owin_ref, gl_ref, ga_ref, gb_ref,
                  d0_ref, d1_ref, d2_ref, l0_ref, l1_ref, l2_ref, x_ref,
                  wa_ref, wb_ref, wo_ref, g_ref, b_ref, wr_ref, br_ref, hf_ref, hp_ref, lg_ref, *, alpha):
    tt = x_ref.shape[0]
    gates = jax.nn.sigmoid(gl_ref[...].astype(F32))
    parts = []
    for h in range(NSA_HEADS):
        sl = slice(h * HEAD_DIM, (h + 1) * HEAD_DIM)
        acc = jnp.zeros((tt, HEAD_DIM), F32)
        for br, ref in enumerate((ocmp_ref, oslc_ref, owin_ref)):
            gcol = gates[:, 3 * h + br:3 * h + br + 1]
            acc = acc + gcol * ref[:, sl].astype(F32)
        parts.append(acc.astype(BF16))
    o_nsa = jnp.concatenate(parts, axis=1)
    l0, l1, l2 = l0_ref[...], l1_ref[...], l2_ref[...]
    lm = jnp.maximum(jnp.maximum(l0, l1), l2)
    e0, e1, e2 = jnp.exp2(l0 - lm), jnp.exp2(l1 - lm), jnp.exp2(l2 - lm)
    inv = 1.0 / (e0 + e1 + e2)
    w0, w1, w2 = e0 * inv, e1 * inv, e2 * inv
    parts = []
    for h in range(DIL_HEADS):
        sl = slice(h * HEAD_DIM, (h + 1) * HEAD_DIM)
        parts.append(w0[:, h:h + 1] * d0_ref[:, sl].astype(F32) + w1[:, h:h + 1] * d1_ref[:, sl].astype(F32)
                     + w2[:, h:h + 1] * d2_ref[:, sl].astype(F32))
    o_dil = jnp.concatenate(parts, axis=1)
    y_a = jnp.dot(o_nsa, wa_ref[...], preferred_element_type=F32)
    y_b = jnp.dot(o_dil.astype(BF16), wb_ref[...], preferred_element_type=F32)
    merged = (jax.nn.sigmoid(ga_ref[...].astype(F32)) * y_a
              + jax.nn.sigmoid(gb_ref[...].astype(F32)) * y_b)
    mix = jnp.dot(merged.astype(BF16), wo_ref[...], preferred_element_type=F32)
    h = _layer_norm(alpha * x_ref[...] + mix, g_ref[...], b_ref[...])
    hf_ref[...] = h
    hb = h.astype(BF16)
    _store_tile_rows(hp_ref, 0, _pack_pairs(hb), hp_ref.shape[0] // tt)
    lg_ref[...] = jnp.dot(hb, wr_ref[...], preferred_element_type=F32) + br_ref[...]


def _merge(o_cmp, o_slc, o_win, main2d, gl_tile, ga_blk, gb_blk, dil_o, dil_lse, x2d,
           w_a, w_b, w_o, ln_g, ln_b, w_r, b_r, alpha, tt):
    T, D = x2d.shape
    nsub = D // 2 // LANES
    nsa_w = NSA_HEADS * HEAD_DIM
    row = lambda i: (i, 0)
    const = lambda i: (0, 0)
    in_specs = [
        pl.BlockSpec((tt, nsa_w), row), pl.BlockSpec((tt, nsa_w), row), pl.BlockSpec((tt, nsa_w), row),
        pl.BlockSpec((tt, LANES), lambda i: (i, gl_tile)),
        pl.BlockSpec((tt, D), lambda i: (i, ga_blk)),
        pl.BlockSpec((tt, D), lambda i: (i, gb_blk)),
    ]
    in_specs += [pl.BlockSpec((tt, DIL_W), row)] * 3 + [pl.BlockSpec((tt, LANES), row)] * 3
    in_specs += [
        pl.BlockSpec((tt, D), row),
        pl.BlockSpec((nsa_w, D), const), pl.BlockSpec((DIL_W, D), const), pl.BlockSpec((D, D), const),
        pl.BlockSpec((1, D), const), pl.BlockSpec((1, D), const),
        pl.BlockSpec((D, LANES), const), pl.BlockSpec((1, LANES), const),
    ]
    return pl.pallas_call(
        functools.partial(_merge_kernel, alpha=alpha),
        out_shape=(jax.ShapeDtypeStruct((T, D), F32), jax.ShapeDtypeStruct((T * nsub, LANES), jnp.uint32),
                   jax.ShapeDtypeStruct((T, LANES), F32)),
        grid=(T // tt,),
        in_specs=in_specs,
        out_specs=(pl.BlockSpec((tt, D), row), pl.BlockSpec((tt * nsub, LANES), row), pl.BlockSpec((tt, LANES), row)),
        compiler_params=_cparams(("parallel",), VMEM_LIMIT),
        name="merge_ln1",
    )(o_cmp, o_slc, o_win, main2d, main2d, main2d, *dil_o, *dil_lse, x2d, w_a, w_b, w_o, ln_g, ln_b, w_r, b_r)


def _router_kernel(lg_ref, tri_ref, meta_ref, cnt_ref, carry_sc):
    i = pl.program_id(0)
    tt = lg_ref.shape[0]

    @pl.when(i == 0)
    def _():
        carry_sc[...] = jnp.zeros(carry_sc.shape, F32)

    logits = lg_ref[...]
    lane = lax.broadcasted_iota(jnp.int32, (tt, LANES), 1)
    v = logits
    onehot = jnp.zeros((tt, LANES), F32)
    vals, idxs = [], []
    for _ in range(TOP_K):
        m = v.max(-1, keepdims=True)
        idx = jnp.where(v == m, lane, LANES).min(-1, keepdims=True)
        hit = lane == idx
        vals.append(m)
        idxs.append(idx)
        onehot = onehot + hit.astype(F32)
        v = jnp.where(hit, -jnp.inf, v)
    exps = [jnp.exp(vk - vals[0]) for vk in vals]
    den = exps[0] + exps[1] + exps[2] + exps[3]
    before = jnp.dot(tri_ref[...], onehot.astype(BF16), preferred_element_type=F32) + carry_sc[0:1, :]
    meta = jnp.zeros((tt, LANES), F32)
    for k in range(TOP_K):
        rank = jnp.where(lane == idxs[k], before, 0.0).sum(-1, keepdims=True)
        meta = jnp.where(lane == k, idxs[k].astype(F32), meta)
        meta = jnp.where(lane == TOP_K + k, exps[k] / den, meta)
        meta = jnp.where(lane == 2 * TOP_K + k, rank, meta)
    meta_ref[...] = meta
    carry_sc[...] = carry_sc[...] + jnp.broadcast_to(onehot.sum(0, keepdims=True), carry_sc.shape)
    cnt_ref[...] = carry_sc[...]


def _router(logits, tt):
    T = logits.shape[0]
    tri = (jnp.arange(tt)[:, None] > jnp.arange(tt)[None, :]).astype(BF16)
    return pl.pallas_call(
        _router_kernel,
        out_shape=(jax.ShapeDtypeStruct((T, LANES), F32), jax.ShapeDtypeStruct((8, LANES), F32)),
        grid=(T // tt,),
        in_specs=[
            pl.BlockSpec((tt, LANES), lambda i: (i, 0)),
            pl.BlockSpec((tt, tt), lambda i: (0, 0)),
        ],
        out_specs=(pl.BlockSpec((tt, LANES), lambda i: (i, 0)), pl.BlockSpec((8, LANES), lambda i: (0, 0))),
        scratch_shapes=[pltpu.VMEM((8, LANES), F32)],
        compiler_params=_cparams(("arbitrary",), VMEM_LIMIT),
        name="router_top4",
    )(logits, tri)


def _dispatch_kernel(dest_ref, h_ref, xs_in_hbm, xs_hbm, sem, *, nsub):
    del xs_in_hbm
    n = h_ref.shape[0] // nsub
    for t in range(n):
        for k in range(TOP_K):
            d = pl.multiple_of(dest_ref[t * TOP_K + k] * nsub, nsub)
            pltpu.make_async_copy(h_ref.at[pl.ds(t * nsub, nsub), :], xs_hbm.at[pl.ds(d, nsub), :],
                                  sem).start(priority=k % 2)
    for k in range(TOP_K):
        pltpu.make_async_copy(h_ref, xs_hbm.at[pl.ds(0, n * nsub), :], sem).wait()


def _dispatch(hp, dest_flat, n_rows, nsub, tt):
    T = hp.shape[0] // nsub
    xs0 = jnp.zeros((n_rows * nsub, LANES), hp.dtype)
    return pl.pallas_call(
        functools.partial(_dispatch_kernel, nsub=nsub),
        out_shape=jax.ShapeDtypeStruct(xs0.shape, xs0.dtype),
        grid=(T // tt,),
        in_specs=[
            pl.BlockSpec((tt * TOP_K,), lambda i: (i,), memory_space=pltpu.SMEM),
            pl.BlockSpec((tt * nsub, LANES), lambda i: (i, 0)),
            pl.BlockSpec(memory_space=pl.ANY),
        ],
        out_specs=pl.BlockSpec(memory_space=pl.ANY),
        scratch_shapes=[pltpu.SemaphoreType.DMA(())],
        input_output_aliases={2: 0},
        compiler_params=_cparams(("arbitrary",)),
        name="moe_dispatch",
    )(dest_flat, hp, xs0)


def _expert_kernel(be_ref, nu_ref, x_ref, wg_ref, wl_ref, bg_ref, bl_ref, wd_ref, bd_ref, y_ref, xb_sc, *, th, nsub):
    i = pl.program_id(0)
    tm = xb_sc.shape[0]
    half = nsub * LANES
    dh = wg_ref.shape[1]

    @pl.when(i < nu_ref[0])
    def _():
        lo, hi = _unpack_pairs(_load_tile_rows(x_ref, 0, tm, nsub))
        xb_sc[:, :half] = lo.astype(BF16)
        xb_sc[:, half:] = hi.astype(BF16)
        x = xb_sc[...]
        y = bd_ref[...]
        for c in range(dh // th):
            sl = slice(c * th, (c + 1) * th)
            glu = jnp.dot(x, wg_ref[:, sl], preferred_element_type=F32) + bg_ref[:, sl]
            lin = jnp.dot(x, wl_ref[:, sl], preferred_element_type=F32) + bl_ref[:, sl]
            glu = jnp.minimum(glu, SWIGLU_LIMIT)
            lin = jnp.clip(lin, -SWIGLU_LIMIT, SWIGLU_LIMIT)
            act = glu * jax.nn.sigmoid(SWIGLU_ALPHA * glu) * (lin + 1.0)
            y = y + jnp.dot(act.astype(BF16), wd_ref[sl, :], preferred_element_type=F32)
        _store_tile_rows(y_ref, 0, _pack_pairs(y.astype(BF16)), nsub)


def _experts(xs, blk_expert, n_used, w_up, b_up, w_down, b_down, tm, th):
    E, D, two_dh = w_up.shape
    nsub = D // 2 // LANES
    n_blk = xs.shape[0] // (tm * nsub)
    dh = two_dh // 2
    th = min(th, dh)
    once = pl.Buffered(1)

    def row(i, be, nu):
        return (jnp.minimum(i, nu[0] - 1), 0)

    grid_spec = pltpu.PrefetchScalarGridSpec(
        num_scalar_prefetch=2,
        grid=(n_blk,),
        in_specs=[
            pl.BlockSpec((tm * nsub, LANES), row),
            pl.BlockSpec((None, D, dh), lambda i, be, nu: (be[i], 0, 0), pipeline_mode=once),
            pl.BlockSpec((None, D, dh), lambda i, be, nu: (be[i], 0, 1), pipeline_mode=once),
            pl.BlockSpec((None, 1, dh), lambda i, be, nu: (be[i], 0, 0)),
            pl.BlockSpec((None, 1, dh), lambda i, be, nu: (be[i], 0, 1)),
            pl.BlockSpec((None, dh, D), lambda i, be, nu: (be[i], 0, 0), pipeline_mode=once),
            pl.BlockSpec((None, 1, D), lambda i, be, nu: (be[i], 0, 0)),
        ],
        out_specs=pl.BlockSpec((tm * nsub, LANES), row),
        scratch_shapes=[pltpu.VMEM((tm, D), BF16)],
    )
    return pl.pallas_call(
        functools.partial(_expert_kernel, th=th, nsub=nsub),
        out_shape=jax.ShapeDtypeStruct(xs.shape, jnp.uint32),
        grid_spec=grid_spec,
        compiler_params=_cparams(("arbitrary",), VMEM_LIMIT),
        name="moe_experts",
    )(blk_expert, n_used, xs, w_up, w_up, b_up, b_up, w_down, b_down)


def _combine_kernel(dest_ref, meta_ref, h_ref, g_ref, b_ref, y_hbm, o_ref, buf, sem, *, alpha, nsub):
    n = o_ref.shape[0]
    for t in range(n):
        for k in range(TOP_K):
            d = pl.multiple_of(dest_ref[t * TOP_K + k] * nsub, nsub)
            pltpu.make_async_copy(y_hbm.at[pl.ds(d, nsub), :], buf.at[pl.ds((k * n + t) * nsub, nsub), :],
                                  sem).start(priority=k % 2)
    for k in range(TOP_K):
        pltpu.make_async_copy(y_hbm.at[pl.ds(0, n * nsub), :], buf.at[pl.ds(k * n * nsub, n * nsub), :], sem).wait()
    ffn_lo = ffn_hi = None
    for k in range(TOP_K):
        lo, hi = _unpack_pairs(_load_tile_rows(buf, k * n * nsub, n, nsub))
        gate = meta_ref[:, TOP_K + k:TOP_K + k + 1]
        ffn_lo = gate * lo if k == 0 else ffn_lo + gate * lo
        ffn_hi = gate * hi if k == 0 else ffn_hi + gate * hi
    ffn = jnp.concatenate([ffn_lo, ffn_hi], axis=1)
    o_ref[...] = _layer_norm(alpha * h_ref[...] + ffn, g_ref[...], b_ref[...])


def _combine_ln(y, dest_flat, meta, h, g, b, alpha, tt):
    T, D = h.shape
    nsub = D // 2 // LANES
    return pl.pallas_call(
        functools.partial(_combine_kernel, alpha=alpha, nsub=nsub),
        out_shape=jax.ShapeDtypeStruct((T, D), F32),
        grid=(T // tt,),
        in_specs=[
            pl.BlockSpec((tt * TOP_K,), lambda i: (i,), memory_space=pltpu.SMEM),
            pl.BlockSpec((tt, LANES), lambda i: (i, 0)),
            pl.BlockSpec((tt, D), lambda i: (i, 0)),
            pl.BlockSpec((1, D), lambda i: (0, 0)),
            pl.BlockSpec((1, D), lambda i: (0, 0)),
            pl.BlockSpec(memory_space=pl.ANY),
        ],
        out_specs=pl.BlockSpec((tt, D), lambda i: (i, 0)),
        scratch_shapes=[pltpu.VMEM((TOP_K * tt * nsub, LANES), y.dtype), pltpu.SemaphoreType.DMA(())],
        compiler_params=_cparams(("arbitrary",), VMEM_LIMIT),
        name="moe_combine_ln2",
    )(dest_flat, meta, h, g, b, y)


def _layer(x, w_in, b_in, pos_k, pos_v, ck_w1, ck_w2, cv_w1, cv_w2, w_br_nsa, w_br_dil, w_out,
           ln1_g, ln1_b, w_router, b_router, w_up, b_up, w_down, b_down, ln2_g, ln2_b, alpha):
    B, S, D = x.shape
    T = B * S
    nd = D // LANES
    G = NSA_KV_GROUPS
    kvw = G * HEAD_DIM
    n_exp = w_router.shape[1]

    o_q = 0
    o_kv = NSA_HEADS * HEAD_DIM
    o_gl = o_kv + 6 * kvw
    o_dil = o_gl + 3 * NSA_HEADS
    o_ga = o_dil + 3 * N_DIL * DIL_W
    o_gb = o_ga + D

    def wcols(a, n):
        return w_in[:, a:a + n], b_in[a:a + n]

    def kv(i):
        return wcols(o_kv + i * kvw, kvw)

    tn = 1024
    gl_w, gl_b = wcols(o_gl, 3 * NSA_HEADS)
    pieces = [wcols(o_ga, D), wcols(o_gb, D), wcols(o_q, NSA_HEADS * HEAD_DIM),
              kv(0), kv(2), kv(4), kv(1), kv(3), kv(5), (gl_w, gl_b)]
    used = sum(p[0].shape[1] for p in pieces)
    n_main = -(-used // tn) * tn
    pieces.append((jnp.zeros((D, n_main - used), F32), jnp.zeros((n_main - used,), F32)))
    w_main = jnp.concatenate([p[0] for p in pieces], axis=1).astype(BF16)
    b_main = jnp.concatenate([p[1] for p in pieces])[None, :]
    t_q = 2 * nd
    t_kc, t_ks, t_kw = t_q + 8, t_q + 10, t_q + 12
    t_vc, t_vs, t_vw = t_q + 14, t_q + 16, t_q + 18
    t_gl = t_q + 20
    tile_id = jnp.arange(n_main // LANES)
    flags_main = jnp.where((tile_id >= t_q) & (tile_id < t_kc), ROPE_Q,
                           jnp.where((tile_id >= t_kc) & (tile_id < t_vc), ROPE, PLAIN)).astype(jnp.int32)
    q_blk0 = t_q // NSA_REP

    pos = jnp.arange(S, dtype=F32)
    inv = ROPE_THETA ** (-jnp.arange(0, HEAD_DIM, 2, dtype=F32) / HEAD_DIM)
    ang = pos[:, None] * inv[None, :]
    cosx = jnp.concatenate([jnp.cos(ang), jnp.cos(ang)], axis=-1)
    sinx = jnp.concatenate([-jnp.sin(ang), jnp.sin(ang)], axis=-1)

    main = _project(x, w_main, b_main, flags_main, cosx, sinx, 1, 1024, tn)

    nC = S // CMP_STRIDE
    half = CMP_STRIDE * HEAD_DIM

    def cmp_in(tile):
        a = main[:, 0, :, tile * LANES:(tile + G) * LANES].reshape(B, S, G, HEAD_DIM)
        return a.transpose(0, 2, 1, 3).reshape(B, G, nC, half)

    xkv = jnp.stack([cmp_in(t_kc), cmp_in(t_vc)])
    pos_kv = jnp.stack([pos_k.reshape(2, half), pos_v.reshape(2, half)])
    w1 = jnp.stack([ck_w1, cv_w1]).astype(BF16)
    w2 = jnp.stack([ck_w2, cv_w2]).astype(BF16)
    kvc = _compress(xkv, pos_kv, w1, w2)

    n_slc = S // SEL_LEN
    assert n_slc <= LANES
    c_start = jnp.arange(nC) * CMP_STRIDE
    jb = jnp.arange(LANES)
    overlap = ((c_start[:, None] < (jb[None, :] + 1) * SEL_LEN) & (c_start[:, None] + CMP_LEN > jb[None, :] * SEL_LEN)
               & (jb[None, :] < n_slc) & (c_start[:, None] + CMP_LEN <= S)).astype(BF16)
    tq = min(256, S)
    o_cmp, sel = _cmp_attention(main, kvc, overlap, q_blk0, tq)

    onehot_t = (jnp.arange(S)[:, None] // SEL_LEN == jnp.arange(LANES)[None, :]).astype(BF16)
    o_slc, w_up_b, w_down_b = _sel_attention(main, sel, onehot_t, w_up, w_down, q_blk0, t_ks, t_vs, 256, 512)

    o_win = _band_attention(
        main, (B, G), S,
        lambda bb, g, i: (bb, 0, i, q_blk0 + g),
        lambda bb, g, i: (bb, 0, i, t_kw + g),
        lambda bb, g, i: (bb, 0, i, t_vw + g),
        HEAD_DIM, (B, S, NSA_HEADS * HEAD_DIM), lambda bb, g, i: (bb, i, g), 256, 2, WIN_LEN - 1, False)

    flags_dil = jnp.array([ROPE_Q] * DIL_HEADS + [ROPE] * DIL_HEADS + [PLAIN] * DIL_HEADS, jnp.int32)
    dil_o, dil_lse = [], []
    for gi, (w, d) in enumerate(DIL_CONFIGS):
        wd, bd = wcols(o_dil + gi * 3 * DIL_W, 3 * DIL_W)
        sub = _project(x, wd.astype(BF16), bd[None, :], flags_dil, cosx, sinx, d, 512, 3 * DIL_W)
        L = S // d
        o_g, lse_g = _band_attention(
            sub, (B, d), L,
            lambda bb, r, i: (bb, r, i, 0),
            lambda bb, r, i: (bb, r, i, 1),
            lambda bb, r, i: (bb, r, i, 2),
            DIL_W, (B, L, d * DIL_W), lambda bb, r, i: (bb, i, r), 128, 4, w // d, True)
        dil_o.append(o_g.reshape(T, DIL_W))
        dil_lse.append(lse_g.reshape(T, LANES))

    w_r = jnp.concatenate([w_router, jnp.zeros((D, LANES - n_exp), F32)], axis=1).astype(BF16)
    b_r = jnp.concatenate([b_router, jnp.full((LANES - n_exp,), NEG, F32)])[None, :]
    h_f, h_p, logits = _merge(
        o_cmp.reshape(T, -1), o_slc.reshape(T, -1), o_win.reshape(T, -1), main.reshape(T, n_main),
        t_gl, 0, 1, dil_o, dil_lse, x.reshape(T, D),
        w_br_nsa.astype(BF16), w_br_dil.astype(BF16), w_out.astype(BF16),
        ln1_g[None, :], ln1_b[None, :], w_r, b_r, alpha, min(256, T))

    meta, cnt = _router(logits, min(512, T))
    top_idx = meta[:, 0:TOP_K].astype(jnp.int32)
    gates = meta[:, TOP_K:2 * TOP_K]
    rank = meta[:, 2 * TOP_K:3 * TOP_K].astype(jnp.int32)

    tm = 512 if T * TOP_K >= 512 * n_exp else 128
    counts = cnt[0, :n_exp].astype(jnp.int32)
    padded = (counts + tm - 1) // tm * tm
    pad_end = jnp.cumsum(padded)
    pad_start = pad_end - padded
    dest = (pad_start[top_idx] + rank).reshape(T * TOP_K)
    n_rows = T * TOP_K + n_exp * tm
    n_blk = n_rows // tm
    blk_start = jnp.arange(n_blk, dtype=jnp.int32) * tm
    blk_expert = jnp.minimum((pad_end[None, :] <= blk_start[:, None]).sum(-1), n_exp - 1).astype(jnp.int32)
    n_used = (pad_end[-1:] // tm).astype(jnp.int32)

    xs = _dispatch(h_p, dest, n_rows, D // 2 // LANES, min(256, T))
    y = _experts(xs, blk_expert, n_used, w_up_b, b_up[:, None, :], w_down_b, b_down[:, None, :], tm, 512)
    out = _combine_ln(y, dest, meta, h_f, ln2_g[None, :], ln2_b[None, :], alpha, min(256, T))
    return out.reshape(B, S, D)


def kernel(x, w_in, b_in, cmp_pos_k, cmp_pos_v, cmp_k_w1, cmp_k_w2, cmp_v_w1, cmp_v_w2, w_br_nsa, w_br_dil,
           w_out, ln1_g, ln1_b, w_router, b_router, w_up, b_up, w_down, b_down, ln2_g, ln2_b):
    depth = w_in.shape[0]
    alpha = (2.0 * depth) ** 0.25
    h = x
    for l in range(depth):
        h = _layer(h, w_in[l], b_in[l], cmp_pos_k[l], cmp_pos_v[l], cmp_k_w1[l], cmp_k_w2[l],
                   cmp_v_w1[l], cmp_v_w2[l], w_br_nsa[l], w_br_dil[l], w_out[l], ln1_g[l], ln1_b[l],
                   w_router[l], b_router[l], w_up[l], b_up[l], w_down[l], b_down[l], ln2_g[l], ln2_b[l], alpha)
    return h
```

```python
import functools

import jax
import jax.numpy as jnp
from jax import lax
from jax.experimental import pallas as pl
from jax.experimental.pallas import tpu as pltpu

F32 = jnp.float32
BF16 = jnp.bfloat16

HEAD_DIM = 128
LANES = 128
ROPE_THETA = 10000.0
NSA_HEADS = 8
NSA_KV_GROUPS = 2
NSA_REP = NSA_HEADS // NSA_KV_GROUPS
CMP_LEN = 32
CMP_STRIDE = 16
SEL_LEN = 64
SEL_TOPK = 16
WIN_LEN = 512
DIL_CONFIGS = ((128, 1), (512, 4), (2048, 16))
N_DIL = len(DIL_CONFIGS)
DIL_HEADS = 4
TOP_K = 4
SWIGLU_LIMIT = 7.0
SWIGLU_ALPHA = 1.702
LN_EPS = 1e-5
NEG = -1e30
FORCE = 1e9
SCALE = HEAD_DIM ** -0.5
LOG2E = 1.4426950408889634
Q_SCALE = SCALE * LOG2E
PLAIN, ROPE, ROPE_Q = 0, 1, 2

GROUP_W = NSA_REP * HEAD_DIM
DIL_W = DIL_HEADS * HEAD_DIM
VMEM_LIMIT = 56 * 1024 * 1024


def _cparams(sem, vmem=None):
    return pltpu.CompilerParams(dimension_semantics=sem, vmem_limit_bytes=vmem)


def _masked_softmax(s, mask):
    s = jnp.where(mask, s, NEG)
    m = s.max(-1, keepdims=True)
    p = jnp.where(mask, jnp.exp2(s - m), 0.0)
    den = p.sum(-1, keepdims=True)
    safe = jnp.where(den > 0, den, 1.0)
    return p, m, safe


def _qk(q, k):
    return lax.dot_general(q, k, (((1,), (1,)), ((), ())), preferred_element_type=F32)


def _pack_pairs(xb):
    m = xb.shape[1] // 2
    lo = lax.bitcast_convert_type(xb[:, :m].astype(F32), jnp.uint32)
    hi = lax.bitcast_convert_type(xb[:, m:].astype(F32), jnp.uint32)
    return (lo >> 16) | (hi & jnp.uint32(0xFFFF0000))


def _unpack_pairs(w):
    lo = lax.bitcast_convert_type(w << 16, F32)
    hi = lax.bitcast_convert_type(w & jnp.uint32(0xFFFF0000), F32)
    return lo, hi


def _store_tile_rows(ref, start, packed, nsub):
    n = packed.shape[0]
    for j in range(nsub):
        ref[pl.ds(start + j, n, stride=nsub), :] = packed[:, j * LANES:(j + 1) * LANES]


def _load_tile_rows(ref, start, n, nsub, lead=()):
    return jnp.concatenate([ref[lead + (pl.ds(start + j, n, stride=nsub), slice(None))] for j in range(nsub)],
                           axis=1)


def _proj_kernel(flags_ref, x_ref, w_ref, b_ref, cos_ref, sin_ref, o_ref, xb_ref, *xcol_ref):
    j = pl.program_id(2)
    d, sub, tn = o_ref.shape
    n_sub = tn // LANES

    @pl.when(j == 0)
    def _():
        if d == 1:
            xb_ref[...] = x_ref[...].astype(BF16)
        else:
            xcol, = xcol_ref
            for c in range(xcol.shape[0]):
                xcol[c] = x_ref[:, c * LANES:(c + 1) * LANES]
            for c in range(xcol.shape[0]):
                for r in range(d):
                    xb_ref[r * sub:(r + 1) * sub, c * LANES:(c + 1) * LANES] = (
                        xcol[c, pl.ds(r, sub, stride=d), :].astype(BF16))

    acc = jnp.dot(xb_ref[...], w_ref[...], preferred_element_type=F32) + b_ref[...]
    for u in range(n_sub):
        a = acc[:, u * LANES:(u + 1) * LANES]
        roped = a * cos_ref[...] + pltpu.roll(a, HEAD_DIM // 2, 1) * sin_ref[...]
        flag = flags_ref[j * n_sub + u]
        mult = jnp.where(flag == ROPE_Q, Q_SCALE, 1.0).astype(F32)
        res = (jnp.where(flag == PLAIN, a, roped) * mult).astype(o_ref.dtype)
        for r in range(d):
            o_ref[r, :, u * LANES:(u + 1) * LANES] = res[r * sub:(r + 1) * sub]


def _project(x, w, b, flags, cosx, sinx, d, tm, tn):
    B, S, D = x.shape
    N = w.shape[1]
    L = S // d
    tm = min(tm, S)
    sub = tm // d

    def regroup(tab):
        return tab.reshape(S // tm, sub, d, HEAD_DIM).transpose(0, 2, 1, 3).reshape(S, HEAD_DIM)

    grid_spec = pltpu.PrefetchScalarGridSpec(
        num_scalar_prefetch=1,
        grid=(B, S // tm, N // tn),
        in_specs=[
            pl.BlockSpec((None, tm, D), lambda bb, i, j, f: (bb, i, 0)),
            pl.BlockSpec((D, tn), lambda bb, i, j, f: (0, j)),
            pl.BlockSpec((1, tn), lambda bb, i, j, f: (0, j)),
            pl.BlockSpec((tm, HEAD_DIM), lambda bb, i, j, f: (i, 0)),
            pl.BlockSpec((tm, HEAD_DIM), lambda bb, i, j, f: (i, 0)),
        ],
        out_specs=pl.BlockSpec((None, d, sub, tn), lambda bb, i, j, f: (bb, 0, i, j)),
        scratch_shapes=[pltpu.VMEM((tm, D), BF16)] + ([pltpu.VMEM((D // LANES, tm, LANES), F32)] if d > 1 else []),
    )
    return pl.pallas_call(
        _proj_kernel,
        out_shape=jax.ShapeDtypeStruct((B, d, L, N), BF16),
        grid_spec=grid_spec,
        compiler_params=_cparams(("parallel", "arbitrary", "arbitrary"), VMEM_LIMIT),
        name="proj_rope",
    )(flags, x, w, b, regroup(cosx), regroup(sinx))


def _gelu_tanh(x):
    return 0.5 * x * (1.0 + jnp.tanh(0.7978845608028654 * (x + 0.044715 * x * x * x)))


def _compress_kernel(x_ref, pos_ref, w1_ref, w2_ref, o_ref):
    half = x_ref.shape[-1]
    nc = x_ref.shape[0]
    x = x_ref[...].astype(F32)
    lo = (x + pos_ref[0:1, :]).astype(BF16)
    hi = (x + pos_ref[1:2, :]).astype(BF16)
    y_lo = jnp.dot(lo, w1_ref[0:half, :], preferred_element_type=F32)
    y_hi = jnp.dot(hi, w1_ref[half:2 * half, :], preferred_element_type=F32)
    h = y_lo + pltpu.roll(y_hi, nc - 1, 0)
    g = _gelu_tanh(h).astype(BF16)
    o_ref[...] = jnp.dot(g, w2_ref[...], preferred_element_type=F32).astype(o_ref.dtype)


def _compress(xkv, pos, w1, w2):
    _, B, G, nC, half = xkv.shape
    hid = w1.shape[-1]
    return pl.pallas_call(
        _compress_kernel,
        out_shape=jax.ShapeDtypeStruct((2, B, G, nC, HEAD_DIM), BF16),
        grid=(2, B, G),
        in_specs=[
            pl.BlockSpec((None, None, None, nC, half), lambda a, bb, g: (a, bb, g, 0, 0)),
            pl.BlockSpec((None, 2, half), lambda a, bb, g: (a, 0, 0)),
            pl.BlockSpec((None, 2 * half, hid), lambda a, bb, g: (a, 0, 0)),
            pl.BlockSpec((None, hid, HEAD_DIM), lambda a, bb, g: (a, 0, 0)),
        ],
        out_specs=pl.BlockSpec((None, None, None, nC, HEAD_DIM), lambda a, bb, g: (a, bb, g, 0, 0)),
        compiler_params=_cparams(("arbitrary", "arbitrary", "arbitrary"), VMEM_LIMIT),
        name="compress_mlp",
    )(xkv, pos, w1, w2)


def _cmp_attn_kernel(q_ref, kc_ref, vc_ref, ov_ref, o_ref, sel_ref, *, n_slc, n_sel):
    qi = pl.program_id(2)
    tq = q_ref.shape[0]
    nc = kc_ref.shape[0]
    t = qi * tq + lax.broadcasted_iota(jnp.int32, (tq, nc), 0)
    c = lax.broadcasted_iota(jnp.int32, (tq, nc), 1)
    mask = (c * CMP_STRIDE + (CMP_LEN - 1)) <= t
    kc = kc_ref[...]
    vc = vc_ref[...]
    ps = jnp.zeros((tq, nc), F32)
    for h in range(NSA_REP):
        s = _qk(q_ref[:, h * HEAD_DIM:(h + 1) * HEAD_DIM], kc)
        p, _, safe = _masked_softmax(s, mask)
        p = p / safe
        o = jnp.dot(p.astype(BF16), vc, preferred_element_type=F32)
        o_ref[:, h * HEAD_DIM:(h + 1) * HEAD_DIM] = o.astype(o_ref.dtype)
        ps = ps + p
    imp = jnp.dot(ps.astype(BF16), ov_ref[...], preferred_element_type=F32)
    tj = qi * tq + lax.broadcasted_iota(jnp.int32, (tq, LANES), 0)
    j = lax.broadcasted_iota(jnp.int32, (tq, LANES), 1)
    cur = tj // SEL_LEN
    forced = (j == 0) | (j == cur) | (j == cur - 1)
    imp = jnp.where(forced, FORCE, jnp.where(j > cur, NEG, imp))
    imp_t = imp.T[0:n_slc, :]
    grp = 8
    groups = [imp_t[a:a + grp, :] for a in range(0, n_slc, grp)]
    ranks = [jnp.zeros(gv.shape, jnp.int32) for gv in groups]
    for j2 in range(n_slc):
        row = imp_t[j2:j2 + 1, :]
        for gi, gv in enumerate(groups):
            lo = gi * grp
            if lo > j2:
                ahead = row >= gv
            elif lo + gv.shape[0] - 1 <= j2:
                ahead = row > gv
            else:
                later = lax.broadcasted_iota(jnp.int32, gv.shape, 0) + lo > j2
                ahead = (row > gv) | ((row == gv) & later)
            ranks[gi] = jnp.where(ahead, ranks[gi] + 1, ranks[gi])
    rank = jnp.concatenate(ranks, axis=0)
    sel_t = jnp.where(rank < n_sel, 0.0, NEG)
    if n_slc < LANES:
        sel_t = jnp.concatenate([sel_t, jnp.zeros((LANES - n_slc, tq), F32)], axis=0)
    sel_ref[...] = sel_t.T.astype(sel_ref.dtype)


def _cmp_attention(main, kvc, overlap, q_blk0, tq):
    B, _, S, _ = main.shape
    G = NSA_KV_GROUPS
    nC = kvc.shape[3]
    n_slc = S // SEL_LEN
    n_sel = min(SEL_TOPK, n_slc)
    kern = functools.partial(_cmp_attn_kernel, n_slc=n_slc, n_sel=n_sel)
    return pl.pallas_call(
        kern,
        out_shape=(jax.ShapeDtypeStruct((B, S, NSA_HEADS * HEAD_DIM), BF16),
                   jax.ShapeDtypeStruct((B, G, S, LANES), BF16)),
        grid=(B, G, S // tq),
        in_specs=[
            pl.BlockSpec((None, None, tq, GROUP_W), lambda bb, g, i: (bb, 0, i, q_blk0 + g)),
            pl.BlockSpec((None, None, None, nC, HEAD_DIM), lambda bb, g, i: (0, bb, g, 0, 0)),
            pl.BlockSpec((None, None, None, nC, HEAD_DIM), lambda bb, g, i: (1, bb, g, 0, 0)),
            pl.BlockSpec((nC, LANES), lambda bb, g, i: (0, 0)),
        ],
        out_specs=(pl.BlockSpec((None, tq, GROUP_W), lambda bb, g, i: (bb, i, g)),
                   pl.BlockSpec((None, None, tq, LANES), lambda bb, g, i: (bb, g, i, 0))),
        compiler_params=_cparams(("parallel", "arbitrary", "arbitrary"), VMEM_LIMIT),
        name="cmp_attn_select",
    )(main, kvc, kvc, overlap)


def _fold_lanes(x, op):
    out = x[:, 0:LANES]
    for u in range(1, x.shape[1] // LANES):
        out = op(out, x[:, u * LANES:(u + 1) * LANES])
    return out


def _sel_attn_kernel(q_ref, k_ref, v_ref, bias_ref, et_ref, wu_ref, wd_ref,
                     o_ref, wu_out, wd_out, qx_sc, s_sc, m_sc, l_sc, acc_sc):
    wu_out[...] = wu_ref[...].astype(wu_out.dtype)
    wd_out[...] = wd_ref[...].astype(wd_out.dtype)
    qi = pl.program_id(2)
    t = q_ref.shape[0]
    tc = s_sc.shape[2]
    rows = NSA_REP * t
    for h in range(NSA_REP):
        qx_sc[h * t:(h + 1) * t, 0:HEAD_DIM] = q_ref[:, h * HEAD_DIM:(h + 1) * HEAD_DIM]
        qx_sc[h * t:(h + 1) * t, HEAD_DIM:2 * HEAD_DIM] = bias_ref[...]

    def scores(c):
        start = pl.multiple_of(c * tc, tc)
        kx = jnp.concatenate([k_ref[pl.ds(start, tc), :], et_ref[pl.ds(start, tc), :]], axis=1)
        return _qk(qx_sc[...], kx)

    m_sc[...] = jnp.full(m_sc.shape, NEG, F32)

    def max_pass(c, carry):
        s = scores(c)
        s_sc[c] = s
        m_sc[...] = jnp.maximum(m_sc[...], _fold_lanes(s, jnp.maximum))
        return carry

    n_full = (qi * t) // tc
    lax.fori_loop(0, n_full, max_pass, 0)
    qpos = qi * t + lax.broadcasted_iota(jnp.int32, (rows, tc), 0) % t
    kpos = n_full * tc + lax.broadcasted_iota(jnp.int32, (rows, tc), 1)
    s = jnp.where(kpos <= qpos, scores(n_full), NEG)
    s_sc[n_full] = s
    m = jnp.maximum(m_sc[...], _fold_lanes(s, jnp.maximum)).max(-1, keepdims=True)
    m_sc[...] = jnp.broadcast_to(m, m_sc.shape)
    l_sc[...] = jnp.zeros(l_sc.shape, F32)
    acc_sc[...] = jnp.zeros(acc_sc.shape, F32)

    def exp_pass(c, carry):
        mb = m_sc[...]
        sc = s_sc[c]
        p = jnp.concatenate([jnp.exp2(sc[:, u * LANES:(u + 1) * LANES] - mb) for u in range(tc // LANES)], axis=1)
        l_sc[...] += _fold_lanes(p, jnp.add)
        v = v_ref[pl.ds(pl.multiple_of(c * tc, tc), tc), :]
        acc_sc[...] += jnp.dot(p.astype(BF16), v, preferred_element_type=F32)
        return carry

    lax.fori_loop(0, n_full + 1, exp_pass, 0)
    l = l_sc[...].sum(-1, keepdims=True)
    o = acc_sc[...] / jnp.where(l > 0, l, 1.0)
    for h in range(NSA_REP):
        o_ref[:, h * HEAD_DIM:(h + 1) * HEAD_DIM] = o[h * t:(h + 1) * t].astype(o_ref.dtype)


def _sel_attention(main, bias, onehot_t, w_up, w_down, q_blk0, k_tile0, v_tile0, t, tc):
    B, _, S, _ = main.shape
    G = NSA_KV_GROUPS
    t = min(t, S)
    tc = min(tc, S)
    rows = NSA_REP * t
    nq = S // t
    n_steps = B * G * nq
    wu2 = w_up.reshape(-1, w_up.shape[-1])
    wd2 = w_down.reshape(-1, w_down.shape[-1])
    ru, rd = wu2.shape[0] // n_steps, wd2.shape[0] // n_steps
    assert ru * n_steps == wu2.shape[0] and rd * n_steps == wd2.shape[0]

    def slab(bb, g, i):
        return ((bb * G + g) * nq + i, 0)

    o, wu_b, wd_b = pl.pallas_call(
        _sel_attn_kernel,
        out_shape=(jax.ShapeDtypeStruct((B, S, NSA_HEADS * HEAD_DIM), BF16),
                   jax.ShapeDtypeStruct(wu2.shape, BF16), jax.ShapeDtypeStruct(wd2.shape, BF16)),
        grid=(B, G, nq),
        in_specs=[
            pl.BlockSpec((None, None, t, GROUP_W), lambda bb, g, i: (bb, 0, i, q_blk0 + g)),
            pl.BlockSpec((None, None, S, HEAD_DIM), lambda bb, g, i: (bb, 0, 0, k_tile0 + g)),
            pl.BlockSpec((None, None, S, HEAD_DIM), lambda bb, g, i: (bb, 0, 0, v_tile0 + g)),
            pl.BlockSpec((None, None, t, LANES), lambda bb, g, i: (bb, g, i, 0)),
            pl.BlockSpec((S, LANES), lambda bb, g, i: (0, 0)),
            pl.BlockSpec((ru, wu2.shape[1]), slab),
            pl.BlockSpec((rd, wd2.shape[1]), slab),
        ],
        out_specs=(pl.BlockSpec((None, t, GROUP_W), lambda bb, g, i: (bb, i, g)),
                   pl.BlockSpec((ru, wu2.shape[1]), slab), pl.BlockSpec((rd, wd2.shape[1]), slab)),
        scratch_shapes=[pltpu.VMEM((rows, 2 * HEAD_DIM), BF16), pltpu.VMEM((S // tc, rows, tc), F32),
                        pltpu.VMEM((rows, LANES), F32), pltpu.VMEM((rows, LANES), F32),
                        pltpu.VMEM((rows, HEAD_DIM), F32)],
        compiler_params=_cparams(("parallel", "arbitrary", "arbitrary"), VMEM_LIMIT),
        name="selected_attn",
    )(main, main, main, bias, onehot_t, wu2, wd2)
    return o, wu_b.reshape(w_up.shape), wd_b.reshape(w_down.shape)


def _band_kernel(*refs, n_prev, n_sub, max_dist, kv_heads, with_lse):
    q_ref = refs[0]
    k_refs = refs[1:2 + n_prev]
    v_refs = refs[2 + n_prev:3 + 2 * n_prev]
    o_ref = refs[3 + 2 * n_prev]
    qi = pl.program_id(2)
    t = k_refs[0].shape[0]
    nk = (n_prev + 1) * t
    n_heads = q_ref.shape[1] // HEAD_DIM
    k_all = jnp.concatenate([r[...] for r in k_refs], axis=0)
    v_all = jnp.concatenate([r[...] for r in v_refs], axis=0)
    kcol = lax.broadcasted_iota(jnp.int32, (t, nk), 1)
    diff = n_prev * t + lax.broadcasted_iota(jnp.int32, (t, nk), 0) - kcol
    in_band = (diff >= 0) & (diff <= max_dist)
    lane = lax.broadcasted_iota(jnp.int32, (t, LANES), 1)
    for u in range(n_sub):
        first_key = (qi * n_sub + u - n_prev) * t
        bias = jnp.where(in_band & (kcol + first_key >= 0), 0.0, NEG)
        rows = slice(u * t, (u + 1) * t)
        k_u = k_all[u * t:u * t + nk]
        v_u = v_all[u * t:u * t + nk]
        lse = jnp.zeros((t, LANES), F32)
        if kv_heads == 1:
            q = jnp.concatenate([q_ref[rows, h * HEAD_DIM:(h + 1) * HEAD_DIM] for h in range(n_heads)], axis=0)
            s = _qk(q, k_u).reshape(n_heads, t, nk) + bias[None]
            m = s.max(-1, keepdims=True)
            p = jnp.exp2(s - m)
            l = p.sum(-1, keepdims=True)
            o = jnp.dot(p.astype(BF16).reshape(n_heads * t, nk), v_u, preferred_element_type=F32)
            o = o.reshape(n_heads, t, HEAD_DIM) / l
            for h in range(n_heads):
                o_ref[rows, h * HEAD_DIM:(h + 1) * HEAD_DIM] = o[h].astype(o_ref.dtype)
                lse = jnp.where(lane == h, m[h] + jnp.log2(l[h]), lse)
        else:
            for h in range(n_heads):
                cols = slice(h * HEAD_DIM, (h + 1) * HEAD_DIM)
                s = _qk(q_ref[rows, cols], k_u[:, cols]) + bias
                m = s.max(-1, keepdims=True)
                p = jnp.exp2(s - m)
                l = p.sum(-1, keepdims=True)
                o = jnp.dot(p.astype(BF16), v_u[:, cols], preferred_element_type=F32)
                o_ref[rows, cols] = (o / l).astype(o_ref.dtype)
                lse = jnp.where(lane == h, m + jnp.log2(l), lse)
        if with_lse:
            refs[4 + 2 * n_prev][rows, :] = lse


def _band_attention(src, lead_grid, length, q_map, k_map, v_map, kv_width, out_shape, o_map,
                    t, n_sub, max_dist, with_lse):
    t = min(t, length)
    n_prev = -(-max_dist // t)
    n_sub = min(n_sub, length // t)
    tile = n_sub * t
    kern = functools.partial(_band_kernel, n_prev=n_prev, n_sub=n_sub, max_dist=max_dist,
                             kv_heads=kv_width // HEAD_DIM, with_lse=with_lse)

    def preceding(fn, j):
        def index_map(bb, a, i):
            return fn(bb, a, jnp.maximum(i * n_sub - n_prev + j, 0))
        return index_map

    def kv_specs(fn):
        return ([pl.BlockSpec((None, None, t, kv_width), preceding(fn, j)) for j in range(n_prev)]
                + [pl.BlockSpec((None, None, tile, kv_width), fn)])

    in_specs = [pl.BlockSpec((None, None, tile, GROUP_W), q_map)] + kv_specs(k_map) + kv_specs(v_map)
    o_spec = pl.BlockSpec((None, tile, GROUP_W), o_map)
    if with_lse:
        lse_shape = out_shape[:-1] + (out_shape[-1] // GROUP_W * LANES,)
        out_shapes = (jax.ShapeDtypeStruct(out_shape, BF16), jax.ShapeDtypeStruct(lse_shape, F32))
        out_specs = (o_spec, pl.BlockSpec((None, tile, LANES), o_map))
    else:
        out_shapes = jax.ShapeDtypeStruct(out_shape, BF16)
        out_specs = o_spec
    return pl.pallas_call(
        kern, out_shape=out_shapes, grid=lead_grid + (length // tile,), in_specs=in_specs, out_specs=out_specs,
        compiler_params=_cparams(("parallel", "arbitrary", "arbitrary"), VMEM_LIMIT),
        name="band_attn",
    )(*([src] * (3 + 2 * n_prev)))


def _layer_norm(z, g, b):
    mu = z.mean(-1, keepdims=True)
    zc = z - mu
    var = (zc * zc).mean(-1, keepdims=True)
    return zc * lax.rsqrt(var + LN_EPS) * g + b


def _merge_kernel(ocmp_ref, oslc_ref, owin_ref, gl_ref, ga_ref, gb_ref,
                  d0_ref, d1_ref, d2_ref, l0_ref, l1_ref, l2_ref, x_ref,
                  wa_ref, wb_ref, wo_ref, g_ref, b_ref, wr_ref, br_ref, hf_ref, hp_ref, lg_ref, *, alpha):
    tt = x_ref.shape[0]
    gates = jax.nn.sigmoid(gl_ref[...].astype(F32))
    parts = []
    for h in range(NSA_HEADS):
        sl = slice(h * HEAD_DIM, (h + 1) * HEAD_DIM)
        acc = jnp.zeros((tt, HEAD_DIM), F32)
        for br, ref in enumerate((ocmp_ref, oslc_ref, owin_ref)):
            gcol = gates[:, 3 * h + br:3 * h + br + 1]
            acc = acc + gcol * ref[:, sl].astype(F32)
        parts.append(acc.astype(BF16))
    o_nsa = jnp.concatenate(parts, axis=1)
    l0, l1, l2 = l0_ref[...], l1_ref[...], l2_ref[...]
    lm = jnp.maximum(jnp.maximum(l0, l1), l2)
    e0, e1, e2 = jnp.exp2(l0 - lm), jnp.exp2(l1 - lm), jnp.exp2(l2 - lm)
    inv = 1.0 / (e0 + e1 + e2)
    w0, w1, w2 = e0 * inv, e1 * inv, e2 * inv
    parts = []
    for h in range(DIL_HEADS):
        sl = slice(h * HEAD_DIM, (h + 1) * HEAD_DIM)
        parts.append(w0[:, h:h + 1] * d0_ref[:, sl].astype(F32) + w1[:, h:h + 1] * d1_ref[:, sl].astype(F32)
                     + w2[:, h:h + 1] * d2_ref[:, sl].astype(F32))
    o_dil = jnp.concatenate(parts, axis=1)
    y_a = jnp.dot(o_nsa, wa_ref[...], preferred_element_type=F32)
    y_b = jnp.dot(o_dil.astype(BF16), wb_ref[...], preferred_element_type=F32)
    merged = (jax.nn.sigmoid(ga_ref[...].astype(F32)) * y_a
              + jax.nn.sigmoid(gb_ref[...].astype(F32)) * y_b)
    mix = jnp.dot(merged.astype(BF16), wo_ref[...], preferred_element_type=F32)
    h = _layer_norm(alpha * x_ref[...] + mix, g_ref[...], b_ref[...])
    hf_ref[...] = h
    hb = h.astype(BF16)
    _store_tile_rows(hp_ref, 0, _pack_pairs(hb), hp_ref.shape[0] // tt)
    lg_ref[...] = jnp.dot(hb, wr_ref[...], preferred_element_type=F32) + br_ref[...]


def _merge(o_cmp, o_slc, o_win, main2d, gl_tile, ga_blk, gb_blk, dil_o, dil_lse, x2d,
           w_a, w_b, w_o, ln_g, ln_b, w_r, b_r, alpha, tt):
    T, D = x2d.shape
    nsub = D // 2 // LANES
    nsa_w = NSA_HEADS * HEAD_DIM
    row = lambda i: (i, 0)
    const = lambda i: (0, 0)
    in_specs = [
        pl.BlockSpec((tt, nsa_w), row), pl.BlockSpec((tt, nsa_w), row), pl.BlockSpec((tt, nsa_w), row),
        pl.BlockSpec((tt, LANES), lambda i: (i, gl_tile)),
        pl.BlockSpec((tt, D), lambda i: (i, ga_blk)),
        pl.BlockSpec((tt, D), lambda i: (i, gb_blk)),
    ]
    in_specs += [pl.BlockSpec((tt, DIL_W), row)] * 3 + [pl.BlockSpec((tt, LANES), row)] * 3
    in_specs += [
        pl.BlockSpec((tt, D), row),
        pl.BlockSpec((nsa_w, D), const), pl.BlockSpec((DIL_W, D), const), pl.BlockSpec((D, D), const),
        pl.BlockSpec((1, D), const), pl.BlockSpec((1, D), const),
        pl.BlockSpec((D, LANES), const), pl.BlockSpec((1, LANES), const),
    ]
    return pl.pallas_call(
        functools.partial(_merge_kernel, alpha=alpha),
        out_shape=(jax.ShapeDtypeStruct((T, D), F32), jax.ShapeDtypeStruct((T * nsub, LANES), jnp.uint32),
                   jax.ShapeDtypeStruct((T, LANES), F32)),
        grid=(T // tt,),
        in_specs=in_specs,
        out_specs=(pl.BlockSpec((tt, D), row), pl.BlockSpec((tt * nsub, LANES), row), pl.BlockSpec((tt, LANES), row)),
        compiler_params=_cparams(("parallel",), VMEM_LIMIT),
        name="merge_ln1",
    )(o_cmp, o_slc, o_win, main2d, main2d, main2d, *dil_o, *dil_lse, x2d, w_a, w_b, w_o, ln_g, ln_b, w_r, b_r)


def _router_kernel(lg_ref, tri_ref, meta_ref, cnt_ref, carry_sc):
    i = pl.program_id(0)
    tt = lg_ref.shape[0]

    @pl.when(i == 0)
    def _():
        carry_sc[...] = jnp.zeros(carry_sc.shape, F32)

    logits = lg_ref[...]
    lane = lax.broadcasted_iota(jnp.int32, (tt, LANES), 1)
    v = logits
    onehot = jnp.zeros((tt, LANES), F32)
    vals, idxs = [], []
    for _ in range(TOP_K):
        m = v.max(-1, keepdims=True)
        idx = jnp.where(v == m, lane, LANES).min(-1, keepdims=True)
        hit = lane == idx
        vals.append(m)
        idxs.append(idx)
        onehot = onehot + hit.astype(F32)
        v = jnp.where(hit, -jnp.inf, v)
    exps = [jnp.exp(vk - vals[0]) for vk in vals]
    den = exps[0] + exps[1] + exps[2] + exps[3]
    before = jnp.dot(tri_ref[...], onehot.astype(BF16), preferred_element_type=F32) + carry_sc[0:1, :]
    meta = jnp.zeros((tt, LANES), F32)
    for k in range(TOP_K):
        rank = jnp.where(lane == idxs[k], before, 0.0).sum(-1, keepdims=True)
        meta = jnp.where(lane == k, idxs[k].astype(F32), meta)
        meta = jnp.where(lane == TOP_K + k, exps[k] / den, meta)
        meta = jnp.where(lane == 2 * TOP_K + k, rank, meta)
    meta_ref[...] = meta
    carry_sc[...] = carry_sc[...] + jnp.broadcast_to(onehot.sum(0, keepdims=True), carry_sc.shape)
    cnt_ref[...] = carry_sc[...]


def _router(logits, tt):
    T = logits.shape[0]
    tri = (jnp.arange(tt)[:, None] > jnp.arange(tt)[None, :]).astype(BF16)
    return pl.pallas_call(
        _router_kernel,
        out_shape=(jax.ShapeDtypeStruct((T, LANES), F32), jax.ShapeDtypeStruct((8, LANES), F32)),
        grid=(T // tt,),
        in_specs=[
            pl.BlockSpec((tt, LANES), lambda i: (i, 0)),
            pl.BlockSpec((tt, tt), lambda i: (0, 0)),
        ],
        out_specs=(pl.BlockSpec((tt, LANES), lambda i: (i, 0)), pl.BlockSpec((8, LANES), lambda i: (0, 0))),
        scratch_shapes=[pltpu.VMEM((8, LANES), F32)],
        compiler_params=_cparams(("arbitrary",), VMEM_LIMIT),
        name="router_top4",
    )(logits, tri)


def _dispatch_kernel(pe_ref, dest_ref, h_ref, xs_hbm, zero_sc, sem, *, nsub, tm):
    n = h_ref.shape[0] // nsub
    blk = tm * nsub

    @pl.when(pl.program_id(0) == 0)
    def _():
        zero_sc[...] = jnp.zeros(zero_sc.shape, zero_sc.dtype)

        def clear(e):
            end = pe_ref[e]
            begin = pe_ref[e - 1] if e else 0
            start = pl.multiple_of((end - tm) * nsub, nsub)
            return end > begin, pltpu.make_async_copy(zero_sc, xs_hbm.at[pl.ds(start, blk), :], sem)

        for e in range(pe_ref.shape[0]):
            nonempty, copy = clear(e)
            pl.when(nonempty)(copy.start)
        for e in range(pe_ref.shape[0]):
            nonempty, copy = clear(e)
            pl.when(nonempty)(copy.wait)

    for t in range(n):
        for k in range(TOP_K):
            d = pl.multiple_of(dest_ref[t * TOP_K + k] * nsub, nsub)
            pltpu.make_async_copy(h_ref.at[pl.ds(t * nsub, nsub), :], xs_hbm.at[pl.ds(d, nsub), :],
                                  sem).start(priority=k % 2)
    for k in range(TOP_K):
        pltpu.make_async_copy(h_ref, xs_hbm.at[pl.ds(0, n * nsub), :], sem).wait()


def _dispatch(hp, dest_flat, pad_end, n_rows, nsub, tm, tt):
    T = hp.shape[0] // nsub
    grid_spec = pltpu.PrefetchScalarGridSpec(
        num_scalar_prefetch=1,
        grid=(T // tt,),
        in_specs=[
            pl.BlockSpec((tt * TOP_K,), lambda i, pe: (i,), memory_space=pltpu.SMEM),
            pl.BlockSpec((tt * nsub, LANES), lambda i, pe: (i, 0)),
        ],
        out_specs=pl.BlockSpec(memory_space=pl.ANY),
        scratch_shapes=[pltpu.VMEM((tm * nsub, LANES), hp.dtype), pltpu.SemaphoreType.DMA(())],
    )
    return pl.pallas_call(
        functools.partial(_dispatch_kernel, nsub=nsub, tm=tm),
        out_shape=jax.ShapeDtypeStruct((n_rows * nsub, LANES), hp.dtype),
        grid_spec=grid_spec,
        compiler_params=_cparams(("arbitrary",), VMEM_LIMIT),
        name="moe_dispatch",
    )(pad_end, dest_flat, hp)


def _expert_kernel(be_ref, nu_ref, x_ref, wg_ref, wl_ref, bg_ref, bl_ref, wd_ref, bd_ref, y_ref, xb_sc, *, th, nsub):
    i = pl.program_id(0)
    tm = xb_sc.shape[0]
    half = nsub * LANES
    dh = wg_ref.shape[1]

    @pl.when(i < nu_ref[0])
    def _():
        lo, hi = _unpack_pairs(_load_tile_rows(x_ref, 0, tm, nsub))
        xb_sc[:, :half] = lo.astype(BF16)
        xb_sc[:, half:] = hi.astype(BF16)
        x = xb_sc[...]
        y = bd_ref[...]
        for c in range(dh // th):
            sl = slice(c * th, (c + 1) * th)
            glu = jnp.dot(x, wg_ref[:, sl], preferred_element_type=F32) + bg_ref[:, sl]
            lin = jnp.dot(x, wl_ref[:, sl], preferred_element_type=F32) + bl_ref[:, sl]
            glu = jnp.minimum(glu, SWIGLU_LIMIT)
            lin = jnp.clip(lin, -SWIGLU_LIMIT, SWIGLU_LIMIT)
            act = glu * jax.nn.sigmoid(SWIGLU_ALPHA * glu) * (lin + 1.0)
            y = y + jnp.dot(act.astype(BF16), wd_ref[sl, :], preferred_element_type=F32)
        _store_tile_rows(y_ref, 0, _pack_pairs(y.astype(BF16)), nsub)


def _experts(xs, blk_expert, n_used, w_up, b_up, w_down, b_down, tm, th):
    E, D, two_dh = w_up.shape
    nsub = D // 2 // LANES
    n_blk = xs.shape[0] // (tm * nsub)
    dh = two_dh // 2
    th = min(th, dh)
    once = pl.Buffered(1)

    def row(i, be, nu):
        return (jnp.minimum(i, nu[0] - 1), 0)

    grid_spec = pltpu.PrefetchScalarGridSpec(
        num_scalar_prefetch=2,
        grid=(n_blk,),
        in_specs=[
            pl.BlockSpec((tm * nsub, LANES), row),
            pl.BlockSpec((None, D, dh), lambda i, be, nu: (be[i], 0, 0), pipeline_mode=once),
            pl.BlockSpec((None, D, dh), lambda i, be, nu: (be[i], 0, 1), pipeline_mode=once),
            pl.BlockSpec((None, 1, dh), lambda i, be, nu: (be[i], 0, 0)),
            pl.BlockSpec((None, 1, dh), lambda i, be, nu: (be[i], 0, 1)),
            pl.BlockSpec((None, dh, D), lambda i, be, nu: (be[i], 0, 0), pipeline_mode=once),
            pl.BlockSpec((None, 1, D), lambda i, be, nu: (be[i], 0, 0)),
        ],
        out_specs=pl.BlockSpec((tm * nsub, LANES), row),
        scratch_shapes=[pltpu.VMEM((tm, D), BF16)],
    )
    return pl.pallas_call(
        functools.partial(_expert_kernel, th=th, nsub=nsub),
        out_shape=jax.ShapeDtypeStruct(xs.shape, jnp.uint32),
        grid_spec=grid_spec,
        compiler_params=_cparams(("arbitrary",), VMEM_LIMIT),
        name="moe_experts",
    )(blk_expert, n_used, xs, w_up, w_up, b_up, b_up, w_down, b_down)


def _combine_kernel(dest_ref, dest_next_ref, meta_ref, h_ref, g_ref, b_ref, y_hbm, o_ref, buf, sem, *, alpha, nsub):
    i = pl.program_id(0)
    n = o_ref.shape[0]
    slot = i % 2

    def gather(idx_ref, s):
        for t in range(n):
            for k in range(TOP_K):
                d = pl.multiple_of(idx_ref[t * TOP_K + k] * nsub, nsub)
                pltpu.make_async_copy(y_hbm.at[pl.ds(d, nsub), :], buf.at[s, pl.ds((k * n + t) * nsub, nsub), :],
                                      sem.at[s]).start(priority=k % 2)

    @pl.when(i == 0)
    def _():
        gather(dest_ref, 0)

    @pl.when(i + 1 < pl.num_programs(0))
    def _():
        gather(dest_next_ref, 1 - slot)

    for k in range(TOP_K):
        pltpu.make_async_copy(y_hbm.at[pl.ds(0, n * nsub), :], buf.at[slot, pl.ds(k * n * nsub, n * nsub), :],
                              sem.at[slot]).wait()
    ffn_lo = ffn_hi = None
    for k in range(TOP_K):
        lo, hi = _unpack_pairs(_load_tile_rows(buf, k * n * nsub, n, nsub, lead=(slot,)))
        gate = meta_ref[:, TOP_K + k:TOP_K + k + 1]
        ffn_lo = gate * lo if k == 0 else ffn_lo + gate * lo
        ffn_hi = gate * hi if k == 0 else ffn_hi + gate * hi
    ffn = jnp.concatenate([ffn_lo, ffn_hi], axis=1)
    o_ref[...] = _layer_norm(alpha * h_ref[...] + ffn, g_ref[...], b_ref[...])


def _combine_ln(y, dest_flat, meta, h, g, b, alpha, tt):
    T, D = h.shape
    nsub = D // 2 // LANES
    nt = T // tt
    return pl.pallas_call(
        functools.partial(_combine_kernel, alpha=alpha, nsub=nsub),
        out_shape=jax.ShapeDtypeStruct((T, D), F32),
        grid=(nt,),
        in_specs=[
            pl.BlockSpec((tt * TOP_K,), lambda i: (i,), memory_space=pltpu.SMEM),
            pl.BlockSpec((tt * TOP_K,), lambda i: (jnp.minimum(i + 1, nt - 1),), memory_space=pltpu.SMEM),
            pl.BlockSpec((tt, LANES), lambda i: (i, 0)),
            pl.BlockSpec((tt, D), lambda i: (i, 0)),
            pl.BlockSpec((1, D), lambda i: (0, 0)),
            pl.BlockSpec((1, D), lambda i: (0, 0)),
            pl.BlockSpec(memory_space=pl.ANY),
        ],
        out_specs=pl.BlockSpec((tt, D), lambda i: (i, 0)),
        scratch_shapes=[pltpu.VMEM((2, TOP_K * tt * nsub, LANES), y.dtype), pltpu.SemaphoreType.DMA((2,))],
        compiler_params=_cparams(("arbitrary",), VMEM_LIMIT),
        name="moe_combine_ln2",
    )(dest_flat, dest_flat, meta, h, g, b, y)


def _layer(x, w_in, b_in, pos_k, pos_v, ck_w1, ck_w2, cv_w1, cv_w2, w_br_nsa, w_br_dil, w_out,
           ln1_g, ln1_b, w_router, b_router, w_up, b_up, w_down, b_down, ln2_g, ln2_b, alpha):
    B, S, D = x.shape
    T = B * S
    nd = D // LANES
    G = NSA_KV_GROUPS
    kvw = G * HEAD_DIM
    n_exp = w_router.shape[1]

    o_q = 0
    o_kv = NSA_HEADS * HEAD_DIM
    o_gl = o_kv + 6 * kvw
    o_dil = o_gl + 3 * NSA_HEADS
    o_ga = o_dil + 3 * N_DIL * DIL_W
    o_gb = o_ga + D

    def wcols(a, n):
        return w_in[:, a:a + n], b_in[a:a + n]

    def kv(i):
        return wcols(o_kv + i * kvw, kvw)

    tn = 1024
    gl_w, gl_b = wcols(o_gl, 3 * NSA_HEADS)
    pieces = [wcols(o_ga, D), wcols(o_gb, D), wcols(o_q, NSA_HEADS * HEAD_DIM),
              kv(0), kv(2), kv(4), kv(1), kv(3), kv(5), (gl_w, gl_b)]
    used = sum(p[0].shape[1] for p in pieces)
    n_main = -(-used // tn) * tn
    pieces.append((jnp.zeros((D, n_main - used), F32), jnp.zeros((n_main - used,), F32)))
    w_main = jnp.concatenate([p[0] for p in pieces], axis=1).astype(BF16)
    b_main = jnp.concatenate([p[1] for p in pieces])[None, :]
    t_q = 2 * nd
    t_kc, t_ks, t_kw = t_q + 8, t_q + 10, t_q + 12
    t_vc, t_vs, t_vw = t_q + 14, t_q + 16, t_q + 18
    t_gl = t_q + 20
    tile_id = jnp.arange(n_main // LANES)
    flags_main = jnp.where((tile_id >= t_q) & (tile_id < t_kc), ROPE_Q,
                           jnp.where((tile_id >= t_kc) & (tile_id < t_vc), ROPE, PLAIN)).astype(jnp.int32)
    q_blk0 = t_q // NSA_REP

    pos = jnp.arange(S, dtype=F32)
    inv = ROPE_THETA ** (-jnp.arange(0, HEAD_DIM, 2, dtype=F32) / HEAD_DIM)
    ang = pos[:, None] * inv[None, :]
    cosx = jnp.concatenate([jnp.cos(ang), jnp.cos(ang)], axis=-1)
    sinx = jnp.concatenate([-jnp.sin(ang), jnp.sin(ang)], axis=-1)

    main = _project(x, w_main, b_main, flags_main, cosx, sinx, 1, 1024, tn)

    nC = S // CMP_STRIDE
    half = CMP_STRIDE * HEAD_DIM

    def cmp_in(tile):
        a = main[:, 0, :, tile * LANES:(tile + G) * LANES].reshape(B, S, G, HEAD_DIM)
        return a.transpose(0, 2, 1, 3).reshape(B, G, nC, half)

    xkv = jnp.stack([cmp_in(t_kc), cmp_in(t_vc)])
    pos_kv = jnp.stack([pos_k.reshape(2, half), pos_v.reshape(2, half)])
    w1 = jnp.stack([ck_w1, cv_w1]).astype(BF16)
    w2 = jnp.stack([ck_w2, cv_w2]).astype(BF16)
    kvc = _compress(xkv, pos_kv, w1, w2)

    n_slc = S // SEL_LEN
    assert n_slc <= LANES
    c_start = jnp.arange(nC) * CMP_STRIDE
    jb = jnp.arange(LANES)
    overlap = ((c_start[:, None] < (jb[None, :] + 1) * SEL_LEN) & (c_start[:, None] + CMP_LEN > jb[None, :] * SEL_LEN)
               & (jb[None, :] < n_slc) & (c_start[:, None] + CMP_LEN <= S)).astype(BF16)
    tq = min(256, S)
    o_cmp, sel = _cmp_attention(main, kvc, overlap, q_blk0, tq)

    onehot_t = (jnp.arange(S)[:, None] // SEL_LEN == jnp.arange(LANES)[None, :]).astype(BF16)
    o_slc, w_up_b, w_down_b = _sel_attention(main, sel, onehot_t, w_up, w_down, q_blk0, t_ks, t_vs, 256, 512)

    o_win = _band_attention(
        main, (B, G), S,
        lambda bb, g, i: (bb, 0, i, q_blk0 + g),
        lambda bb, g, i: (bb, 0, i, t_kw + g),
        lambda bb, g, i: (bb, 0, i, t_vw + g),
        HEAD_DIM, (B, S, NSA_HEADS * HEAD_DIM), lambda bb, g, i: (bb, i, g), 256, 2, WIN_LEN - 1, False)

    flags_dil = jnp.array([ROPE_Q] * DIL_HEADS + [ROPE] * DIL_HEADS + [PLAIN] * DIL_HEADS, jnp.int32)
    dil_o, dil_lse = [], []
    for gi, (w, d) in enumerate(DIL_CONFIGS):
        wd, bd = wcols(o_dil + gi * 3 * DIL_W, 3 * DIL_W)
        sub = _project(x, wd.astype(BF16), bd[None, :], flags_dil, cosx, sinx, d, 512, 3 * DIL_W)
        L = S // d
        o_g, lse_g = _band_attention(
            sub, (B, d), L,
            lambda bb, r, i: (bb, r, i, 0),
            lambda bb, r, i: (bb, r, i, 1),
            lambda bb, r, i: (bb, r, i, 2),
            DIL_W, (B, L, d * DIL_W), lambda bb, r, i: (bb, i, r), 128, 4, w // d, True)
        dil_o.append(o_g.reshape(T, DIL_W))
        dil_lse.append(lse_g.reshape(T, LANES))

    w_r = jnp.concatenate([w_router, jnp.zeros((D, LANES - n_exp), F32)], axis=1).astype(BF16)
    b_r = jnp.concatenate([b_router, jnp.full((LANES - n_exp,), NEG, F32)])[None, :]
    h_f, h_p, logits = _merge(
        o_cmp.reshape(T, -1), o_slc.reshape(T, -1), o_win.reshape(T, -1), main.reshape(T, n_main),
        t_gl, 0, 1, dil_o, dil_lse, x.reshape(T, D),
        w_br_nsa.astype(BF16), w_br_dil.astype(BF16), w_out.astype(BF16),
        ln1_g[None, :], ln1_b[None, :], w_r, b_r, alpha, min(256, T))

    meta, cnt = _router(logits, min(512, T))
    top_idx = meta[:, 0:TOP_K].astype(jnp.int32)
    gates = meta[:, TOP_K:2 * TOP_K]
    rank = meta[:, 2 * TOP_K:3 * TOP_K].astype(jnp.int32)

    tm = 512 if T * TOP_K >= 512 * n_exp else 128
    counts = cnt[0, :n_exp].astype(jnp.int32)
    padded = (counts + tm - 1) // tm * tm
    pad_end = jnp.cumsum(padded)
    pad_start = pad_end - padded
    start_of = jnp.where(top_idx[..., None] == jnp.arange(n_exp), pad_start, 0).sum(-1)
    dest = (start_of + rank).reshape(T * TOP_K)
    n_rows = T * TOP_K + n_exp * tm
    n_blk = n_rows // tm
    blk_start = jnp.arange(n_blk, dtype=jnp.int32) * tm
    blk_expert = jnp.minimum((pad_end[None, :] <= blk_start[:, None]).sum(-1), n_exp - 1).astype(jnp.int32)
    n_used = (pad_end[-1:] // tm).astype(jnp.int32)

    xs = _dispatch(h_p, dest, pad_end.astype(jnp.int32), n_rows, D // 2 // LANES, tm, min(256, T))
    y = _experts(xs, blk_expert, n_used, w_up_b, b_up[:, None, :], w_down_b, b_down[:, None, :], tm, 512)
    out = _combine_ln(y, dest, meta, h_f, ln2_g[None, :], ln2_b[None, :], alpha, min(256, T))
    return out.reshape(B, S, D)


def kernel(x, w_in, b_in, cmp_pos_k, cmp_pos_v, cmp_k_w1, cmp_k_w2, cmp_v_w1, cmp_v_w2, w_br_nsa, w_br_dil,
           w_out, ln1_g, ln1_b, w_router, b_router, w_up, b_up, w_down, b_down, ln2_g, ln2_b):
    depth = w_in.shape[0]
    alpha = (2.0 * depth) ** 0.25
    h = x
    for l in range(depth):
        h = _layer(h, w_in[l], b_in[l], cmp_pos_k[l], cmp_pos_v[l], cmp_k_w1[l], cmp_k_w2[l],
                   cmp_v_w1[l], cmp_v_w2[l], w_br_nsa[l], w_br_dil[l], w_out[l], ln1_g[l], ln1_b[l],
                   w_router[l], b_router[l], w_up[l], b_up[l], w_down[l], b_down[l], ln2_g[l], ln2_b[l], alpha)
    return h
```

```python
import functools

import jax
import jax.numpy as jnp
from jax import lax
from jax.experimental import pallas as pl
from jax.experimental.pallas import tpu as pltpu

F32 = jnp.float32
BF16 = jnp.bfloat16

HEAD_DIM = 128
LANES = 128
ROPE_THETA = 10000.0
NSA_HEADS = 8
NSA_KV_GROUPS = 2
NSA_REP = NSA_HEADS // NSA_KV_GROUPS
CMP_LEN = 32
CMP_STRIDE = 16
SEL_LEN = 64
SEL_TOPK = 16
WIN_LEN = 512
DIL_CONFIGS = ((128, 1), (512, 4), (2048, 16))
N_DIL = len(DIL_CONFIGS)
DIL_HEADS = 4
TOP_K = 4
SWIGLU_LIMIT = 7.0
SWIGLU_ALPHA = 1.702
LN_EPS = 1e-5
NEG = -1e30
FORCE = 1e9
SCALE = HEAD_DIM ** -0.5
LOG2E = 1.4426950408889634
Q_SCALE = SCALE * LOG2E
PLAIN, ROPE, ROPE_Q = 0, 1, 2

GROUP_W = NSA_REP * HEAD_DIM
DIL_W = DIL_HEADS * HEAD_DIM
VMEM_LIMIT = 56 * 1024 * 1024


def _cparams(sem, vmem=None):
    return pltpu.CompilerParams(dimension_semantics=sem, vmem_limit_bytes=vmem)


def _masked_softmax(s, mask):
    s = jnp.where(mask, s, NEG)
    m = s.max(-1, keepdims=True)
    p = jnp.where(mask, jnp.exp2(s - m), 0.0)
    den = p.sum(-1, keepdims=True)
    safe = jnp.where(den > 0, den, 1.0)
    return p, m, safe


def _qk(q, k):
    return lax.dot_general(q, k, (((1,), (1,)), ((), ())), preferred_element_type=F32)


def _pack_pairs(xb):
    m = xb.shape[1] // 2
    lo = lax.bitcast_convert_type(xb[:, :m].astype(F32), jnp.uint32)
    hi = lax.bitcast_convert_type(xb[:, m:].astype(F32), jnp.uint32)
    return (lo >> 16) | (hi & jnp.uint32(0xFFFF0000))


def _unpack_pairs(w):
    lo = lax.bitcast_convert_type(w << 16, F32)
    hi = lax.bitcast_convert_type(w & jnp.uint32(0xFFFF0000), F32)
    return lo, hi


def _store_tile_rows(ref, start, packed, nsub):
    n = packed.shape[0]
    for j in range(nsub):
        ref[pl.ds(start + j, n, stride=nsub), :] = packed[:, j * LANES:(j + 1) * LANES]


def _load_tile_rows(ref, start, n, nsub, lead=()):
    return jnp.concatenate([ref[lead + (pl.ds(start + j, n, stride=nsub), slice(None))] for j in range(nsub)],
                           axis=1)


def _proj_kernel(flags_ref, x_ref, w_ref, b_ref, cos_ref, sin_ref, o_ref, xb_ref, *xcol_ref):
    j = pl.program_id(2)
    d, sub, tn = o_ref.shape
    n_sub = tn // LANES

    @pl.when(j == 0)
    def _():
        if d == 1:
            xb_ref[...] = x_ref[...].astype(BF16)
        else:
            xcol, = xcol_ref
            for c in range(xcol.shape[0]):
                xcol[c] = x_ref[:, c * LANES:(c + 1) * LANES]
            for c in range(xcol.shape[0]):
                for r in range(d):
                    xb_ref[r * sub:(r + 1) * sub, c * LANES:(c + 1) * LANES] = (
                        xcol[c, pl.ds(r, sub, stride=d), :].astype(BF16))

    acc = jnp.dot(xb_ref[...], w_ref[...], preferred_element_type=F32) + b_ref[...]
    for u in range(n_sub):
        a = acc[:, u * LANES:(u + 1) * LANES]
        roped = a * cos_ref[...] + pltpu.roll(a, HEAD_DIM // 2, 1) * sin_ref[...]
        flag = flags_ref[j * n_sub + u]
        mult = jnp.where(flag == ROPE_Q, Q_SCALE, 1.0).astype(F32)
        res = (jnp.where(flag == PLAIN, a, roped) * mult).astype(o_ref.dtype)
        for r in range(d):
            o_ref[r, :, u * LANES:(u + 1) * LANES] = res[r * sub:(r + 1) * sub]


def _project(x, w, b, flags, cosx, sinx, d, tm, tn):
    B, S, D = x.shape
    N = w.shape[1]
    L = S // d
    tm = min(tm, S)
    sub = tm // d

    def regroup(tab):
        return tab.reshape(S // tm, sub, d, HEAD_DIM).transpose(0, 2, 1, 3).reshape(S, HEAD_DIM)

    grid_spec = pltpu.PrefetchScalarGridSpec(
        num_scalar_prefetch=1,
        grid=(B, S // tm, N // tn),
        in_specs=[
            pl.BlockSpec((None, tm, D), lambda bb, i, j, f: (bb, i, 0)),
            pl.BlockSpec((D, tn), lambda bb, i, j, f: (0, j)),
            pl.BlockSpec((1, tn), lambda bb, i, j, f: (0, j)),
            pl.BlockSpec((tm, HEAD_DIM), lambda bb, i, j, f: (i, 0)),
            pl.BlockSpec((tm, HEAD_DIM), lambda bb, i, j, f: (i, 0)),
        ],
        out_specs=pl.BlockSpec((None, d, sub, tn), lambda bb, i, j, f: (bb, 0, i, j)),
        scratch_shapes=[pltpu.VMEM((tm, D), BF16)] + ([pltpu.VMEM((D // LANES, tm, LANES), F32)] if d > 1 else []),
    )
    return pl.pallas_call(
        _proj_kernel,
        out_shape=jax.ShapeDtypeStruct((B, d, L, N), BF16),
        grid_spec=grid_spec,
        compiler_params=_cparams(("parallel", "arbitrary", "arbitrary"), VMEM_LIMIT),
        name="proj_rope",
    )(flags, x, w, b, regroup(cosx), regroup(sinx))


def _gelu_tanh(x):
    return 0.5 * x * (1.0 + jnp.tanh(0.7978845608028654 * (x + 0.044715 * x * x * x)))


def _compress_kernel(x_ref, pos_ref, w1_ref, w2_ref, o_ref, xf_sc):
    s_len = x_ref.shape[0]
    nc = s_len // CMP_STRIDE
    xf_sc[0:s_len, :] = x_ref[...].astype(F32)
    xf_sc[s_len:s_len + CMP_STRIDE, :] = jnp.zeros((CMP_STRIDE, HEAD_DIM), F32)
    h = jnp.zeros((nc, w1_ref.shape[1]), F32)
    for l in range(CMP_LEN):
        tok = xf_sc[pl.ds(l, nc, stride=CMP_STRIDE), :]
        a = (tok + pos_ref[l:l + 1, :]).astype(BF16)
        h = h + jnp.dot(a, w1_ref[l * HEAD_DIM:(l + 1) * HEAD_DIM, :], preferred_element_type=F32)
    g = _gelu_tanh(h).astype(BF16)
    o_ref[...] = jnp.dot(g, w2_ref[...], preferred_element_type=F32).astype(o_ref.dtype)


def _compress(main, k_tile0, v_tile0, pos, w1, w2):
    B, _, S, _ = main.shape
    G = NSA_KV_GROUPS
    nC = S // CMP_STRIDE
    hid = w1.shape[-1]
    return pl.pallas_call(
        _compress_kernel,
        out_shape=jax.ShapeDtypeStruct((2, B, G, nC, HEAD_DIM), BF16),
        grid=(2, B, G),
        in_specs=[
            pl.BlockSpec((None, None, S, HEAD_DIM),
                         lambda a, bb, g: (bb, 0, 0, k_tile0 + a * (v_tile0 - k_tile0) + g)),
            pl.BlockSpec((None, CMP_LEN, HEAD_DIM), lambda a, bb, g: (a, 0, 0)),
            pl.BlockSpec((None, CMP_LEN * HEAD_DIM, hid), lambda a, bb, g: (a, 0, 0)),
            pl.BlockSpec((None, hid, HEAD_DIM), lambda a, bb, g: (a, 0, 0)),
        ],
        out_specs=pl.BlockSpec((None, None, None, nC, HEAD_DIM), lambda a, bb, g: (a, bb, g, 0, 0)),
        scratch_shapes=[pltpu.VMEM((S + CMP_STRIDE, HEAD_DIM), F32)],
        compiler_params=_cparams(("arbitrary", "arbitrary", "arbitrary"), VMEM_LIMIT),
        name="compress_mlp",
    )(main, pos, w1, w2)


def _cmp_attn_kernel(q_ref, kc_ref, vc_ref, ov_ref, o_ref, sel_ref, *, n_slc, n_sel):
    qi = pl.program_id(2)
    tq = q_ref.shape[0]
    nc = kc_ref.shape[0]
    t = qi * tq + lax.broadcasted_iota(jnp.int32, (tq, nc), 0)
    c = lax.broadcasted_iota(jnp.int32, (tq, nc), 1)
    mask = (c * CMP_STRIDE + (CMP_LEN - 1)) <= t
    kc = kc_ref[...]
    vc = vc_ref[...]
    ps = jnp.zeros((tq, nc), F32)
    for h in range(NSA_REP):
        s = _qk(q_ref[:, h * HEAD_DIM:(h + 1) * HEAD_DIM], kc)
        p, _, safe = _masked_softmax(s, mask)
        p = p / safe
        o = jnp.dot(p.astype(BF16), vc, preferred_element_type=F32)
        o_ref[:, h * HEAD_DIM:(h + 1) * HEAD_DIM] = o.astype(o_ref.dtype)
        ps = ps + p
    imp = jnp.dot(ps.astype(BF16), ov_ref[...], preferred_element_type=F32)
    tj = qi * tq + lax.broadcasted_iota(jnp.int32, (tq, LANES), 0)
    j = lax.broadcasted_iota(jnp.int32, (tq, LANES), 1)
    cur = tj // SEL_LEN
    forced = (j == 0) | (j == cur) | (j == cur - 1)
    imp = jnp.where(forced, FORCE, jnp.where(j > cur, NEG, imp))
    imp_t = imp.T[0:n_slc, :]
    grp = 8
    groups = [imp_t[a:a + grp, :] for a in range(0, n_slc, grp)]
    ranks = [jnp.zeros(gv.shape, jnp.int32) for gv in groups]
    for j2 in range(n_slc):
        row = imp_t[j2:j2 + 1, :]
        for gi, gv in enumerate(groups):
            lo = gi * grp
            if lo > j2:
                ahead = row >= gv
            elif lo + gv.shape[0] - 1 <= j2:
                ahead = row > gv
            else:
                later = lax.broadcasted_iota(jnp.int32, gv.shape, 0) + lo > j2
                ahead = (row > gv) | ((row == gv) & later)
            ranks[gi] = jnp.where(ahead, ranks[gi] + 1, ranks[gi])
    rank = jnp.concatenate(ranks, axis=0)
    sel_t = jnp.where(rank < n_sel, 0.0, NEG)
    if n_slc < LANES:
        sel_t = jnp.concatenate([sel_t, jnp.zeros((LANES - n_slc, tq), F32)], axis=0)
    sel_ref[...] = sel_t.T.astype(sel_ref.dtype)


def _cmp_attention(main, kvc, overlap, q_blk0, tq):
    B, _, S, _ = main.shape
    G = NSA_KV_GROUPS
    nC = kvc.shape[3]
    n_slc = S // SEL_LEN
    n_sel = min(SEL_TOPK, n_slc)
    kern = functools.partial(_cmp_attn_kernel, n_slc=n_slc, n_sel=n_sel)
    return pl.pallas_call(
        kern,
        out_shape=(jax.ShapeDtypeStruct((B, S, NSA_HEADS * HEAD_DIM), BF16),
                   jax.ShapeDtypeStruct((B, G, S, LANES), BF16)),
        grid=(B, G, S // tq),
        in_specs=[
            pl.BlockSpec((None, None, tq, GROUP_W), lambda bb, g, i: (bb, 0, i, q_blk0 + g)),
            pl.BlockSpec((None, None, None, nC, HEAD_DIM), lambda bb, g, i: (0, bb, g, 0, 0)),
            pl.BlockSpec((None, None, None, nC, HEAD_DIM), lambda bb, g, i: (1, bb, g, 0, 0)),
            pl.BlockSpec((nC, LANES), lambda bb, g, i: (0, 0)),
        ],
        out_specs=(pl.BlockSpec((None, tq, GROUP_W), lambda bb, g, i: (bb, i, g)),
                   pl.BlockSpec((None, None, tq, LANES), lambda bb, g, i: (bb, g, i, 0))),
        compiler_params=_cparams(("parallel", "arbitrary", "arbitrary"), VMEM_LIMIT),
        name="cmp_attn_select",
    )(main, kvc, kvc, overlap)


def _fold_lanes(x, op):
    out = x[:, 0:LANES]
    for u in range(1, x.shape[1] // LANES):
        out = op(out, x[:, u * LANES:(u + 1) * LANES])
    return out


def _sel_attn_kernel(q_ref, k_ref, v_ref, bias_ref, et_ref, wu_ref, wd_ref,
                     o_ref, wu_out, wd_out, qx_sc, s_sc, m_sc, l_sc, acc_sc):
    wu_out[...] = wu_ref[...].astype(wu_out.dtype)
    wd_out[...] = wd_ref[...].astype(wd_out.dtype)
    qi = pl.program_id(2)
    t = q_ref.shape[0]
    tc = s_sc.shape[2]
    rows = NSA_REP * t
    for h in range(NSA_REP):
        qx_sc[h * t:(h + 1) * t, 0:HEAD_DIM] = q_ref[:, h * HEAD_DIM:(h + 1) * HEAD_DIM]
        qx_sc[h * t:(h + 1) * t, HEAD_DIM:2 * HEAD_DIM] = bias_ref[...]

    def scores(c, w=1):
        start = pl.multiple_of(c * tc, tc)
        kx = jnp.concatenate([k_ref[pl.ds(start, w * tc), :], et_ref[pl.ds(start, w * tc), :]], axis=1)
        return _qk(qx_sc[...], kx)

    m_sc[...] = jnp.full(m_sc.shape, NEG, F32)

    def for_spans(n, fn):
        def trip(j, carry):
            fn(2 * j, 2)
            return carry

        lax.fori_loop(0, n // 2, trip, 0)

        @pl.when(n % 2 == 1)
        def _():
            fn(n - 1, 1)

    def max_span(c, w):
        s = scores(c, w)
        for u in range(w):
            s_sc[c + u] = s[:, u * tc:(u + 1) * tc]
        m_sc[...] = jnp.maximum(m_sc[...], _fold_lanes(s, jnp.maximum))

    n_full = (qi * t) // tc
    for_spans(n_full, max_span)
    qpos = qi * t + lax.broadcasted_iota(jnp.int32, (rows, tc), 0) % t
    kpos = n_full * tc + lax.broadcasted_iota(jnp.int32, (rows, tc), 1)
    s = jnp.where(kpos <= qpos, scores(n_full), NEG)
    s_sc[n_full] = s
    m = jnp.maximum(m_sc[...], _fold_lanes(s, jnp.maximum)).max(-1, keepdims=True)
    m_sc[...] = jnp.broadcast_to(m, m_sc.shape)
    l_sc[...] = jnp.zeros(l_sc.shape, F32)
    acc_sc[...] = jnp.zeros(acc_sc.shape, F32)

    def exp_span(c, w):
        mb = m_sc[...]
        p = jnp.concatenate([jnp.exp2(s_sc[c + u][:, v * LANES:(v + 1) * LANES] - mb)
                             for u in range(w) for v in range(tc // LANES)], axis=1)
        l_sc[...] += _fold_lanes(p, jnp.add)
        vals = v_ref[pl.ds(pl.multiple_of(c * tc, tc), w * tc), :]
        acc_sc[...] += jnp.dot(p.astype(BF16), vals, preferred_element_type=F32)

    for_spans(n_full + 1, exp_span)
    l = l_sc[...].sum(-1, keepdims=True)
    o = acc_sc[...] / jnp.where(l > 0, l, 1.0)
    for h in range(NSA_REP):
        o_ref[:, h * HEAD_DIM:(h + 1) * HEAD_DIM] = o[h * t:(h + 1) * t].astype(o_ref.dtype)


def _sel_attention(main, bias, onehot_t, w_up, w_down, q_blk0, k_tile0, v_tile0, t, tc):
    B, _, S, _ = main.shape
    G = NSA_KV_GROUPS
    t = min(t, S)
    tc = min(tc, S)
    rows = NSA_REP * t
    nq = S // t
    n_steps = B * G * nq
    wu2 = w_up.reshape(-1, w_up.shape[-1])
    wd2 = w_down.reshape(-1, w_down.shape[-1])
    ru, rd = wu2.shape[0] // n_steps, wd2.shape[0] // n_steps
    assert ru * n_steps == wu2.shape[0] and rd * n_steps == wd2.shape[0]

    def slab(bb, g, i):
        return ((bb * G + g) * nq + i, 0)

    o, wu_b, wd_b = pl.pallas_call(
        _sel_attn_kernel,
        out_shape=(jax.ShapeDtypeStruct((B, S, NSA_HEADS * HEAD_DIM), BF16),
                   jax.ShapeDtypeStruct(wu2.shape, BF16), jax.ShapeDtypeStruct(wd2.shape, BF16)),
        grid=(B, G, nq),
        in_specs=[
            pl.BlockSpec((None, None, t, GROUP_W), lambda bb, g, i: (bb, 0, i, q_blk0 + g)),
            pl.BlockSpec((None, None, S, HEAD_DIM), lambda bb, g, i: (bb, 0, 0, k_tile0 + g)),
            pl.BlockSpec((None, None, S, HEAD_DIM), lambda bb, g, i: (bb, 0, 0, v_tile0 + g)),
            pl.BlockSpec((None, None, t, LANES), lambda bb, g, i: (bb, g, i, 0)),
            pl.BlockSpec((S, LANES), lambda bb, g, i: (0, 0)),
            pl.BlockSpec((ru, wu2.shape[1]), slab),
            pl.BlockSpec((rd, wd2.shape[1]), slab),
        ],
        out_specs=(pl.BlockSpec((None, t, GROUP_W), lambda bb, g, i: (bb, i, g)),
                   pl.BlockSpec((ru, wu2.shape[1]), slab), pl.BlockSpec((rd, wd2.shape[1]), slab)),
        scratch_shapes=[pltpu.VMEM((rows, 2 * HEAD_DIM), BF16), pltpu.VMEM((S // tc, rows, tc), F32),
                        pltpu.VMEM((rows, LANES), F32), pltpu.VMEM((rows, LANES), F32),
                        pltpu.VMEM((rows, HEAD_DIM), F32)],
        compiler_params=_cparams(("parallel", "arbitrary", "arbitrary"), VMEM_LIMIT),
        name="selected_attn",
    )(main, main, main, bias, onehot_t, wu2, wd2)
    return o, wu_b.reshape(w_up.shape), wd_b.reshape(w_down.shape)


def _band_kernel(*refs, n_prev, n_sub, max_dist, kv_heads, with_lse):
    q_ref = refs[0]
    k_refs = refs[1:2 + n_prev]
    v_refs = refs[2 + n_prev:3 + 2 * n_prev]
    o_ref = refs[3 + 2 * n_prev]
    qi = pl.program_id(2)
    t = k_refs[0].shape[0]
    nk = (n_prev + 1) * t
    n_heads = q_ref.shape[1] // HEAD_DIM
    k_all = jnp.concatenate([r[...] for r in k_refs], axis=0)
    v_all = jnp.concatenate([r[...] for r in v_refs], axis=0)
    kcol = lax.broadcasted_iota(jnp.int32, (t, nk), 1)
    diff = n_prev * t + lax.broadcasted_iota(jnp.int32, (t, nk), 0) - kcol
    in_band = (diff >= 0) & (diff <= max_dist)
    lane = lax.broadcasted_iota(jnp.int32, (t, LANES), 1)
    for u in range(n_sub):
        first_key = (qi * n_sub + u - n_prev) * t
        bias = jnp.where(in_band & (kcol + first_key >= 0), 0.0, NEG)
        rows = slice(u * t, (u + 1) * t)
        k_u = k_all[u * t:u * t + nk]
        v_u = v_all[u * t:u * t + nk]
        lse = jnp.zeros((t, LANES), F32)
        if kv_heads == 1:
            q = jnp.concatenate([q_ref[rows, h * HEAD_DIM:(h + 1) * HEAD_DIM] for h in range(n_heads)], axis=0)
            s = _qk(q, k_u).reshape(n_heads, t, nk) + bias[None]
            m = s.max(-1, keepdims=True)
            p = jnp.exp2(s - m)
            l = p.sum(-1, keepdims=True)
            o = jnp.dot(p.astype(BF16).reshape(n_heads * t, nk), v_u, preferred_element_type=F32)
            o = o.reshape(n_heads, t, HEAD_DIM) / l
            for h in range(n_heads):
                o_ref[rows, h * HEAD_DIM:(h + 1) * HEAD_DIM] = o[h].astype(o_ref.dtype)
                lse = jnp.where(lane == h, m[h] + jnp.log2(l[h]), lse)
        else:
            for h in range(n_heads):
                cols = slice(h * HEAD_DIM, (h + 1) * HEAD_DIM)
                s = _qk(q_ref[rows, cols], k_u[:, cols]) + bias
                m = s.max(-1, keepdims=True)
                p = jnp.exp2(s - m)
                l = p.sum(-1, keepdims=True)
                o = jnp.dot(p.astype(BF16), v_u[:, cols], preferred_element_type=F32)
                o_ref[rows, cols] = (o / l).astype(o_ref.dtype)
                lse = jnp.where(lane == h, m + jnp.log2(l), lse)
        if with_lse:
            refs[4 + 2 * n_prev][rows, :] = lse


def _band_attention(src, lead_grid, length, q_map, k_map, v_map, kv_width, out_shape, o_map,
                    t, n_sub, max_dist, with_lse):
    t = min(t, length)
    n_prev = -(-max_dist // t)
    n_sub = min(n_sub, length // t)
    tile = n_sub * t
    kern = functools.partial(_band_kernel, n_prev=n_prev, n_sub=n_sub, max_dist=max_dist,
                             kv_heads=kv_width // HEAD_DIM, with_lse=with_lse)

    def preceding(fn, j):
        def index_map(bb, a, i):
            return fn(bb, a, jnp.maximum(i * n_sub - n_prev + j, 0))
        return index_map

    def kv_specs(fn):
        return ([pl.BlockSpec((None, None, t, kv_width), preceding(fn, j)) for j in range(n_prev)]
                + [pl.BlockSpec((None, None, tile, kv_width), fn)])

    in_specs = [pl.BlockSpec((None, None, tile, GROUP_W), q_map)] + kv_specs(k_map) + kv_specs(v_map)
    o_spec = pl.BlockSpec((None, tile, GROUP_W), o_map)
    if with_lse:
        lse_shape = out_shape[:-1] + (out_shape[-1] // GROUP_W * LANES,)
        out_shapes = (jax.ShapeDtypeStruct(out_shape, BF16), jax.ShapeDtypeStruct(lse_shape, F32))
        out_specs = (o_spec, pl.BlockSpec((None, tile, LANES), o_map))
    else:
        out_shapes = jax.ShapeDtypeStruct(out_shape, BF16)
        out_specs = o_spec
    return pl.pallas_call(
        kern, out_shape=out_shapes, grid=lead_grid + (length // tile,), in_specs=in_specs, out_specs=out_specs,
        compiler_params=_cparams(("parallel", "arbitrary", "arbitrary"), VMEM_LIMIT),
        name="band_attn",
    )(*([src] * (3 + 2 * n_prev)))


def _layer_norm(z, g, b):
    mu = z.mean(-1, keepdims=True)
    zc = z - mu
    var = (zc * zc).mean(-1, keepdims=True)
    return zc * lax.rsqrt(var + LN_EPS) * g + b


def _merge_kernel(ocmp_ref, oslc_ref, owin_ref, gl_ref, ga_ref, gb_ref,
                  d0_ref, d1_ref, d2_ref, l0_ref, l1_ref, l2_ref, x_ref,
                  wa_ref, wb_ref, wo_ref, g_ref, b_ref, wr_ref, br_ref, hf_ref, hp_ref, lg_ref, *, alpha):
    tt = x_ref.shape[0]
    gates = jax.nn.sigmoid(gl_ref[...].astype(F32))
    parts = []
    for h in range(NSA_HEADS):
        sl = slice(h * HEAD_DIM, (h + 1) * HEAD_DIM)
        acc = jnp.zeros((tt, HEAD_DIM), F32)
        for br, ref in enumerate((ocmp_ref, oslc_ref, owin_ref)):
            gcol = gates[:, 3 * h + br:3 * h + br + 1]
            acc = acc + gcol * ref[:, sl].astype(F32)
        parts.append(acc.astype(BF16))
    o_nsa = jnp.concatenate(parts, axis=1)
    l0, l1, l2 = l0_ref[...], l1_ref[...], l2_ref[...]
    lm = jnp.maximum(jnp.maximum(l0, l1), l2)
    e0, e1, e2 = jnp.exp2(l0 - lm), jnp.exp2(l1 - lm), jnp.exp2(l2 - lm)
    inv = 1.0 / (e0 + e1 + e2)
    w0, w1, w2 = e0 * inv, e1 * inv, e2 * inv
    parts = []
    for h in range(DIL_HEADS):
        sl = slice(h * HEAD_DIM, (h + 1) * HEAD_DIM)
        parts.append(w0[:, h:h + 1] * d0_ref[:, sl].astype(F32) + w1[:, h:h + 1] * d1_ref[:, sl].astype(F32)
                     + w2[:, h:h + 1] * d2_ref[:, sl].astype(F32))
    o_dil = jnp.concatenate(parts, axis=1)
    y_a = jnp.dot(o_nsa, wa_ref[...], preferred_element_type=F32)
    y_b = jnp.dot(o_dil.astype(BF16), wb_ref[...], preferred_element_type=F32)
    merged = (jax.nn.sigmoid(ga_ref[...].astype(F32)) * y_a
              + jax.nn.sigmoid(gb_ref[...].astype(F32)) * y_b)
    mix = jnp.dot(merged.astype(BF16), wo_ref[...], preferred_element_type=F32)
    h = _layer_norm(alpha * x_ref[...] + mix, g_ref[...], b_ref[...])
    hf_ref[...] = h
    hb = h.astype(BF16)
    _store_tile_rows(hp_ref, 0, _pack_pairs(hb), hp_ref.shape[0] // tt)
    lg_ref[...] = jnp.dot(hb, wr_ref[...], preferred_element_type=F32) + br_ref[...]


def _merge(o_cmp, o_slc, o_win, main2d, gl_tile, ga_blk, gb_blk, dil_o, dil_lse, x2d,
           w_a, w_b, w_o, ln_g, ln_b, w_r, b_r, alpha, tt):
    T, D = x2d.shape
    nsub = D // 2 // LANES
    nsa_w = NSA_HEADS * HEAD_DIM
    row = lambda i: (i, 0)
    const = lambda i: (0, 0)
    in_specs = [
        pl.BlockSpec((tt, nsa_w), row), pl.BlockSpec((tt, nsa_w), row), pl.BlockSpec((tt, nsa_w), row),
        pl.BlockSpec((tt, LANES), lambda i: (i, gl_tile)),
        pl.BlockSpec((tt, D), lambda i: (i, ga_blk)),
        pl.BlockSpec((tt, D), lambda i: (i, gb_blk)),
    ]
    in_specs += [pl.BlockSpec((tt, DIL_W), row)] * 3 + [pl.BlockSpec((tt, LANES), row)] * 3
    in_specs += [
        pl.BlockSpec((tt, D), row),
        pl.BlockSpec((nsa_w, D), const), pl.BlockSpec((DIL_W, D), const), pl.BlockSpec((D, D), const),
        pl.BlockSpec((1, D), const), pl.BlockSpec((1, D), const),
        pl.BlockSpec((D, LANES), const), pl.BlockSpec((1, LANES), const),
    ]
    return pl.pallas_call(
        functools.partial(_merge_kernel, alpha=alpha),
        out_shape=(jax.ShapeDtypeStruct((T, D), F32), jax.ShapeDtypeStruct((T * nsub, LANES), jnp.uint32),
                   jax.ShapeDtypeStruct((T, LANES), F32)),
        grid=(T // tt,),
        in_specs=in_specs,
        out_specs=(pl.BlockSpec((tt, D), row), pl.BlockSpec((tt * nsub, LANES), row), pl.BlockSpec((tt, LANES), row)),
        compiler_params=_cparams(("parallel",), VMEM_LIMIT),
        name="merge_ln1",
    )(o_cmp, o_slc, o_win, main2d, main2d, main2d, *dil_o, *dil_lse, x2d, w_a, w_b, w_o, ln_g, ln_b, w_r, b_r)


def _router_kernel(lg_ref, tri_ref, meta_ref, cnt_ref, carry_sc):
    i = pl.program_id(0)
    tt = lg_ref.shape[0]

    @pl.when(i == 0)
    def _():
        carry_sc[...] = jnp.zeros(carry_sc.shape, F32)

    logits = lg_ref[...]
    lane = lax.broadcasted_iota(jnp.int32, (tt, LANES), 1)
    v = logits
    onehot = jnp.zeros((tt, LANES), F32)
    vals, idxs = [], []
    for _ in range(TOP_K):
        m = v.max(-1, keepdims=True)
        idx = jnp.where(v == m, lane, LANES).min(-1, keepdims=True)
        hit = lane == idx
        vals.append(m)
        idxs.append(idx)
        onehot = onehot + hit.astype(F32)
        v = jnp.where(hit, -jnp.inf, v)
    exps = [jnp.exp(vk - vals[0]) for vk in vals]
    den = exps[0] + exps[1] + exps[2] + exps[3]
    before = jnp.dot(tri_ref[...], onehot.astype(BF16), preferred_element_type=F32) + carry_sc[0:1, :]
    meta = jnp.zeros((tt, LANES), F32)
    for k in range(TOP_K):
        rank = jnp.where(lane == idxs[k], before, 0.0).sum(-1, keepdims=True)
        meta = jnp.where(lane == k, idxs[k].astype(F32), meta)
        meta = jnp.where(lane == TOP_K + k, exps[k] / den, meta)
        meta = jnp.where(lane == 2 * TOP_K + k, rank, meta)
    meta_ref[...] = meta
    carry_sc[...] = carry_sc[...] + jnp.broadcast_to(onehot.sum(0, keepdims=True), carry_sc.shape)
    cnt_ref[...] = carry_sc[...]


def _router(logits, tt):
    T = logits.shape[0]
    tri = (jnp.arange(tt)[:, None] > jnp.arange(tt)[None, :]).astype(BF16)
    return pl.pallas_call(
        _router_kernel,
        out_shape=(jax.ShapeDtypeStruct((T, LANES), F32), jax.ShapeDtypeStruct((8, LANES), F32)),
        grid=(T // tt,),
        in_specs=[
            pl.BlockSpec((tt, LANES), lambda i: (i, 0)),
            pl.BlockSpec((tt, tt), lambda i: (0, 0)),
        ],
        out_specs=(pl.BlockSpec((tt, LANES), lambda i: (i, 0)), pl.BlockSpec((8, LANES), lambda i: (0, 0))),
        scratch_shapes=[pltpu.VMEM((8, LANES), F32)],
        compiler_params=_cparams(("arbitrary",), VMEM_LIMIT),
        name="router_top4",
    )(logits, tri)


def _dispatch_kernel(pe_ref, dest_ref, h_ref, xs_hbm, zero_sc, sem, *, nsub, tm):
    n = h_ref.shape[0] // nsub
    blk = tm * nsub

    @pl.when(pl.program_id(0) == 0)
    def _():
        zero_sc[...] = jnp.zeros(zero_sc.shape, zero_sc.dtype)

        def clear(e):
            end = pe_ref[e]
            begin = pe_ref[e - 1] if e else 0
            start = pl.multiple_of((end - tm) * nsub, nsub)
            return end > begin, pltpu.make_async_copy(zero_sc, xs_hbm.at[pl.ds(start, blk), :], sem)

        for e in range(pe_ref.shape[0]):
            nonempty, copy = clear(e)
            pl.when(nonempty)(copy.start)
        for e in range(pe_ref.shape[0]):
            nonempty, copy = clear(e)
            pl.when(nonempty)(copy.wait)

    for t in range(n):
        for k in range(TOP_K):
            d = pl.multiple_of(dest_ref[t * TOP_K + k] * nsub, nsub)
            pltpu.make_async_copy(h_ref.at[pl.ds(t * nsub, nsub), :], xs_hbm.at[pl.ds(d, nsub), :],
                                  sem).start(priority=k % 2)
    for k in range(TOP_K):
        pltpu.make_async_copy(h_ref, xs_hbm.at[pl.ds(0, n * nsub), :], sem).wait()


def _dispatch(hp, dest_flat, pad_end, n_rows, nsub, tm, tt):
    T = hp.shape[0] // nsub
    grid_spec = pltpu.PrefetchScalarGridSpec(
        num_scalar_prefetch=1,
        grid=(T // tt,),
        in_specs=[
            pl.BlockSpec((tt * TOP_K,), lambda i, pe: (i,), memory_space=pltpu.SMEM),
            pl.BlockSpec((tt * nsub, LANES), lambda i, pe: (i, 0)),
        ],
        out_specs=pl.BlockSpec(memory_space=pl.ANY),
        scratch_shapes=[pltpu.VMEM((tm * nsub, LANES), hp.dtype), pltpu.SemaphoreType.DMA(())],
    )
    return pl.pallas_call(
        functools.partial(_dispatch_kernel, nsub=nsub, tm=tm),
        out_shape=jax.ShapeDtypeStruct((n_rows * nsub, LANES), hp.dtype),
        grid_spec=grid_spec,
        compiler_params=_cparams(("arbitrary",), VMEM_LIMIT),
        name="moe_dispatch",
    )(pad_end, dest_flat, hp)


def _expert_kernel(be_ref, nu_ref, x_ref, wg_ref, wl_ref, bg_ref, bl_ref, wd_ref, bd_ref, y_ref, xb_sc, *, th, nsub):
    i = pl.program_id(0)
    tm = xb_sc.shape[0]
    half = nsub * LANES
    dh = wg_ref.shape[1]

    @pl.when(i < nu_ref[0])
    def _():
        lo, hi = _unpack_pairs(_load_tile_rows(x_ref, 0, tm, nsub))
        xb_sc[:, :half] = lo.astype(BF16)
        xb_sc[:, half:] = hi.astype(BF16)
        x = xb_sc[...]
        y = bd_ref[...]
        for c in range(dh // th):
            sl = slice(c * th, (c + 1) * th)
            glu = jnp.dot(x, wg_ref[:, sl], preferred_element_type=F32) + bg_ref[:, sl]
            lin = jnp.dot(x, wl_ref[:, sl], preferred_element_type=F32) + bl_ref[:, sl]
            glu = jnp.minimum(glu, SWIGLU_LIMIT)
            lin = jnp.clip(lin, -SWIGLU_LIMIT, SWIGLU_LIMIT)
            act = glu * jax.nn.sigmoid(SWIGLU_ALPHA * glu) * (lin + 1.0)
            y = y + jnp.dot(act.astype(BF16), wd_ref[sl, :], preferred_element_type=F32)
        _store_tile_rows(y_ref, 0, _pack_pairs(y.astype(BF16)), nsub)


def _experts(xs, blk_expert, n_used, w_up, b_up, w_down, b_down, tm, th):
    E, D, two_dh = w_up.shape
    nsub = D // 2 // LANES
    n_blk = xs.shape[0] // (tm * nsub)
    dh = two_dh // 2
    th = min(th, dh)
    once = pl.Buffered(1)

    def row(i, be, nu):
        return (jnp.minimum(i, nu[0] - 1), 0)

    grid_spec = pltpu.PrefetchScalarGridSpec(
        num_scalar_prefetch=2,
        grid=(n_blk,),
        in_specs=[
            pl.BlockSpec((tm * nsub, LANES), row),
            pl.BlockSpec((None, D, dh), lambda i, be, nu: (be[i], 0, 0), pipeline_mode=once),
            pl.BlockSpec((None, D, dh), lambda i, be, nu: (be[i], 0, 1), pipeline_mode=once),
            pl.BlockSpec((None, 1, dh), lambda i, be, nu: (be[i], 0, 0)),
            pl.BlockSpec((None, 1, dh), lambda i, be, nu: (be[i], 0, 1)),
            pl.BlockSpec((None, dh, D), lambda i, be, nu: (be[i], 0, 0), pipeline_mode=once),
            pl.BlockSpec((None, 1, D), lambda i, be, nu: (be[i], 0, 0)),
        ],
        out_specs=pl.BlockSpec((tm * nsub, LANES), row),
        scratch_shapes=[pltpu.VMEM((tm, D), BF16)],
    )
    return pl.pallas_call(
        functools.partial(_expert_kernel, th=th, nsub=nsub),
        out_shape=jax.ShapeDtypeStruct(xs.shape, jnp.uint32),
        grid_spec=grid_spec,
        compiler_params=_cparams(("arbitrary",), VMEM_LIMIT),
        name="moe_experts",
    )(blk_expert, n_used, xs, w_up, w_up, b_up, b_up, w_down, b_down)


def _combine_kernel(dest_ref, dest_next_ref, meta_ref, h_ref, g_ref, b_ref, y_hbm, o_ref, buf, sem, *, alpha, nsub):
    i = pl.program_id(0)
    n = o_ref.shape[0]
    slot = i % 2

    def gather(idx_ref, s):
        for t in range(n):
            for k in range(TOP_K):
                d = pl.multiple_of(idx_ref[t * TOP_K + k] * nsub, nsub)
                pltpu.make_async_copy(y_hbm.at[pl.ds(d, nsub), :], buf.at[s, pl.ds((k * n + t) * nsub, nsub), :],
                                      sem.at[s]).start(priority=k % 2)

    @pl.when(i == 0)
    def _():
        gather(dest_ref, 0)

    @pl.when(i + 1 < pl.num_programs(0))
    def _():
        gather(dest_next_ref, 1 - slot)

    for k in range(TOP_K):
        pltpu.make_async_copy(y_hbm.at[pl.ds(0, n * nsub), :], buf.at[slot, pl.ds(k * n * nsub, n * nsub), :],
                              sem.at[slot]).wait()
    ffn_lo = ffn_hi = None
    for k in range(TOP_K):
        lo, hi = _unpack_pairs(_load_tile_rows(buf, k * n * nsub, n, nsub, lead=(slot,)))
        gate = meta_ref[:, TOP_K + k:TOP_K + k + 1]
        ffn_lo = gate * lo if k == 0 else ffn_lo + gate * lo
        ffn_hi = gate * hi if k == 0 else ffn_hi + gate * hi
    ffn = jnp.concatenate([ffn_lo, ffn_hi], axis=1)
    o_ref[...] = _layer_norm(alpha * h_ref[...] + ffn, g_ref[...], b_ref[...])


def _combine_ln(y, dest_flat, meta, h, g, b, alpha, tt):
    T, D = h.shape
    nsub = D // 2 // LANES
    nt = T // tt
    return pl.pallas_call(
        functools.partial(_combine_kernel, alpha=alpha, nsub=nsub),
        out_shape=jax.ShapeDtypeStruct((T, D), F32),
        grid=(nt,),
        in_specs=[
            pl.BlockSpec((tt * TOP_K,), lambda i: (i,), memory_space=pltpu.SMEM),
            pl.BlockSpec((tt * TOP_K,), lambda i: (jnp.minimum(i + 1, nt - 1),), memory_space=pltpu.SMEM),
            pl.BlockSpec((tt, LANES), lambda i: (i, 0)),
            pl.BlockSpec((tt, D), lambda i: (i, 0)),
            pl.BlockSpec((1, D), lambda i: (0, 0)),
            pl.BlockSpec((1, D), lambda i: (0, 0)),
            pl.BlockSpec(memory_space=pl.ANY),
        ],
        out_specs=pl.BlockSpec((tt, D), lambda i: (i, 0)),
        scratch_shapes=[pltpu.VMEM((2, TOP_K * tt * nsub, LANES), y.dtype), pltpu.SemaphoreType.DMA((2,))],
        compiler_params=_cparams(("arbitrary",), VMEM_LIMIT),
        name="moe_combine_ln2",
    )(dest_flat, dest_flat, meta, h, g, b, y)


def _layer(x, w_in, b_in, pos_k, pos_v, ck_w1, ck_w2, cv_w1, cv_w2, w_br_nsa, w_br_dil, w_out,
           ln1_g, ln1_b, w_router, b_router, w_up, b_up, w_down, b_down, ln2_g, ln2_b, alpha):
    B, S, D = x.shape
    T = B * S
    nd = D // LANES
    G = NSA_KV_GROUPS
    kvw = G * HEAD_DIM
    n_exp = w_router.shape[1]

    o_q = 0
    o_kv = NSA_HEADS * HEAD_DIM
    o_gl = o_kv + 6 * kvw
    o_dil = o_gl + 3 * NSA_HEADS
    o_ga = o_dil + 3 * N_DIL * DIL_W
    o_gb = o_ga + D

    def wcols(a, n):
        return w_in[:, a:a + n], b_in[a:a + n]

    def kv(i):
        return wcols(o_kv + i * kvw, kvw)

    tn = 1024
    gl_w, gl_b = wcols(o_gl, 3 * NSA_HEADS)
    pieces = [wcols(o_ga, D), wcols(o_gb, D), wcols(o_q, NSA_HEADS * HEAD_DIM),
              kv(0), kv(2), kv(4), kv(1), kv(3), kv(5), (gl_w, gl_b)]
    used = sum(p[0].shape[1] for p in pieces)
    n_main = -(-used // tn) * tn
    pieces.append((jnp.zeros((D, n_main - used), F32), jnp.zeros((n_main - used,), F32)))
    w_main = jnp.concatenate([p[0] for p in pieces], axis=1).astype(BF16)
    b_main = jnp.concatenate([p[1] for p in pieces])[None, :]
    t_q = 2 * nd
    t_kc, t_ks, t_kw = t_q + 8, t_q + 10, t_q + 12
    t_vc, t_vs, t_vw = t_q + 14, t_q + 16, t_q + 18
    t_gl = t_q + 20
    tile_id = jnp.arange(n_main // LANES)
    flags_main = jnp.where((tile_id >= t_q) & (tile_id < t_kc), ROPE_Q,
                           jnp.where((tile_id >= t_kc) & (tile_id < t_vc), ROPE, PLAIN)).astype(jnp.int32)
    q_blk0 = t_q // NSA_REP

    pos = jnp.arange(S, dtype=F32)
    inv = ROPE_THETA ** (-jnp.arange(0, HEAD_DIM, 2, dtype=F32) / HEAD_DIM)
    ang = pos[:, None] * inv[None, :]
    cosx = jnp.concatenate([jnp.cos(ang), jnp.cos(ang)], axis=-1)
    sinx = jnp.concatenate([-jnp.sin(ang), jnp.sin(ang)], axis=-1)

    main = _project(x, w_main, b_main, flags_main, cosx, sinx, 1, 1024, tn)

    nC = S // CMP_STRIDE
    w1 = jnp.stack([ck_w1, cv_w1]).astype(BF16)
    w2 = jnp.stack([ck_w2, cv_w2]).astype(BF16)
    kvc = _compress(main, t_kc, t_vc, jnp.stack([pos_k, pos_v]), w1, w2)

    n_slc = S // SEL_LEN
    assert n_slc <= LANES
    c_start = jnp.arange(nC) * CMP_STRIDE
    jb = jnp.arange(LANES)
    overlap = ((c_start[:, None] < (jb[None, :] + 1) * SEL_LEN) & (c_start[:, None] + CMP_LEN > jb[None, :] * SEL_LEN)
               & (jb[None, :] < n_slc) & (c_start[:, None] + CMP_LEN <= S)).astype(BF16)
    tq = min(256, S)
    o_cmp, sel = _cmp_attention(main, kvc, overlap, q_blk0, tq)

    onehot_t = (jnp.arange(S)[:, None] // SEL_LEN == jnp.arange(LANES)[None, :]).astype(BF16)
    o_slc, w_up_b, w_down_b = _sel_attention(main, sel, onehot_t, w_up, w_down, q_blk0, t_ks, t_vs, 256, 512)

    o_win = _band_attention(
        main, (B, G), S,
        lambda bb, g, i: (bb, 0, i, q_blk0 + g),
        lambda bb, g, i: (bb, 0, i, t_kw + g),
        lambda bb, g, i: (bb, 0, i, t_vw + g),
        HEAD_DIM, (B, S, NSA_HEADS * HEAD_DIM), lambda bb, g, i: (bb, i, g), 256, 2, WIN_LEN - 1, False)

    flags_dil = jnp.array([ROPE_Q] * DIL_HEADS + [ROPE] * DIL_HEADS + [PLAIN] * DIL_HEADS, jnp.int32)
    dil_o, dil_lse = [], []
    for gi, (w, d) in enumerate(DIL_CONFIGS):
        wd, bd = wcols(o_dil + gi * 3 * DIL_W, 3 * DIL_W)
        sub = _project(x, wd.astype(BF16), bd[None, :], flags_dil, cosx, sinx, d, 512, 3 * DIL_W)
        L = S // d
        o_g, lse_g = _band_attention(
            sub, (B, d), L,
            lambda bb, r, i: (bb, r, i, 0),
            lambda bb, r, i: (bb, r, i, 1),
            lambda bb, r, i: (bb, r, i, 2),
            DIL_W, (B, L, d * DIL_W), lambda bb, r, i: (bb, i, r), 128, 4, w // d, True)
        dil_o.append(o_g.reshape(T, DIL_W))
        dil_lse.append(lse_g.reshape(T, LANES))

    w_r = jnp.concatenate([w_router, jnp.zeros((D, LANES - n_exp), F32)], axis=1).astype(BF16)
    b_r = jnp.concatenate([b_router, jnp.full((LANES - n_exp,), NEG, F32)])[None, :]
    h_f, h_p, logits = _merge(
        o_cmp.reshape(T, -1), o_slc.reshape(T, -1), o_win.reshape(T, -1), main.reshape(T, n_main),
        t_gl, 0, 1, dil_o, dil_lse, x.reshape(T, D),
        w_br_nsa.astype(BF16), w_br_dil.astype(BF16), w_out.astype(BF16),
        ln1_g[None, :], ln1_b[None, :], w_r, b_r, alpha, min(256, T))

    meta, cnt = _router(logits, min(512, T))
    top_idx = meta[:, 0:TOP_K].astype(jnp.int32)
    gates = meta[:, TOP_K:2 * TOP_K]
    rank = meta[:, 2 * TOP_K:3 * TOP_K].astype(jnp.int32)

    tm = 512 if T * TOP_K >= 512 * n_exp else 128
    counts = cnt[0, :n_exp].astype(jnp.int32)
    padded = (counts + tm - 1) // tm * tm
    pad_end = jnp.cumsum(padded)
    pad_start = pad_end - padded
    start_of = jnp.where(top_idx[..., None] == jnp.arange(n_exp), pad_start, 0).sum(-1)
    dest = (start_of + rank).reshape(T * TOP_K)
    n_rows = T * TOP_K + n_exp * tm
    n_blk = n_rows // tm
    blk_start = jnp.arange(n_blk, dtype=jnp.int32) * tm
    blk_expert = jnp.minimum((pad_end[None, :] <= blk_start[:, None]).sum(-1), n_exp - 1).astype(jnp.int32)
    n_used = (pad_end[-1:] // tm).astype(jnp.int32)

    xs = _dispatch(h_p, dest, pad_end.astype(jnp.int32), n_rows, D // 2 // LANES, tm, min(256, T))
    y = _experts(xs, blk_expert, n_used, w_up_b, b_up[:, None, :], w_down_b, b_down[:, None, :], tm, 512)
    out = _combine_ln(y, dest, meta, h_f, ln2_g[None, :], ln2_b[None, :], alpha, min(256, T))
    return out.reshape(B, S, D)


def kernel(x, w_in, b_in, cmp_pos_k, cmp_pos_v, cmp_k_w1, cmp_k_w2, cmp_v_w1, cmp_v_w2, w_br_nsa, w_br_dil,
           w_out, ln1_g, ln1_b, w_router, b_router, w_up, b_up, w_down, b_down, ln2_g, ln2_b):
    depth = w_in.shape[0]
    alpha = (2.0 * depth) ** 0.25
    h = x
    for l in range(depth):
        h = _layer(h, w_in[l], b_in[l], cmp_pos_k[l], cmp_pos_v[l], cmp_k_w1[l], cmp_k_w2[l],
                   cmp_v_w1[l], cmp_v_w2[l], w_br_nsa[l], w_br_dil[l], w_out[l], ln1_g[l], ln1_b[l],
                   w_router[l], b_router[l], w_up[l], b_up[l], w_down[l], b_down[l], ln2_g[l], ln2_b[l], alpha)
    return h
```

```python
import functools

import jax
import jax.numpy as jnp
from jax import lax
from jax.experimental import pallas as pl
from jax.experimental.pallas import tpu as pltpu

F32 = jnp.float32
BF16 = jnp.bfloat16

HEAD_DIM = 128
LANES = 128
ROPE_THETA = 10000.0
NSA_HEADS = 8
NSA_KV_GROUPS = 2
NSA_REP = NSA_HEADS // NSA_KV_GROUPS
CMP_LEN = 32
CMP_STRIDE = 16
SEL_LEN = 64
SEL_TOPK = 16
WIN_LEN = 512
DIL_CONFIGS = ((128, 1), (512, 4), (2048, 16))
N_DIL = len(DIL_CONFIGS)
DIL_HEADS = 4
TOP_K = 4
SWIGLU_LIMIT = 7.0
SWIGLU_ALPHA = 1.702
LN_EPS = 1e-5
NEG = -1e30
FORCE = 1e9
SCALE = HEAD_DIM ** -0.5
LOG2E = 1.4426950408889634
Q_SCALE = SCALE * LOG2E
PLAIN, ROPE, ROPE_Q = 0, 1, 2

GROUP_W = NSA_REP * HEAD_DIM
DIL_W = DIL_HEADS * HEAD_DIM
VMEM_LIMIT = 56 * 1024 * 1024


def _cparams(sem, vmem=None):
    return pltpu.CompilerParams(dimension_semantics=sem, vmem_limit_bytes=vmem)


def _masked_softmax(s, mask):
    s = jnp.where(mask, s, NEG)
    m = s.max(-1, keepdims=True)
    p = jnp.where(mask, jnp.exp2(s - m), 0.0)
    den = p.sum(-1, keepdims=True)
    safe = jnp.where(den > 0, den, 1.0)
    return p, m, safe


def _qk(q, k):
    return lax.dot_general(q, k, (((1,), (1,)), ((), ())), preferred_element_type=F32)


def _pack_pairs(xb):
    m = xb.shape[1] // 2
    lo = lax.bitcast_convert_type(xb[:, :m].astype(F32), jnp.uint32)
    hi = lax.bitcast_convert_type(xb[:, m:].astype(F32), jnp.uint32)
    return (lo >> 16) | (hi & jnp.uint32(0xFFFF0000))


def _unpack_pairs(w):
    lo = lax.bitcast_convert_type(w << 16, F32)
    hi = lax.bitcast_convert_type(w & jnp.uint32(0xFFFF0000), F32)
    return lo, hi


def _store_tile_rows(ref, start, packed, nsub):
    n = packed.shape[0]
    for j in range(nsub):
        ref[pl.ds(start + j, n, stride=nsub), :] = packed[:, j * LANES:(j + 1) * LANES]


def _load_tile_rows(ref, start, n, nsub, lead=()):
    return jnp.concatenate([ref[lead + (pl.ds(start + j, n, stride=nsub), slice(None))] for j in range(nsub)],
                           axis=1)


def _proj_kernel(flags_ref, x_ref, w_ref, b_ref, cos_ref, sin_ref, o_ref, xb_ref, *xcol_ref):
    j = pl.program_id(2)
    d, sub, tn = o_ref.shape
    n_sub = tn // LANES

    @pl.when(j == 0)
    def _():
        if d == 1:
            xb_ref[...] = x_ref[...].astype(BF16)
        else:
            xcol, = xcol_ref
            for c in range(xcol.shape[0]):
                xcol[c] = x_ref[:, c * LANES:(c + 1) * LANES]
            for c in range(xcol.shape[0]):
                for r in range(d):
                    xb_ref[r * sub:(r + 1) * sub, c * LANES:(c + 1) * LANES] = (
                        xcol[c, pl.ds(r, sub, stride=d), :].astype(BF16))

    acc = jnp.dot(xb_ref[...], w_ref[...], preferred_element_type=F32) + b_ref[...]
    for u in range(n_sub):
        a = acc[:, u * LANES:(u + 1) * LANES]
        roped = a * cos_ref[...] + pltpu.roll(a, HEAD_DIM // 2, 1) * sin_ref[...]
        flag = flags_ref[j * n_sub + u]
        mult = jnp.where(flag == ROPE_Q, Q_SCALE, 1.0).astype(F32)
        res = (jnp.where(flag == PLAIN, a, roped) * mult).astype(o_ref.dtype)
        for r in range(d):
            o_ref[r, :, u * LANES:(u + 1) * LANES] = res[r * sub:(r + 1) * sub]


def _project(x, w, b, flags, cosx, sinx, d, tm, tn):
    B, S, D = x.shape
    N = w.shape[1]
    L = S // d
    tm = min(tm, S)
    sub = tm // d

    def regroup(tab):
        return tab.reshape(S // tm, sub, d, HEAD_DIM).transpose(0, 2, 1, 3).reshape(S, HEAD_DIM)

    grid_spec = pltpu.PrefetchScalarGridSpec(
        num_scalar_prefetch=1,
        grid=(B, S // tm, N // tn),
        in_specs=[
            pl.BlockSpec((None, tm, D), lambda bb, i, j, f: (bb, i, 0)),
            pl.BlockSpec((D, tn), lambda bb, i, j, f: (0, j)),
            pl.BlockSpec((1, tn), lambda bb, i, j, f: (0, j)),
            pl.BlockSpec((tm, HEAD_DIM), lambda bb, i, j, f: (i, 0)),
            pl.BlockSpec((tm, HEAD_DIM), lambda bb, i, j, f: (i, 0)),
        ],
        out_specs=pl.BlockSpec((None, d, sub, tn), lambda bb, i, j, f: (bb, 0, i, j)),
        scratch_shapes=[pltpu.VMEM((tm, D), BF16)] + ([pltpu.VMEM((D // LANES, tm, LANES), F32)] if d > 1 else []),
    )
    return pl.pallas_call(
        _proj_kernel,
        out_shape=jax.ShapeDtypeStruct((B, d, L, N), BF16),
        grid_spec=grid_spec,
        compiler_params=_cparams(("parallel", "arbitrary", "arbitrary"), VMEM_LIMIT),
        name="proj_rope",
    )(flags, x, w, b, regroup(cosx), regroup(sinx))


def _gelu_tanh(x):
    return 0.5 * x * (1.0 + jnp.tanh(0.7978845608028654 * (x + 0.044715 * x * x * x)))


def _compress_kernel(x_ref, pos_ref, w1_ref, w2_ref, o_ref, xf_sc):
    s_len = x_ref.shape[0]
    nc = s_len // CMP_STRIDE
    xf_sc[0:s_len, :] = x_ref[...].astype(F32)
    xf_sc[s_len:s_len + CMP_STRIDE, :] = jnp.zeros((CMP_STRIDE, HEAD_DIM), F32)
    h = jnp.zeros((nc, w1_ref.shape[1]), F32)
    for l in range(CMP_LEN):
        tok = xf_sc[pl.ds(l, nc, stride=CMP_STRIDE), :]
        a = (tok + pos_ref[l:l + 1, :]).astype(BF16)
        h = h + jnp.dot(a, w1_ref[l * HEAD_DIM:(l + 1) * HEAD_DIM, :], preferred_element_type=F32)
    g = _gelu_tanh(h).astype(BF16)
    o_ref[...] = jnp.dot(g, w2_ref[...], preferred_element_type=F32).astype(o_ref.dtype)


def _compress(main, k_tile0, v_tile0, pos, w1, w2):
    B, _, S, _ = main.shape
    G = NSA_KV_GROUPS
    nC = S // CMP_STRIDE
    hid = w1.shape[-1]
    return pl.pallas_call(
        _compress_kernel,
        out_shape=jax.ShapeDtypeStruct((2, B, G, nC, HEAD_DIM), BF16),
        grid=(2, B, G),
        in_specs=[
            pl.BlockSpec((None, None, S, HEAD_DIM),
                         lambda a, bb, g: (bb, 0, 0, k_tile0 + a * (v_tile0 - k_tile0) + g)),
            pl.BlockSpec((None, CMP_LEN, HEAD_DIM), lambda a, bb, g: (a, 0, 0)),
            pl.BlockSpec((None, CMP_LEN * HEAD_DIM, hid), lambda a, bb, g: (a, 0, 0)),
            pl.BlockSpec((None, hid, HEAD_DIM), lambda a, bb, g: (a, 0, 0)),
        ],
        out_specs=pl.BlockSpec((None, None, None, nC, HEAD_DIM), lambda a, bb, g: (a, bb, g, 0, 0)),
        scratch_shapes=[pltpu.VMEM((S + CMP_STRIDE, HEAD_DIM), F32)],
        compiler_params=_cparams(("arbitrary", "arbitrary", "arbitrary"), VMEM_LIMIT),
        name="compress_mlp",
    )(main, pos, w1, w2)


def _cmp_attn_kernel(q_ref, kc_ref, vc_ref, ov_ref, o_ref, sel_ref, *, n_slc, n_sel):
    qi = pl.program_id(2)
    tq = q_ref.shape[0]
    nc = kc_ref.shape[0]
    t = qi * tq + lax.broadcasted_iota(jnp.int32, (tq, nc), 0)
    c = lax.broadcasted_iota(jnp.int32, (tq, nc), 1)
    mask = (c * CMP_STRIDE + (CMP_LEN - 1)) <= t
    kc = kc_ref[...]
    vc = vc_ref[...]
    ps = jnp.zeros((tq, nc), F32)
    for h in range(NSA_REP):
        s = _qk(q_ref[:, h * HEAD_DIM:(h + 1) * HEAD_DIM], kc)
        p, _, safe = _masked_softmax(s, mask)
        p = p / safe
        o = jnp.dot(p.astype(BF16), vc, preferred_element_type=F32)
        o_ref[:, h * HEAD_DIM:(h + 1) * HEAD_DIM] = o.astype(o_ref.dtype)
        ps = ps + p
    imp = jnp.dot(ps.astype(BF16), ov_ref[...], preferred_element_type=F32)
    tj = qi * tq + lax.broadcasted_iota(jnp.int32, (tq, LANES), 0)
    j = lax.broadcasted_iota(jnp.int32, (tq, LANES), 1)
    cur = tj // SEL_LEN
    forced = (j == 0) | (j == cur) | (j == cur - 1)
    imp = jnp.where(forced, FORCE, jnp.where(j > cur, NEG, imp))
    imp_t = imp.T[0:n_slc, :]
    grp = 8
    groups = [imp_t[a:a + grp, :] for a in range(0, n_slc, grp)]
    ranks = [jnp.zeros(gv.shape, jnp.int32) for gv in groups]
    for j2 in range(n_slc):
        row = imp_t[j2:j2 + 1, :]
        for gi, gv in enumerate(groups):
            lo = gi * grp
            if lo > j2:
                ahead = row >= gv
            elif lo + gv.shape[0] - 1 <= j2:
                ahead = row > gv
            else:
                later = lax.broadcasted_iota(jnp.int32, gv.shape, 0) + lo > j2
                ahead = (row > gv) | ((row == gv) & later)
            ranks[gi] = jnp.where(ahead, ranks[gi] + 1, ranks[gi])
    rank = jnp.concatenate(ranks, axis=0)
    sel_t = jnp.where(rank < n_sel, 0.0, NEG)
    if n_slc < LANES:
        sel_t = jnp.concatenate([sel_t, jnp.zeros((LANES - n_slc, tq), F32)], axis=0)
    sel_ref[...] = sel_t.T.astype(sel_ref.dtype)


def _cmp_attention(main, kvc, overlap, q_blk0, tq):
    B, _, S, _ = main.shape
    G = NSA_KV_GROUPS
    nC = kvc.shape[3]
    n_slc = S // SEL_LEN
    n_sel = min(SEL_TOPK, n_slc)
    kern = functools.partial(_cmp_attn_kernel, n_slc=n_slc, n_sel=n_sel)
    return pl.pallas_call(
        kern,
        out_shape=(jax.ShapeDtypeStruct((B, S, NSA_HEADS * HEAD_DIM), BF16),
                   jax.ShapeDtypeStruct((B, G, S, LANES), BF16)),
        grid=(B, G, S // tq),
        in_specs=[
            pl.BlockSpec((None, None, tq, GROUP_W), lambda bb, g, i: (bb, 0, i, q_blk0 + g)),
            pl.BlockSpec((None, None, None, nC, HEAD_DIM), lambda bb, g, i: (0, bb, g, 0, 0)),
            pl.BlockSpec((None, None, None, nC, HEAD_DIM), lambda bb, g, i: (1, bb, g, 0, 0)),
            pl.BlockSpec((nC, LANES), lambda bb, g, i: (0, 0)),
        ],
        out_specs=(pl.BlockSpec((None, tq, GROUP_W), lambda bb, g, i: (bb, i, g)),
                   pl.BlockSpec((None, None, tq, LANES), lambda bb, g, i: (bb, g, i, 0))),
        compiler_params=_cparams(("parallel", "arbitrary", "arbitrary"), VMEM_LIMIT),
        name="cmp_attn_select",
    )(main, kvc, kvc, overlap)


def _fold_lanes(x, op):
    out = x[:, 0:LANES]
    for u in range(1, x.shape[1] // LANES):
        out = op(out, x[:, u * LANES:(u + 1) * LANES])
    return out


def _sel_attn_kernel(q_ref, k_ref, v_ref, bias_ref, et_ref, wu_ref, wd_ref,
                     o_ref, wu_out, wd_out, qx_sc, s_sc, m_sc, l_sc, acc_sc):
    wu_out[...] = wu_ref[...].astype(wu_out.dtype)
    wd_out[...] = wd_ref[...].astype(wd_out.dtype)
    qi = pl.program_id(2)
    t = q_ref.shape[0]
    tc = s_sc.shape[2]
    rows = NSA_REP * t
    for h in range(NSA_REP):
        qx_sc[h * t:(h + 1) * t, 0:HEAD_DIM] = q_ref[:, h * HEAD_DIM:(h + 1) * HEAD_DIM]
        qx_sc[h * t:(h + 1) * t, HEAD_DIM:2 * HEAD_DIM] = bias_ref[...]

    def scores(c, w=1):
        start = pl.multiple_of(c * tc, tc)
        kx = jnp.concatenate([k_ref[pl.ds(start, w * tc), :], et_ref[pl.ds(start, w * tc), :]], axis=1)
        return _qk(qx_sc[...], kx)

    m_sc[...] = jnp.full(m_sc.shape, NEG, F32)

    def for_spans(n, fn):
        def trip(j, carry):
            fn(2 * j, 2)
            return carry

        lax.fori_loop(0, n // 2, trip, 0)

        @pl.when(n % 2 == 1)
        def _():
            fn(n - 1, 1)

    def max_span(c, w):
        s = scores(c, w)
        for u in range(w):
            s_sc[c + u] = s[:, u * tc:(u + 1) * tc]
        m_sc[...] = jnp.maximum(m_sc[...], _fold_lanes(s, jnp.maximum))

    n_full = (qi * t) // tc
    for_spans(n_full, max_span)
    qpos = qi * t + lax.broadcasted_iota(jnp.int32, (rows, tc), 0) % t
    kpos = n_full * tc + lax.broadcasted_iota(jnp.int32, (rows, tc), 1)
    s = jnp.where(kpos <= qpos, scores(n_full), NEG)
    s_sc[n_full] = s
    m = jnp.maximum(m_sc[...], _fold_lanes(s, jnp.maximum)).max(-1, keepdims=True)
    m_sc[...] = jnp.broadcast_to(m, m_sc.shape)
    l_sc[...] = jnp.zeros(l_sc.shape, F32)
    acc_sc[...] = jnp.zeros(acc_sc.shape, F32)

    def exp_span(c, w):
        mb = m_sc[...]
        p = jnp.concatenate([jnp.exp2(s_sc[c + u][:, v * LANES:(v + 1) * LANES] - mb)
                             for u in range(w) for v in range(tc // LANES)], axis=1)
        l_sc[...] += _fold_lanes(p, jnp.add)
        vals = v_ref[pl.ds(pl.multiple_of(c * tc, tc), w * tc), :]
        acc_sc[...] += jnp.dot(p.astype(BF16), vals, preferred_element_type=F32)

    for_spans(n_full + 1, exp_span)
    l = l_sc[...].sum(-1, keepdims=True)
    o = acc_sc[...] / jnp.where(l > 0, l, 1.0)
    for h in range(NSA_REP):
        o_ref[:, h * HEAD_DIM:(h + 1) * HEAD_DIM] = o[h * t:(h + 1) * t].astype(o_ref.dtype)


def _sel_attention(main, bias, onehot_t, w_up, w_down, q_blk0, k_tile0, v_tile0, t, tc):
    B, _, S, _ = main.shape
    G = NSA_KV_GROUPS
    t = min(t, S)
    tc = min(tc, S)
    rows = NSA_REP * t
    nq = S // t
    n_steps = B * G * nq
    wu2 = w_up.reshape(-1, w_up.shape[-1])
    wd2 = w_down.reshape(-1, w_down.shape[-1])
    ru, rd = wu2.shape[0] // n_steps, wd2.shape[0] // n_steps
    assert ru * n_steps == wu2.shape[0] and rd * n_steps == wd2.shape[0]

    def slab(bb, g, i):
        return ((bb * G + g) * nq + i, 0)

    o, wu_b, wd_b = pl.pallas_call(
        _sel_attn_kernel,
        out_shape=(jax.ShapeDtypeStruct((B, S, NSA_HEADS * HEAD_DIM), BF16),
                   jax.ShapeDtypeStruct(wu2.shape, BF16), jax.ShapeDtypeStruct(wd2.shape, BF16)),
        grid=(B, G, nq),
        in_specs=[
            pl.BlockSpec((None, None, t, GROUP_W), lambda bb, g, i: (bb, 0, i, q_blk0 + g)),
            pl.BlockSpec((None, None, S, HEAD_DIM), lambda bb, g, i: (bb, 0, 0, k_tile0 + g)),
            pl.BlockSpec((None, None, S, HEAD_DIM), lambda bb, g, i: (bb, 0, 0, v_tile0 + g)),
            pl.BlockSpec((None, None, t, LANES), lambda bb, g, i: (bb, g, i, 0)),
            pl.BlockSpec((S, LANES), lambda bb, g, i: (0, 0)),
            pl.BlockSpec((ru, wu2.shape[1]), slab),
            pl.BlockSpec((rd, wd2.shape[1]), slab),
        ],
        out_specs=(pl.BlockSpec((None, t, GROUP_W), lambda bb, g, i: (bb, i, g)),
                   pl.BlockSpec((ru, wu2.shape[1]), slab), pl.BlockSpec((rd, wd2.shape[1]), slab)),
        scratch_shapes=[pltpu.VMEM((rows, 2 * HEAD_DIM), BF16), pltpu.VMEM((S // tc, rows, tc), F32),
                        pltpu.VMEM((rows, LANES), F32), pltpu.VMEM((rows, LANES), F32),
                        pltpu.VMEM((rows, HEAD_DIM), F32)],
        compiler_params=_cparams(("parallel", "arbitrary", "arbitrary"), VMEM_LIMIT),
        name="selected_attn",
    )(main, main, main, bias, onehot_t, wu2, wd2)
    return o, wu_b.reshape(w_up.shape), wd_b.reshape(w_down.shape)


def _band_kernel(*refs, n_prev, n_sub, max_dist, kv_heads, with_lse):
    q_ref = refs[0]
    k_refs = refs[1:2 + n_prev]
    v_refs = refs[2 + n_prev:3 + 2 * n_prev]
    o_ref = refs[3 + 2 * n_prev]
    qi = pl.program_id(2)
    t = k_refs[0].shape[0]
    nk = (n_prev + 1) * t
    n_heads = q_ref.shape[1] // HEAD_DIM
    k_all = jnp.concatenate([r[...] for r in k_refs], axis=0)
    v_all = jnp.concatenate([r[...] for r in v_refs], axis=0)
    kcol = lax.broadcasted_iota(jnp.int32, (t, nk), 1)
    diff = n_prev * t + lax.broadcasted_iota(jnp.int32, (t, nk), 0) - kcol
    in_band = (diff >= 0) & (diff <= max_dist)
    lane = lax.broadcasted_iota(jnp.int32, (t, LANES), 1)
    for u in range(n_sub):
        first_key = (qi * n_sub + u - n_prev) * t
        bias = jnp.where(in_band & (kcol + first_key >= 0), 0.0, NEG)
        rows = slice(u * t, (u + 1) * t)
        k_u = k_all[u * t:u * t + nk]
        v_u = v_all[u * t:u * t + nk]
        lse = jnp.zeros((t, LANES), F32)
        if kv_heads == 1:
            q = jnp.concatenate([q_ref[rows, h * HEAD_DIM:(h + 1) * HEAD_DIM] for h in range(n_heads)], axis=0)
            s = _qk(q, k_u).reshape(n_heads, t, nk) + bias[None]
            m = s.max(-1, keepdims=True)
            p = jnp.exp2(s - m)
            l = p.sum(-1, keepdims=True)
            o = jnp.dot(p.astype(BF16).reshape(n_heads * t, nk), v_u, preferred_element_type=F32)
            o = o.reshape(n_heads, t, HEAD_DIM) / l
            for h in range(n_heads):
                o_ref[rows, h * HEAD_DIM:(h + 1) * HEAD_DIM] = o[h].astype(o_ref.dtype)
                lse = jnp.where(lane == h, m[h] + jnp.log2(l[h]), lse)
        else:
            for h in range(n_heads):
                cols = slice(h * HEAD_DIM, (h + 1) * HEAD_DIM)
                s = _qk(q_ref[rows, cols], k_u[:, cols]) + bias
                m = s.max(-1, keepdims=True)
                p = jnp.exp2(s - m)
                l = p.sum(-1, keepdims=True)
                o = jnp.dot(p.astype(BF16), v_u[:, cols], preferred_element_type=F32)
                o_ref[rows, cols] = (o / l).astype(o_ref.dtype)
                lse = jnp.where(lane == h, m + jnp.log2(l), lse)
        if with_lse:
            refs[4 + 2 * n_prev][rows, :] = lse


def _band_attention(src, lead_grid, length, q_map, k_map, v_map, kv_width, out_shape, o_map,
                    t, n_sub, max_dist, with_lse):
    t = min(t, length)
    n_prev = -(-max_dist // t)
    n_sub = min(n_sub, length // t)
    tile = n_sub * t
    kern = functools.partial(_band_kernel, n_prev=n_prev, n_sub=n_sub, max_dist=max_dist,
                             kv_heads=kv_width // HEAD_DIM, with_lse=with_lse)

    def preceding(fn, j):
        def index_map(bb, a, i):
            return fn(bb, a, jnp.maximum(i * n_sub - n_prev + j, 0))
        return index_map

    def kv_specs(fn):
        return ([pl.BlockSpec((None, None, t, kv_width), preceding(fn, j)) for j in range(n_prev)]
                + [pl.BlockSpec((None, None, tile, kv_width), fn)])

    in_specs = [pl.BlockSpec((None, None, tile, GROUP_W), q_map)] + kv_specs(k_map) + kv_specs(v_map)
    o_spec = pl.BlockSpec((None, tile, GROUP_W), o_map)
    if with_lse:
        lse_shape = out_shape[:-1] + (out_shape[-1] // GROUP_W * LANES,)
        out_shapes = (jax.ShapeDtypeStruct(out_shape, BF16), jax.ShapeDtypeStruct(lse_shape, F32))
        out_specs = (o_spec, pl.BlockSpec((None, tile, LANES), o_map))
    else:
        out_shapes = jax.ShapeDtypeStruct(out_shape, BF16)
        out_specs = o_spec
    return pl.pallas_call(
        kern, out_shape=out_shapes, grid=lead_grid + (length // tile,), in_specs=in_specs, out_specs=out_specs,
        compiler_params=_cparams(("parallel", "arbitrary", "arbitrary"), VMEM_LIMIT),
        name="band_attn",
    )(*([src] * (3 + 2 * n_prev)))


def _layer_norm(z, g, b):
    mu = z.mean(-1, keepdims=True)
    zc = z - mu
    var = (zc * zc).mean(-1, keepdims=True)
    return zc * lax.rsqrt(var + LN_EPS) * g + b


def _merge_kernel(ocmp_ref, oslc_ref, owin_ref, gl_ref, ga_ref, gb_ref,
                  d0_ref, d1_ref, d2_ref, l0_ref, l1_ref, l2_ref, x_ref,
                  wa_ref, wb_ref, wo_ref, g_ref, b_ref, wr_ref, br_ref, hf_ref, hp_ref, lg_ref, *, alpha):
    tt = x_ref.shape[0]
    gates = jax.nn.sigmoid(gl_ref[...].astype(F32))
    parts = []
    for h in range(NSA_HEADS):
        sl = slice(h * HEAD_DIM, (h + 1) * HEAD_DIM)
        acc = jnp.zeros((tt, HEAD_DIM), F32)
        for br, ref in enumerate((ocmp_ref, oslc_ref, owin_ref)):
            gcol = gates[:, 3 * h + br:3 * h + br + 1]
            acc = acc + gcol * ref[:, sl].astype(F32)
        parts.append(acc.astype(BF16))
    o_nsa = jnp.concatenate(parts, axis=1)
    l0, l1, l2 = l0_ref[...], l1_ref[...], l2_ref[...]
    lm = jnp.maximum(jnp.maximum(l0, l1), l2)
    e0, e1, e2 = jnp.exp2(l0 - lm), jnp.exp2(l1 - lm), jnp.exp2(l2 - lm)
    inv = 1.0 / (e0 + e1 + e2)
    w0, w1, w2 = e0 * inv, e1 * inv, e2 * inv
    parts = []
    for h in range(DIL_HEADS):
        sl = slice(h * HEAD_DIM, (h + 1) * HEAD_DIM)
        parts.append(w0[:, h:h + 1] * d0_ref[:, sl].astype(F32) + w1[:, h:h + 1] * d1_ref[:, sl].astype(F32)
                     + w2[:, h:h + 1] * d2_ref[:, sl].astype(F32))
    o_dil = jnp.concatenate(parts, axis=1)
    y_a = jnp.dot(o_nsa, wa_ref[...], preferred_element_type=F32)
    y_b = jnp.dot(o_dil.astype(BF16), wb_ref[...], preferred_element_type=F32)
    merged = (jax.nn.sigmoid(ga_ref[...].astype(F32)) * y_a
              + jax.nn.sigmoid(gb_ref[...].astype(F32)) * y_b)
    mix = jnp.dot(merged.astype(BF16), wo_ref[...], preferred_element_type=F32)
    h = _layer_norm(alpha * x_ref[...] + mix, g_ref[...], b_ref[...])
    hf_ref[...] = h
    hb = h.astype(BF16)
    _store_tile_rows(hp_ref, 0, _pack_pairs(hb), hp_ref.shape[0] // tt)
    lg_ref[...] = jnp.dot(hb, wr_ref[...], preferred_element_type=F32) + br_ref[...]


def _merge(o_cmp, o_slc, o_win, main2d, gl_tile, ga_blk, gb_blk, dil_o, dil_lse, x2d,
           w_a, w_b, w_o, ln_g, ln_b, w_r, b_r, alpha, tt):
    T, D = x2d.shape
    nsub = D // 2 // LANES
    nsa_w = NSA_HEADS * HEAD_DIM
    row = lambda i: (i, 0)
    const = lambda i: (0, 0)
    in_specs = [
        pl.BlockSpec((tt, nsa_w), row), pl.BlockSpec((tt, nsa_w), row), pl.BlockSpec((tt, nsa_w), row),
        pl.BlockSpec((tt, LANES), lambda i: (i, gl_tile)),
        pl.BlockSpec((tt, D), lambda i: (i, ga_blk)),
        pl.BlockSpec((tt, D), lambda i: (i, gb_blk)),
    ]
    in_specs += [pl.BlockSpec((tt, DIL_W), row)] * 3 + [pl.BlockSpec((tt, LANES), row)] * 3
    in_specs += [
        pl.BlockSpec((tt, D), row),
        pl.BlockSpec((nsa_w, D), const), pl.BlockSpec((DIL_W, D), const), pl.BlockSpec((D, D), const),
        pl.BlockSpec((1, D), const), pl.BlockSpec((1, D), const),
        pl.BlockSpec((D, LANES), const), pl.BlockSpec((1, LANES), const),
    ]
    return pl.pallas_call(
        functools.partial(_merge_kernel, alpha=alpha),
        out_shape=(jax.ShapeDtypeStruct((T, D), F32), jax.ShapeDtypeStruct((T * nsub, LANES), jnp.uint32),
                   jax.ShapeDtypeStruct((T, LANES), F32)),
        grid=(T // tt,),
        in_specs=in_specs,
        out_specs=(pl.BlockSpec((tt, D), row), pl.BlockSpec((tt * nsub, LANES), row), pl.BlockSpec((tt, LANES), row)),
        compiler_params=_cparams(("parallel",), VMEM_LIMIT),
        name="merge_ln1",
    )(o_cmp, o_slc, o_win, main2d, main2d, main2d, *dil_o, *dil_lse, x2d, w_a, w_b, w_o, ln_g, ln_b, w_r, b_r)


def _router_kernel(lg_ref, tri_ref, meta_ref, cnt_ref, carry_sc):
    i = pl.program_id(0)
    tt = lg_ref.shape[0]

    @pl.when(i == 0)
    def _():
        carry_sc[...] = jnp.zeros(carry_sc.shape, F32)

    logits = lg_ref[...]
    lane = lax.broadcasted_iota(jnp.int32, (tt, LANES), 1)
    v = logits
    onehot = jnp.zeros((tt, LANES), F32)
    vals, idxs = [], []
    for _ in range(TOP_K):
        m = v.max(-1, keepdims=True)
        idx = jnp.where(v == m, lane, LANES).min(-1, keepdims=True)
        hit = lane == idx
        vals.append(m)
        idxs.append(idx)
        onehot = onehot + hit.astype(F32)
        v = jnp.where(hit, -jnp.inf, v)
    exps = [jnp.exp(vk - vals[0]) for vk in vals]
    den = exps[0] + exps[1] + exps[2] + exps[3]
    before = jnp.dot(tri_ref[...], onehot.astype(BF16), preferred_element_type=F32) + carry_sc[0:1, :]
    meta = jnp.zeros((tt, LANES), F32)
    for k in range(TOP_K):
        rank = jnp.where(lane == idxs[k], before, 0.0).sum(-1, keepdims=True)
        meta = jnp.where(lane == k, idxs[k].astype(F32), meta)
        meta = jnp.where(lane == TOP_K + k, exps[k] / den, meta)
        meta = jnp.where(lane == 2 * TOP_K + k, rank, meta)
    meta_ref[...] = meta
    carry_sc[...] = carry_sc[...] + jnp.broadcast_to(onehot.sum(0, keepdims=True), carry_sc.shape)
    cnt_ref[...] = carry_sc[...]


def _router(logits, tt):
    T = logits.shape[0]
    tri = (jnp.arange(tt)[:, None] > jnp.arange(tt)[None, :]).astype(BF16)
    return pl.pallas_call(
        _router_kernel,
        out_shape=(jax.ShapeDtypeStruct((T, LANES), F32), jax.ShapeDtypeStruct((8, LANES), F32)),
        grid=(T // tt,),
        in_specs=[
            pl.BlockSpec((tt, LANES), lambda i: (i, 0)),
            pl.BlockSpec((tt, tt), lambda i: (0, 0)),
        ],
        out_specs=(pl.BlockSpec((tt, LANES), lambda i: (i, 0)), pl.BlockSpec((8, LANES), lambda i: (0, 0))),
        scratch_shapes=[pltpu.VMEM((8, LANES), F32)],
        compiler_params=_cparams(("arbitrary",), VMEM_LIMIT),
        name="router_top4",
    )(logits, tri)


def _dispatch_kernel(pe_ref, dest_ref, h_ref, xs_hbm, zero_sc, sem, *, nsub, tm):
    n = h_ref.shape[0] // nsub
    blk = tm * nsub

    @pl.when(pl.program_id(0) == 0)
    def _():
        zero_sc[...] = jnp.zeros(zero_sc.shape, zero_sc.dtype)

        def clear(e):
            end = pe_ref[e]
            begin = pe_ref[e - 1] if e else 0
            start = pl.multiple_of((end - tm) * nsub, nsub)
            return end > begin, pltpu.make_async_copy(zero_sc, xs_hbm.at[pl.ds(start, blk), :], sem)

        for e in range(pe_ref.shape[0]):
            nonempty, copy = clear(e)
            pl.when(nonempty)(copy.start)
        for e in range(pe_ref.shape[0]):
            nonempty, copy = clear(e)
            pl.when(nonempty)(copy.wait)

    for t in range(n):
        for k in range(TOP_K):
            d = pl.multiple_of(dest_ref[t * TOP_K + k] * nsub, nsub)
            pltpu.make_async_copy(h_ref.at[pl.ds(t * nsub, nsub), :], xs_hbm.at[pl.ds(d, nsub), :],
                                  sem).start(priority=k % 2)
    for k in range(TOP_K):
        pltpu.make_async_copy(h_ref, xs_hbm.at[pl.ds(0, n * nsub), :], sem).wait()


def _dispatch(hp, dest_flat, pad_end, n_rows, nsub, tm, tt):
    T = hp.shape[0] // nsub
    grid_spec = pltpu.PrefetchScalarGridSpec(
        num_scalar_prefetch=1,
        grid=(T // tt,),
        in_specs=[
            pl.BlockSpec((tt * TOP_K,), lambda i, pe: (i,), memory_space=pltpu.SMEM),
            pl.BlockSpec((tt * nsub, LANES), lambda i, pe: (i, 0)),
        ],
        out_specs=pl.BlockSpec(memory_space=pl.ANY),
        scratch_shapes=[pltpu.VMEM((tm * nsub, LANES), hp.dtype), pltpu.SemaphoreType.DMA(())],
    )
    return pl.pallas_call(
        functools.partial(_dispatch_kernel, nsub=nsub, tm=tm),
        out_shape=jax.ShapeDtypeStruct((n_rows * nsub, LANES), hp.dtype),
        grid_spec=grid_spec,
        compiler_params=_cparams(("arbitrary",), VMEM_LIMIT),
        name="moe_dispatch",
    )(pad_end, dest_flat, hp)


def _expert_kernel(be_ref, nu_ref, x_ref, wg_ref, wl_ref, bg_ref, bl_ref, wd_ref, bd_ref, y_ref, xb_sc, *, th, nsub):
    i = pl.program_id(0)
    tm = xb_sc.shape[0]
    half = nsub * LANES
    dh = wg_ref.shape[1]

    @pl.when(i < nu_ref[0])
    def _():
        lo, hi = _unpack_pairs(_load_tile_rows(x_ref, 0, tm, nsub))
        xb_sc[:, :half] = lo.astype(BF16)
        xb_sc[:, half:] = hi.astype(BF16)
        x = xb_sc[...]
        y = bd_ref[...]
        for c in range(dh // th):
            sl = slice(c * th, (c + 1) * th)
            glu = jnp.dot(x, wg_ref[:, sl], preferred_element_type=F32) + bg_ref[:, sl]
            lin = jnp.dot(x, wl_ref[:, sl], preferred_element_type=F32) + bl_ref[:, sl]
            glu = jnp.minimum(glu, SWIGLU_LIMIT)
            lin = jnp.clip(lin, -SWIGLU_LIMIT, SWIGLU_LIMIT)
            act = glu * jax.nn.sigmoid(SWIGLU_ALPHA * glu) * (lin + 1.0)
            y = y + jnp.dot(act.astype(BF16), wd_ref[sl, :], preferred_element_type=F32)
        _store_tile_rows(y_ref, 0, _pack_pairs(y.astype(BF16)), nsub)


def _experts(xs, blk_expert, n_used, w_up, b_up, w_down, b_down, tm, th):
    E, D, two_dh = w_up.shape
    nsub = D // 2 // LANES
    n_blk = xs.shape[0] // (tm * nsub)
    dh = two_dh // 2
    th = min(th, dh)
    once = pl.Buffered(1)

    def row(i, be, nu):
        return (jnp.minimum(i, nu[0] - 1), 0)

    grid_spec = pltpu.PrefetchScalarGridSpec(
        num_scalar_prefetch=2,
        grid=(n_blk,),
        in_specs=[
            pl.BlockSpec((tm * nsub, LANES), row),
            pl.BlockSpec((None, D, dh), lambda i, be, nu: (be[i], 0, 0), pipeline_mode=once),
            pl.BlockSpec((None, D, dh), lambda i, be, nu: (be[i], 0, 1), pipeline_mode=once),
            pl.BlockSpec((None, 1, dh), lambda i, be, nu: (be[i], 0, 0)),
            pl.BlockSpec((None, 1, dh), lambda i, be, nu: (be[i], 0, 1)),
            pl.BlockSpec((None, dh, D), lambda i, be, nu: (be[i], 0, 0), pipeline_mode=once),
            pl.BlockSpec((None, 1, D), lambda i, be, nu: (be[i], 0, 0)),
        ],
        out_specs=pl.BlockSpec((tm * nsub, LANES), row),
        scratch_shapes=[pltpu.VMEM((tm, D), BF16)],
    )
    return pl.pallas_call(
        functools.partial(_expert_kernel, th=th, nsub=nsub),
        out_shape=jax.ShapeDtypeStruct(xs.shape, jnp.uint32),
        grid_spec=grid_spec,
        compiler_params=_cparams(("arbitrary",), VMEM_LIMIT),
        name="moe_experts",
    )(blk_expert, n_used, xs, w_up, w_up, b_up, b_up, w_down, b_down)


def _combine_kernel(dest_ref, dest_next_ref, meta_ref, h_ref, g_ref, b_ref, y_hbm, o_ref, buf, sem, *, alpha, nsub):
    i = pl.program_id(0)
    n = o_ref.shape[0]
    slot = i % 2

    def gather(idx_ref, s):
        for t in range(n):
            for k in range(TOP_K):
                d = pl.multiple_of(idx_ref[t * TOP_K + k] * nsub, nsub)
                pltpu.make_async_copy(y_hbm.at[pl.ds(d, nsub), :], buf.at[s, pl.ds((k * n + t) * nsub, nsub), :],
                                      sem.at[s]).start(priority=k % 2)

    @pl.when(i == 0)
    def _():
        gather(dest_ref, 0)

    @pl.when(i + 1 < pl.num_programs(0))
    def _():
        gather(dest_next_ref, 1 - slot)

    for k in range(TOP_K):
        pltpu.make_async_copy(y_hbm.at[pl.ds(0, n * nsub), :], buf.at[slot, pl.ds(k * n * nsub, n * nsub), :],
                              sem.at[slot]).wait()
    ffn_lo = ffn_hi = None
    for k in range(TOP_K):
        lo, hi = _unpack_pairs(_load_tile_rows(buf, k * n * nsub, n, nsub, lead=(slot,)))
        gate = meta_ref[:, TOP_K + k:TOP_K + k + 1]
        ffn_lo = gate * lo if k == 0 else ffn_lo + gate * lo
        ffn_hi = gate * hi if k == 0 else ffn_hi + gate * hi
    ffn = jnp.concatenate([ffn_lo, ffn_hi], axis=1)
    o_ref[...] = _layer_norm(alpha * h_ref[...] + ffn, g_ref[...], b_ref[...])


def _combine_ln(y, dest_flat, meta, h, g, b, alpha, tt):
    T, D = h.shape
    nsub = D // 2 // LANES
    nt = T // tt
    return pl.pallas_call(
        functools.partial(_combine_kernel, alpha=alpha, nsub=nsub),
        out_shape=jax.ShapeDtypeStruct((T, D), F32),
        grid=(nt,),
        in_specs=[
            pl.BlockSpec((tt * TOP_K,), lambda i: (i,), memory_space=pltpu.SMEM),
            pl.BlockSpec((tt * TOP_K,), lambda i: (jnp.minimum(i + 1, nt - 1),), memory_space=pltpu.SMEM),
            pl.BlockSpec((tt, LANES), lambda i: (i, 0)),
            pl.BlockSpec((tt, D), lambda i: (i, 0)),
            pl.BlockSpec((1, D), lambda i: (0, 0)),
            pl.BlockSpec((1, D), lambda i: (0, 0)),
            pl.BlockSpec(memory_space=pl.ANY),
        ],
        out_specs=pl.BlockSpec((tt, D), lambda i: (i, 0)),
        scratch_shapes=[pltpu.VMEM((2, TOP_K * tt * nsub, LANES), y.dtype), pltpu.SemaphoreType.DMA((2,))],
        compiler_params=_cparams(("arbitrary",), VMEM_LIMIT),
        name="moe_combine_ln2",
    )(dest_flat, dest_flat, meta, h, g, b, y)


def _layer(x, w_in, b_in, pos_k, pos_v, ck_w1, ck_w2, cv_w1, cv_w2, w_br_nsa, w_br_dil, w_out,
           ln1_g, ln1_b, w_router, b_router, w_up, b_up, w_down, b_down, ln2_g, ln2_b, alpha):
    B, S, D = x.shape
    T = B * S
    nd = D // LANES
    G = NSA_KV_GROUPS
    kvw = G * HEAD_DIM
    n_exp = w_router.shape[1]

    o_q = 0
    o_kv = NSA_HEADS * HEAD_DIM
    o_gl = o_kv + 6 * kvw
    o_dil = o_gl + 3 * NSA_HEADS
    o_ga = o_dil + 3 * N_DIL * DIL_W
    o_gb = o_ga + D

    def wcols(a, n):
        return w_in[:, a:a + n], b_in[a:a + n]

    def kv(i):
        return wcols(o_kv + i * kvw, kvw)

    tn = 1024
    gl_w, gl_b = wcols(o_gl, 3 * NSA_HEADS)
    pieces = [wcols(o_ga, D), wcols(o_gb, D), wcols(o_q, NSA_HEADS * HEAD_DIM),
              kv(0), kv(2), kv(4), kv(1), kv(3), kv(5), (gl_w, gl_b)]
    used = sum(p[0].shape[1] for p in pieces)
    n_main = -(-used // tn) * tn
    pieces.append((jnp.zeros((D, n_main - used), F32), jnp.zeros((n_main - used,), F32)))
    w_main = jnp.concatenate([p[0] for p in pieces], axis=1).astype(BF16)
    b_main = jnp.concatenate([p[1] for p in pieces])[None, :]
    t_q = 2 * nd
    t_kc, t_ks, t_kw = t_q + 8, t_q + 10, t_q + 12
    t_vc, t_vs, t_vw = t_q + 14, t_q + 16, t_q + 18
    t_gl = t_q + 20
    tile_id = jnp.arange(n_main // LANES)
    flags_main = jnp.where((tile_id >= t_q) & (tile_id < t_kc), ROPE_Q,
                           jnp.where((tile_id >= t_kc) & (tile_id < t_vc), ROPE, PLAIN)).astype(jnp.int32)
    q_blk0 = t_q // NSA_REP

    pos = jnp.arange(S, dtype=F32)
    inv = ROPE_THETA ** (-jnp.arange(0, HEAD_DIM, 2, dtype=F32) / HEAD_DIM)
    ang = pos[:, None] * inv[None, :]
    cosx = jnp.concatenate([jnp.cos(ang), jnp.cos(ang)], axis=-1)
    sinx = jnp.concatenate([-jnp.sin(ang), jnp.sin(ang)], axis=-1)

    main = _project(x, w_main, b_main, flags_main, cosx, sinx, 1, 1024, tn)

    nC = S // CMP_STRIDE
    w1 = jnp.stack([ck_w1, cv_w1]).astype(BF16)
    w2 = jnp.stack([ck_w2, cv_w2]).astype(BF16)
    kvc = _compress(main, t_kc, t_vc, jnp.stack([pos_k, pos_v]), w1, w2)

    n_slc = S // SEL_LEN
    assert n_slc <= LANES
    c_start = jnp.arange(nC) * CMP_STRIDE
    jb = jnp.arange(LANES)
    overlap = ((c_start[:, None] < (jb[None, :] + 1) * SEL_LEN) & (c_start[:, None] + CMP_LEN > jb[None, :] * SEL_LEN)
               & (jb[None, :] < n_slc) & (c_start[:, None] + CMP_LEN <= S)).astype(BF16)
    tq = min(512, S)
    o_cmp, sel = _cmp_attention(main, kvc, overlap, q_blk0, tq)

    onehot_t = (jnp.arange(S)[:, None] // SEL_LEN == jnp.arange(LANES)[None, :]).astype(BF16)
    o_slc, w_up_b, w_down_b = _sel_attention(main, sel, onehot_t, w_up, w_down, q_blk0, t_ks, t_vs, 256, 512)

    o_win = _band_attention(
        main, (B, G), S,
        lambda bb, g, i: (bb, 0, i, q_blk0 + g),
        lambda bb, g, i: (bb, 0, i, t_kw + g),
        lambda bb, g, i: (bb, 0, i, t_vw + g),
        HEAD_DIM, (B, S, NSA_HEADS * HEAD_DIM), lambda bb, g, i: (bb, i, g), 128, 4, WIN_LEN - 1, False)

    flags_dil = jnp.array([ROPE_Q] * DIL_HEADS + [ROPE] * DIL_HEADS + [PLAIN] * DIL_HEADS, jnp.int32)
    dil_o, dil_lse = [], []
    for gi, (w, d) in enumerate(DIL_CONFIGS):
        wd, bd = wcols(o_dil + gi * 3 * DIL_W, 3 * DIL_W)
        sub = _project(x, wd.astype(BF16), bd[None, :], flags_dil, cosx, sinx, d, 512, 3 * DIL_W)
        L = S // d
        o_g, lse_g = _band_attention(
            sub, (B, d), L,
            lambda bb, r, i: (bb, r, i, 0),
            lambda bb, r, i: (bb, r, i, 1),
            lambda bb, r, i: (bb, r, i, 2),
            DIL_W, (B, L, d * DIL_W), lambda bb, r, i: (bb, i, r), 128, 4, w // d, True)
        dil_o.append(o_g.reshape(T, DIL_W))
        dil_lse.append(lse_g.reshape(T, LANES))

    w_r = jnp.concatenate([w_router, jnp.zeros((D, LANES - n_exp), F32)], axis=1).astype(BF16)
    b_r = jnp.concatenate([b_router, jnp.full((LANES - n_exp,), NEG, F32)])[None, :]
    h_f, h_p, logits = _merge(
        o_cmp.reshape(T, -1), o_slc.reshape(T, -1), o_win.reshape(T, -1), main.reshape(T, n_main),
        t_gl, 0, 1, dil_o, dil_lse, x.reshape(T, D),
        w_br_nsa.astype(BF16), w_br_dil.astype(BF16), w_out.astype(BF16),
        ln1_g[None, :], ln1_b[None, :], w_r, b_r, alpha, min(256, T))

    meta, cnt = _router(logits, min(512, T))
    top_idx = meta[:, 0:TOP_K].astype(jnp.int32)
    rank = meta[:, 2 * TOP_K:3 * TOP_K].astype(jnp.int32)

    tm = 512 if T * TOP_K >= 512 * n_exp else 128
    counts = cnt[0, :n_exp].astype(jnp.int32)
    padded = (counts + tm - 1) // tm * tm
    pad_end = jnp.cumsum(padded)
    pad_start = pad_end - padded
    start_of = jnp.where(top_idx[..., None] == jnp.arange(n_exp), pad_start, 0).sum(-1)
    dest = (start_of + rank).reshape(T * TOP_K)
    n_rows = T * TOP_K + n_exp * tm
    n_blk = n_rows // tm
    blk_start = jnp.arange(n_blk, dtype=jnp.int32) * tm
    blk_expert = jnp.minimum((pad_end[None, :] <= blk_start[:, None]).sum(-1), n_exp - 1).astype(jnp.int32)
    n_used = (pad_end[-1:] // tm).astype(jnp.int32)

    xs = _dispatch(h_p, dest, pad_end.astype(jnp.int32), n_rows, D // 2 // LANES, tm, min(256, T))
    y = _experts(xs, blk_expert, n_used, w_up_b, b_up[:, None, :], w_down_b, b_down[:, None, :], tm, 512)
    out = _combine_ln(y, dest, meta, h_f, ln2_g[None, :], ln2_b[None, :], alpha, min(256, T))
    return out.reshape(B, S, D)


def kernel(x, w_in, b_in, cmp_pos_k, cmp_pos_v, cmp_k_w1, cmp_k_w2, cmp_v_w1, cmp_v_w2, w_br_nsa, w_br_dil,
           w_out, ln1_g, ln1_b, w_router, b_router, w_up, b_up, w_down, b_down, ln2_g, ln2_b):
    depth = w_in.shape[0]
    alpha = (2.0 * depth) ** 0.25
    h = x
    for l in range(depth):
        h = _layer(h, w_in[l], b_in[l], cmp_pos_k[l], cmp_pos_v[l], cmp_k_w1[l], cmp_k_w2[l],
                   cmp_v_w1[l], cmp_v_w2[l], w_br_nsa[l], w_br_dil[l], w_out[l], ln1_g[l], ln1_b[l],
                   w_router[l], b_router[l], w_up[l], b_up[l], w_down[l], b_down[l], ln2_g[l], ln2_b[l], alpha)
    return h
```

```python
import functools

import jax
import jax.numpy as jnp
from jax import lax
from jax.experimental import pallas as pl
from jax.experimental.pallas import tpu as pltpu

F32 = jnp.float32
BF16 = jnp.bfloat16

HEAD_DIM = 128
LANES = 128
ROPE_THETA = 10000.0
NSA_HEADS = 8
NSA_KV_GROUPS = 2
NSA_REP = NSA_HEADS // NSA_KV_GROUPS
CMP_LEN = 32
CMP_STRIDE = 16
SEL_LEN = 64
SEL_TOPK = 16
WIN_LEN = 512
DIL_CONFIGS = ((128, 1), (512, 4), (2048, 16))
N_DIL = len(DIL_CONFIGS)
DIL_HEADS = 4
TOP_K = 4
SWIGLU_LIMIT = 7.0
SWIGLU_ALPHA = 1.702
LN_EPS = 1e-5
NEG = -1e30
FORCE = 1e9
SCALE = HEAD_DIM ** -0.5
LOG2E = 1.4426950408889634
Q_SCALE = SCALE * LOG2E
PLAIN, ROPE, ROPE_Q = 0, 1, 2

GROUP_W = NSA_REP * HEAD_DIM
DIL_W = DIL_HEADS * HEAD_DIM
VMEM_LIMIT = 56 * 1024 * 1024


def _cparams(sem, vmem=None):
    return pltpu.CompilerParams(dimension_semantics=sem, vmem_limit_bytes=vmem)


def _masked_softmax(s, mask):
    s = jnp.where(mask, s, NEG)
    m = s.max(-1, keepdims=True)
    p = jnp.where(mask, jnp.exp2(s - m), 0.0)
    den = p.sum(-1, keepdims=True)
    safe = jnp.where(den > 0, den, 1.0)
    return p, m, safe


def _qk(q, k):
    return lax.dot_general(q, k, (((1,), (1,)), ((), ())), preferred_element_type=F32)


def _pack_pairs(xb):
    m = xb.shape[1] // 2
    lo = lax.bitcast_convert_type(xb[:, :m].astype(F32), jnp.uint32)
    hi = lax.bitcast_convert_type(xb[:, m:].astype(F32), jnp.uint32)
    return (lo >> 16) | (hi & jnp.uint32(0xFFFF0000))


def _unpack_pairs(w):
    lo = lax.bitcast_convert_type(w << 16, F32)
    hi = lax.bitcast_convert_type(w & jnp.uint32(0xFFFF0000), F32)
    return lo, hi


def _store_tile_rows(ref, start, packed, nsub):
    n = packed.shape[0]
    for j in range(nsub):
        ref[pl.ds(start + j, n, stride=nsub), :] = packed[:, j * LANES:(j + 1) * LANES]


def _load_tile_rows(ref, start, n, nsub, lead=()):
    return jnp.concatenate([ref[lead + (pl.ds(start + j, n, stride=nsub), slice(None))] for j in range(nsub)],
                           axis=1)


def _proj_kernel(flags_ref, x_ref, w_ref, b_ref, cos_ref, sin_ref, o_ref, xb_ref, *xcol_ref):
    j = pl.program_id(2)
    d, sub, tn = o_ref.shape
    n_sub = tn // LANES

    @pl.when(j == 0)
    def _():
        if d == 1:
            xb_ref[...] = x_ref[...].astype(BF16)
        else:
            xcol, = xcol_ref
            for c in range(xcol.shape[0]):
                xcol[c] = x_ref[:, c * LANES:(c + 1) * LANES]
            for c in range(xcol.shape[0]):
                for r in range(d):
                    xb_ref[r * sub:(r + 1) * sub, c * LANES:(c + 1) * LANES] = (
                        xcol[c, pl.ds(r, sub, stride=d), :].astype(BF16))

    acc = jnp.dot(xb_ref[...], w_ref[...], preferred_element_type=F32) + b_ref[...]
    for u in range(n_sub):
        a = acc[:, u * LANES:(u + 1) * LANES]
        roped = a * cos_ref[...] + pltpu.roll(a, HEAD_DIM // 2, 1) * sin_ref[...]
        flag = flags_ref[j * n_sub + u]
        mult = jnp.where(flag == ROPE_Q, Q_SCALE, 1.0).astype(F32)
        res = (jnp.where(flag == PLAIN, a, roped) * mult).astype(o_ref.dtype)
        for r in range(d):
            o_ref[r, :, u * LANES:(u + 1) * LANES] = res[r * sub:(r + 1) * sub]


def _project(x, w, b, flags, cosx, sinx, d, tm, tn):
    B, S, D = x.shape
    N = w.shape[1]
    L = S // d
    tm = min(tm, S)
    sub = tm // d

    def regroup(tab):
        return tab.reshape(S // tm, sub, d, HEAD_DIM).transpose(0, 2, 1, 3).reshape(S, HEAD_DIM)

    grid_spec = pltpu.PrefetchScalarGridSpec(
        num_scalar_prefetch=1,
        grid=(B, S // tm, N // tn),
        in_specs=[
            pl.BlockSpec((None, tm, D), lambda bb, i, j, f: (bb, i, 0)),
            pl.BlockSpec((D, tn), lambda bb, i, j, f: (0, j)),
            pl.BlockSpec((1, tn), lambda bb, i, j, f: (0, j)),
            pl.BlockSpec((tm, HEAD_DIM), lambda bb, i, j, f: (i, 0)),
            pl.BlockSpec((tm, HEAD_DIM), lambda bb, i, j, f: (i, 0)),
        ],
        out_specs=pl.BlockSpec((None, d, sub, tn), lambda bb, i, j, f: (bb, 0, i, j)),
        scratch_shapes=[pltpu.VMEM((tm, D), BF16)] + ([pltpu.VMEM((D // LANES, tm, LANES), F32)] if d > 1 else []),
    )
    return pl.pallas_call(
        _proj_kernel,
        out_shape=jax.ShapeDtypeStruct((B, d, L, N), BF16),
        grid_spec=grid_spec,
        compiler_params=_cparams(("parallel", "arbitrary", "arbitrary"), VMEM_LIMIT),
        name="proj_rope",
    )(flags, x, w, b, regroup(cosx), regroup(sinx))


def _gelu_tanh(x):
    return 0.5 * x * (1.0 + jnp.tanh(0.7978845608028654 * (x + 0.044715 * x * x * x)))


def _compress_kernel(x_ref, pos_ref, w1_ref, w2_ref, o_ref, xf_sc):
    s_len = x_ref.shape[0]
    nc = s_len // CMP_STRIDE
    xf_sc[0:s_len, :] = x_ref[...].astype(F32)
    xf_sc[s_len:s_len + CMP_STRIDE, :] = jnp.zeros((CMP_STRIDE, HEAD_DIM), F32)
    h = jnp.zeros((nc, w1_ref.shape[1]), F32)
    for l in range(CMP_LEN):
        tok = xf_sc[pl.ds(l, nc, stride=CMP_STRIDE), :]
        a = (tok + pos_ref[l:l + 1, :]).astype(BF16)
        h = h + jnp.dot(a, w1_ref[l * HEAD_DIM:(l + 1) * HEAD_DIM, :], preferred_element_type=F32)
    g = _gelu_tanh(h).astype(BF16)
    o_ref[...] = jnp.dot(g, w2_ref[...], preferred_element_type=F32).astype(o_ref.dtype)


def _compress(main, k_tile0, v_tile0, pos, w1, w2):
    B, _, S, _ = main.shape
    G = NSA_KV_GROUPS
    nC = S // CMP_STRIDE
    hid = w1.shape[-1]
    return pl.pallas_call(
        _compress_kernel,
        out_shape=jax.ShapeDtypeStruct((2, B, G, nC, HEAD_DIM), BF16),
        grid=(2, B, G),
        in_specs=[
            pl.BlockSpec((None, None, S, HEAD_DIM),
                         lambda a, bb, g: (bb, 0, 0, k_tile0 + a * (v_tile0 - k_tile0) + g)),
            pl.BlockSpec((None, CMP_LEN, HEAD_DIM), lambda a, bb, g: (a, 0, 0)),
            pl.BlockSpec((None, CMP_LEN * HEAD_DIM, hid), lambda a, bb, g: (a, 0, 0)),
            pl.BlockSpec((None, hid, HEAD_DIM), lambda a, bb, g: (a, 0, 0)),
        ],
        out_specs=pl.BlockSpec((None, None, None, nC, HEAD_DIM), lambda a, bb, g: (a, bb, g, 0, 0)),
        scratch_shapes=[pltpu.VMEM((S + CMP_STRIDE, HEAD_DIM), F32)],
        compiler_params=_cparams(("arbitrary", "arbitrary", "arbitrary"), VMEM_LIMIT),
        name="compress_mlp",
    )(main, pos, w1, w2)


def _cmp_attn_kernel(q_ref, kc_ref, vc_ref, ov_ref, o_ref, sel_ref, *, n_slc, n_sel):
    qi = pl.program_id(2)
    tq = q_ref.shape[0]
    nc = kc_ref.shape[0]
    t = qi * tq + lax.broadcasted_iota(jnp.int32, (tq, nc), 0)
    c = lax.broadcasted_iota(jnp.int32, (tq, nc), 1)
    mask = (c * CMP_STRIDE + (CMP_LEN - 1)) <= t
    kc = kc_ref[...]
    vc = vc_ref[...]
    ps = jnp.zeros((tq, nc), F32)
    for h in range(NSA_REP):
        s = _qk(q_ref[:, h * HEAD_DIM:(h + 1) * HEAD_DIM], kc)
        p, _, safe = _masked_softmax(s, mask)
        p = p / safe
        o = jnp.dot(p.astype(BF16), vc, preferred_element_type=F32)
        o_ref[:, h * HEAD_DIM:(h + 1) * HEAD_DIM] = o.astype(o_ref.dtype)
        ps = ps + p
    imp = jnp.dot(ps.astype(BF16), ov_ref[...], preferred_element_type=F32)
    tj = qi * tq + lax.broadcasted_iota(jnp.int32, (tq, LANES), 0)
    j = lax.broadcasted_iota(jnp.int32, (tq, LANES), 1)
    cur = tj // SEL_LEN
    forced = (j == 0) | (j == cur) | (j == cur - 1)
    imp = jnp.where(forced, FORCE, jnp.where(j > cur, NEG, imp))
    imp_t = imp.T[0:n_slc, :]
    grp = 8
    groups = [imp_t[a:a + grp, :] for a in range(0, n_slc, grp)]
    ranks = [jnp.zeros(gv.shape, jnp.int32) for gv in groups]
    for j2 in range(n_slc):
        row = imp_t[j2:j2 + 1, :]
        for gi, gv in enumerate(groups):
            lo = gi * grp
            if lo > j2:
                ahead = row >= gv
            elif lo + gv.shape[0] - 1 <= j2:
                ahead = row > gv
            else:
                later = lax.broadcasted_iota(jnp.int32, gv.shape, 0) + lo > j2
                ahead = (row > gv) | ((row == gv) & later)
            ranks[gi] = jnp.where(ahead, ranks[gi] + 1, ranks[gi])
    rank = jnp.concatenate(ranks, axis=0)
    sel_t = jnp.where(rank < n_sel, 0.0, NEG)
    if n_slc < LANES:
        sel_t = jnp.concatenate([sel_t, jnp.zeros((LANES - n_slc, tq), F32)], axis=0)
    sel_ref[...] = sel_t.T.astype(sel_ref.dtype)


def _cmp_attention(main, kvc, overlap, q_blk0, tq):
    B, _, S, _ = main.shape
    G = NSA_KV_GROUPS
    nC = kvc.shape[3]
    n_slc = S // SEL_LEN
    n_sel = min(SEL_TOPK, n_slc)
    kern = functools.partial(_cmp_attn_kernel, n_slc=n_slc, n_sel=n_sel)
    return pl.pallas_call(
        kern,
        out_shape=(jax.ShapeDtypeStruct((B, S, NSA_HEADS * HEAD_DIM), BF16),
                   jax.ShapeDtypeStruct((B, G, S, LANES), BF16)),
        grid=(B, G, S // tq),
        in_specs=[
            pl.BlockSpec((None, None, tq, GROUP_W), lambda bb, g, i: (bb, 0, i, q_blk0 + g)),
            pl.BlockSpec((None, None, None, nC, HEAD_DIM), lambda bb, g, i: (0, bb, g, 0, 0)),
            pl.BlockSpec((None, None, None, nC, HEAD_DIM), lambda bb, g, i: (1, bb, g, 0, 0)),
            pl.BlockSpec((nC, LANES), lambda bb, g, i: (0, 0)),
        ],
        out_specs=(pl.BlockSpec((None, tq, GROUP_W), lambda bb, g, i: (bb, i, g)),
                   pl.BlockSpec((None, None, tq, LANES), lambda bb, g, i: (bb, g, i, 0))),
        compiler_params=_cparams(("parallel", "arbitrary", "arbitrary"), VMEM_LIMIT),
        name="cmp_attn_select",
    )(main, kvc, kvc, overlap)


def _fold_lanes(x, op):
    out = x[:, 0:LANES]
    for u in range(1, x.shape[1] // LANES):
        out = op(out, x[:, u * LANES:(u + 1) * LANES])
    return out


def _sel_attn_kernel(q_ref, k_ref, v_ref, bias_ref, et_ref, wu_ref, wd_ref,
                     o_ref, wu_out, wd_out, qx_sc, s_sc, m_sc, l_sc, acc_sc):
    wu_out[...] = wu_ref[...].astype(wu_out.dtype)
    wd_out[...] = wd_ref[...].astype(wd_out.dtype)
    qi = pl.program_id(2)
    t = q_ref.shape[0]
    tc = s_sc.shape[2]
    rows = NSA_REP * t
    for h in range(NSA_REP):
        qx_sc[h * t:(h + 1) * t, 0:HEAD_DIM] = q_ref[:, h * HEAD_DIM:(h + 1) * HEAD_DIM]
        qx_sc[h * t:(h + 1) * t, HEAD_DIM:2 * HEAD_DIM] = bias_ref[...]

    def scores(c, w=1):
        start = pl.multiple_of(c * tc, tc)
        kx = jnp.concatenate([k_ref[pl.ds(start, w * tc), :], et_ref[pl.ds(start, w * tc), :]], axis=1)
        return _qk(qx_sc[...], kx)

    m_sc[...] = jnp.full(m_sc.shape, NEG, F32)

    def for_spans(n, fn):
        def trip(j, carry):
            fn(2 * j, 2)
            return carry

        lax.fori_loop(0, n // 2, trip, 0)

        @pl.when(n % 2 == 1)
        def _():
            fn(n - 1, 1)

    def max_span(c, w):
        s = scores(c, w)
        for u in range(w):
            s_sc[c + u] = s[:, u * tc:(u + 1) * tc]
        m_sc[...] = jnp.maximum(m_sc[...], _fold_lanes(s, jnp.maximum))

    n_full = (qi * t) // tc
    for_spans(n_full, max_span)
    qpos = qi * t + lax.broadcasted_iota(jnp.int32, (rows, tc), 0) % t
    kpos = n_full * tc + lax.broadcasted_iota(jnp.int32, (rows, tc), 1)
    s = jnp.where(kpos <= qpos, scores(n_full), NEG)
    s_sc[n_full] = s
    m = jnp.maximum(m_sc[...], _fold_lanes(s, jnp.maximum)).max(-1, keepdims=True)
    m_sc[...] = jnp.broadcast_to(m, m_sc.shape)
    l_sc[...] = jnp.zeros(l_sc.shape, F32)
    acc_sc[...] = jnp.zeros(acc_sc.shape, F32)

    def exp_span(c, w):
        mb = m_sc[...]
        p = jnp.concatenate([jnp.exp2(s_sc[c + u][:, v * LANES:(v + 1) * LANES] - mb)
                             for u in range(w) for v in range(tc // LANES)], axis=1)
        l_sc[...] += _fold_lanes(p, jnp.add)
        vals = v_ref[pl.ds(pl.multiple_of(c * tc, tc), w * tc), :]
        acc_sc[...] += jnp.dot(p.astype(BF16), vals, preferred_element_type=F32)

    for_spans(n_full + 1, exp_span)
    l = l_sc[...].sum(-1, keepdims=True)
    o = acc_sc[...] / jnp.where(l > 0, l, 1.0)
    for h in range(NSA_REP):
        o_ref[:, h * HEAD_DIM:(h + 1) * HEAD_DIM] = o[h * t:(h + 1) * t].astype(o_ref.dtype)


def _sel_attention(main, bias, onehot_t, w_up, w_down, q_blk0, k_tile0, v_tile0, t, tc):
    B, _, S, _ = main.shape
    G = NSA_KV_GROUPS
    t = min(t, S)
    tc = min(tc, S)
    rows = NSA_REP * t
    nq = S // t
    n_steps = B * G * nq
    wu2 = w_up.reshape(-1, w_up.shape[-1])
    wd2 = w_down.reshape(-1, w_down.shape[-1])
    ru, rd = wu2.shape[0] // n_steps, wd2.shape[0] // n_steps
    assert ru * n_steps == wu2.shape[0] and rd * n_steps == wd2.shape[0]

    def slab(bb, g, i):
        return ((bb * G + g) * nq + i, 0)

    o, wu_b, wd_b = pl.pallas_call(
        _sel_attn_kernel,
        out_shape=(jax.ShapeDtypeStruct((B, S, NSA_HEADS * HEAD_DIM), BF16),
                   jax.ShapeDtypeStruct(wu2.shape, BF16), jax.ShapeDtypeStruct(wd2.shape, BF16)),
        grid=(B, G, nq),
        in_specs=[
            pl.BlockSpec((None, None, t, GROUP_W), lambda bb, g, i: (bb, 0, i, q_blk0 + g)),
            pl.BlockSpec((None, None, S, HEAD_DIM), lambda bb, g, i: (bb, 0, 0, k_tile0 + g)),
            pl.BlockSpec((None, None, S, HEAD_DIM), lambda bb, g, i: (bb, 0, 0, v_tile0 + g)),
            pl.BlockSpec((None, None, t, LANES), lambda bb, g, i: (bb, g, i, 0)),
            pl.BlockSpec((S, LANES), lambda bb, g, i: (0, 0)),
            pl.BlockSpec((ru, wu2.shape[1]), slab),
            pl.BlockSpec((rd, wd2.shape[1]), slab),
        ],
        out_specs=(pl.BlockSpec((None, t, GROUP_W), lambda bb, g, i: (bb, i, g)),
                   pl.BlockSpec((ru, wu2.shape[1]), slab), pl.BlockSpec((rd, wd2.shape[1]), slab)),
        scratch_shapes=[pltpu.VMEM((rows, 2 * HEAD_DIM), BF16), pltpu.VMEM((S // tc, rows, tc), F32),
                        pltpu.VMEM((rows, LANES), F32), pltpu.VMEM((rows, LANES), F32),
                        pltpu.VMEM((rows, HEAD_DIM), F32)],
        compiler_params=_cparams(("parallel", "arbitrary", "arbitrary"), VMEM_LIMIT),
        name="selected_attn",
    )(main, main, main, bias, onehot_t, wu2, wd2)
    return o, wu_b.reshape(w_up.shape), wd_b.reshape(w_down.shape)


def _band_kernel(*refs, n_prev, n_sub, max_dist, kv_heads, with_lse):
    q_ref = refs[0]
    k_refs = refs[1:2 + n_prev]
    v_refs = refs[2 + n_prev:3 + 2 * n_prev]
    o_ref = refs[3 + 2 * n_prev]
    qi = pl.program_id(2)
    t = k_refs[0].shape[0]
    nk = (n_prev + 1) * t
    n_heads = q_ref.shape[1] // HEAD_DIM
    k_all = jnp.concatenate([r[...] for r in k_refs], axis=0)
    v_all = jnp.concatenate([r[...] for r in v_refs], axis=0)
    kcol = lax.broadcasted_iota(jnp.int32, (t, nk), 1)
    diff = n_prev * t + lax.broadcasted_iota(jnp.int32, (t, nk), 0) - kcol
    in_band = (diff >= 0) & (diff <= max_dist)
    lane = lax.broadcasted_iota(jnp.int32, (t, LANES), 1)
    for u in range(n_sub):
        first_key = (qi * n_sub + u - n_prev) * t
        bias = jnp.where(in_band & (kcol + first_key >= 0), 0.0, NEG)
        rows = slice(u * t, (u + 1) * t)
        k_u = k_all[u * t:u * t + nk]
        v_u = v_all[u * t:u * t + nk]
        lse = jnp.zeros((t, LANES), F32)
        if kv_heads == 1:
            q = jnp.concatenate([q_ref[rows, h * HEAD_DIM:(h + 1) * HEAD_DIM] for h in range(n_heads)], axis=0)
            s = _qk(q, k_u).reshape(n_heads, t, nk) + bias[None]
            m = s.max(-1, keepdims=True)
            p = jnp.exp2(s - m)
            l = p.sum(-1, keepdims=True)
            o = jnp.dot(p.astype(BF16).reshape(n_heads * t, nk), v_u, preferred_element_type=F32)
            o = o.reshape(n_heads, t, HEAD_DIM) / l
            for h in range(n_heads):
                o_ref[rows, h * HEAD_DIM:(h + 1) * HEAD_DIM] = o[h].astype(o_ref.dtype)
                lse = jnp.where(lane == h, m[h] + jnp.log2(l[h]), lse)
        else:
            for h in range(n_heads):
                cols = slice(h * HEAD_DIM, (h + 1) * HEAD_DIM)
                s = _qk(q_ref[rows, cols], k_u[:, cols]) + bias
                m = s.max(-1, keepdims=True)
                p = jnp.exp2(s - m)
                l = p.sum(-1, keepdims=True)
                o = jnp.dot(p.astype(BF16), v_u[:, cols], preferred_element_type=F32)
                o_ref[rows, cols] = (o / l).astype(o_ref.dtype)
                lse = jnp.where(lane == h, m + jnp.log2(l), lse)
        if with_lse:
            refs[4 + 2 * n_prev][rows, :] = lse


def _band_attention(src, lead_grid, length, q_map, k_map, v_map, kv_width, out_shape, o_map,
                    t, n_sub, max_dist, with_lse):
    t = min(t, length)
    n_prev = -(-max_dist // t)
    n_sub = min(n_sub, length // t)
    tile = n_sub * t
    kern = functools.partial(_band_kernel, n_prev=n_prev, n_sub=n_sub, max_dist=max_dist,
                             kv_heads=kv_width // HEAD_DIM, with_lse=with_lse)

    def preceding(fn, j):
        def index_map(bb, a, i):
            return fn(bb, a, jnp.maximum(i * n_sub - n_prev + j, 0))
        return index_map

    def kv_specs(fn):
        return ([pl.BlockSpec((None, None, t, kv_width), preceding(fn, j)) for j in range(n_prev)]
                + [pl.BlockSpec((None, None, tile, kv_width), fn)])

    in_specs = [pl.BlockSpec((None, None, tile, GROUP_W), q_map)] + kv_specs(k_map) + kv_specs(v_map)
    o_spec = pl.BlockSpec((None, tile, GROUP_W), o_map)
    if with_lse:
        lse_shape = out_shape[:-1] + (out_shape[-1] // GROUP_W * LANES,)
        out_shapes = (jax.ShapeDtypeStruct(out_shape, BF16), jax.ShapeDtypeStruct(lse_shape, F32))
        out_specs = (o_spec, pl.BlockSpec((None, tile, LANES), o_map))
    else:
        out_shapes = jax.ShapeDtypeStruct(out_shape, BF16)
        out_specs = o_spec
    return pl.pallas_call(
        kern, out_shape=out_shapes, grid=lead_grid + (length // tile,), in_specs=in_specs, out_specs=out_specs,
        compiler_params=_cparams(("parallel", "arbitrary", "arbitrary"), VMEM_LIMIT),
        name="band_attn",
    )(*([src] * (3 + 2 * n_prev)))


def _layer_norm(z, g, b):
    mu = z.mean(-1, keepdims=True)
    zc = z - mu
    var = (zc * zc).mean(-1, keepdims=True)
    return zc * lax.rsqrt(var + LN_EPS) * g + b


def _merge_kernel(ocmp_ref, oslc_ref, owin_ref, gl_ref, d0_ref, d1_ref, d2_ref, l0_ref, l1_ref, l2_ref,
                  ga_ref, gb_ref, x_ref, wa_ref, wb_ref, wo_ref, g_ref, b_ref, wr_ref, br_ref,
                  hf_ref, hp_ref, lg_ref, nsa_a, nsa_b, dil_a, dil_b, *, alpha):
    s = pl.program_id(0)
    tt = x_ref.shape[0]

    @pl.when(s == 0)
    def _():
        nsa_b[...] = jnp.zeros(nsa_b.shape, nsa_b.dtype)
        dil_b[...] = jnp.zeros(dil_b.shape, dil_b.dtype)

    def gate_stage(nsa_out, dil_out):
        gates = jax.nn.sigmoid(gl_ref[...].astype(F32))
        for h in range(NSA_HEADS):
            sl = slice(h * HEAD_DIM, (h + 1) * HEAD_DIM)
            acc = jnp.zeros((tt, HEAD_DIM), F32)
            for br, ref in enumerate((ocmp_ref, oslc_ref, owin_ref)):
                gcol = gates[:, 3 * h + br:3 * h + br + 1]
                acc = acc + gcol * ref[:, sl].astype(F32)
            nsa_out[:, sl] = acc.astype(BF16)
        l0, l1, l2 = l0_ref[...], l1_ref[...], l2_ref[...]
        lm = jnp.maximum(jnp.maximum(l0, l1), l2)
        e0, e1, e2 = jnp.exp2(l0 - lm), jnp.exp2(l1 - lm), jnp.exp2(l2 - lm)
        inv = 1.0 / (e0 + e1 + e2)
        w0, w1, w2 = e0 * inv, e1 * inv, e2 * inv
        for h in range(DIL_HEADS):
            sl = slice(h * HEAD_DIM, (h + 1) * HEAD_DIM)
            o = (w0[:, h:h + 1] * d0_ref[:, sl].astype(F32) + w1[:, h:h + 1] * d1_ref[:, sl].astype(F32)
                 + w2[:, h:h + 1] * d2_ref[:, sl].astype(F32))
            dil_out[:, sl] = o.astype(BF16)

    def matmul_stage(nsa_in, dil_in):
        y_a = jnp.dot(nsa_in[...], wa_ref[...], preferred_element_type=F32)
        y_b = jnp.dot(dil_in[...], wb_ref[...], preferred_element_type=F32)
        merged = (jax.nn.sigmoid(ga_ref[...].astype(F32)) * y_a
                  + jax.nn.sigmoid(gb_ref[...].astype(F32)) * y_b)
        mix = jnp.dot(merged.astype(BF16), wo_ref[...], preferred_element_type=F32)
        h = _layer_norm(alpha * x_ref[...] + mix, g_ref[...], b_ref[...])
        hf_ref[...] = h
        hb = h.astype(BF16)
        _store_tile_rows(hp_ref, 0, _pack_pairs(hb), hp_ref.shape[0] // tt)
        lg_ref[...] = jnp.dot(hb, wr_ref[...], preferred_element_type=F32) + br_ref[...]

    @pl.when(s % 2 == 0)
    def _():
        gate_stage(nsa_a, dil_a)
        matmul_stage(nsa_b, dil_b)

    @pl.when(s % 2 == 1)
    def _():
        gate_stage(nsa_b, dil_b)
        matmul_stage(nsa_a, dil_a)


def _merge(o_cmp, o_slc, o_win, main2d, gl_tile, ga_blk, gb_blk, dil_o, dil_lse, x2d,
           w_a, w_b, w_o, ln_g, ln_b, w_r, b_r, alpha, tt):
    T, D = x2d.shape
    nsub = D // 2 // LANES
    nsa_w = NSA_HEADS * HEAD_DIM
    nt = T // tt
    ahead = lambda s: (jnp.minimum(s, nt - 1), 0)
    behind = lambda s: (jnp.maximum(s - 1, 0), 0)
    once = pl.Buffered(1)

    def const(shape):
        return pl.BlockSpec(shape, lambda s: (0, 0), pipeline_mode=once)

    in_specs = [
        pl.BlockSpec((tt, nsa_w), ahead), pl.BlockSpec((tt, nsa_w), ahead), pl.BlockSpec((tt, nsa_w), ahead),
        pl.BlockSpec((tt, LANES), lambda s: (jnp.minimum(s, nt - 1), gl_tile)),
    ]
    in_specs += [pl.BlockSpec((tt, DIL_W), ahead)] * 3 + [pl.BlockSpec((tt, LANES), ahead)] * 3
    in_specs += [
        pl.BlockSpec((tt, D), lambda s: (jnp.maximum(s - 1, 0), ga_blk)),
        pl.BlockSpec((tt, D), lambda s: (jnp.maximum(s - 1, 0), gb_blk)),
        pl.BlockSpec((tt, D), behind),
        const((nsa_w, D)), const((DIL_W, D)), const((D, D)), const((1, D)), const((1, D)),
        const((D, LANES)), const((1, LANES)),
    ]
    return pl.pallas_call(
        functools.partial(_merge_kernel, alpha=alpha),
        out_shape=(jax.ShapeDtypeStruct((T, D), F32), jax.ShapeDtypeStruct((T * nsub, LANES), jnp.uint32),
                   jax.ShapeDtypeStruct((T, LANES), F32)),
        grid=(nt + 1,),
        in_specs=in_specs,
        out_specs=(pl.BlockSpec((tt, D), behind), pl.BlockSpec((tt * nsub, LANES), behind),
                   pl.BlockSpec((tt, LANES), behind)),
        scratch_shapes=[pltpu.VMEM((tt, nsa_w), BF16), pltpu.VMEM((tt, nsa_w), BF16),
                        pltpu.VMEM((tt, DIL_W), BF16), pltpu.VMEM((tt, DIL_W), BF16)],
        compiler_params=_cparams(("arbitrary",), VMEM_LIMIT),
        name="merge_ln1",
    )(o_cmp, o_slc, o_win, main2d, *dil_o, *dil_lse, main2d, main2d, x2d, w_a, w_b, w_o, ln_g, ln_b, w_r, b_r)


def _router_kernel(lg_ref, tri_ref, meta_ref, cnt_ref, carry_sc):
    i = pl.program_id(0)
    tt = lg_ref.shape[0]

    @pl.when(i == 0)
    def _():
        carry_sc[...] = jnp.zeros(carry_sc.shape, F32)

    logits = lg_ref[...]
    lane = lax.broadcasted_iota(jnp.int32, (tt, LANES), 1)
    v = logits
    onehot = jnp.zeros((tt, LANES), F32)
    vals, idxs = [], []
    for _ in range(TOP_K):
        m = v.max(-1, keepdims=True)
        idx = jnp.where(v == m, lane, LANES).min(-1, keepdims=True)
        hit = lane == idx
        vals.append(m)
        idxs.append(idx)
        onehot = onehot + hit.astype(F32)
        v = jnp.where(hit, -jnp.inf, v)
    exps = [jnp.exp(vk - vals[0]) for vk in vals]
    den = exps[0] + exps[1] + exps[2] + exps[3]
    before = jnp.dot(tri_ref[...], onehot.astype(BF16), preferred_element_type=F32) + carry_sc[0:1, :]
    meta = jnp.zeros((tt, LANES), F32)
    for k in range(TOP_K):
        rank = jnp.where(lane == idxs[k], before, 0.0).sum(-1, keepdims=True)
        meta = jnp.where(lane == k, idxs[k].astype(F32), meta)
        meta = jnp.where(lane == TOP_K + k, exps[k] / den, meta)
        meta = jnp.where(lane == 2 * TOP_K + k, rank, meta)
    meta_ref[...] = meta
    carry_sc[...] = carry_sc[...] + jnp.broadcast_to(onehot.sum(0, keepdims=True), carry_sc.shape)
    cnt_ref[...] = carry_sc[...]


def _router(logits, tt):
    T = logits.shape[0]
    tri = (jnp.arange(tt)[:, None] > jnp.arange(tt)[None, :]).astype(BF16)
    return pl.pallas_call(
        _router_kernel,
        out_shape=(jax.ShapeDtypeStruct((T, LANES), F32), jax.ShapeDtypeStruct((8, LANES), F32)),
        grid=(T // tt,),
        in_specs=[
            pl.BlockSpec((tt, LANES), lambda i: (i, 0)),
            pl.BlockSpec((tt, tt), lambda i: (0, 0)),
        ],
        out_specs=(pl.BlockSpec((tt, LANES), lambda i: (i, 0)), pl.BlockSpec((8, LANES), lambda i: (0, 0))),
        scratch_shapes=[pltpu.VMEM((8, LANES), F32)],
        compiler_params=_cparams(("arbitrary",), VMEM_LIMIT),
        name="router_top4",
    )(logits, tri)


def _dispatch_kernel(pe_ref, dest_ref, h_ref, xs_hbm, zero_sc, sem, *, nsub, tm):
    n = h_ref.shape[0] // nsub
    blk = tm * nsub

    @pl.when(pl.program_id(0) == 0)
    def _():
        zero_sc[...] = jnp.zeros(zero_sc.shape, zero_sc.dtype)

        def clear(e):
            end = pe_ref[e]
            begin = pe_ref[e - 1] if e else 0
            start = pl.multiple_of((end - tm) * nsub, nsub)
            return end > begin, pltpu.make_async_copy(zero_sc, xs_hbm.at[pl.ds(start, blk), :], sem)

        for e in range(pe_ref.shape[0]):
            nonempty, copy = clear(e)
            pl.when(nonempty)(copy.start)
        for e in range(pe_ref.shape[0]):
            nonempty, copy = clear(e)
            pl.when(nonempty)(copy.wait)

    for t in range(n):
        for k in range(TOP_K):
            d = pl.multiple_of(dest_ref[t * TOP_K + k] * nsub, nsub)
            pltpu.make_async_copy(h_ref.at[pl.ds(t * nsub, nsub), :], xs_hbm.at[pl.ds(d, nsub), :],
                                  sem).start(priority=k % 2)
    for k in range(TOP_K):
        pltpu.make_async_copy(h_ref, xs_hbm.at[pl.ds(0, n * nsub), :], sem).wait()


def _dispatch(hp, dest_flat, pad_end, n_rows, nsub, tm, tt):
    T = hp.shape[0] // nsub
    grid_spec = pltpu.PrefetchScalarGridSpec(
        num_scalar_prefetch=1,
        grid=(T // tt,),
        in_specs=[
            pl.BlockSpec((tt * TOP_K,), lambda i, pe: (i,), memory_space=pltpu.SMEM),
            pl.BlockSpec((tt * nsub, LANES), lambda i, pe: (i, 0)),
        ],
        out_specs=pl.BlockSpec(memory_space=pl.ANY),
        scratch_shapes=[pltpu.VMEM((tm * nsub, LANES), hp.dtype), pltpu.SemaphoreType.DMA(())],
    )
    return pl.pallas_call(
        functools.partial(_dispatch_kernel, nsub=nsub, tm=tm),
        out_shape=jax.ShapeDtypeStruct((n_rows * nsub, LANES), hp.dtype),
        grid_spec=grid_spec,
        compiler_params=_cparams(("arbitrary",), VMEM_LIMIT),
        name="moe_dispatch",
    )(pad_end, dest_flat, hp)


def _expert_kernel(be_ref, nu_ref, x_ref, wg_ref, wl_ref, bg_ref, bl_ref, wd_ref, bd_ref, y_ref, xb_sc, *, th, nsub):
    i = pl.program_id(0)
    tm = xb_sc.shape[0]
    half = nsub * LANES
    dh = wg_ref.shape[1]

    @pl.when(i < nu_ref[0])
    def _():
        lo, hi = _unpack_pairs(_load_tile_rows(x_ref, 0, tm, nsub))
        xb_sc[:, :half] = lo.astype(BF16)
        xb_sc[:, half:] = hi.astype(BF16)
        x = xb_sc[...]
        y = bd_ref[...]
        for c in range(dh // th):
            sl = slice(c * th, (c + 1) * th)
            glu = jnp.dot(x, wg_ref[:, sl], preferred_element_type=F32) + bg_ref[:, sl]
            lin = jnp.dot(x, wl_ref[:, sl], preferred_element_type=F32) + bl_ref[:, sl]
            glu = jnp.minimum(glu, SWIGLU_LIMIT)
            lin = jnp.clip(lin, -SWIGLU_LIMIT, SWIGLU_LIMIT)
            act = glu * jax.nn.sigmoid(SWIGLU_ALPHA * glu) * (lin + 1.0)
            y = y + jnp.dot(act.astype(BF16), wd_ref[sl, :], preferred_element_type=F32)
        _store_tile_rows(y_ref, 0, _pack_pairs(y.astype(BF16)), nsub)


def _experts(xs, blk_expert, n_used, w_up, b_up, w_down, b_down, tm, th):
    E, D, two_dh = w_up.shape
    nsub = D // 2 // LANES
    n_blk = xs.shape[0] // (tm * nsub)
    dh = two_dh // 2
    th = min(th, dh)
    once = pl.Buffered(1)

    def row(i, be, nu):
        return (jnp.minimum(i, nu[0] - 1), 0)

    grid_spec = pltpu.PrefetchScalarGridSpec(
        num_scalar_prefetch=2,
        grid=(n_blk,),
        in_specs=[
            pl.BlockSpec((tm * nsub, LANES), row),
            pl.BlockSpec((None, D, dh), lambda i, be, nu: (be[i], 0, 0), pipeline_mode=once),
            pl.BlockSpec((None, D, dh), lambda i, be, nu: (be[i], 0, 1), pipeline_mode=once),
            pl.BlockSpec((None, 1, dh), lambda i, be, nu: (be[i], 0, 0)),
            pl.BlockSpec((None, 1, dh), lambda i, be, nu: (be[i], 0, 1)),
            pl.BlockSpec((None, dh, D), lambda i, be, nu: (be[i], 0, 0), pipeline_mode=once),
            pl.BlockSpec((None, 1, D), lambda i, be, nu: (be[i], 0, 0)),
        ],
        out_specs=pl.BlockSpec((tm * nsub, LANES), row),
        scratch_shapes=[pltpu.VMEM((tm, D), BF16)],
    )
    return pl.pallas_call(
        functools.partial(_expert_kernel, th=th, nsub=nsub),
        out_shape=jax.ShapeDtypeStruct(xs.shape, jnp.uint32),
        grid_spec=grid_spec,
        compiler_params=_cparams(("arbitrary",), VMEM_LIMIT),
        name="moe_experts",
    )(blk_expert, n_used, xs, w_up, w_up, b_up, b_up, w_down, b_down)


def _combine_kernel(dest_ref, dest_next_ref, meta_ref, h_ref, g_ref, b_ref, y_hbm, o_ref, buf, sem, *, alpha, nsub):
    i = pl.program_id(0)
    n = o_ref.shape[0]
    slot = i % 2

    def gather(idx_ref, s):
        for t in range(n):
            for k in range(TOP_K):
                d = pl.multiple_of(idx_ref[t * TOP_K + k] * nsub, nsub)
                pltpu.make_async_copy(y_hbm.at[pl.ds(d, nsub), :], buf.at[s, pl.ds((k * n + t) * nsub, nsub), :],
                                      sem.at[s]).start(priority=k % 2)

    @pl.when(i == 0)
    def _():
        gather(dest_ref, 0)

    @pl.when(i + 1 < pl.num_programs(0))
    def _():
        gather(dest_next_ref, 1 - slot)

    for k in range(TOP_K):
        pltpu.make_async_copy(y_hbm.at[pl.ds(0, n * nsub), :], buf.at[slot, pl.ds(k * n * nsub, n * nsub), :],
                              sem.at[slot]).wait()
    ffn_lo = ffn_hi = None
    for k in range(TOP_K):
        lo, hi = _unpack_pairs(_load_tile_rows(buf, k * n * nsub, n, nsub, lead=(slot,)))
        gate = meta_ref[:, TOP_K + k:TOP_K + k + 1]
        ffn_lo = gate * lo if k == 0 else ffn_lo + gate * lo
        ffn_hi = gate * hi if k == 0 else ffn_hi + gate * hi
    ffn = jnp.concatenate([ffn_lo, ffn_hi], axis=1)
    o_ref[...] = _layer_norm(alpha * h_ref[...] + ffn, g_ref[...], b_ref[...])


def _combine_ln(y, dest_flat, meta, h, g, b, alpha, tt):
    T, D = h.shape
    nsub = D // 2 // LANES
    nt = T // tt
    return pl.pallas_call(
        functools.partial(_combine_kernel, alpha=alpha, nsub=nsub),
        out_shape=jax.ShapeDtypeStruct((T, D), F32),
        grid=(nt,),
        in_specs=[
            pl.BlockSpec((tt * TOP_K,), lambda i: (i,), memory_space=pltpu.SMEM),
            pl.BlockSpec((tt * TOP_K,), lambda i: (jnp.minimum(i + 1, nt - 1),), memory_space=pltpu.SMEM),
            pl.BlockSpec((tt, LANES), lambda i: (i, 0)),
            pl.BlockSpec((tt, D), lambda i: (i, 0)),
            pl.BlockSpec((1, D), lambda i: (0, 0)),
            pl.BlockSpec((1, D), lambda i: (0, 0)),
            pl.BlockSpec(memory_space=pl.ANY),
        ],
        out_specs=pl.BlockSpec((tt, D), lambda i: (i, 0)),
        scratch_shapes=[pltpu.VMEM((2, TOP_K * tt * nsub, LANES), y.dtype), pltpu.SemaphoreType.DMA((2,))],
        compiler_params=_cparams(("arbitrary",), VMEM_LIMIT),
        name="moe_combine_ln2",
    )(dest_flat, dest_flat, meta, h, g, b, y)


def _layer(x, w_in, b_in, pos_k, pos_v, ck_w1, ck_w2, cv_w1, cv_w2, w_br_nsa, w_br_dil, w_out,
           ln1_g, ln1_b, w_router, b_router, w_up, b_up, w_down, b_down, ln2_g, ln2_b, alpha):
    B, S, D = x.shape
    T = B * S
    nd = D // LANES
    G = NSA_KV_GROUPS
    kvw = G * HEAD_DIM
    n_exp = w_router.shape[1]

    o_q = 0
    o_kv = NSA_HEADS * HEAD_DIM
    o_gl = o_kv + 6 * kvw
    o_dil = o_gl + 3 * NSA_HEADS
    o_ga = o_dil + 3 * N_DIL * DIL_W
    o_gb = o_ga + D

    def wcols(a, n):
        return w_in[:, a:a + n], b_in[a:a + n]

    def kv(i):
        return wcols(o_kv + i * kvw, kvw)

    tn = 1024
    gl_w, gl_b = wcols(o_gl, 3 * NSA_HEADS)
    pieces = [wcols(o_ga, D), wcols(o_gb, D), wcols(o_q, NSA_HEADS * HEAD_DIM),
              kv(0), kv(2), kv(4), kv(1), kv(3), kv(5), (gl_w, gl_b)]
    used = sum(p[0].shape[1] for p in pieces)
    n_main = -(-used // tn) * tn
    pieces.append((jnp.zeros((D, n_main - used), F32), jnp.zeros((n_main - used,), F32)))
    w_main = jnp.concatenate([p[0] for p in pieces], axis=1).astype(BF16)
    b_main = jnp.concatenate([p[1] for p in pieces])[None, :]
    t_q = 2 * nd
    t_kc, t_ks, t_kw = t_q + 8, t_q + 10, t_q + 12
    t_vc, t_vs, t_vw = t_q + 14, t_q + 16, t_q + 18
    t_gl = t_q + 20
    tile_id = jnp.arange(n_main // LANES)
    flags_main = jnp.where((tile_id >= t_q) & (tile_id < t_kc), ROPE_Q,
                           jnp.where((tile_id >= t_kc) & (tile_id < t_vc), ROPE, PLAIN)).astype(jnp.int32)
    q_blk0 = t_q // NSA_REP

    pos = jnp.arange(S, dtype=F32)
    inv = ROPE_THETA ** (-jnp.arange(0, HEAD_DIM, 2, dtype=F32) / HEAD_DIM)
    ang = pos[:, None] * inv[None, :]
    cosx = jnp.concatenate([jnp.cos(ang), jnp.cos(ang)], axis=-1)
    sinx = jnp.concatenate([-jnp.sin(ang), jnp.sin(ang)], axis=-1)

    main = _project(x, w_main, b_main, flags_main, cosx, sinx, 1, 1024, tn)

    nC = S // CMP_STRIDE
    w1 = jnp.stack([ck_w1, cv_w1]).astype(BF16)
    w2 = jnp.stack([ck_w2, cv_w2]).astype(BF16)
    kvc = _compress(main, t_kc, t_vc, jnp.stack([pos_k, pos_v]), w1, w2)

    n_slc = S // SEL_LEN
    assert n_slc <= LANES
    c_start = jnp.arange(nC) * CMP_STRIDE
    jb = jnp.arange(LANES)
    overlap = ((c_start[:, None] < (jb[None, :] + 1) * SEL_LEN) & (c_start[:, None] + CMP_LEN > jb[None, :] * SEL_LEN)
               & (jb[None, :] < n_slc) & (c_start[:, None] + CMP_LEN <= S)).astype(BF16)
    tq = min(512, S)
    o_cmp, sel = _cmp_attention(main, kvc, overlap, q_blk0, tq)

    onehot_t = (jnp.arange(S)[:, None] // SEL_LEN == jnp.arange(LANES)[None, :]).astype(BF16)
    o_slc, w_up_b, w_down_b = _sel_attention(main, sel, onehot_t, w_up, w_down, q_blk0, t_ks, t_vs, 256, 512)

    o_win = _band_attention(
        main, (B, G), S,
        lambda bb, g, i: (bb, 0, i, q_blk0 + g),
        lambda bb, g, i: (bb, 0, i, t_kw + g),
        lambda bb, g, i: (bb, 0, i, t_vw + g),
        HEAD_DIM, (B, S, NSA_HEADS * HEAD_DIM), lambda bb, g, i: (bb, i, g), 128, 4, WIN_LEN - 1, False)

    flags_dil = jnp.array([ROPE_Q] * DIL_HEADS + [ROPE] * DIL_HEADS + [PLAIN] * DIL_HEADS, jnp.int32)
    dil_o, dil_lse = [], []
    for gi, (w, d) in enumerate(DIL_CONFIGS):
        wd, bd = wcols(o_dil + gi * 3 * DIL_W, 3 * DIL_W)
        sub = _project(x, wd.astype(BF16), bd[None, :], flags_dil, cosx, sinx, d, 512, 3 * DIL_W)
        L = S // d
        o_g, lse_g = _band_attention(
            sub, (B, d), L,
            lambda bb, r, i: (bb, r, i, 0),
            lambda bb, r, i: (bb, r, i, 1),
            lambda bb, r, i: (bb, r, i, 2),
            DIL_W, (B, L, d * DIL_W), lambda bb, r, i: (bb, i, r), 128, 4, w // d, True)
        dil_o.append(o_g.reshape(T, DIL_W))
        dil_lse.append(lse_g.reshape(T, LANES))

    w_r = jnp.concatenate([w_router, jnp.zeros((D, LANES - n_exp), F32)], axis=1).astype(BF16)
    b_r = jnp.concatenate([b_router, jnp.full((LANES - n_exp,), NEG, F32)])[None, :]
    h_f, h_p, logits = _merge(
        o_cmp.reshape(T, -1), o_slc.reshape(T, -1), o_win.reshape(T, -1), main.reshape(T, n_main),
        t_gl, 0, 1, dil_o, dil_lse, x.reshape(T, D),
        w_br_nsa.astype(BF16), w_br_dil.astype(BF16), w_out.astype(BF16),
        ln1_g[None, :], ln1_b[None, :], w_r, b_r, alpha, min(256, T))

    meta, cnt = _router(logits, min(512, T))
    top_idx = meta[:, 0:TOP_K].astype(jnp.int32)
    rank = meta[:, 2 * TOP_K:3 * TOP_K].astype(jnp.int32)

    tm = 512 if T * TOP_K >= 512 * n_exp else 128
    counts = cnt[0, :n_exp].astype(jnp.int32)
    padded = (counts + tm - 1) // tm * tm
    pad_end = jnp.cumsum(padded)
    pad_start = pad_end - padded
    start_of = jnp.where(top_idx[..., None] == jnp.arange(n_exp), pad_start, 0).sum(-1)
    dest = (start_of + rank).reshape(T * TOP_K)
    n_rows = T * TOP_K + n_exp * tm
    n_blk = n_rows // tm
    blk_start = jnp.arange(n_blk, dtype=jnp.int32) * tm
    blk_expert = jnp.minimum((pad_end[None, :] <= blk_start[:, None]).sum(-1), n_exp - 1).astype(jnp.int32)
    n_used = (pad_end[-1:] // tm).astype(jnp.int32)

    xs = _dispatch(h_p, dest, pad_end.astype(jnp.int32), n_rows, D // 2 // LANES, tm, min(256, T))
    y = _experts(xs, blk_expert, n_used, w_up_b, b_up[:, None, :], w_down_b, b_down[:, None, :], tm, 512)
    out = _combine_ln(y, dest, meta, h_f, ln2_g[None, :], ln2_b[None, :], alpha, min(256, T))
    return out.reshape(B, S, D)


def kernel(x, w_in, b_in, cmp_pos_k, cmp_pos_v, cmp_k_w1, cmp_k_w2, cmp_v_w1, cmp_v_w2, w_br_nsa, w_br_dil,
           w_out, ln1_g, ln1_b, w_router, b_router, w_up, b_up, w_down, b_down, ln2_g, ln2_b):
    depth = w_in.shape[0]
    alpha = (2.0 * depth) ** 0.25
    h = x
    for l in range(depth):
        h = _layer(h, w_in[l], b_in[l], cmp_pos_k[l], cmp_pos_v[l], cmp_k_w1[l], cmp_k_w2[l],
                   cmp_v_w1[l], cmp_v_w2[l], w_br_nsa[l], w_br_dil[l], w_out[l], ln1_g[l], ln1_b[l],
                   w_router[l], b_router[l], w_up[l], b_up[l], w_down[l], b_down[l], ln2_g[l], ln2_b[l], alpha)
    return h
```

```python
import functools

import jax
import jax.numpy as jnp
from jax import lax
from jax.experimental import pallas as pl
from jax.experimental.pallas import tpu as pltpu

F32 = jnp.float32
BF16 = jnp.bfloat16

HEAD_DIM = 128
LANES = 128
ROPE_THETA = 10000.0
NSA_HEADS = 8
NSA_KV_GROUPS = 2
NSA_REP = NSA_HEADS // NSA_KV_GROUPS
CMP_LEN = 32
CMP_STRIDE = 16
SEL_LEN = 64
SEL_TOPK = 16
WIN_LEN = 512
DIL_CONFIGS = ((128, 1), (512, 4), (2048, 16))
N_DIL = len(DIL_CONFIGS)
DIL_HEADS = 4
TOP_K = 4
SWIGLU_LIMIT = 7.0
SWIGLU_ALPHA = 1.702
LN_EPS = 1e-5
NEG = -1e30
FORCE = 1e9
SCALE = HEAD_DIM ** -0.5
LOG2E = 1.4426950408889634
Q_SCALE = SCALE * LOG2E
PLAIN, ROPE, ROPE_Q = 0, 1, 2

GROUP_W = NSA_REP * HEAD_DIM
DIL_W = DIL_HEADS * HEAD_DIM
VMEM_LIMIT = 56 * 1024 * 1024


def _cparams(sem, vmem=None):
    return pltpu.CompilerParams(dimension_semantics=sem, vmem_limit_bytes=vmem)


def _masked_softmax(s, mask):
    s = jnp.where(mask, s, NEG)
    m = s.max(-1, keepdims=True)
    p = jnp.where(mask, jnp.exp2(s - m), 0.0)
    den = p.sum(-1, keepdims=True)
    safe = jnp.where(den > 0, den, 1.0)
    return p, m, safe


def _qk(q, k):
    return lax.dot_general(q, k, (((1,), (1,)), ((), ())), preferred_element_type=F32)


def _pack_pairs(xb):
    m = xb.shape[1] // 2
    lo = lax.bitcast_convert_type(xb[:, :m].astype(F32), jnp.uint32)
    hi = lax.bitcast_convert_type(xb[:, m:].astype(F32), jnp.uint32)
    return (lo >> 16) | (hi & jnp.uint32(0xFFFF0000))


def _unpack_pairs(w):
    lo = lax.bitcast_convert_type(w << 16, F32)
    hi = lax.bitcast_convert_type(w & jnp.uint32(0xFFFF0000), F32)
    return lo, hi


def _store_tile_rows(ref, start, packed, nsub):
    n = packed.shape[0]
    for j in range(nsub):
        ref[pl.ds(start + j, n, stride=nsub), :] = packed[:, j * LANES:(j + 1) * LANES]


def _load_tile_rows(ref, start, n, nsub, lead=()):
    return jnp.concatenate([ref[lead + (pl.ds(start + j, n, stride=nsub), slice(None))] for j in range(nsub)],
                           axis=1)


def _proj_kernel(flags_ref, x_ref, w_ref, b_ref, cos_ref, sin_ref, o_ref, xb_ref, *xcol_ref):
    j = pl.program_id(2)
    d, sub, tn = o_ref.shape
    n_sub = tn // LANES

    @pl.when(j == 0)
    def _():
        if d == 1:
            xb_ref[...] = x_ref[...].astype(BF16)
        else:
            xcol, = xcol_ref
            for c in range(xcol.shape[0]):
                xcol[c] = x_ref[:, c * LANES:(c + 1) * LANES]
            for c in range(xcol.shape[0]):
                for r in range(d):
                    xb_ref[r * sub:(r + 1) * sub, c * LANES:(c + 1) * LANES] = (
                        xcol[c, pl.ds(r, sub, stride=d), :].astype(BF16))

    acc = jnp.dot(xb_ref[...], w_ref[...], preferred_element_type=F32) + b_ref[...]
    for u in range(n_sub):
        a = acc[:, u * LANES:(u + 1) * LANES]
        roped = a * cos_ref[...] + pltpu.roll(a, HEAD_DIM // 2, 1) * sin_ref[...]
        flag = flags_ref[j * n_sub + u]
        mult = jnp.where(flag == ROPE_Q, Q_SCALE, 1.0).astype(F32)
        res = (jnp.where(flag == PLAIN, a, roped) * mult).astype(o_ref.dtype)
        for r in range(d):
            o_ref[r, :, u * LANES:(u + 1) * LANES] = res[r * sub:(r + 1) * sub]


def _project(x, w, b, flags, cosx, sinx, d, tm, tn):
    B, S, D = x.shape
    N = w.shape[1]
    L = S // d
    tm = min(tm, S)
    sub = tm // d

    def regroup(tab):
        return tab.reshape(S // tm, sub, d, HEAD_DIM).transpose(0, 2, 1, 3).reshape(S, HEAD_DIM)

    grid_spec = pltpu.PrefetchScalarGridSpec(
        num_scalar_prefetch=1,
        grid=(B, S // tm, N // tn),
        in_specs=[
            pl.BlockSpec((None, tm, D), lambda bb, i, j, f: (bb, i, 0)),
            pl.BlockSpec((D, tn), lambda bb, i, j, f: (0, j)),
            pl.BlockSpec((1, tn), lambda bb, i, j, f: (0, j)),
            pl.BlockSpec((tm, HEAD_DIM), lambda bb, i, j, f: (i, 0)),
            pl.BlockSpec((tm, HEAD_DIM), lambda bb, i, j, f: (i, 0)),
        ],
        out_specs=pl.BlockSpec((None, d, sub, tn), lambda bb, i, j, f: (bb, 0, i, j)),
        scratch_shapes=[pltpu.VMEM((tm, D), BF16)] + ([pltpu.VMEM((D // LANES, tm, LANES), F32)] if d > 1 else []),
    )
    return pl.pallas_call(
        _proj_kernel,
        out_shape=jax.ShapeDtypeStruct((B, d, L, N), BF16),
        grid_spec=grid_spec,
        compiler_params=_cparams(("parallel", "arbitrary", "arbitrary"), VMEM_LIMIT),
        name="proj_rope",
    )(flags, x, w, b, regroup(cosx), regroup(sinx))


def _gelu_tanh(x):
    return 0.5 * x * (1.0 + jnp.tanh(0.7978845608028654 * (x + 0.044715 * x * x * x)))


def _compress_kernel(x_ref, pos_ref, w1_ref, w2_ref, o_ref, xf_sc):
    s_len = x_ref.shape[0]
    nc = s_len // CMP_STRIDE
    xf_sc[0:s_len, :] = x_ref[...].astype(F32)
    xf_sc[s_len:s_len + CMP_STRIDE, :] = jnp.zeros((CMP_STRIDE, HEAD_DIM), F32)
    h = jnp.zeros((nc, w1_ref.shape[1]), F32)
    for l in range(CMP_LEN):
        tok = xf_sc[pl.ds(l, nc, stride=CMP_STRIDE), :]
        a = (tok + pos_ref[l:l + 1, :]).astype(BF16)
        h = h + jnp.dot(a, w1_ref[l * HEAD_DIM:(l + 1) * HEAD_DIM, :], preferred_element_type=F32)
    g = _gelu_tanh(h).astype(BF16)
    o_ref[...] = jnp.dot(g, w2_ref[...], preferred_element_type=F32).astype(o_ref.dtype)


def _compress(main, k_tile0, v_tile0, pos, w1, w2):
    B, _, S, _ = main.shape
    G = NSA_KV_GROUPS
    nC = S // CMP_STRIDE
    hid = w1.shape[-1]
    return pl.pallas_call(
        _compress_kernel,
        out_shape=jax.ShapeDtypeStruct((2, B, G, nC, HEAD_DIM), BF16),
        grid=(2, B, G),
        in_specs=[
            pl.BlockSpec((None, None, S, HEAD_DIM),
                         lambda a, bb, g: (bb, 0, 0, k_tile0 + a * (v_tile0 - k_tile0) + g)),
            pl.BlockSpec((None, CMP_LEN, HEAD_DIM), lambda a, bb, g: (a, 0, 0)),
            pl.BlockSpec((None, CMP_LEN * HEAD_DIM, hid), lambda a, bb, g: (a, 0, 0)),
            pl.BlockSpec((None, hid, HEAD_DIM), lambda a, bb, g: (a, 0, 0)),
        ],
        out_specs=pl.BlockSpec((None, None, None, nC, HEAD_DIM), lambda a, bb, g: (a, bb, g, 0, 0)),
        scratch_shapes=[pltpu.VMEM((S + CMP_STRIDE, HEAD_DIM), F32)],
        compiler_params=_cparams(("arbitrary", "arbitrary", "arbitrary"), VMEM_LIMIT),
        name="compress_mlp",
    )(main, pos, w1, w2)


def _cmp_attn_kernel(q_ref, kc_ref, vc_ref, ov_ref, o_ref, sel_ref, *, n_slc, n_sel):
    qi = pl.program_id(2)
    tq = q_ref.shape[0]
    nc = kc_ref.shape[0]
    t = qi * tq + lax.broadcasted_iota(jnp.int32, (tq, nc), 0)
    c = lax.broadcasted_iota(jnp.int32, (tq, nc), 1)
    mask = (c * CMP_STRIDE + (CMP_LEN - 1)) <= t
    kc = kc_ref[...]
    vc = vc_ref[...]
    ps = jnp.zeros((tq, nc), F32)
    for h in range(NSA_REP):
        s = _qk(q_ref[:, h * HEAD_DIM:(h + 1) * HEAD_DIM], kc)
        p, _, safe = _masked_softmax(s, mask)
        p = p / safe
        o = jnp.dot(p.astype(BF16), vc, preferred_element_type=F32)
        o_ref[:, h * HEAD_DIM:(h + 1) * HEAD_DIM] = o.astype(o_ref.dtype)
        ps = ps + p
    imp = jnp.dot(ps.astype(BF16), ov_ref[...], preferred_element_type=F32)
    tj = qi * tq + lax.broadcasted_iota(jnp.int32, (tq, LANES), 0)
    j = lax.broadcasted_iota(jnp.int32, (tq, LANES), 1)
    cur = tj // SEL_LEN
    forced = (j == 0) | (j == cur) | (j == cur - 1)
    imp = jnp.where(forced, FORCE, jnp.where(j > cur, NEG, imp))
    imp_t = imp.T[0:n_slc, :]
    grp = 8
    groups = [imp_t[a:a + grp, :] for a in range(0, n_slc, grp)]
    ranks = [jnp.zeros(gv.shape, jnp.int32) for gv in groups]
    for j2 in range(n_slc):
        row = imp_t[j2:j2 + 1, :]
        for gi, gv in enumerate(groups):
            lo = gi * grp
            if lo > j2:
                ahead = row >= gv
            elif lo + gv.shape[0] - 1 <= j2:
                ahead = row > gv
            else:
                later = lax.broadcasted_iota(jnp.int32, gv.shape, 0) + lo > j2
                ahead = (row > gv) | ((row == gv) & later)
            ranks[gi] = jnp.where(ahead, ranks[gi] + 1, ranks[gi])
    rank = jnp.concatenate(ranks, axis=0)
    sel_t = jnp.where(rank < n_sel, 0.0, NEG)
    if n_slc < LANES:
        sel_t = jnp.concatenate([sel_t, jnp.zeros((LANES - n_slc, tq), F32)], axis=0)
    sel_ref[...] = sel_t.T.astype(sel_ref.dtype)


def _cmp_attention(main, kvc, overlap, q_blk0, tq):
    B, _, S, _ = main.shape
    G = NSA_KV_GROUPS
    nC = kvc.shape[3]
    n_slc = S // SEL_LEN
    n_sel = min(SEL_TOPK, n_slc)
    kern = functools.partial(_cmp_attn_kernel, n_slc=n_slc, n_sel=n_sel)
    return pl.pallas_call(
        kern,
        out_shape=(jax.ShapeDtypeStruct((B, S, NSA_HEADS * HEAD_DIM), BF16),
                   jax.ShapeDtypeStruct((B, G, S, LANES), BF16)),
        grid=(B, G, S // tq),
        in_specs=[
            pl.BlockSpec((None, None, tq, GROUP_W), lambda bb, g, i: (bb, 0, i, q_blk0 + g)),
            pl.BlockSpec((None, None, None, nC, HEAD_DIM), lambda bb, g, i: (0, bb, g, 0, 0)),
            pl.BlockSpec((None, None, None, nC, HEAD_DIM), lambda bb, g, i: (1, bb, g, 0, 0)),
            pl.BlockSpec((nC, LANES), lambda bb, g, i: (0, 0)),
        ],
        out_specs=(pl.BlockSpec((None, tq, GROUP_W), lambda bb, g, i: (bb, i, g)),
                   pl.BlockSpec((None, None, tq, LANES), lambda bb, g, i: (bb, g, i, 0))),
        compiler_params=_cparams(("parallel", "arbitrary", "arbitrary"), VMEM_LIMIT),
        name="cmp_attn_select",
    )(main, kvc, kvc, overlap)


def _fold_lanes(x, op):
    out = x[:, 0:LANES]
    for u in range(1, x.shape[1] // LANES):
        out = op(out, x[:, u * LANES:(u + 1) * LANES])
    return out


def _sel_attn_kernel(q_ref, k_ref, v_ref, bias_ref, et_ref, wu_ref, wd_ref,
                     o_ref, wu_out, wd_out, qx_sc, s_sc, m_sc, l_sc, acc_sc):
    wu_out[...] = wu_ref[...].astype(wu_out.dtype)
    wd_out[...] = wd_ref[...].astype(wd_out.dtype)
    qi = pl.program_id(2)
    t = q_ref.shape[0]
    tc = s_sc.shape[2]
    rows = NSA_REP * t
    for h in range(NSA_REP):
        qx_sc[h * t:(h + 1) * t, 0:HEAD_DIM] = q_ref[:, h * HEAD_DIM:(h + 1) * HEAD_DIM]
        qx_sc[h * t:(h + 1) * t, HEAD_DIM:2 * HEAD_DIM] = bias_ref[...]

    def scores(c, w=1):
        start = pl.multiple_of(c * tc, tc)
        kx = jnp.concatenate([k_ref[pl.ds(start, w * tc), :], et_ref[pl.ds(start, w * tc), :]], axis=1)
        return _qk(qx_sc[...], kx)

    m_sc[...] = jnp.full(m_sc.shape, NEG, F32)

    def for_spans(n, fn):
        def trip(j, carry):
            fn(2 * j, 2)
            return carry

        lax.fori_loop(0, n // 2, trip, 0)

        @pl.when(n % 2 == 1)
        def _():
            fn(n - 1, 1)

    def max_span(c, w):
        s = scores(c, w)
        for u in range(w):
            s_sc[c + u] = s[:, u * tc:(u + 1) * tc]
        m_sc[...] = jnp.maximum(m_sc[...], _fold_lanes(s, jnp.maximum))

    n_full = (qi * t) // tc
    for_spans(n_full, max_span)
    qpos = qi * t + lax.broadcasted_iota(jnp.int32, (rows, tc), 0) % t
    kpos = n_full * tc + lax.broadcasted_iota(jnp.int32, (rows, tc), 1)
    s = jnp.where(kpos <= qpos, scores(n_full), NEG)
    s_sc[n_full] = s
    m = jnp.maximum(m_sc[...], _fold_lanes(s, jnp.maximum)).max(-1, keepdims=True)
    m_sc[...] = jnp.broadcast_to(m, m_sc.shape)
    l_sc[...] = jnp.zeros(l_sc.shape, F32)
    acc_sc[...] = jnp.zeros(acc_sc.shape, F32)

    def exp_span(c, w):
        mb = m_sc[...]
        p = jnp.concatenate([jnp.exp2(s_sc[c + u][:, v * LANES:(v + 1) * LANES] - mb)
                             for u in range(w) for v in range(tc // LANES)], axis=1)
        l_sc[...] += _fold_lanes(p, jnp.add)
        vals = v_ref[pl.ds(pl.multiple_of(c * tc, tc), w * tc), :]
        acc_sc[...] += jnp.dot(p.astype(BF16), vals, preferred_element_type=F32)

    for_spans(n_full + 1, exp_span)
    l = l_sc[...].sum(-1, keepdims=True)
    o = acc_sc[...] / jnp.where(l > 0, l, 1.0)
    for h in range(NSA_REP):
        o_ref[:, h * HEAD_DIM:(h + 1) * HEAD_DIM] = o[h * t:(h + 1) * t].astype(o_ref.dtype)


def _sel_attention(main, bias, onehot_t, w_up, w_down, q_blk0, k_tile0, v_tile0, t, tc):
    B, _, S, _ = main.shape
    G = NSA_KV_GROUPS
    t = min(t, S)
    tc = min(tc, S)
    rows = NSA_REP * t
    nq = S // t
    n_steps = B * G * nq
    wu2 = w_up.reshape(-1, w_up.shape[-1])
    wd2 = w_down.reshape(-1, w_down.shape[-1])
    ru, rd = wu2.shape[0] // n_steps, wd2.shape[0] // n_steps
    assert ru * n_steps == wu2.shape[0] and rd * n_steps == wd2.shape[0]

    def slab(bb, g, i):
        return ((bb * G + g) * nq + i, 0)

    o, wu_b, wd_b = pl.pallas_call(
        _sel_attn_kernel,
        out_shape=(jax.ShapeDtypeStruct((B, S, NSA_HEADS * HEAD_DIM), BF16),
                   jax.ShapeDtypeStruct(wu2.shape, BF16), jax.ShapeDtypeStruct(wd2.shape, BF16)),
        grid=(B, G, nq),
        in_specs=[
            pl.BlockSpec((None, None, t, GROUP_W), lambda bb, g, i: (bb, 0, i, q_blk0 + g)),
            pl.BlockSpec((None, None, S, HEAD_DIM), lambda bb, g, i: (bb, 0, 0, k_tile0 + g)),
            pl.BlockSpec((None, None, S, HEAD_DIM), lambda bb, g, i: (bb, 0, 0, v_tile0 + g)),
            pl.BlockSpec((None, None, t, LANES), lambda bb, g, i: (bb, g, i, 0)),
            pl.BlockSpec((S, LANES), lambda bb, g, i: (0, 0)),
            pl.BlockSpec((ru, wu2.shape[1]), slab),
            pl.BlockSpec((rd, wd2.shape[1]), slab),
        ],
        out_specs=(pl.BlockSpec((None, t, GROUP_W), lambda bb, g, i: (bb, i, g)),
                   pl.BlockSpec((ru, wu2.shape[1]), slab), pl.BlockSpec((rd, wd2.shape[1]), slab)),
        scratch_shapes=[pltpu.VMEM((rows, 2 * HEAD_DIM), BF16), pltpu.VMEM((S // tc, rows, tc), F32),
                        pltpu.VMEM((rows, LANES), F32), pltpu.VMEM((rows, LANES), F32),
                        pltpu.VMEM((rows, HEAD_DIM), F32)],
        compiler_params=_cparams(("parallel", "arbitrary", "arbitrary"), VMEM_LIMIT),
        name="selected_attn",
    )(main, main, main, bias, onehot_t, wu2, wd2)
    return o, wu_b.reshape(w_up.shape), wd_b.reshape(w_down.shape)


def _band_kernel(*refs, n_prev, n_sub, max_dist, kv_heads, with_lse):
    q_ref = refs[0]
    k_refs = refs[1:2 + n_prev]
    v_refs = refs[2 + n_prev:3 + 2 * n_prev]
    o_ref = refs[3 + 2 * n_prev]
    qi = pl.program_id(2)
    t = k_refs[0].shape[0]
    nk = (n_prev + 1) * t
    n_heads = q_ref.shape[1] // HEAD_DIM
    k_all = jnp.concatenate([r[...] for r in k_refs], axis=0)
    v_all = jnp.concatenate([r[...] for r in v_refs], axis=0)
    kcol = lax.broadcasted_iota(jnp.int32, (t, nk), 1)
    diff = n_prev * t + lax.broadcasted_iota(jnp.int32, (t, nk), 0) - kcol
    in_band = (diff >= 0) & (diff <= max_dist)
    lane = lax.broadcasted_iota(jnp.int32, (t, LANES), 1)
    for u in range(n_sub):
        first_key = (qi * n_sub + u - n_prev) * t
        bias = jnp.where(in_band & (kcol + first_key >= 0), 0.0, NEG)
        rows = slice(u * t, (u + 1) * t)
        k_u = k_all[u * t:u * t + nk]
        v_u = v_all[u * t:u * t + nk]
        lse = jnp.zeros((t, LANES), F32)
        if kv_heads == 1:
            q = jnp.concatenate([q_ref[rows, h * HEAD_DIM:(h + 1) * HEAD_DIM] for h in range(n_heads)], axis=0)
            s = _qk(q, k_u).reshape(n_heads, t, nk) + bias[None]
            m = s.max(-1, keepdims=True)
            p = jnp.exp2(s - m)
            l = p.sum(-1, keepdims=True)
            o = jnp.dot(p.astype(BF16).reshape(n_heads * t, nk), v_u, preferred_element_type=F32)
            o = o.reshape(n_heads, t, HEAD_DIM) / l
            for h in range(n_heads):
                o_ref[rows, h * HEAD_DIM:(h + 1) * HEAD_DIM] = o[h].astype(o_ref.dtype)
                lse = jnp.where(lane == h, m[h] + jnp.log2(l[h]), lse)
        else:
            for h in range(n_heads):
                cols = slice(h * HEAD_DIM, (h + 1) * HEAD_DIM)
                s = _qk(q_ref[rows, cols], k_u[:, cols]) + bias
                m = s.max(-1, keepdims=True)
                p = jnp.exp2(s - m)
                l = p.sum(-1, keepdims=True)
                o = jnp.dot(p.astype(BF16), v_u[:, cols], preferred_element_type=F32)
                o_ref[rows, cols] = (o / l).astype(o_ref.dtype)
                lse = jnp.where(lane == h, m + jnp.log2(l), lse)
        if with_lse:
            refs[4 + 2 * n_prev][rows, :] = lse


def _band_attention(src, lead_grid, length, q_map, k_map, v_map, kv_width, out_shape, o_map,
                    t, n_sub, max_dist, with_lse):
    t = min(t, length)
    n_prev = -(-max_dist // t)
    n_sub = min(n_sub, length // t)
    tile = n_sub * t
    kern = functools.partial(_band_kernel, n_prev=n_prev, n_sub=n_sub, max_dist=max_dist,
                             kv_heads=kv_width // HEAD_DIM, with_lse=with_lse)

    def preceding(fn, j):
        def index_map(bb, a, i):
            return fn(bb, a, jnp.maximum(i * n_sub - n_prev + j, 0))
        return index_map

    def kv_specs(fn):
        return ([pl.BlockSpec((None, None, t, kv_width), preceding(fn, j)) for j in range(n_prev)]
                + [pl.BlockSpec((None, None, tile, kv_width), fn)])

    in_specs = [pl.BlockSpec((None, None, tile, GROUP_W), q_map)] + kv_specs(k_map) + kv_specs(v_map)
    o_spec = pl.BlockSpec((None, tile, GROUP_W), o_map)
    if with_lse:
        lse_shape = out_shape[:-1] + (out_shape[-1] // GROUP_W * LANES,)
        out_shapes = (jax.ShapeDtypeStruct(out_shape, BF16), jax.ShapeDtypeStruct(lse_shape, F32))
        out_specs = (o_spec, pl.BlockSpec((None, tile, LANES), o_map))
    else:
        out_shapes = jax.ShapeDtypeStruct(out_shape, BF16)
        out_specs = o_spec
    return pl.pallas_call(
        kern, out_shape=out_shapes, grid=lead_grid + (length // tile,), in_specs=in_specs, out_specs=out_specs,
        compiler_params=_cparams(("parallel", "arbitrary", "arbitrary"), VMEM_LIMIT),
        name="band_attn",
    )(*([src] * (3 + 2 * n_prev)))


def _layer_norm(z, g, b):
    mu = z.mean(-1, keepdims=True)
    zc = z - mu
    var = (zc * zc).mean(-1, keepdims=True)
    return zc * lax.rsqrt(var + LN_EPS) * g + b


def _merge_kernel(ocmp_ref, oslc_ref, owin_ref, gl_ref, d0_ref, d1_ref, d2_ref, l0_ref, l1_ref, l2_ref,
                  ga_ref, gb_ref, x_ref, wa_ref, wb_ref, wo_ref, g_ref, b_ref, wr_ref, br_ref,
                  hf_ref, hp_ref, lg_ref, nsa_a, nsa_b, dil_a, dil_b, *tok_sc, alpha):
    s = pl.program_id(0)
    tt = x_ref.shape[0]

    def token_order(ref, scr):
        n_col = ref.shape[-1] // LANES
        if len(ref.shape) == 2:
            return [ref[:, c * LANES:(c + 1) * LANES].astype(F32) for c in range(n_col)]
        d, rows = ref.shape[0], ref.shape[1]
        for r in range(d):
            for c in range(n_col):
                scr[c, pl.ds(r, rows, stride=d), :] = ref[r, :, c * LANES:(c + 1) * LANES].astype(F32)
        return [scr[c] for c in range(n_col)]

    @pl.when(s == 0)
    def _():
        nsa_b[...] = jnp.zeros(nsa_b.shape, nsa_b.dtype)
        dil_b[...] = jnp.zeros(dil_b.shape, dil_b.dtype)

    def gate_stage(nsa_out, dil_out):
        gates = jax.nn.sigmoid(gl_ref[...].astype(F32))
        for h in range(NSA_HEADS):
            sl = slice(h * HEAD_DIM, (h + 1) * HEAD_DIM)
            acc = jnp.zeros((tt, HEAD_DIM), F32)
            for br, ref in enumerate((ocmp_ref, oslc_ref, owin_ref)):
                gcol = gates[:, 3 * h + br:3 * h + br + 1]
                acc = acc + gcol * ref[:, sl].astype(F32)
            nsa_out[:, sl] = acc.astype(BF16)
        scr = tok_sc
        outs = [token_order(ref, scr[2 * g]) for g, ref in enumerate((d0_ref, d1_ref, d2_ref))]
        l0, l1, l2 = [token_order(ref, scr[2 * g + 1])[0] for g, ref in enumerate((l0_ref, l1_ref, l2_ref))]
        lm = jnp.maximum(jnp.maximum(l0, l1), l2)
        e0, e1, e2 = jnp.exp2(l0 - lm), jnp.exp2(l1 - lm), jnp.exp2(l2 - lm)
        inv = 1.0 / (e0 + e1 + e2)
        w0, w1, w2 = e0 * inv, e1 * inv, e2 * inv
        for h in range(DIL_HEADS):
            o = w0[:, h:h + 1] * outs[0][h] + w1[:, h:h + 1] * outs[1][h] + w2[:, h:h + 1] * outs[2][h]
            dil_out[:, h * HEAD_DIM:(h + 1) * HEAD_DIM] = o.astype(BF16)

    def matmul_stage(nsa_in, dil_in):
        y_a = jnp.dot(nsa_in[...], wa_ref[...], preferred_element_type=F32)
        y_b = jnp.dot(dil_in[...], wb_ref[...], preferred_element_type=F32)
        merged = (jax.nn.sigmoid(ga_ref[...].astype(F32)) * y_a
                  + jax.nn.sigmoid(gb_ref[...].astype(F32)) * y_b)
        mix = jnp.dot(merged.astype(BF16), wo_ref[...], preferred_element_type=F32)
        h = _layer_norm(alpha * x_ref[...] + mix, g_ref[...], b_ref[...])
        hf_ref[...] = h
        hb = h.astype(BF16)
        _store_tile_rows(hp_ref, 0, _pack_pairs(hb), hp_ref.shape[0] // tt)
        lg_ref[...] = jnp.dot(hb, wr_ref[...], preferred_element_type=F32) + br_ref[...]

    @pl.when(s % 2 == 0)
    def _():
        gate_stage(nsa_a, dil_a)
        matmul_stage(nsa_b, dil_b)

    @pl.when(s % 2 == 1)
    def _():
        gate_stage(nsa_b, dil_b)
        matmul_stage(nsa_a, dil_a)


def _merge(o_cmp, o_slc, o_win, main2d, gl_tile, ga_blk, gb_blk, dil_o, dil_lse, x2d,
           w_a, w_b, w_o, ln_g, ln_b, w_r, b_r, alpha, tt):
    T, D = x2d.shape
    nsub = D // 2 // LANES
    nsa_w = NSA_HEADS * HEAD_DIM
    nt = T // tt
    ahead = lambda s: (jnp.minimum(s, nt - 1), 0)
    behind = lambda s: (jnp.maximum(s - 1, 0), 0)
    once = pl.Buffered(1)

    def const(shape):
        return pl.BlockSpec(shape, lambda s: (0, 0), pipeline_mode=once)

    in_specs = [
        pl.BlockSpec((tt, nsa_w), ahead), pl.BlockSpec((tt, nsa_w), ahead), pl.BlockSpec((tt, nsa_w), ahead),
        pl.BlockSpec((tt, LANES), lambda s: (jnp.minimum(s, nt - 1), gl_tile)),
    ]
    def sub_order(arr):
        if arr.ndim == 2:
            return pl.BlockSpec((tt, arr.shape[1]), ahead)
        _, d, length, cols = arr.shape
        per_batch = length * d // tt

        def index_map(s):
            tile = jnp.minimum(s, nt - 1)
            return (tile // per_batch, 0, tile % per_batch, 0)

        return pl.BlockSpec((None, d, tt // d, cols), index_map)

    in_specs += [sub_order(a) for a in dil_o] + [sub_order(a) for a in dil_lse]
    in_specs += [
        pl.BlockSpec((tt, D), lambda s: (jnp.maximum(s - 1, 0), ga_blk)),
        pl.BlockSpec((tt, D), lambda s: (jnp.maximum(s - 1, 0), gb_blk)),
        pl.BlockSpec((tt, D), behind),
        const((nsa_w, D)), const((DIL_W, D)), const((D, D)), const((1, D)), const((1, D)),
        const((D, LANES)), const((1, LANES)),
    ]
    return pl.pallas_call(
        functools.partial(_merge_kernel, alpha=alpha),
        out_shape=(jax.ShapeDtypeStruct((T, D), F32), jax.ShapeDtypeStruct((T * nsub, LANES), jnp.uint32),
                   jax.ShapeDtypeStruct((T, LANES), F32)),
        grid=(nt + 1,),
        in_specs=in_specs,
        out_specs=(pl.BlockSpec((tt, D), behind), pl.BlockSpec((tt * nsub, LANES), behind),
                   pl.BlockSpec((tt, LANES), behind)),
        scratch_shapes=[pltpu.VMEM((tt, nsa_w), BF16), pltpu.VMEM((tt, nsa_w), BF16),
                        pltpu.VMEM((tt, DIL_W), BF16), pltpu.VMEM((tt, DIL_W), BF16)]
        + [pltpu.VMEM((a.shape[-1] // LANES, tt, LANES), F32) for pair in zip(dil_o, dil_lse) for a in pair],
        compiler_params=_cparams(("arbitrary",), VMEM_LIMIT),
        name="merge_ln1",
    )(o_cmp, o_slc, o_win, main2d, *dil_o, *dil_lse, main2d, main2d, x2d, w_a, w_b, w_o, ln_g, ln_b, w_r, b_r)


def _router_kernel(lg_ref, tri_ref, meta_ref, cnt_ref, carry_sc):
    i = pl.program_id(0)
    tt = lg_ref.shape[0]

    @pl.when(i == 0)
    def _():
        carry_sc[...] = jnp.zeros(carry_sc.shape, F32)

    logits = lg_ref[...]
    lane = lax.broadcasted_iota(jnp.int32, (tt, LANES), 1)
    v = logits
    onehot = jnp.zeros((tt, LANES), F32)
    vals, idxs = [], []
    for _ in range(TOP_K):
        m = v.max(-1, keepdims=True)
        idx = jnp.where(v == m, lane, LANES).min(-1, keepdims=True)
        hit = lane == idx
        vals.append(m)
        idxs.append(idx)
        onehot = onehot + hit.astype(F32)
        v = jnp.where(hit, -jnp.inf, v)
    exps = [jnp.exp(vk - vals[0]) for vk in vals]
    den = exps[0] + exps[1] + exps[2] + exps[3]
    before = jnp.dot(tri_ref[...], onehot.astype(BF16), preferred_element_type=F32) + carry_sc[0:1, :]
    meta = jnp.zeros((tt, LANES), F32)
    for k in range(TOP_K):
        rank = jnp.where(lane == idxs[k], before, 0.0).sum(-1, keepdims=True)
        meta = jnp.where(lane == k, idxs[k].astype(F32), meta)
        meta = jnp.where(lane == TOP_K + k, exps[k] / den, meta)
        meta = jnp.where(lane == 2 * TOP_K + k, rank, meta)
    meta_ref[...] = meta
    carry_sc[...] = carry_sc[...] + jnp.broadcast_to(onehot.sum(0, keepdims=True), carry_sc.shape)
    cnt_ref[...] = carry_sc[...]


def _router(logits, tt):
    T = logits.shape[0]
    tri = (jnp.arange(tt)[:, None] > jnp.arange(tt)[None, :]).astype(BF16)
    return pl.pallas_call(
        _router_kernel,
        out_shape=(jax.ShapeDtypeStruct((T, LANES), F32), jax.ShapeDtypeStruct((8, LANES), F32)),
        grid=(T // tt,),
        in_specs=[
            pl.BlockSpec((tt, LANES), lambda i: (i, 0)),
            pl.BlockSpec((tt, tt), lambda i: (0, 0)),
        ],
        out_specs=(pl.BlockSpec((tt, LANES), lambda i: (i, 0)), pl.BlockSpec((8, LANES), lambda i: (0, 0))),
        scratch_shapes=[pltpu.VMEM((8, LANES), F32)],
        compiler_params=_cparams(("arbitrary",), VMEM_LIMIT),
        name="router_top4",
    )(logits, tri)


def _dispatch_kernel(pe_ref, dest_ref, h_ref, xs_hbm, zero_sc, sem, *, nsub, tm):
    n = h_ref.shape[0] // nsub
    blk = tm * nsub

    @pl.when(pl.program_id(0) == 0)
    def _():
        zero_sc[...] = jnp.zeros(zero_sc.shape, zero_sc.dtype)

        def clear(e):
            end = pe_ref[e]
            begin = pe_ref[e - 1] if e else 0
            start = pl.multiple_of((end - tm) * nsub, nsub)
            return end > begin, pltpu.make_async_copy(zero_sc, xs_hbm.at[pl.ds(start, blk), :], sem)

        for e in range(pe_ref.shape[0]):
            nonempty, copy = clear(e)
            pl.when(nonempty)(copy.start)
        for e in range(pe_ref.shape[0]):
            nonempty, copy = clear(e)
            pl.when(nonempty)(copy.wait)

    for t in range(n):
        for k in range(TOP_K):
            d = pl.multiple_of(dest_ref[t * TOP_K + k] * nsub, nsub)
            pltpu.make_async_copy(h_ref.at[pl.ds(t * nsub, nsub), :], xs_hbm.at[pl.ds(d, nsub), :],
                                  sem).start(priority=k % 2)
    for k in range(TOP_K):
        pltpu.make_async_copy(h_ref, xs_hbm.at[pl.ds(0, n * nsub), :], sem).wait()


def _dispatch(hp, dest_flat, pad_end, n_rows, nsub, tm, tt):
    T = hp.shape[0] // nsub
    grid_spec = pltpu.PrefetchScalarGridSpec(
        num_scalar_prefetch=1,
        grid=(T // tt,),
        in_specs=[
            pl.BlockSpec((tt * TOP_K,), lambda i, pe: (i,), memory_space=pltpu.SMEM),
            pl.BlockSpec((tt * nsub, LANES), lambda i, pe: (i, 0)),
        ],
        out_specs=pl.BlockSpec(memory_space=pl.ANY),
        scratch_shapes=[pltpu.VMEM((tm * nsub, LANES), hp.dtype), pltpu.SemaphoreType.DMA(())],
    )
    return pl.pallas_call(
        functools.partial(_dispatch_kernel, nsub=nsub, tm=tm),
        out_shape=jax.ShapeDtypeStruct((n_rows * nsub, LANES), hp.dtype),
        grid_spec=grid_spec,
        compiler_params=_cparams(("arbitrary",), VMEM_LIMIT),
        name="moe_dispatch",
    )(pad_end, dest_flat, hp)


def _expert_kernel(be_ref, nu_ref, x_ref, wg_ref, wl_ref, bg_ref, bl_ref, wd_ref, bd_ref, y_ref, xb_sc, *, th, nsub):
    i = pl.program_id(0)
    tm = xb_sc.shape[0]
    half = nsub * LANES
    dh = wg_ref.shape[1]

    @pl.when(i < nu_ref[0])
    def _():
        lo, hi = _unpack_pairs(_load_tile_rows(x_ref, 0, tm, nsub))
        xb_sc[:, :half] = lo.astype(BF16)
        xb_sc[:, half:] = hi.astype(BF16)
        x = xb_sc[...]
        y = bd_ref[...]
        for c in range(dh // th):
            sl = slice(c * th, (c + 1) * th)
            glu = jnp.dot(x, wg_ref[:, sl], preferred_element_type=F32) + bg_ref[:, sl]
            lin = jnp.dot(x, wl_ref[:, sl], preferred_element_type=F32) + bl_ref[:, sl]
            glu = jnp.minimum(glu, SWIGLU_LIMIT)
            lin = jnp.clip(lin, -SWIGLU_LIMIT, SWIGLU_LIMIT)
            act = glu * jax.nn.sigmoid(SWIGLU_ALPHA * glu) * (lin + 1.0)
            y = y + jnp.dot(act.astype(BF16), wd_ref[sl, :], preferred_element_type=F32)
        _store_tile_rows(y_ref, 0, _pack_pairs(y.astype(BF16)), nsub)


def _experts(xs, blk_expert, n_used, w_up, b_up, w_down, b_down, tm, th):
    E, D, two_dh = w_up.shape
    nsub = D // 2 // LANES
    n_blk = xs.shape[0] // (tm * nsub)
    dh = two_dh // 2
    th = min(th, dh)
    once = pl.Buffered(1)

    def row(i, be, nu):
        return (jnp.minimum(i, nu[0] - 1), 0)

    grid_spec = pltpu.PrefetchScalarGridSpec(
        num_scalar_prefetch=2,
        grid=(n_blk,),
        in_specs=[
            pl.BlockSpec((tm * nsub, LANES), row),
            pl.BlockSpec((None, D, dh), lambda i, be, nu: (be[i], 0, 0), pipeline_mode=once),
            pl.BlockSpec((None, D, dh), lambda i, be, nu: (be[i], 0, 1), pipeline_mode=once),
            pl.BlockSpec((None, 1, dh), lambda i, be, nu: (be[i], 0, 0)),
            pl.BlockSpec((None, 1, dh), lambda i, be, nu: (be[i], 0, 1)),
            pl.BlockSpec((None, dh, D), lambda i, be, nu: (be[i], 0, 0), pipeline_mode=once),
            pl.BlockSpec((None, 1, D), lambda i, be, nu: (be[i], 0, 0)),
        ],
        out_specs=pl.BlockSpec((tm * nsub, LANES), row),
        scratch_shapes=[pltpu.VMEM((tm, D), BF16)],
    )
    return pl.pallas_call(
        functools.partial(_expert_kernel, th=th, nsub=nsub),
        out_shape=jax.ShapeDtypeStruct(xs.shape, jnp.uint32),
        grid_spec=grid_spec,
        compiler_params=_cparams(("arbitrary",), VMEM_LIMIT),
        name="moe_experts",
    )(blk_expert, n_used, xs, w_up, w_up, b_up, b_up, w_down, b_down)


def _combine_kernel(dest_ref, dest_next_ref, meta_ref, h_ref, g_ref, b_ref, y_hbm, o_ref, buf, sem, *, alpha, nsub):
    i = pl.program_id(0)
    n = o_ref.shape[0]
    slot = i % 2

    def gather(idx_ref, s):
        for t in range(n):
            for k in range(TOP_K):
                d = pl.multiple_of(idx_ref[t * TOP_K + k] * nsub, nsub)
                pltpu.make_async_copy(y_hbm.at[pl.ds(d, nsub), :], buf.at[s, pl.ds((k * n + t) * nsub, nsub), :],
                                      sem.at[s]).start(priority=k % 2)

    @pl.when(i == 0)
    def _():
        gather(dest_ref, 0)

    @pl.when(i + 1 < pl.num_programs(0))
    def _():
        gather(dest_next_ref, 1 - slot)

    for k in range(TOP_K):
        pltpu.make_async_copy(y_hbm.at[pl.ds(0, n * nsub), :], buf.at[slot, pl.ds(k * n * nsub, n * nsub), :],
                              sem.at[slot]).wait()
    ffn_lo = ffn_hi = None
    for k in range(TOP_K):
        lo, hi = _unpack_pairs(_load_tile_rows(buf, k * n * nsub, n, nsub, lead=(slot,)))
        gate = meta_ref[:, TOP_K + k:TOP_K + k + 1]
        ffn_lo = gate * lo if k == 0 else ffn_lo + gate * lo
        ffn_hi = gate * hi if k == 0 else ffn_hi + gate * hi
    ffn = jnp.concatenate([ffn_lo, ffn_hi], axis=1)
    o_ref[...] = _layer_norm(alpha * h_ref[...] + ffn, g_ref[...], b_ref[...])


def _combine_ln(y, dest_flat, meta, h, g, b, alpha, tt):
    T, D = h.shape
    nsub = D // 2 // LANES
    nt = T // tt
    return pl.pallas_call(
        functools.partial(_combine_kernel, alpha=alpha, nsub=nsub),
        out_shape=jax.ShapeDtypeStruct((T, D), F32),
        grid=(nt,),
        in_specs=[
            pl.BlockSpec((tt * TOP_K,), lambda i: (i,), memory_space=pltpu.SMEM),
            pl.BlockSpec((tt * TOP_K,), lambda i: (jnp.minimum(i + 1, nt - 1),), memory_space=pltpu.SMEM),
            pl.BlockSpec((tt, LANES), lambda i: (i, 0)),
            pl.BlockSpec((tt, D), lambda i: (i, 0)),
            pl.BlockSpec((1, D), lambda i: (0, 0)),
            pl.BlockSpec((1, D), lambda i: (0, 0)),
            pl.BlockSpec(memory_space=pl.ANY),
        ],
        out_specs=pl.BlockSpec((tt, D), lambda i: (i, 0)),
        scratch_shapes=[pltpu.VMEM((2, TOP_K * tt * nsub, LANES), y.dtype), pltpu.SemaphoreType.DMA((2,))],
        compiler_params=_cparams(("arbitrary",), VMEM_LIMIT),
        name="moe_combine_ln2",
    )(dest_flat, dest_flat, meta, h, g, b, y)


def _layer(x, w_in, b_in, pos_k, pos_v, ck_w1, ck_w2, cv_w1, cv_w2, w_br_nsa, w_br_dil, w_out,
           ln1_g, ln1_b, w_router, b_router, w_up, b_up, w_down, b_down, ln2_g, ln2_b, alpha):
    B, S, D = x.shape
    T = B * S
    nd = D // LANES
    G = NSA_KV_GROUPS
    kvw = G * HEAD_DIM
    n_exp = w_router.shape[1]

    o_q = 0
    o_kv = NSA_HEADS * HEAD_DIM
    o_gl = o_kv + 6 * kvw
    o_dil = o_gl + 3 * NSA_HEADS
    o_ga = o_dil + 3 * N_DIL * DIL_W
    o_gb = o_ga + D

    def wcols(a, n):
        return w_in[:, a:a + n], b_in[a:a + n]

    def kv(i):
        return wcols(o_kv + i * kvw, kvw)

    tn = 1024
    gl_w, gl_b = wcols(o_gl, 3 * NSA_HEADS)
    pieces = [wcols(o_ga, D), wcols(o_gb, D), wcols(o_q, NSA_HEADS * HEAD_DIM),
              kv(0), kv(2), kv(4), kv(1), kv(3), kv(5), (gl_w, gl_b)]
    used = sum(p[0].shape[1] for p in pieces)
    n_main = -(-used // tn) * tn
    pieces.append((jnp.zeros((D, n_main - used), F32), jnp.zeros((n_main - used,), F32)))
    w_main = jnp.concatenate([p[0] for p in pieces], axis=1).astype(BF16)
    b_main = jnp.concatenate([p[1] for p in pieces])[None, :]
    t_q = 2 * nd
    t_kc, t_ks, t_kw = t_q + 8, t_q + 10, t_q + 12
    t_vc, t_vs, t_vw = t_q + 14, t_q + 16, t_q + 18
    t_gl = t_q + 20
    tile_id = jnp.arange(n_main // LANES)
    flags_main = jnp.where((tile_id >= t_q) & (tile_id < t_kc), ROPE_Q,
                           jnp.where((tile_id >= t_kc) & (tile_id < t_vc), ROPE, PLAIN)).astype(jnp.int32)
    q_blk0 = t_q // NSA_REP

    pos = jnp.arange(S, dtype=F32)
    inv = ROPE_THETA ** (-jnp.arange(0, HEAD_DIM, 2, dtype=F32) / HEAD_DIM)
    ang = pos[:, None] * inv[None, :]
    cosx = jnp.concatenate([jnp.cos(ang), jnp.cos(ang)], axis=-1)
    sinx = jnp.concatenate([-jnp.sin(ang), jnp.sin(ang)], axis=-1)

    main = _project(x, w_main, b_main, flags_main, cosx, sinx, 1, 1024, tn)

    nC = S // CMP_STRIDE
    w1 = jnp.stack([ck_w1, cv_w1]).astype(BF16)
    w2 = jnp.stack([ck_w2, cv_w2]).astype(BF16)
    kvc = _compress(main, t_kc, t_vc, jnp.stack([pos_k, pos_v]), w1, w2)

    n_slc = S // SEL_LEN
    assert n_slc <= LANES
    c_start = jnp.arange(nC) * CMP_STRIDE
    jb = jnp.arange(LANES)
    overlap = ((c_start[:, None] < (jb[None, :] + 1) * SEL_LEN) & (c_start[:, None] + CMP_LEN > jb[None, :] * SEL_LEN)
               & (jb[None, :] < n_slc) & (c_start[:, None] + CMP_LEN <= S)).astype(BF16)
    tq = min(512, S)
    o_cmp, sel = _cmp_attention(main, kvc, overlap, q_blk0, tq)

    onehot_t = (jnp.arange(S)[:, None] // SEL_LEN == jnp.arange(LANES)[None, :]).astype(BF16)
    o_slc, w_up_b, w_down_b = _sel_attention(main, sel, onehot_t, w_up, w_down, q_blk0, t_ks, t_vs, 256, 512)

    o_win = _band_attention(
        main, (B, G), S,
        lambda bb, g, i: (bb, 0, i, q_blk0 + g),
        lambda bb, g, i: (bb, 0, i, t_kw + g),
        lambda bb, g, i: (bb, 0, i, t_vw + g),
        HEAD_DIM, (B, S, NSA_HEADS * HEAD_DIM), lambda bb, g, i: (bb, i, g), 128, 4, WIN_LEN - 1, False)

    flags_dil = jnp.array([ROPE_Q] * DIL_HEADS + [ROPE] * DIL_HEADS + [PLAIN] * DIL_HEADS, jnp.int32)
    dil_o, dil_lse = [], []
    for gi, (w, d) in enumerate(DIL_CONFIGS):
        wd, bd = wcols(o_dil + gi * 3 * DIL_W, 3 * DIL_W)
        sub = _project(x, wd.astype(BF16), bd[None, :], flags_dil, cosx, sinx, d, 512, 3 * DIL_W)
        L = S // d
        o_g, lse_g = _band_attention(
            sub, (B, d), L,
            lambda bb, r, i: (bb, r, i, 0),
            lambda bb, r, i: (bb, r, i, 1),
            lambda bb, r, i: (bb, r, i, 2),
            DIL_W, (B * d, L, DIL_W), lambda bb, r, i, d=d: (bb * d + r, i, 0), 128, 4, w // d, True)
        dil_o.append(o_g.reshape(T, DIL_W) if d == 1 else o_g.reshape(B, d, L, DIL_W))
        dil_lse.append(lse_g.reshape(T, LANES) if d == 1 else lse_g.reshape(B, d, L, LANES))

    w_r = jnp.concatenate([w_router, jnp.zeros((D, LANES - n_exp), F32)], axis=1).astype(BF16)
    b_r = jnp.concatenate([b_router, jnp.full((LANES - n_exp,), NEG, F32)])[None, :]
    h_f, h_p, logits = _merge(
        o_cmp.reshape(T, -1), o_slc.reshape(T, -1), o_win.reshape(T, -1), main.reshape(T, n_main),
        t_gl, 0, 1, dil_o, dil_lse, x.reshape(T, D),
        w_br_nsa.astype(BF16), w_br_dil.astype(BF16), w_out.astype(BF16),
        ln1_g[None, :], ln1_b[None, :], w_r, b_r, alpha, min(256, T))

    meta, cnt = _router(logits, min(512, T))
    top_idx = meta[:, 0:TOP_K].astype(jnp.int32)
    rank = meta[:, 2 * TOP_K:3 * TOP_K].astype(jnp.int32)

    tm = 512 if T * TOP_K >= 512 * n_exp else 128
    counts = cnt[0, :n_exp].astype(jnp.int32)
    padded = (counts + tm - 1) // tm * tm
    pad_end = jnp.cumsum(padded)
    pad_start = pad_end - padded
    start_of = jnp.where(top_idx[..., None] == jnp.arange(n_exp), pad_start, 0).sum(-1)
    dest = (start_of + rank).reshape(T * TOP_K)
    n_rows = T * TOP_K + n_exp * tm
    n_blk = n_rows // tm
    blk_start = jnp.arange(n_blk, dtype=jnp.int32) * tm
    blk_expert = jnp.minimum((pad_end[None, :] <= blk_start[:, None]).sum(-1), n_exp - 1).astype(jnp.int32)
    n_used = (pad_end[-1:] // tm).astype(jnp.int32)

    xs = _dispatch(h_p, dest, pad_end.astype(jnp.int32), n_rows, D // 2 // LANES, tm, min(256, T))
    y = _experts(xs, blk_expert, n_used, w_up_b, b_up[:, None, :], w_down_b, b_down[:, None, :], tm, 512)
    out = _combine_ln(y, dest, meta, h_f, ln2_g[None, :], ln2_b[None, :], alpha, min(256, T))
    return out.reshape(B, S, D)


def kernel(x, w_in, b_in, cmp_pos_k, cmp_pos_v, cmp_k_w1, cmp_k_w2, cmp_v_w1, cmp_v_w2, w_br_nsa, w_br_dil,
           w_out, ln1_g, ln1_b, w_router, b_router, w_up, b_up, w_down, b_down, ln2_g, ln2_b):
    depth = w_in.shape[0]
    alpha = (2.0 * depth) ** 0.25
    h = x
    for l in range(depth):
        h = _layer(h, w_in[l], b_in[l], cmp_pos_k[l], cmp_pos_v[l], cmp_k_w1[l], cmp_k_w2[l],
                   cmp_v_w1[l], cmp_v_w2[l], w_br_nsa[l], w_br_dil[l], w_out[l], ln1_g[l], ln1_b[l],
                   w_router[l], b_router[l], w_up[l], b_up[l], w_down[l], b_down[l], ln2_g[l], ln2_b[l], alpha)
    return h
```

```python
import functools

import jax
import jax.numpy as jnp
from jax import lax
from jax.experimental import pallas as pl
from jax.experimental.pallas import tpu as pltpu

F32 = jnp.float32
BF16 = jnp.bfloat16

HEAD_DIM = 128
LANES = 128
ROPE_THETA = 10000.0
NSA_HEADS = 8
NSA_KV_GROUPS = 2
NSA_REP = NSA_HEADS // NSA_KV_GROUPS
CMP_LEN = 32
CMP_STRIDE = 16
SEL_LEN = 64
SEL_TOPK = 16
WIN_LEN = 512
DIL_CONFIGS = ((128, 1), (512, 4), (2048, 16))
N_DIL = len(DIL_CONFIGS)
DIL_HEADS = 4
TOP_K = 4
SWIGLU_LIMIT = 7.0
SWIGLU_ALPHA = 1.702
LN_EPS = 1e-5
NEG = -1e30
FORCE = 1e9
SCALE = HEAD_DIM ** -0.5
LOG2E = 1.4426950408889634
Q_SCALE = SCALE * LOG2E
PLAIN, ROPE, ROPE_Q = 0, 1, 2

GROUP_W = NSA_REP * HEAD_DIM
DIL_W = DIL_HEADS * HEAD_DIM
VMEM_LIMIT = 56 * 1024 * 1024


def _cparams(sem, vmem=None):
    return pltpu.CompilerParams(dimension_semantics=sem, vmem_limit_bytes=vmem)


def _masked_softmax(s, mask):
    s = jnp.where(mask, s, NEG)
    m = s.max(-1, keepdims=True)
    p = jnp.where(mask, jnp.exp2(s - m), 0.0)
    den = p.sum(-1, keepdims=True)
    safe = jnp.where(den > 0, den, 1.0)
    return p, m, safe


def _qk(q, k):
    return lax.dot_general(q, k, (((1,), (1,)), ((), ())), preferred_element_type=F32)


def _pack_pairs(xb):
    m = xb.shape[1] // 2
    lo = lax.bitcast_convert_type(xb[:, :m].astype(F32), jnp.uint32)
    hi = lax.bitcast_convert_type(xb[:, m:].astype(F32), jnp.uint32)
    return (lo >> 16) | (hi & jnp.uint32(0xFFFF0000))


def _unpack_pairs(w):
    lo = lax.bitcast_convert_type(w << 16, F32)
    hi = lax.bitcast_convert_type(w & jnp.uint32(0xFFFF0000), F32)
    return lo, hi


def _store_tile_rows(ref, start, packed, nsub):
    n = packed.shape[0]
    for j in range(nsub):
        ref[pl.ds(start + j, n, stride=nsub), :] = packed[:, j * LANES:(j + 1) * LANES]


def _load_tile_rows(ref, start, n, nsub, lead=()):
    return jnp.concatenate([ref[lead + (pl.ds(start + j, n, stride=nsub), slice(None))] for j in range(nsub)],
                           axis=1)


def _proj_kernel(flags_ref, x_ref, w_ref, b_ref, cos_ref, sin_ref, o_ref, xb_ref, *xcol_ref):
    j = pl.program_id(2)
    d, sub, tn = o_ref.shape
    n_sub = tn // LANES

    @pl.when(j == 0)
    def _():
        if d == 1:
            xb_ref[...] = x_ref[...].astype(BF16)
        else:
            xcol, = xcol_ref
            for c in range(xcol.shape[0]):
                xcol[c] = x_ref[:, c * LANES:(c + 1) * LANES]
            for c in range(xcol.shape[0]):
                for r in range(d):
                    xb_ref[r * sub:(r + 1) * sub, c * LANES:(c + 1) * LANES] = (
                        xcol[c, pl.ds(r, sub, stride=d), :].astype(BF16))

    acc = jnp.dot(xb_ref[...], w_ref[...], preferred_element_type=F32) + b_ref[...]
    for u in range(n_sub):
        a = acc[:, u * LANES:(u + 1) * LANES]
        roped = a * cos_ref[...] + pltpu.roll(a, HEAD_DIM // 2, 1) * sin_ref[...]
        flag = flags_ref[j * n_sub + u]
        mult = jnp.where(flag == ROPE_Q, Q_SCALE, 1.0).astype(F32)
        res = (jnp.where(flag == PLAIN, a, roped) * mult).astype(o_ref.dtype)
        for r in range(d):
            o_ref[r, :, u * LANES:(u + 1) * LANES] = res[r * sub:(r + 1) * sub]


def _project(x, w, b, flags, cosx, sinx, d, tm, tn):
    B, S, D = x.shape
    N = w.shape[1]
    L = S // d
    tm = min(tm, S)
    sub = tm // d

    def regroup(tab):
        return tab.reshape(S // tm, sub, d, HEAD_DIM).transpose(0, 2, 1, 3).reshape(S, HEAD_DIM)

    grid_spec = pltpu.PrefetchScalarGridSpec(
        num_scalar_prefetch=1,
        grid=(B, S // tm, N // tn),
        in_specs=[
            pl.BlockSpec((None, tm, D), lambda bb, i, j, f: (bb, i, 0)),
            pl.BlockSpec((D, tn), lambda bb, i, j, f: (0, j)),
            pl.BlockSpec((1, tn), lambda bb, i, j, f: (0, j)),
            pl.BlockSpec((tm, HEAD_DIM), lambda bb, i, j, f: (i, 0)),
            pl.BlockSpec((tm, HEAD_DIM), lambda bb, i, j, f: (i, 0)),
        ],
        out_specs=pl.BlockSpec((None, d, sub, tn), lambda bb, i, j, f: (bb, 0, i, j)),
        scratch_shapes=[pltpu.VMEM((tm, D), BF16)] + ([pltpu.VMEM((D // LANES, tm, LANES), F32)] if d > 1 else []),
    )
    return pl.pallas_call(
        _proj_kernel,
        out_shape=jax.ShapeDtypeStruct((B, d, L, N), BF16),
        grid_spec=grid_spec,
        compiler_params=_cparams(("parallel", "arbitrary", "arbitrary"), VMEM_LIMIT),
        name="proj_rope",
    )(flags, x, w, b, regroup(cosx), regroup(sinx))


def _gelu_tanh(x):
    return 0.5 * x * (1.0 + jnp.tanh(0.7978845608028654 * (x + 0.044715 * x * x * x)))


def _compress_kernel(x_ref, pos_ref, w1_ref, w2_ref, o_ref, xf_sc):
    s_len = x_ref.shape[0]
    nc = s_len // CMP_STRIDE
    xf_sc[0:s_len, :] = x_ref[...].astype(F32)
    xf_sc[s_len:s_len + CMP_STRIDE, :] = jnp.zeros((CMP_STRIDE, HEAD_DIM), F32)
    h = jnp.zeros((nc, w1_ref.shape[1]), F32)
    for l in range(CMP_LEN):
        tok = xf_sc[pl.ds(l, nc, stride=CMP_STRIDE), :]
        a = (tok + pos_ref[l:l + 1, :]).astype(BF16)
        h = h + jnp.dot(a, w1_ref[l * HEAD_DIM:(l + 1) * HEAD_DIM, :], preferred_element_type=F32)
    g = _gelu_tanh(h).astype(BF16)
    o_ref[...] = jnp.dot(g, w2_ref[...], preferred_element_type=F32).astype(o_ref.dtype)


def _compress(main, k_tile0, v_tile0, pos, w1, w2):
    B, _, S, _ = main.shape
    G = NSA_KV_GROUPS
    nC = S // CMP_STRIDE
    hid = w1.shape[-1]
    return pl.pallas_call(
        _compress_kernel,
        out_shape=jax.ShapeDtypeStruct((2, B, G, nC, HEAD_DIM), BF16),
        grid=(2, B, G),
        in_specs=[
            pl.BlockSpec((None, None, S, HEAD_DIM),
                         lambda a, bb, g: (bb, 0, 0, k_tile0 + a * (v_tile0 - k_tile0) + g)),
            pl.BlockSpec((None, CMP_LEN, HEAD_DIM), lambda a, bb, g: (a, 0, 0)),
            pl.BlockSpec((None, CMP_LEN * HEAD_DIM, hid), lambda a, bb, g: (a, 0, 0)),
            pl.BlockSpec((None, hid, HEAD_DIM), lambda a, bb, g: (a, 0, 0)),
        ],
        out_specs=pl.BlockSpec((None, None, None, nC, HEAD_DIM), lambda a, bb, g: (a, bb, g, 0, 0)),
        scratch_shapes=[pltpu.VMEM((S + CMP_STRIDE, HEAD_DIM), F32)],
        compiler_params=_cparams(("arbitrary", "arbitrary", "arbitrary"), VMEM_LIMIT),
        name="compress_mlp",
    )(main, pos, w1, w2)


def _cmp_attn_kernel(q_ref, kc_ref, vc_ref, ov_ref, o_ref, sel_ref, *, n_slc, n_sel):
    qi = pl.program_id(2)
    tq = q_ref.shape[0]
    nc = kc_ref.shape[0]
    t = qi * tq + lax.broadcasted_iota(jnp.int32, (tq, nc), 0)
    c = lax.broadcasted_iota(jnp.int32, (tq, nc), 1)
    mask = (c * CMP_STRIDE + (CMP_LEN - 1)) <= t
    kc = kc_ref[...]
    vc = vc_ref[...]
    ps = jnp.zeros((tq, nc), F32)
    for h in range(NSA_REP):
        s = _qk(q_ref[:, h * HEAD_DIM:(h + 1) * HEAD_DIM], kc)
        p, _, safe = _masked_softmax(s, mask)
        p = p / safe
        o = jnp.dot(p.astype(BF16), vc, preferred_element_type=F32)
        o_ref[:, h * HEAD_DIM:(h + 1) * HEAD_DIM] = o.astype(o_ref.dtype)
        ps = ps + p
    imp = jnp.dot(ps.astype(BF16), ov_ref[...], preferred_element_type=F32)
    tj = qi * tq + lax.broadcasted_iota(jnp.int32, (tq, LANES), 0)
    j = lax.broadcasted_iota(jnp.int32, (tq, LANES), 1)
    cur = tj // SEL_LEN
    forced = (j == 0) | (j == cur) | (j == cur - 1)
    imp = jnp.where(forced, FORCE, jnp.where(j > cur, NEG, imp))
    imp_t = imp.T[0:n_slc, :]
    grp = 8
    groups = [imp_t[a:a + grp, :] for a in range(0, n_slc, grp)]
    ranks = [jnp.zeros(gv.shape, jnp.int32) for gv in groups]
    for j2 in range(n_slc):
        row = imp_t[j2:j2 + 1, :]
        for gi, gv in enumerate(groups):
            lo = gi * grp
            if lo > j2:
                ahead = row >= gv
            elif lo + gv.shape[0] - 1 <= j2:
                ahead = row > gv
            else:
                later = lax.broadcasted_iota(jnp.int32, gv.shape, 0) + lo > j2
                ahead = (row > gv) | ((row == gv) & later)
            ranks[gi] = jnp.where(ahead, ranks[gi] + 1, ranks[gi])
    rank = jnp.concatenate(ranks, axis=0)
    sel_t = jnp.where(rank < n_sel, 0.0, NEG)
    if n_slc < LANES:
        sel_t = jnp.concatenate([sel_t, jnp.zeros((LANES - n_slc, tq), F32)], axis=0)
    sel_ref[...] = sel_t.T.astype(sel_ref.dtype)


def _cmp_attention(main, kvc, overlap, q_blk0, tq):
    B, _, S, _ = main.shape
    G = NSA_KV_GROUPS
    nC = kvc.shape[3]
    n_slc = S // SEL_LEN
    n_sel = min(SEL_TOPK, n_slc)
    kern = functools.partial(_cmp_attn_kernel, n_slc=n_slc, n_sel=n_sel)
    return pl.pallas_call(
        kern,
        out_shape=(jax.ShapeDtypeStruct((B, S, NSA_HEADS * HEAD_DIM), BF16),
                   jax.ShapeDtypeStruct((B, G, S, LANES), BF16)),
        grid=(B, G, S // tq),
        in_specs=[
            pl.BlockSpec((None, None, tq, GROUP_W), lambda bb, g, i: (bb, 0, i, q_blk0 + g)),
            pl.BlockSpec((None, None, None, nC, HEAD_DIM), lambda bb, g, i: (0, bb, g, 0, 0)),
            pl.BlockSpec((None, None, None, nC, HEAD_DIM), lambda bb, g, i: (1, bb, g, 0, 0)),
            pl.BlockSpec((nC, LANES), lambda bb, g, i: (0, 0)),
        ],
        out_specs=(pl.BlockSpec((None, tq, GROUP_W), lambda bb, g, i: (bb, i, g)),
                   pl.BlockSpec((None, None, tq, LANES), lambda bb, g, i: (bb, g, i, 0))),
        compiler_params=_cparams(("parallel", "arbitrary", "arbitrary"), VMEM_LIMIT),
        name="cmp_attn_select",
    )(main, kvc, kvc, overlap)


def _fold_lanes(x, op):
    out = x[:, 0:LANES]
    for u in range(1, x.shape[1] // LANES):
        out = op(out, x[:, u * LANES:(u + 1) * LANES])
    return out


def _sel_attn_kernel(q_ref, k_ref, v_ref, bias_ref, et_ref, wu_ref, wd_ref,
                     o_ref, wu_out, wd_out, qx_sc, s_sc, m_sc, l_sc, acc_sc):
    wu_out[...] = wu_ref[...].astype(wu_out.dtype)
    wd_out[...] = wd_ref[...].astype(wd_out.dtype)
    qi = pl.program_id(2)
    t = q_ref.shape[0]
    tc = s_sc.shape[2]
    rows = NSA_REP * t
    for h in range(NSA_REP):
        qx_sc[h * t:(h + 1) * t, 0:HEAD_DIM] = q_ref[:, h * HEAD_DIM:(h + 1) * HEAD_DIM]
        qx_sc[h * t:(h + 1) * t, HEAD_DIM:2 * HEAD_DIM] = bias_ref[...]

    def scores(c, w=1):
        start = pl.multiple_of(c * tc, tc)
        kx = jnp.concatenate([k_ref[pl.ds(start, w * tc), :], et_ref[pl.ds(start, w * tc), :]], axis=1)
        return _qk(qx_sc[...], kx)

    m_sc[...] = jnp.full(m_sc.shape, NEG, F32)

    def for_spans(n, fn):
        def trip(j, carry):
            fn(2 * j, 2)
            return carry

        lax.fori_loop(0, n // 2, trip, 0)

        @pl.when(n % 2 == 1)
        def _():
            fn(n - 1, 1)

    def max_span(c, w):
        s = scores(c, w)
        for u in range(w):
            s_sc[c + u] = s[:, u * tc:(u + 1) * tc]
        m_sc[...] = jnp.maximum(m_sc[...], _fold_lanes(s, jnp.maximum))

    n_full = (qi * t) // tc
    for_spans(n_full, max_span)
    qpos = qi * t + lax.broadcasted_iota(jnp.int32, (rows, tc), 0) % t
    kpos = n_full * tc + lax.broadcasted_iota(jnp.int32, (rows, tc), 1)
    s = jnp.where(kpos <= qpos, scores(n_full), NEG)
    s_sc[n_full] = s
    m = jnp.maximum(m_sc[...], _fold_lanes(s, jnp.maximum)).max(-1, keepdims=True)
    m_sc[...] = jnp.broadcast_to(m, m_sc.shape)
    l_sc[...] = jnp.zeros(l_sc.shape, F32)
    acc_sc[...] = jnp.zeros(acc_sc.shape, F32)

    def exp_span(c, w):
        mb = m_sc[...]
        p = jnp.concatenate([jnp.exp2(s_sc[c + u][:, v * LANES:(v + 1) * LANES] - mb)
                             for u in range(w) for v in range(tc // LANES)], axis=1)
        l_sc[...] += _fold_lanes(p, jnp.add)
        vals = v_ref[pl.ds(pl.multiple_of(c * tc, tc), w * tc), :]
        acc_sc[...] += jnp.dot(p.astype(BF16), vals, preferred_element_type=F32)

    for_spans(n_full + 1, exp_span)
    l = l_sc[...].sum(-1, keepdims=True)
    o = acc_sc[...] / jnp.where(l > 0, l, 1.0)
    for h in range(NSA_REP):
        o_ref[:, h * HEAD_DIM:(h + 1) * HEAD_DIM] = o[h * t:(h + 1) * t].astype(o_ref.dtype)


def _sel_attention(main, bias, onehot_t, w_up, w_down, q_blk0, k_tile0, v_tile0, t, tc):
    B, _, S, _ = main.shape
    G = NSA_KV_GROUPS
    t = min(t, S)
    tc = min(tc, S)
    rows = NSA_REP * t
    nq = S // t
    n_steps = B * G * nq
    wu2 = w_up.reshape(-1, w_up.shape[-1])
    wd2 = w_down.reshape(-1, w_down.shape[-1])
    ru, rd = wu2.shape[0] // n_steps, wd2.shape[0] // n_steps
    assert ru * n_steps == wu2.shape[0] and rd * n_steps == wd2.shape[0]

    def slab(bb, g, i):
        return ((bb * G + g) * nq + i, 0)

    o, wu_b, wd_b = pl.pallas_call(
        _sel_attn_kernel,
        out_shape=(jax.ShapeDtypeStruct((B, S, NSA_HEADS * HEAD_DIM), BF16),
                   jax.ShapeDtypeStruct(wu2.shape, BF16), jax.ShapeDtypeStruct(wd2.shape, BF16)),
        grid=(B, G, nq),
        in_specs=[
            pl.BlockSpec((None, None, t, GROUP_W), lambda bb, g, i: (bb, 0, i, q_blk0 + g)),
            pl.BlockSpec((None, None, S, HEAD_DIM), lambda bb, g, i: (bb, 0, 0, k_tile0 + g)),
            pl.BlockSpec((None, None, S, HEAD_DIM), lambda bb, g, i: (bb, 0, 0, v_tile0 + g)),
            pl.BlockSpec((None, None, t, LANES), lambda bb, g, i: (bb, g, i, 0)),
            pl.BlockSpec((S, LANES), lambda bb, g, i: (0, 0)),
            pl.BlockSpec((ru, wu2.shape[1]), slab),
            pl.BlockSpec((rd, wd2.shape[1]), slab),
        ],
        out_specs=(pl.BlockSpec((None, t, GROUP_W), lambda bb, g, i: (bb, i, g)),
                   pl.BlockSpec((ru, wu2.shape[1]), slab), pl.BlockSpec((rd, wd2.shape[1]), slab)),
        scratch_shapes=[pltpu.VMEM((rows, 2 * HEAD_DIM), BF16), pltpu.VMEM((S // tc, rows, tc), F32),
                        pltpu.VMEM((rows, LANES), F32), pltpu.VMEM((rows, LANES), F32),
                        pltpu.VMEM((rows, HEAD_DIM), F32)],
        compiler_params=_cparams(("parallel", "arbitrary", "arbitrary"), VMEM_LIMIT),
        name="selected_attn",
    )(main, main, main, bias, onehot_t, wu2, wd2)
    return o, wu_b.reshape(w_up.shape), wd_b.reshape(w_down.shape)


def _band_kernel(*refs, n_prev, n_sub, max_dist, kv_heads, with_lse):
    q_ref = refs[0]
    k_refs = refs[1:2 + n_prev]
    v_refs = refs[2 + n_prev:3 + 2 * n_prev]
    o_ref = refs[3 + 2 * n_prev]
    qi = pl.program_id(2)
    t = k_refs[0].shape[0]
    nk = (n_prev + 1) * t
    n_heads = q_ref.shape[1] // HEAD_DIM
    k_all = jnp.concatenate([r[...] for r in k_refs], axis=0)
    v_all = jnp.concatenate([r[...] for r in v_refs], axis=0)
    kcol = lax.broadcasted_iota(jnp.int32, (t, nk), 1)
    diff = n_prev * t + lax.broadcasted_iota(jnp.int32, (t, nk), 0) - kcol
    in_band = (diff >= 0) & (diff <= max_dist)
    lane = lax.broadcasted_iota(jnp.int32, (t, LANES), 1)
    for u in range(n_sub):
        first_key = (qi * n_sub + u - n_prev) * t
        bias = jnp.where(in_band & (kcol + first_key >= 0), 0.0, NEG)
        rows = slice(u * t, (u + 1) * t)
        k_u = k_all[u * t:u * t + nk]
        v_u = v_all[u * t:u * t + nk]
        lse = jnp.zeros((t, LANES), F32)
        if kv_heads == 1:
            q = jnp.concatenate([q_ref[rows, h * HEAD_DIM:(h + 1) * HEAD_DIM] for h in range(n_heads)], axis=0)
            s = _qk(q, k_u).reshape(n_heads, t, nk) + bias[None]
            m = s.max(-1, keepdims=True)
            p = jnp.exp2(s - m)
            l = p.sum(-1, keepdims=True)
            o = jnp.dot(p.astype(BF16).reshape(n_heads * t, nk), v_u, preferred_element_type=F32)
            o = o.reshape(n_heads, t, HEAD_DIM) / l
            for h in range(n_heads):
                o_ref[rows, h * HEAD_DIM:(h + 1) * HEAD_DIM] = o[h].astype(o_ref.dtype)
                lse = jnp.where(lane == h, m[h] + jnp.log2(l[h]), lse)
        else:
            for h in range(n_heads):
                cols = slice(h * HEAD_DIM, (h + 1) * HEAD_DIM)
                s = _qk(q_ref[rows, cols], k_u[:, cols]) + bias
                m = s.max(-1, keepdims=True)
                p = jnp.exp2(s - m)
                l = p.sum(-1, keepdims=True)
                o = jnp.dot(p.astype(BF16), v_u[:, cols], preferred_element_type=F32)
                o_ref[rows, cols] = (o / l).astype(o_ref.dtype)
                lse = jnp.where(lane == h, m + jnp.log2(l), lse)
        if with_lse:
            refs[4 + 2 * n_prev][rows, :] = lse


def _band_attention(src, lead_grid, length, q_map, k_map, v_map, kv_width, out_shape, o_map,
                    t, n_sub, max_dist, with_lse):
    t = min(t, length)
    n_prev = -(-max_dist // t)
    n_sub = min(n_sub, length // t)
    tile = n_sub * t
    kern = functools.partial(_band_kernel, n_prev=n_prev, n_sub=n_sub, max_dist=max_dist,
                             kv_heads=kv_width // HEAD_DIM, with_lse=with_lse)

    def preceding(fn, j):
        def index_map(bb, a, i):
            return fn(bb, a, jnp.maximum(i * n_sub - n_prev + j, 0))
        return index_map

    def kv_specs(fn):
        return ([pl.BlockSpec((None, None, t, kv_width), preceding(fn, j)) for j in range(n_prev)]
                + [pl.BlockSpec((None, None, tile, kv_width), fn)])

    in_specs = [pl.BlockSpec((None, None, tile, GROUP_W), q_map)] + kv_specs(k_map) + kv_specs(v_map)
    o_spec = pl.BlockSpec((None, tile, GROUP_W), o_map)
    if with_lse:
        lse_shape = out_shape[:-1] + (out_shape[-1] // GROUP_W * LANES,)
        out_shapes = (jax.ShapeDtypeStruct(out_shape, BF16), jax.ShapeDtypeStruct(lse_shape, F32))
        out_specs = (o_spec, pl.BlockSpec((None, tile, LANES), o_map))
    else:
        out_shapes = jax.ShapeDtypeStruct(out_shape, BF16)
        out_specs = o_spec
    return pl.pallas_call(
        kern, out_shape=out_shapes, grid=lead_grid + (length // tile,), in_specs=in_specs, out_specs=out_specs,
        compiler_params=_cparams(("parallel", "arbitrary", "arbitrary"), VMEM_LIMIT),
        name="band_attn",
    )(*([src] * (3 + 2 * n_prev)))


def _layer_norm(z, g, b):
    mu = z.mean(-1, keepdims=True)
    zc = z - mu
    var = (zc * zc).mean(-1, keepdims=True)
    return zc * lax.rsqrt(var + LN_EPS) * g + b


def _merge_kernel(ocmp_ref, oslc_ref, owin_ref, gl_ref, d0_ref, d1_ref, d2_ref, l0_ref, l1_ref, l2_ref,
                  ga_ref, gb_ref, x_ref, wa_ref, wb_ref, wo_ref, g_ref, b_ref, wr_ref, br_ref,
                  hf_ref, hp_ref, lg_ref, nsa_a, nsa_b, dil_a, dil_b, *tok_sc, alpha):
    s = pl.program_id(0)
    tt = x_ref.shape[0]

    def token_order(ref, scr):
        n_col = ref.shape[-1] // LANES
        if len(ref.shape) == 2:
            return [ref[:, c * LANES:(c + 1) * LANES].astype(F32) for c in range(n_col)]
        d, rows = ref.shape[0], ref.shape[1]
        for r in range(d):
            for c in range(n_col):
                scr[c, pl.ds(r, rows, stride=d), :] = ref[r, :, c * LANES:(c + 1) * LANES].astype(F32)
        return [scr[c] for c in range(n_col)]

    @pl.when(s == 0)
    def _():
        nsa_b[...] = jnp.zeros(nsa_b.shape, nsa_b.dtype)
        dil_b[...] = jnp.zeros(dil_b.shape, dil_b.dtype)

    def gate_stage(nsa_out, dil_out):
        gates = jax.nn.sigmoid(gl_ref[...].astype(F32))
        for h in range(NSA_HEADS):
            sl = slice(h * HEAD_DIM, (h + 1) * HEAD_DIM)
            acc = jnp.zeros((tt, HEAD_DIM), F32)
            for br, ref in enumerate((ocmp_ref, oslc_ref, owin_ref)):
                gcol = gates[:, 3 * h + br:3 * h + br + 1]
                acc = acc + gcol * ref[:, sl].astype(F32)
            nsa_out[:, sl] = acc.astype(BF16)
        scr = tok_sc
        outs = [token_order(ref, scr[2 * g]) for g, ref in enumerate((d0_ref, d1_ref, d2_ref))]
        l0, l1, l2 = [token_order(ref, scr[2 * g + 1])[0] for g, ref in enumerate((l0_ref, l1_ref, l2_ref))]
        lm = jnp.maximum(jnp.maximum(l0, l1), l2)
        e0, e1, e2 = jnp.exp2(l0 - lm), jnp.exp2(l1 - lm), jnp.exp2(l2 - lm)
        inv = 1.0 / (e0 + e1 + e2)
        w0, w1, w2 = e0 * inv, e1 * inv, e2 * inv
        for h in range(DIL_HEADS):
            o = w0[:, h:h + 1] * outs[0][h] + w1[:, h:h + 1] * outs[1][h] + w2[:, h:h + 1] * outs[2][h]
            dil_out[:, h * HEAD_DIM:(h + 1) * HEAD_DIM] = o.astype(BF16)

    def matmul_stage(nsa_in, dil_in):
        y_a = jnp.dot(nsa_in[...], wa_ref[...], preferred_element_type=F32)
        y_b = jnp.dot(dil_in[...], wb_ref[...], preferred_element_type=F32)
        merged = (jax.nn.sigmoid(ga_ref[...].astype(F32)) * y_a
                  + jax.nn.sigmoid(gb_ref[...].astype(F32)) * y_b)
        mix = jnp.dot(merged.astype(BF16), wo_ref[...], preferred_element_type=F32)
        h = _layer_norm(alpha * x_ref[...] + mix, g_ref[...], b_ref[...])
        hf_ref[...] = h
        hb = h.astype(BF16)
        _store_tile_rows(hp_ref, 0, _pack_pairs(hb), hp_ref.shape[0] // tt)
        lg_ref[...] = jnp.dot(hb, wr_ref[...], preferred_element_type=F32) + br_ref[...]

    @pl.when(s % 2 == 0)
    def _():
        gate_stage(nsa_a, dil_a)
        matmul_stage(nsa_b, dil_b)

    @pl.when(s % 2 == 1)
    def _():
        gate_stage(nsa_b, dil_b)
        matmul_stage(nsa_a, dil_a)


def _merge(o_cmp, o_slc, o_win, main2d, gl_tile, ga_blk, gb_blk, dil_o, dil_lse, x2d,
           w_a, w_b, w_o, ln_g, ln_b, w_r, b_r, alpha, tt):
    T, D = x2d.shape
    nsub = D // 2 // LANES
    nsa_w = NSA_HEADS * HEAD_DIM
    nt = T // tt
    ahead = lambda s: (jnp.minimum(s, nt - 1), 0)
    behind = lambda s: (jnp.maximum(s - 1, 0), 0)
    once = pl.Buffered(1)

    def const(shape):
        return pl.BlockSpec(shape, lambda s: (0, 0), pipeline_mode=once)

    in_specs = [
        pl.BlockSpec((tt, nsa_w), ahead), pl.BlockSpec((tt, nsa_w), ahead), pl.BlockSpec((tt, nsa_w), ahead),
        pl.BlockSpec((tt, LANES), lambda s: (jnp.minimum(s, nt - 1), gl_tile)),
    ]
    def sub_order(arr):
        if arr.ndim == 2:
            return pl.BlockSpec((tt, arr.shape[1]), ahead)
        _, d, length, cols = arr.shape
        per_batch = length * d // tt

        def index_map(s):
            tile = jnp.minimum(s, nt - 1)
            return (tile // per_batch, 0, tile % per_batch, 0)

        return pl.BlockSpec((None, d, tt // d, cols), index_map)

    in_specs += [sub_order(a) for a in dil_o] + [sub_order(a) for a in dil_lse]
    in_specs += [
        pl.BlockSpec((tt, D), lambda s: (jnp.maximum(s - 1, 0), ga_blk)),
        pl.BlockSpec((tt, D), lambda s: (jnp.maximum(s - 1, 0), gb_blk)),
        pl.BlockSpec((tt, D), behind),
        const((nsa_w, D)), const((DIL_W, D)), const((D, D)), const((1, D)), const((1, D)),
        const((D, LANES)), const((1, LANES)),
    ]
    return pl.pallas_call(
        functools.partial(_merge_kernel, alpha=alpha),
        out_shape=(jax.ShapeDtypeStruct((T, D), F32), jax.ShapeDtypeStruct((T * nsub, LANES), jnp.uint32),
                   jax.ShapeDtypeStruct((T, LANES), F32)),
        grid=(nt + 1,),
        in_specs=in_specs,
        out_specs=(pl.BlockSpec((tt, D), behind), pl.BlockSpec((tt * nsub, LANES), behind),
                   pl.BlockSpec((tt, LANES), behind)),
        scratch_shapes=[pltpu.VMEM((tt, nsa_w), BF16), pltpu.VMEM((tt, nsa_w), BF16),
                        pltpu.VMEM((tt, DIL_W), BF16), pltpu.VMEM((tt, DIL_W), BF16)]
        + [pltpu.VMEM((a.shape[-1] // LANES, tt, LANES), F32) for pair in zip(dil_o, dil_lse) for a in pair],
        compiler_params=_cparams(("arbitrary",), VMEM_LIMIT),
        name="merge_ln1",
    )(o_cmp, o_slc, o_win, main2d, *dil_o, *dil_lse, main2d, main2d, x2d, w_a, w_b, w_o, ln_g, ln_b, w_r, b_r)


def _router_kernel(lg_ref, tri_ref, meta_ref, cnt_ref, carry_sc):
    i = pl.program_id(0)
    tt = lg_ref.shape[0]

    @pl.when(i == 0)
    def _():
        carry_sc[...] = jnp.zeros(carry_sc.shape, F32)

    logits = lg_ref[...]
    lane = lax.broadcasted_iota(jnp.int32, (tt, LANES), 1)
    v = logits
    onehot = jnp.zeros((tt, LANES), F32)
    vals, idxs = [], []
    for _ in range(TOP_K):
        m = v.max(-1, keepdims=True)
        idx = jnp.where(v == m, lane, LANES).min(-1, keepdims=True)
        hit = lane == idx
        vals.append(m)
        idxs.append(idx)
        onehot = onehot + hit.astype(F32)
        v = jnp.where(hit, -jnp.inf, v)
    exps = [jnp.exp(vk - vals[0]) for vk in vals]
    den = exps[0] + exps[1] + exps[2] + exps[3]
    before = jnp.dot(tri_ref[...], onehot.astype(BF16), preferred_element_type=F32) + carry_sc[0:1, :]
    meta = jnp.zeros((tt, LANES), F32)
    for k in range(TOP_K):
        rank = jnp.where(lane == idxs[k], before, 0.0).sum(-1, keepdims=True)
        meta = jnp.where(lane == k, idxs[k].astype(F32), meta)
        meta = jnp.where(lane == TOP_K + k, exps[k] / den, meta)
        meta = jnp.where(lane == 2 * TOP_K + k, rank, meta)
    meta_ref[...] = meta
    carry_sc[...] = carry_sc[...] + jnp.broadcast_to(onehot.sum(0, keepdims=True), carry_sc.shape)
    cnt_ref[...] = carry_sc[...]


def _router(logits, tt):
    T = logits.shape[0]
    tri = (jnp.arange(tt)[:, None] > jnp.arange(tt)[None, :]).astype(BF16)
    return pl.pallas_call(
        _router_kernel,
        out_shape=(jax.ShapeDtypeStruct((T, LANES), F32), jax.ShapeDtypeStruct((8, LANES), F32)),
        grid=(T // tt,),
        in_specs=[
            pl.BlockSpec((tt, LANES), lambda i: (i, 0)),
            pl.BlockSpec((tt, tt), lambda i: (0, 0)),
        ],
        out_specs=(pl.BlockSpec((tt, LANES), lambda i: (i, 0)), pl.BlockSpec((8, LANES), lambda i: (0, 0))),
        scratch_shapes=[pltpu.VMEM((8, LANES), F32)],
        compiler_params=_cparams(("arbitrary",), VMEM_LIMIT),
        name="router_top4",
    )(logits, tri)


def _dispatch_kernel(pe_ref, dest_ref, h_ref, xs_hbm, zero_sc, sem, *, nsub, tm):
    n = h_ref.shape[0] // nsub
    blk = tm * nsub

    @pl.when(pl.program_id(0) == 0)
    def _():
        zero_sc[...] = jnp.zeros(zero_sc.shape, zero_sc.dtype)

        def clear(e):
            end = pe_ref[e]
            begin = pe_ref[e - 1] if e else 0
            start = pl.multiple_of((end - tm) * nsub, nsub)
            return end > begin, pltpu.make_async_copy(zero_sc, xs_hbm.at[pl.ds(start, blk), :], sem)

        for e in range(pe_ref.shape[0]):
            nonempty, copy = clear(e)
            pl.when(nonempty)(copy.start)
        for e in range(pe_ref.shape[0]):
            nonempty, copy = clear(e)
            pl.when(nonempty)(copy.wait)

    for t in range(n):
        for k in range(TOP_K):
            d = pl.multiple_of(dest_ref[t * TOP_K + k] * nsub, nsub)
            pltpu.make_async_copy(h_ref.at[pl.ds(t * nsub, nsub), :], xs_hbm.at[pl.ds(d, nsub), :],
                                  sem).start(priority=k % 2)
    for k in range(TOP_K):
        pltpu.make_async_copy(h_ref, xs_hbm.at[pl.ds(0, n * nsub), :], sem).wait()


def _dispatch(hp, dest_flat, pad_end, n_rows, nsub, tm, tt):
    T = hp.shape[0] // nsub
    grid_spec = pltpu.PrefetchScalarGridSpec(
        num_scalar_prefetch=1,
        grid=(T // tt,),
        in_specs=[
            pl.BlockSpec((tt * TOP_K,), lambda i, pe: (i,), memory_space=pltpu.SMEM),
            pl.BlockSpec((tt * nsub, LANES), lambda i, pe: (i, 0)),
        ],
        out_specs=pl.BlockSpec(memory_space=pl.ANY),
        scratch_shapes=[pltpu.VMEM((tm * nsub, LANES), hp.dtype), pltpu.SemaphoreType.DMA(())],
    )
    return pl.pallas_call(
        functools.partial(_dispatch_kernel, nsub=nsub, tm=tm),
        out_shape=jax.ShapeDtypeStruct((n_rows * nsub, LANES), hp.dtype),
        grid_spec=grid_spec,
        compiler_params=_cparams(("arbitrary",), VMEM_LIMIT),
        name="moe_dispatch",
    )(pad_end, dest_flat, hp)


def _expert_kernel(be_ref, nu_ref, x_ref, wg_ref, wl_ref, bg_ref, bl_ref, wd_ref, bd_ref, y_ref, xb_sc, *, th, nsub):
    i = pl.program_id(0)
    tm = xb_sc.shape[0]
    half = nsub * LANES
    dh = wg_ref.shape[1]

    @pl.when(i < nu_ref[0])
    def _():
        lo, hi = _unpack_pairs(_load_tile_rows(x_ref, 0, tm, nsub))
        xb_sc[:, :half] = lo.astype(BF16)
        xb_sc[:, half:] = hi.astype(BF16)
        x = xb_sc[...]
        y = bd_ref[...]
        for c in range(dh // th):
            sl = slice(c * th, (c + 1) * th)
            glu = jnp.dot(x, wg_ref[:, sl], preferred_element_type=F32) + bg_ref[:, sl]
            lin = jnp.dot(x, wl_ref[:, sl], preferred_element_type=F32) + bl_ref[:, sl]
            glu = jnp.minimum(glu, SWIGLU_LIMIT)
            lin = jnp.clip(lin, -SWIGLU_LIMIT, SWIGLU_LIMIT)
            act = glu * jax.nn.sigmoid(SWIGLU_ALPHA * glu) * (lin + 1.0)
            y = y + jnp.dot(act.astype(BF16), wd_ref[sl, :], preferred_element_type=F32)
        _store_tile_rows(y_ref, 0, _pack_pairs(y.astype(BF16)), nsub)


def _experts(xs, blk_expert, n_used, w_up, b_up, w_down, b_down, tm, th):
    E, D, two_dh = w_up.shape
    nsub = D // 2 // LANES
    n_blk = xs.shape[0] // (tm * nsub)
    dh = two_dh // 2
    th = min(th, dh)
    once = pl.Buffered(1)

    def row(i, be, nu):
        return (jnp.minimum(i, nu[0] - 1), 0)

    grid_spec = pltpu.PrefetchScalarGridSpec(
        num_scalar_prefetch=2,
        grid=(n_blk,),
        in_specs=[
            pl.BlockSpec((tm * nsub, LANES), row),
            pl.BlockSpec((None, D, dh), lambda i, be, nu: (be[i], 0, 0), pipeline_mode=once),
            pl.BlockSpec((None, D, dh), lambda i, be, nu: (be[i], 0, 1), pipeline_mode=once),
            pl.BlockSpec((None, 1, dh), lambda i, be, nu: (be[i], 0, 0)),
            pl.BlockSpec((None, 1, dh), lambda i, be, nu: (be[i], 0, 1)),
            pl.BlockSpec((None, dh, D), lambda i, be, nu: (be[i], 0, 0), pipeline_mode=once),
            pl.BlockSpec((None, 1, D), lambda i, be, nu: (be[i], 0, 0)),
        ],
        out_specs=pl.BlockSpec((tm * nsub, LANES), row),
        scratch_shapes=[pltpu.VMEM((tm, D), BF16)],
    )
    return pl.pallas_call(
        functools.partial(_expert_kernel, th=th, nsub=nsub),
        out_shape=jax.ShapeDtypeStruct(xs.shape, jnp.uint32),
        grid_spec=grid_spec,
        compiler_params=_cparams(("arbitrary",), VMEM_LIMIT),
        name="moe_experts",
    )(blk_expert, n_used, xs, w_up, w_up, b_up, b_up, w_down, b_down)


def _combine_kernel(dest_ref, dest_next_ref, meta_ref, h_ref, g_ref, b_ref, y_hbm, o_ref, buf, sem, *, alpha, nsub):
    i = pl.program_id(0)
    n = o_ref.shape[0]
    slot = i % 2

    def gather(idx_ref, s):
        for t in range(n):
            for k in range(TOP_K):
                d = pl.multiple_of(idx_ref[t * TOP_K + k] * nsub, nsub)
                pltpu.make_async_copy(y_hbm.at[pl.ds(d, nsub), :], buf.at[s, pl.ds((k * n + t) * nsub, nsub), :],
                                      sem.at[s]).start(priority=k % 2)

    @pl.when(i == 0)
    def _():
        gather(dest_ref, 0)

    @pl.when(i + 1 < pl.num_programs(0))
    def _():
        gather(dest_next_ref, 1 - slot)

    for k in range(TOP_K):
        pltpu.make_async_copy(y_hbm.at[pl.ds(0, n * nsub), :], buf.at[slot, pl.ds(k * n * nsub, n * nsub), :],
                              sem.at[slot]).wait()
    ffn_lo = ffn_hi = None
    for k in range(TOP_K):
        lo, hi = _unpack_pairs(_load_tile_rows(buf, k * n * nsub, n, nsub, lead=(slot,)))
        gate = meta_ref[:, TOP_K + k:TOP_K + k + 1]
        ffn_lo = gate * lo if k == 0 else ffn_lo + gate * lo
        ffn_hi = gate * hi if k == 0 else ffn_hi + gate * hi
    ffn = jnp.concatenate([ffn_lo, ffn_hi], axis=1)
    o_ref[...] = _layer_norm(alpha * h_ref[...] + ffn, g_ref[...], b_ref[...])


def _combine_ln(y, dest_flat, meta, h, g, b, alpha, tt):
    T, D = h.shape
    nsub = D // 2 // LANES
    nt = T // tt
    return pl.pallas_call(
        functools.partial(_combine_kernel, alpha=alpha, nsub=nsub),
        out_shape=jax.ShapeDtypeStruct((T, D), F32),
        grid=(nt,),
        in_specs=[
            pl.BlockSpec((tt * TOP_K,), lambda i: (i,), memory_space=pltpu.SMEM),
            pl.BlockSpec((tt * TOP_K,), lambda i: (jnp.minimum(i + 1, nt - 1),), memory_space=pltpu.SMEM),
            pl.BlockSpec((tt, LANES), lambda i: (i, 0)),
            pl.BlockSpec((tt, D), lambda i: (i, 0)),
            pl.BlockSpec((1, D), lambda i: (0, 0)),
            pl.BlockSpec((1, D), lambda i: (0, 0)),
            pl.BlockSpec(memory_space=pl.ANY),
        ],
        out_specs=pl.BlockSpec((tt, D), lambda i: (i, 0)),
        scratch_shapes=[pltpu.VMEM((2, TOP_K * tt * nsub, LANES), y.dtype), pltpu.SemaphoreType.DMA((2,))],
        compiler_params=_cparams(("arbitrary",), VMEM_LIMIT),
        name="moe_combine_ln2",
    )(dest_flat, dest_flat, meta, h, g, b, y)


def _layer(x, w_in, b_in, pos_k, pos_v, ck_w1, ck_w2, cv_w1, cv_w2, w_br_nsa, w_br_dil, w_out,
           ln1_g, ln1_b, w_router, b_router, w_up, b_up, w_down, b_down, ln2_g, ln2_b, alpha):
    B, S, D = x.shape
    T = B * S
    nd = D // LANES
    G = NSA_KV_GROUPS
    kvw = G * HEAD_DIM
    n_exp = w_router.shape[1]

    o_q = 0
    o_kv = NSA_HEADS * HEAD_DIM
    o_gl = o_kv + 6 * kvw
    o_dil = o_gl + 3 * NSA_HEADS
    o_ga = o_dil + 3 * N_DIL * DIL_W
    o_gb = o_ga + D

    w_in_b = w_in.astype(BF16)

    def wcols(a, n):
        return w_in_b[:, a:a + n], b_in[a:a + n]

    def kv(i):
        return wcols(o_kv + i * kvw, kvw)

    tn = 1024
    gl_w, gl_b = wcols(o_gl, 3 * NSA_HEADS)
    pieces = [wcols(o_ga, D), wcols(o_gb, D), wcols(o_q, NSA_HEADS * HEAD_DIM),
              kv(0), kv(2), kv(4), kv(1), kv(3), kv(5), (gl_w, gl_b)]
    used = sum(p[0].shape[1] for p in pieces)
    n_main = -(-used // tn) * tn
    pieces.append((jnp.zeros((D, n_main - used), BF16), jnp.zeros((n_main - used,), F32)))
    w_main = jnp.concatenate([p[0] for p in pieces], axis=1)
    b_main = jnp.concatenate([p[1] for p in pieces])[None, :]
    t_q = 2 * nd
    t_kc, t_ks, t_kw = t_q + 8, t_q + 10, t_q + 12
    t_vc, t_vs, t_vw = t_q + 14, t_q + 16, t_q + 18
    t_gl = t_q + 20
    tile_id = jnp.arange(n_main // LANES)
    flags_main = jnp.where((tile_id >= t_q) & (tile_id < t_kc), ROPE_Q,
                           jnp.where((tile_id >= t_kc) & (tile_id < t_vc), ROPE, PLAIN)).astype(jnp.int32)
    q_blk0 = t_q // NSA_REP

    pos = jnp.arange(S, dtype=F32)
    inv = ROPE_THETA ** (-jnp.arange(0, HEAD_DIM, 2, dtype=F32) / HEAD_DIM)
    ang = pos[:, None] * inv[None, :]
    cosx = jnp.concatenate([jnp.cos(ang), jnp.cos(ang)], axis=-1)
    sinx = jnp.concatenate([-jnp.sin(ang), jnp.sin(ang)], axis=-1)

    main = _project(x, w_main, b_main, flags_main, cosx, sinx, 1, 1024, tn)

    nC = S // CMP_STRIDE
    w1 = jnp.stack([ck_w1, cv_w1]).astype(BF16)
    w2 = jnp.stack([ck_w2, cv_w2]).astype(BF16)
    kvc = _compress(main, t_kc, t_vc, jnp.stack([pos_k, pos_v]), w1, w2)

    n_slc = S // SEL_LEN
    assert n_slc <= LANES
    c_start = jnp.arange(nC) * CMP_STRIDE
    jb = jnp.arange(LANES)
    overlap = ((c_start[:, None] < (jb[None, :] + 1) * SEL_LEN) & (c_start[:, None] + CMP_LEN > jb[None, :] * SEL_LEN)
               & (jb[None, :] < n_slc) & (c_start[:, None] + CMP_LEN <= S)).astype(BF16)
    tq = min(512, S)
    o_cmp, sel = _cmp_attention(main, kvc, overlap, q_blk0, tq)

    onehot_t = (jnp.arange(S)[:, None] // SEL_LEN == jnp.arange(LANES)[None, :]).astype(BF16)
    o_slc, w_up_b, w_down_b = _sel_attention(main, sel, onehot_t, w_up, w_down, q_blk0, t_ks, t_vs, 256, 512)

    o_win = _band_attention(
        main, (B, G), S,
        lambda bb, g, i: (bb, 0, i, q_blk0 + g),
        lambda bb, g, i: (bb, 0, i, t_kw + g),
        lambda bb, g, i: (bb, 0, i, t_vw + g),
        HEAD_DIM, (B, S, NSA_HEADS * HEAD_DIM), lambda bb, g, i: (bb, i, g), 128, 4, WIN_LEN - 1, False)

    flags_dil = jnp.array([ROPE_Q] * DIL_HEADS + [ROPE] * DIL_HEADS + [PLAIN] * DIL_HEADS, jnp.int32)
    dil_o, dil_lse = [], []
    for gi, (w, d) in enumerate(DIL_CONFIGS):
        wd, bd = wcols(o_dil + gi * 3 * DIL_W, 3 * DIL_W)
        sub = _project(x, wd, bd[None, :], flags_dil, cosx, sinx, d, 512, 3 * DIL_W)
        L = S // d
        o_g, lse_g = _band_attention(
            sub, (B, d), L,
            lambda bb, r, i: (bb, r, i, 0),
            lambda bb, r, i: (bb, r, i, 1),
            lambda bb, r, i: (bb, r, i, 2),
            DIL_W, (B * d, L, DIL_W), lambda bb, r, i, d=d: (bb * d + r, i, 0), 128, 4, w // d, True)
        dil_o.append(o_g.reshape(T, DIL_W) if d == 1 else o_g.reshape(B, d, L, DIL_W))
        dil_lse.append(lse_g.reshape(T, LANES) if d == 1 else lse_g.reshape(B, d, L, LANES))

    w_r = jnp.concatenate([w_router, jnp.zeros((D, LANES - n_exp), F32)], axis=1).astype(BF16)
    b_r = jnp.concatenate([b_router, jnp.full((LANES - n_exp,), NEG, F32)])[None, :]
    h_f, h_p, logits = _merge(
        o_cmp.reshape(T, -1), o_slc.reshape(T, -1), o_win.reshape(T, -1), main.reshape(T, n_main),
        t_gl, 0, 1, dil_o, dil_lse, x.reshape(T, D),
        w_br_nsa.astype(BF16), w_br_dil.astype(BF16), w_out.astype(BF16),
        ln1_g[None, :], ln1_b[None, :], w_r, b_r, alpha, min(256, T))

    meta, cnt = _router(logits, min(512, T))
    top_idx = meta[:, 0:TOP_K].astype(jnp.int32)
    rank = meta[:, 2 * TOP_K:3 * TOP_K].astype(jnp.int32)

    tm = 512 if T * TOP_K >= 512 * n_exp else 128
    counts = cnt[0, :n_exp].astype(jnp.int32)
    padded = (counts + tm - 1) // tm * tm
    pad_end = jnp.cumsum(padded)
    pad_start = pad_end - padded
    start_of = jnp.where(top_idx[..., None] == jnp.arange(n_exp), pad_start, 0).sum(-1)
    dest = (start_of + rank).reshape(T * TOP_K)
    n_rows = T * TOP_K + n_exp * tm
    n_blk = n_rows // tm
    blk_start = jnp.arange(n_blk, dtype=jnp.int32) * tm
    blk_expert = jnp.minimum((pad_end[None, :] <= blk_start[:, None]).sum(-1), n_exp - 1).astype(jnp.int32)
    n_used = (pad_end[-1:] // tm).astype(jnp.int32)

    xs = _dispatch(h_p, dest, pad_end.astype(jnp.int32), n_rows, D // 2 // LANES, tm, min(256, T))
    y = _experts(xs, blk_expert, n_used, w_up_b, b_up[:, None, :], w_down_b, b_down[:, None, :], tm, 512)
    out = _combine_ln(y, dest, meta, h_f, ln2_g[None, :], ln2_b[None, :], alpha, min(256, T))
    return out.reshape(B, S, D)


def kernel(x, w_in, b_in, cmp_pos_k, cmp_pos_v, cmp_k_w1, cmp_k_w2, cmp_v_w1, cmp_v_w2, w_br_nsa, w_br_dil,
           w_out, ln1_g, ln1_b, w_router, b_router, w_up, b_up, w_down, b_down, ln2_g, ln2_b):
    depth = w_in.shape[0]
    alpha = (2.0 * depth) ** 0.25
    h = x
    for l in range(depth):
        h = _layer(h, w_in[l], b_in[l], cmp_pos_k[l], cmp_pos_v[l], cmp_k_w1[l], cmp_k_w2[l],
                   cmp_v_w1[l], cmp_v_w2[l], w_br_nsa[l], w_br_dil[l], w_out[l], ln1_g[l], ln1_b[l],
                   w_router[l], b_router[l], w_up[l], b_up[l], w_down[l], b_down[l], ln2_g[l], ln2_b[l], alpha)
    return h
```

```python
import functools

import jax
import jax.numpy as jnp
from jax import lax
from jax.experimental import pallas as pl
from jax.experimental.pallas import tpu as pltpu

F32 = jnp.float32
BF16 = jnp.bfloat16

HEAD_DIM = 128
LANES = 128
ROPE_THETA = 10000.0
NSA_HEADS = 8
NSA_KV_GROUPS = 2
NSA_REP = NSA_HEADS // NSA_KV_GROUPS
CMP_LEN = 32
CMP_STRIDE = 16
SEL_LEN = 64
SEL_TOPK = 16
WIN_LEN = 512
DIL_CONFIGS = ((128, 1), (512, 4), (2048, 16))
N_DIL = len(DIL_CONFIGS)
DIL_HEADS = 4
TOP_K = 4
SWIGLU_LIMIT = 7.0
SWIGLU_ALPHA = 1.702
LN_EPS = 1e-5
NEG = -1e30
FORCE = 1e9
SCALE = HEAD_DIM ** -0.5
LOG2E = 1.4426950408889634
Q_SCALE = SCALE * LOG2E
PLAIN, ROPE, ROPE_Q = 0, 1, 2

GROUP_W = NSA_REP * HEAD_DIM
DIL_W = DIL_HEADS * HEAD_DIM
VMEM_LIMIT = 56 * 1024 * 1024


def _cparams(sem, vmem=None):
    return pltpu.CompilerParams(dimension_semantics=sem, vmem_limit_bytes=vmem)


def _masked_softmax(s, mask):
    s = jnp.where(mask, s, NEG)
    m = s.max(-1, keepdims=True)
    p = jnp.where(mask, jnp.exp2(s - m), 0.0)
    den = p.sum(-1, keepdims=True)
    safe = jnp.where(den > 0, den, 1.0)
    return p, m, safe


def _qk(q, k):
    return lax.dot_general(q, k, (((1,), (1,)), ((), ())), preferred_element_type=F32)


def _pack_pairs(xb):
    m = xb.shape[1] // 2
    lo = lax.bitcast_convert_type(xb[:, :m].astype(F32), jnp.uint32)
    hi = lax.bitcast_convert_type(xb[:, m:].astype(F32), jnp.uint32)
    return (lo >> 16) | (hi & jnp.uint32(0xFFFF0000))


def _unpack_pairs(w):
    lo = lax.bitcast_convert_type(w << 16, F32)
    hi = lax.bitcast_convert_type(w & jnp.uint32(0xFFFF0000), F32)
    return lo, hi


def _store_tile_rows(ref, start, packed, nsub):
    n = packed.shape[0]
    for j in range(nsub):
        ref[pl.ds(start + j, n, stride=nsub), :] = packed[:, j * LANES:(j + 1) * LANES]


def _load_tile_rows(ref, start, n, nsub, lead=()):
    return jnp.concatenate([ref[lead + (pl.ds(start + j, n, stride=nsub), slice(None))] for j in range(nsub)],
                           axis=1)


def _proj_kernel(flags_ref, x_ref, w_ref, b_ref, cos_ref, sin_ref, o_ref, xb_ref, *xcol_ref):
    j = pl.program_id(2)
    d, sub, tn = o_ref.shape
    n_sub = tn // LANES

    @pl.when(j == 0)
    def _():
        if d == 1:
            xb_ref[...] = x_ref[...].astype(BF16)
        else:
            xcol, = xcol_ref
            for c in range(xcol.shape[0]):
                xcol[c] = x_ref[:, c * LANES:(c + 1) * LANES]
            for c in range(xcol.shape[0]):
                for r in range(d):
                    xb_ref[r * sub:(r + 1) * sub, c * LANES:(c + 1) * LANES] = (
                        xcol[c, pl.ds(r, sub, stride=d), :].astype(BF16))

    acc = jnp.dot(xb_ref[...], w_ref[...], preferred_element_type=F32) + b_ref[...]
    for u in range(n_sub):
        a = acc[:, u * LANES:(u + 1) * LANES]
        roped = a * cos_ref[...] + pltpu.roll(a, HEAD_DIM // 2, 1) * sin_ref[...]
        flag = flags_ref[j * n_sub + u]
        mult = jnp.where(flag == ROPE_Q, Q_SCALE, 1.0).astype(F32)
        res = (jnp.where(flag == PLAIN, a, roped) * mult).astype(o_ref.dtype)
        for r in range(d):
            o_ref[r, :, u * LANES:(u + 1) * LANES] = res[r * sub:(r + 1) * sub]


def _project(x, w, b, flags, cosx, sinx, d, tm, tn):
    B, S, D = x.shape
    N = w.shape[1]
    L = S // d
    tm = min(tm, S)
    sub = tm // d

    def regroup(tab):
        return tab.reshape(S // tm, sub, d, HEAD_DIM).transpose(0, 2, 1, 3).reshape(S, HEAD_DIM)

    grid_spec = pltpu.PrefetchScalarGridSpec(
        num_scalar_prefetch=1,
        grid=(B, S // tm, N // tn),
        in_specs=[
            pl.BlockSpec((None, tm, D), lambda bb, i, j, f: (bb, i, 0)),
            pl.BlockSpec((D, tn), lambda bb, i, j, f: (0, j)),
            pl.BlockSpec((1, tn), lambda bb, i, j, f: (0, j)),
            pl.BlockSpec((tm, HEAD_DIM), lambda bb, i, j, f: (i, 0)),
            pl.BlockSpec((tm, HEAD_DIM), lambda bb, i, j, f: (i, 0)),
        ],
        out_specs=pl.BlockSpec((None, d, sub, tn), lambda bb, i, j, f: (bb, 0, i, j)),
        scratch_shapes=[pltpu.VMEM((tm, D), BF16)] + ([pltpu.VMEM((D // LANES, tm, LANES), F32)] if d > 1 else []),
    )
    return pl.pallas_call(
        _proj_kernel,
        out_shape=jax.ShapeDtypeStruct((B, d, L, N), BF16),
        grid_spec=grid_spec,
        compiler_params=_cparams(("parallel", "arbitrary", "arbitrary"), VMEM_LIMIT),
        name="proj_rope",
    )(flags, x, w, b, regroup(cosx), regroup(sinx))


def _gelu_tanh(x):
    return 0.5 * x * (1.0 + jnp.tanh(0.7978845608028654 * (x + 0.044715 * x * x * x)))


def _compress_kernel(x_ref, pos_ref, w1_ref, w2_ref, o_ref, xf_sc):
    s_len = x_ref.shape[0]
    nc = s_len // CMP_STRIDE
    xf_sc[0:s_len, :] = x_ref[...].astype(F32)
    xf_sc[s_len:s_len + CMP_STRIDE, :] = jnp.zeros((CMP_STRIDE, HEAD_DIM), F32)
    h = jnp.zeros((nc, w1_ref.shape[1]), F32)
    for l in range(CMP_LEN):
        tok = xf_sc[pl.ds(l, nc, stride=CMP_STRIDE), :]
        a = (tok + pos_ref[l:l + 1, :]).astype(BF16)
        h = h + jnp.dot(a, w1_ref[l * HEAD_DIM:(l + 1) * HEAD_DIM, :], preferred_element_type=F32)
    g = _gelu_tanh(h).astype(BF16)
    o_ref[...] = jnp.dot(g, w2_ref[...], preferred_element_type=F32).astype(o_ref.dtype)


def _compress(main, k_tile0, v_tile0, pos, w1, w2):
    B, _, S, _ = main.shape
    G = NSA_KV_GROUPS
    nC = S // CMP_STRIDE
    hid = w1.shape[-1]
    return pl.pallas_call(
        _compress_kernel,
        out_shape=jax.ShapeDtypeStruct((2, B, G, nC, HEAD_DIM), BF16),
        grid=(2, B, G),
        in_specs=[
            pl.BlockSpec((None, None, S, HEAD_DIM),
                         lambda a, bb, g: (bb, 0, 0, k_tile0 + a * (v_tile0 - k_tile0) + g)),
            pl.BlockSpec((None, CMP_LEN, HEAD_DIM), lambda a, bb, g: (a, 0, 0)),
            pl.BlockSpec((None, CMP_LEN * HEAD_DIM, hid), lambda a, bb, g: (a, 0, 0)),
            pl.BlockSpec((None, hid, HEAD_DIM), lambda a, bb, g: (a, 0, 0)),
        ],
        out_specs=pl.BlockSpec((None, None, None, nC, HEAD_DIM), lambda a, bb, g: (a, bb, g, 0, 0)),
        scratch_shapes=[pltpu.VMEM((S + CMP_STRIDE, HEAD_DIM), F32)],
        compiler_params=_cparams(("arbitrary", "arbitrary", "arbitrary"), VMEM_LIMIT),
        name="compress_mlp",
    )(main, pos, w1, w2)


def _cmp_attn_kernel(q_ref, kc_ref, vc_ref, ov_ref, o_ref, sel_ref, *, n_slc, n_sel):
    qi = pl.program_id(2)
    tq = q_ref.shape[0]
    nc = kc_ref.shape[0]
    t = qi * tq + lax.broadcasted_iota(jnp.int32, (tq, nc), 0)
    c = lax.broadcasted_iota(jnp.int32, (tq, nc), 1)
    mask = (c * CMP_STRIDE + (CMP_LEN - 1)) <= t
    kc = kc_ref[...]
    vc = vc_ref[...]
    ps = jnp.zeros((tq, nc), F32)
    for h in range(NSA_REP):
        s = _qk(q_ref[:, h * HEAD_DIM:(h + 1) * HEAD_DIM], kc)
        p, _, safe = _masked_softmax(s, mask)
        p = p / safe
        o = jnp.dot(p.astype(BF16), vc, preferred_element_type=F32)
        o_ref[:, h * HEAD_DIM:(h + 1) * HEAD_DIM] = o.astype(o_ref.dtype)
        ps = ps + p
    imp = jnp.dot(ps.astype(BF16), ov_ref[...], preferred_element_type=F32)
    tj = qi * tq + lax.broadcasted_iota(jnp.int32, (tq, LANES), 0)
    j = lax.broadcasted_iota(jnp.int32, (tq, LANES), 1)
    cur = tj // SEL_LEN
    forced = (j == 0) | (j == cur) | (j == cur - 1)
    imp = jnp.where(forced, FORCE, jnp.where(j > cur, NEG, imp))
    imp_t = imp.T[0:n_slc, :]
    grp = 8
    groups = [imp_t[a:a + grp, :] for a in range(0, n_slc, grp)]
    ranks = [jnp.zeros(gv.shape, jnp.int32) for gv in groups]
    for j2 in range(n_slc):
        row = imp_t[j2:j2 + 1, :]
        for gi, gv in enumerate(groups):
            lo = gi * grp
            if lo > j2:
                ahead = row >= gv
            elif lo + gv.shape[0] - 1 <= j2:
                ahead = row > gv
            else:
                later = lax.broadcasted_iota(jnp.int32, gv.shape, 0) + lo > j2
                ahead = (row > gv) | ((row == gv) & later)
            ranks[gi] = jnp.where(ahead, ranks[gi] + 1, ranks[gi])
    rank = jnp.concatenate(ranks, axis=0)
    sel_t = jnp.where(rank < n_sel, 0.0, NEG)
    if n_slc < LANES:
        sel_t = jnp.concatenate([sel_t, jnp.zeros((LANES - n_slc, tq), F32)], axis=0)
    sel_ref[...] = sel_t.T.astype(sel_ref.dtype)


def _cmp_attention(main, kvc, overlap, q_blk0, tq):
    B, _, S, _ = main.shape
    G = NSA_KV_GROUPS
    nC = kvc.shape[3]
    n_slc = S // SEL_LEN
    n_sel = min(SEL_TOPK, n_slc)
    kern = functools.partial(_cmp_attn_kernel, n_slc=n_slc, n_sel=n_sel)
    return pl.pallas_call(
        kern,
        out_shape=(jax.ShapeDtypeStruct((B, S, NSA_HEADS * HEAD_DIM), BF16),
                   jax.ShapeDtypeStruct((B, G, S, LANES), BF16)),
        grid=(B, G, S // tq),
        in_specs=[
            pl.BlockSpec((None, None, tq, GROUP_W), lambda bb, g, i: (bb, 0, i, q_blk0 + g)),
            pl.BlockSpec((None, None, None, nC, HEAD_DIM), lambda bb, g, i: (0, bb, g, 0, 0)),
            pl.BlockSpec((None, None, None, nC, HEAD_DIM), lambda bb, g, i: (1, bb, g, 0, 0)),
            pl.BlockSpec((nC, LANES), lambda bb, g, i: (0, 0)),
        ],
        out_specs=(pl.BlockSpec((None, tq, GROUP_W), lambda bb, g, i: (bb, i, g)),
                   pl.BlockSpec((None, None, tq, LANES), lambda bb, g, i: (bb, g, i, 0))),
        compiler_params=_cparams(("parallel", "arbitrary", "arbitrary"), VMEM_LIMIT),
        name="cmp_attn_select",
    )(main, kvc, kvc, overlap)


def _fold_lanes(x, op):
    out = x[:, 0:LANES]
    for u in range(1, x.shape[1] // LANES):
        out = op(out, x[:, u * LANES:(u + 1) * LANES])
    return out


def _sel_attn_kernel(q_ref, k_ref, v_ref, bias_ref, et_ref, wu_ref, wd_ref,
                     o_ref, wu_out, wd_out, qx_sc, s_sc, m_sc, l_sc, acc_sc):
    wu_out[...] = wu_ref[...].astype(wu_out.dtype)
    wd_out[...] = wd_ref[...].astype(wd_out.dtype)
    qi = pl.program_id(2)
    t = q_ref.shape[0]
    tc = s_sc.shape[2]
    rows = NSA_REP * t
    for h in range(NSA_REP):
        qx_sc[h * t:(h + 1) * t, 0:HEAD_DIM] = q_ref[:, h * HEAD_DIM:(h + 1) * HEAD_DIM]
        qx_sc[h * t:(h + 1) * t, HEAD_DIM:2 * HEAD_DIM] = bias_ref[...]

    def scores(c, w=1):
        start = pl.multiple_of(c * tc, tc)
        kx = jnp.concatenate([k_ref[pl.ds(start, w * tc), :], et_ref[pl.ds(start, w * tc), :]], axis=1)
        return _qk(qx_sc[...], kx)

    m_sc[...] = jnp.full(m_sc.shape, NEG, F32)

    def for_spans(n, fn):
        def trip(j, carry):
            fn(2 * j, 2)
            return carry

        lax.fori_loop(0, n // 2, trip, 0)

        @pl.when(n % 2 == 1)
        def _():
            fn(n - 1, 1)

    def max_span(c, w):
        s = scores(c, w)
        for u in range(w):
            s_sc[c + u] = s[:, u * tc:(u + 1) * tc]
        m_sc[...] = jnp.maximum(m_sc[...], _fold_lanes(s, jnp.maximum))

    n_full = (qi * t) // tc
    for_spans(n_full, max_span)
    qpos = qi * t + lax.broadcasted_iota(jnp.int32, (rows, tc), 0) % t
    kpos = n_full * tc + lax.broadcasted_iota(jnp.int32, (rows, tc), 1)
    s = jnp.where(kpos <= qpos, scores(n_full), NEG)
    s_sc[n_full] = s
    m = jnp.maximum(m_sc[...], _fold_lanes(s, jnp.maximum)).max(-1, keepdims=True)
    m_sc[...] = jnp.broadcast_to(m, m_sc.shape)
    l_sc[...] = jnp.zeros(l_sc.shape, F32)
    acc_sc[...] = jnp.zeros(acc_sc.shape, F32)

    def exp_span(c, w):
        mb = m_sc[...]
        p = jnp.concatenate([jnp.exp2(s_sc[c + u][:, v * LANES:(v + 1) * LANES] - mb)
                             for u in range(w) for v in range(tc // LANES)], axis=1)
        l_sc[...] += _fold_lanes(p, jnp.add)
        vals = v_ref[pl.ds(pl.multiple_of(c * tc, tc), w * tc), :]
        acc_sc[...] += jnp.dot(p.astype(BF16), vals, preferred_element_type=F32)

    for_spans(n_full + 1, exp_span)
    l = l_sc[...].sum(-1, keepdims=True)
    o = acc_sc[...] / jnp.where(l > 0, l, 1.0)
    for h in range(NSA_REP):
        o_ref[:, h * HEAD_DIM:(h + 1) * HEAD_DIM] = o[h * t:(h + 1) * t].astype(o_ref.dtype)


def _sel_attention(main, bias, onehot_t, w_up, w_down, q_blk0, k_tile0, v_tile0, t, tc):
    B, _, S, _ = main.shape
    G = NSA_KV_GROUPS
    t = min(t, S)
    tc = min(tc, S)
    rows = NSA_REP * t
    nq = S // t
    n_steps = B * G * nq
    wu2 = w_up.reshape(-1, w_up.shape[-1])
    wd2 = w_down.reshape(-1, w_down.shape[-1])
    ru, rd = wu2.shape[0] // n_steps, wd2.shape[0] // n_steps
    assert ru * n_steps == wu2.shape[0] and rd * n_steps == wd2.shape[0]

    def slab(bb, g, i):
        return ((bb * G + g) * nq + i, 0)

    o, wu_b, wd_b = pl.pallas_call(
        _sel_attn_kernel,
        out_shape=(jax.ShapeDtypeStruct((B, S, NSA_HEADS * HEAD_DIM), BF16),
                   jax.ShapeDtypeStruct(wu2.shape, BF16), jax.ShapeDtypeStruct(wd2.shape, BF16)),
        grid=(B, G, nq),
        in_specs=[
            pl.BlockSpec((None, None, t, GROUP_W), lambda bb, g, i: (bb, 0, i, q_blk0 + g)),
            pl.BlockSpec((None, None, S, HEAD_DIM), lambda bb, g, i: (bb, 0, 0, k_tile0 + g)),
            pl.BlockSpec((None, None, S, HEAD_DIM), lambda bb, g, i: (bb, 0, 0, v_tile0 + g)),
            pl.BlockSpec((None, None, t, LANES), lambda bb, g, i: (bb, g, i, 0)),
            pl.BlockSpec((S, LANES), lambda bb, g, i: (0, 0)),
            pl.BlockSpec((ru, wu2.shape[1]), slab),
            pl.BlockSpec((rd, wd2.shape[1]), slab),
        ],
        out_specs=(pl.BlockSpec((None, t, GROUP_W), lambda bb, g, i: (bb, i, g)),
                   pl.BlockSpec((ru, wu2.shape[1]), slab), pl.BlockSpec((rd, wd2.shape[1]), slab)),
        scratch_shapes=[pltpu.VMEM((rows, 2 * HEAD_DIM), BF16), pltpu.VMEM((S // tc, rows, tc), F32),
                        pltpu.VMEM((rows, LANES), F32), pltpu.VMEM((rows, LANES), F32),
                        pltpu.VMEM((rows, HEAD_DIM), F32)],
        compiler_params=_cparams(("parallel", "arbitrary", "arbitrary"), VMEM_LIMIT),
        name="selected_attn",
    )(main, main, main, bias, onehot_t, wu2, wd2)
    return o, wu_b.reshape(w_up.shape), wd_b.reshape(w_down.shape)


def _band_kernel(*refs, n_prev, n_sub, max_dist, kv_heads, with_lse):
    q_ref = refs[0]
    k_refs = refs[1:2 + n_prev]
    v_refs = refs[2 + n_prev:3 + 2 * n_prev]
    o_ref = refs[3 + 2 * n_prev]
    qi = pl.program_id(2)
    t = k_refs[0].shape[0]
    nk = (n_prev + 1) * t
    n_heads = q_ref.shape[1] // HEAD_DIM
    k_all = jnp.concatenate([r[...] for r in k_refs], axis=0)
    v_all = jnp.concatenate([r[...] for r in v_refs], axis=0)
    kcol = lax.broadcasted_iota(jnp.int32, (t, nk), 1)
    diff = n_prev * t + lax.broadcasted_iota(jnp.int32, (t, nk), 0) - kcol
    in_band = (diff >= 0) & (diff <= max_dist)
    lane = lax.broadcasted_iota(jnp.int32, (t, LANES), 1)
    for u in range(n_sub):
        first_key = (qi * n_sub + u - n_prev) * t
        bias = jnp.where(in_band & (kcol + first_key >= 0), 0.0, NEG)
        rows = slice(u * t, (u + 1) * t)
        k_u = k_all[u * t:u * t + nk]
        v_u = v_all[u * t:u * t + nk]
        lse = jnp.zeros((t, LANES), F32)
        if kv_heads == 1:
            q = jnp.concatenate([q_ref[rows, h * HEAD_DIM:(h + 1) * HEAD_DIM] for h in range(n_heads)], axis=0)
            s = _qk(q, k_u).reshape(n_heads, t, nk) + bias[None]
            m = s.max(-1, keepdims=True)
            p = jnp.exp2(s - m)
            l = p.sum(-1, keepdims=True)
            o = jnp.dot(p.astype(BF16).reshape(n_heads * t, nk), v_u, preferred_element_type=F32)
            o = o.reshape(n_heads, t, HEAD_DIM) / l
            for h in range(n_heads):
                o_ref[rows, h * HEAD_DIM:(h + 1) * HEAD_DIM] = o[h].astype(o_ref.dtype)
                lse = jnp.where(lane == h, m[h] + jnp.log2(l[h]), lse)
        else:
            for h in range(n_heads):
                cols = slice(h * HEAD_DIM, (h + 1) * HEAD_DIM)
                s = _qk(q_ref[rows, cols], k_u[:, cols]) + bias
                m = s.max(-1, keepdims=True)
                p = jnp.exp2(s - m)
                l = p.sum(-1, keepdims=True)
                o = jnp.dot(p.astype(BF16), v_u[:, cols], preferred_element_type=F32)
                o_ref[rows, cols] = (o / l).astype(o_ref.dtype)
                lse = jnp.where(lane == h, m + jnp.log2(l), lse)
        if with_lse:
            refs[4 + 2 * n_prev][rows, :] = lse


def _band_attention(src, lead_grid, length, q_map, k_map, v_map, kv_width, out_shape, o_map,
                    t, n_sub, max_dist, with_lse):
    t = min(t, length)
    n_prev = -(-max_dist // t)
    n_sub = min(n_sub, length // t)
    tile = n_sub * t
    kern = functools.partial(_band_kernel, n_prev=n_prev, n_sub=n_sub, max_dist=max_dist,
                             kv_heads=kv_width // HEAD_DIM, with_lse=with_lse)

    def preceding(fn, j):
        def index_map(bb, a, i):
            return fn(bb, a, jnp.maximum(i * n_sub - n_prev + j, 0))
        return index_map

    def kv_specs(fn):
        return ([pl.BlockSpec((None, None, t, kv_width), preceding(fn, j)) for j in range(n_prev)]
                + [pl.BlockSpec((None, None, tile, kv_width), fn)])

    in_specs = [pl.BlockSpec((None, None, tile, GROUP_W), q_map)] + kv_specs(k_map) + kv_specs(v_map)
    o_spec = pl.BlockSpec((None, tile, GROUP_W), o_map)
    if with_lse:
        lse_shape = out_shape[:-1] + (out_shape[-1] // GROUP_W * LANES,)
        out_shapes = (jax.ShapeDtypeStruct(out_shape, BF16), jax.ShapeDtypeStruct(lse_shape, F32))
        out_specs = (o_spec, pl.BlockSpec((None, tile, LANES), o_map))
    else:
        out_shapes = jax.ShapeDtypeStruct(out_shape, BF16)
        out_specs = o_spec
    return pl.pallas_call(
        kern, out_shape=out_shapes, grid=lead_grid + (length // tile,), in_specs=in_specs, out_specs=out_specs,
        compiler_params=_cparams(("parallel", "arbitrary", "arbitrary"), VMEM_LIMIT),
        name="band_attn",
    )(*([src] * (3 + 2 * n_prev)))


def _layer_norm(z, g, b):
    mu = z.mean(-1, keepdims=True)
    zc = z - mu
    var = (zc * zc).mean(-1, keepdims=True)
    return zc * lax.rsqrt(var + LN_EPS) * g + b


def _merge_kernel(ocmp_ref, oslc_ref, owin_ref, gl_ref, d0_ref, d1_ref, d2_ref, l0_ref, l1_ref, l2_ref,
                  ga_ref, gb_ref, x_ref, wa_ref, wb_ref, wo_ref, g_ref, b_ref, wr_ref, br_ref,
                  tri_ref, hf_ref, hp_ref, meta_ref, cnt_ref, nsa_a, nsa_b, dil_a, dil_b, carry_sc, *tok_sc, alpha):
    s = pl.program_id(0)
    tt = x_ref.shape[0]

    def token_order(ref, scr):
        n_col = ref.shape[-1] // LANES
        if len(ref.shape) == 2:
            return [ref[:, c * LANES:(c + 1) * LANES].astype(F32) for c in range(n_col)]
        d, rows = ref.shape[0], ref.shape[1]
        for r in range(d):
            for c in range(n_col):
                scr[c, pl.ds(r, rows, stride=d), :] = ref[r, :, c * LANES:(c + 1) * LANES].astype(F32)
        return [scr[c] for c in range(n_col)]

    @pl.when(s == 0)
    def _():
        nsa_b[...] = jnp.zeros(nsa_b.shape, nsa_b.dtype)
        dil_b[...] = jnp.zeros(dil_b.shape, dil_b.dtype)
        carry_sc[...] = jnp.zeros(carry_sc.shape, F32)

    def gate_stage(nsa_out, dil_out):
        gates = jax.nn.sigmoid(gl_ref[...].astype(F32))
        for h in range(NSA_HEADS):
            sl = slice(h * HEAD_DIM, (h + 1) * HEAD_DIM)
            acc = jnp.zeros((tt, HEAD_DIM), F32)
            for br, ref in enumerate((ocmp_ref, oslc_ref, owin_ref)):
                gcol = gates[:, 3 * h + br:3 * h + br + 1]
                acc = acc + gcol * ref[:, sl].astype(F32)
            nsa_out[:, sl] = acc.astype(BF16)
        scr = tok_sc
        outs = [token_order(ref, scr[2 * g]) for g, ref in enumerate((d0_ref, d1_ref, d2_ref))]
        l0, l1, l2 = [token_order(ref, scr[2 * g + 1])[0] for g, ref in enumerate((l0_ref, l1_ref, l2_ref))]
        lm = jnp.maximum(jnp.maximum(l0, l1), l2)
        e0, e1, e2 = jnp.exp2(l0 - lm), jnp.exp2(l1 - lm), jnp.exp2(l2 - lm)
        inv = 1.0 / (e0 + e1 + e2)
        w0, w1, w2 = e0 * inv, e1 * inv, e2 * inv
        for h in range(DIL_HEADS):
            o = w0[:, h:h + 1] * outs[0][h] + w1[:, h:h + 1] * outs[1][h] + w2[:, h:h + 1] * outs[2][h]
            dil_out[:, h * HEAD_DIM:(h + 1) * HEAD_DIM] = o.astype(BF16)

    def matmul_stage(nsa_in, dil_in):
        y_a = jnp.dot(nsa_in[...], wa_ref[...], preferred_element_type=F32)
        y_b = jnp.dot(dil_in[...], wb_ref[...], preferred_element_type=F32)
        merged = (jax.nn.sigmoid(ga_ref[...].astype(F32)) * y_a
                  + jax.nn.sigmoid(gb_ref[...].astype(F32)) * y_b)
        mix = jnp.dot(merged.astype(BF16), wo_ref[...], preferred_element_type=F32)
        h = _layer_norm(alpha * x_ref[...] + mix, g_ref[...], b_ref[...])
        hf_ref[...] = h
        hb = h.astype(BF16)
        _store_tile_rows(hp_ref, 0, _pack_pairs(hb), hp_ref.shape[0] // tt)
        logits = jnp.dot(hb, wr_ref[...], preferred_element_type=F32) + br_ref[...]
        meta, counts = _route(logits, tri_ref[...], carry_sc[0:1, :])
        meta_ref[...] = meta
        counts = counts * jnp.where(s >= 1, 1.0, 0.0)
        carry_sc[...] = carry_sc[...] + jnp.broadcast_to(counts, carry_sc.shape)
        cnt_ref[...] = carry_sc[...]

    @pl.when(s % 2 == 0)
    def _():
        gate_stage(nsa_a, dil_a)
        matmul_stage(nsa_b, dil_b)

    @pl.when(s % 2 == 1)
    def _():
        gate_stage(nsa_b, dil_b)
        matmul_stage(nsa_a, dil_a)


def _merge(o_cmp, o_slc, o_win, main2d, gl_tile, ga_blk, gb_blk, dil_o, dil_lse, x2d,
           w_a, w_b, w_o, ln_g, ln_b, w_r, b_r, alpha, tt):
    T, D = x2d.shape
    nsub = D // 2 // LANES
    nsa_w = NSA_HEADS * HEAD_DIM
    nt = T // tt
    ahead = lambda s: (jnp.minimum(s, nt - 1), 0)
    behind = lambda s: (jnp.maximum(s - 1, 0), 0)
    once = pl.Buffered(1)

    def const(shape):
        return pl.BlockSpec(shape, lambda s: (0, 0), pipeline_mode=once)

    in_specs = [
        pl.BlockSpec((tt, nsa_w), ahead), pl.BlockSpec((tt, nsa_w), ahead), pl.BlockSpec((tt, nsa_w), ahead),
        pl.BlockSpec((tt, LANES), lambda s: (jnp.minimum(s, nt - 1), gl_tile)),
    ]
    def sub_order(arr):
        if arr.ndim == 2:
            return pl.BlockSpec((tt, arr.shape[1]), ahead)
        _, d, length, cols = arr.shape
        per_batch = length * d // tt

        def index_map(s):
            tile = jnp.minimum(s, nt - 1)
            return (tile // per_batch, 0, tile % per_batch, 0)

        return pl.BlockSpec((None, d, tt // d, cols), index_map)

    in_specs += [sub_order(a) for a in dil_o] + [sub_order(a) for a in dil_lse]
    in_specs += [
        pl.BlockSpec((tt, D), lambda s: (jnp.maximum(s - 1, 0), ga_blk)),
        pl.BlockSpec((tt, D), lambda s: (jnp.maximum(s - 1, 0), gb_blk)),
        pl.BlockSpec((tt, D), behind),
        const((nsa_w, D)), const((DIL_W, D)), const((D, D)), const((1, D)), const((1, D)),
        const((D, LANES)), const((1, LANES)), const((tt, tt)),
    ]
    tri = (jnp.arange(tt)[:, None] > jnp.arange(tt)[None, :]).astype(BF16)
    return pl.pallas_call(
        functools.partial(_merge_kernel, alpha=alpha),
        out_shape=(jax.ShapeDtypeStruct((T, D), F32), jax.ShapeDtypeStruct((T * nsub, LANES), jnp.uint32),
                   jax.ShapeDtypeStruct((T, LANES), F32), jax.ShapeDtypeStruct((8, LANES), F32)),
        grid=(nt + 1,),
        in_specs=in_specs,
        out_specs=(pl.BlockSpec((tt, D), behind), pl.BlockSpec((tt * nsub, LANES), behind),
                   pl.BlockSpec((tt, LANES), behind), pl.BlockSpec((8, LANES), lambda s: (0, 0))),
        scratch_shapes=[pltpu.VMEM((tt, nsa_w), BF16), pltpu.VMEM((tt, nsa_w), BF16),
                        pltpu.VMEM((tt, DIL_W), BF16), pltpu.VMEM((tt, DIL_W), BF16), pltpu.VMEM((8, LANES), F32)]
        + [pltpu.VMEM((a.shape[-1] // LANES, tt, LANES), F32) for pair in zip(dil_o, dil_lse) for a in pair],
        compiler_params=_cparams(("arbitrary",), VMEM_LIMIT),
        name="merge_ln1",
    )(o_cmp, o_slc, o_win, main2d, *dil_o, *dil_lse, main2d, main2d, x2d, w_a, w_b, w_o, ln_g, ln_b, w_r, b_r, tri)


def _route(logits, tri, carry_row):
    tt = logits.shape[0]
    lane = lax.broadcasted_iota(jnp.int32, (tt, LANES), 1)
    v = logits
    onehot = jnp.zeros((tt, LANES), F32)
    vals, idxs = [], []
    for _ in range(TOP_K):
        m = v.max(-1, keepdims=True)
        idx = jnp.where(v == m, lane, LANES).min(-1, keepdims=True)
        hit = lane == idx
        vals.append(m)
        idxs.append(idx)
        onehot = onehot + hit.astype(F32)
        v = jnp.where(hit, -jnp.inf, v)
    exps = [jnp.exp(vk - vals[0]) for vk in vals]
    den = exps[0] + exps[1] + exps[2] + exps[3]
    before = jnp.dot(tri, onehot.astype(BF16), preferred_element_type=F32) + carry_row
    meta = jnp.zeros((tt, LANES), F32)
    for k in range(TOP_K):
        rank = jnp.where(lane == idxs[k], before, 0.0).sum(-1, keepdims=True)
        meta = jnp.where(lane == k, idxs[k].astype(F32), meta)
        meta = jnp.where(lane == TOP_K + k, exps[k] / den, meta)
        meta = jnp.where(lane == 2 * TOP_K + k, rank, meta)
    return meta, onehot.sum(0, keepdims=True)


def _dispatch_kernel(pe_ref, dest_ref, h_ref, xs_hbm, zero_sc, sem, *, nsub, tm):
    n = h_ref.shape[0] // nsub
    blk = tm * nsub

    @pl.when(pl.program_id(0) == 0)
    def _():
        zero_sc[...] = jnp.zeros(zero_sc.shape, zero_sc.dtype)

        def clear(e):
            end = pe_ref[e]
            begin = pe_ref[e - 1] if e else 0
            start = pl.multiple_of((end - tm) * nsub, nsub)
            return end > begin, pltpu.make_async_copy(zero_sc, xs_hbm.at[pl.ds(start, blk), :], sem)

        for e in range(pe_ref.shape[0]):
            nonempty, copy = clear(e)
            pl.when(nonempty)(copy.start)
        for e in range(pe_ref.shape[0]):
            nonempty, copy = clear(e)
            pl.when(nonempty)(copy.wait)

    for t in range(n):
        for k in range(TOP_K):
            d = pl.multiple_of(dest_ref[t * TOP_K + k] * nsub, nsub)
            pltpu.make_async_copy(h_ref.at[pl.ds(t * nsub, nsub), :], xs_hbm.at[pl.ds(d, nsub), :],
                                  sem).start(priority=k % 2)
    for k in range(TOP_K):
        pltpu.make_async_copy(h_ref, xs_hbm.at[pl.ds(0, n * nsub), :], sem).wait()


def _dispatch(hp, dest_flat, pad_end, n_rows, nsub, tm, tt):
    T = hp.shape[0] // nsub
    grid_spec = pltpu.PrefetchScalarGridSpec(
        num_scalar_prefetch=1,
        grid=(T // tt,),
        in_specs=[
            pl.BlockSpec((tt * TOP_K,), lambda i, pe: (i,), memory_space=pltpu.SMEM),
            pl.BlockSpec((tt * nsub, LANES), lambda i, pe: (i, 0)),
        ],
        out_specs=pl.BlockSpec(memory_space=pl.ANY),
        scratch_shapes=[pltpu.VMEM((tm * nsub, LANES), hp.dtype), pltpu.SemaphoreType.DMA(())],
    )
    return pl.pallas_call(
        functools.partial(_dispatch_kernel, nsub=nsub, tm=tm),
        out_shape=jax.ShapeDtypeStruct((n_rows * nsub, LANES), hp.dtype),
        grid_spec=grid_spec,
        compiler_params=_cparams(("arbitrary",), VMEM_LIMIT),
        name="moe_dispatch",
    )(pad_end, dest_flat, hp)


def _expert_kernel(be_ref, nu_ref, x_ref, wg_ref, wl_ref, bg_ref, bl_ref, wd_ref, bd_ref, y_ref, xb_sc, *, th, nsub):
    i = pl.program_id(0)
    tm = xb_sc.shape[0]
    half = nsub * LANES
    dh = wg_ref.shape[1]

    @pl.when(i < nu_ref[0])
    def _():
        lo, hi = _unpack_pairs(_load_tile_rows(x_ref, 0, tm, nsub))
        xb_sc[:, :half] = lo.astype(BF16)
        xb_sc[:, half:] = hi.astype(BF16)
        x = xb_sc[...]
        y = bd_ref[...]
        for c in range(dh // th):
            sl = slice(c * th, (c + 1) * th)
            glu = jnp.dot(x, wg_ref[:, sl], preferred_element_type=F32) + bg_ref[:, sl]
            lin = jnp.dot(x, wl_ref[:, sl], preferred_element_type=F32) + bl_ref[:, sl]
            glu = jnp.minimum(glu, SWIGLU_LIMIT)
            lin = jnp.clip(lin, -SWIGLU_LIMIT, SWIGLU_LIMIT)
            act = glu * jax.nn.sigmoid(SWIGLU_ALPHA * glu) * (lin + 1.0)
            y = y + jnp.dot(act.astype(BF16), wd_ref[sl, :], preferred_element_type=F32)
        _store_tile_rows(y_ref, 0, _pack_pairs(y.astype(BF16)), nsub)


def _experts(xs, blk_expert, n_used, w_up, b_up, w_down, b_down, tm, th):
    E, D, two_dh = w_up.shape
    nsub = D // 2 // LANES
    n_blk = xs.shape[0] // (tm * nsub)
    dh = two_dh // 2
    th = min(th, dh)
    once = pl.Buffered(1)

    def row(i, be, nu):
        return (jnp.minimum(i, nu[0] - 1), 0)

    grid_spec = pltpu.PrefetchScalarGridSpec(
        num_scalar_prefetch=2,
        grid=(n_blk,),
        in_specs=[
            pl.BlockSpec((tm * nsub, LANES), row),
            pl.BlockSpec((None, D, dh), lambda i, be, nu: (be[i], 0, 0), pipeline_mode=once),
            pl.BlockSpec((None, D, dh), lambda i, be, nu: (be[i], 0, 1), pipeline_mode=once),
            pl.BlockSpec((None, 1, dh), lambda i, be, nu: (be[i], 0, 0)),
            pl.BlockSpec((None, 1, dh), lambda i, be, nu: (be[i], 0, 1)),
            pl.BlockSpec((None, dh, D), lambda i, be, nu: (be[i], 0, 0), pipeline_mode=once),
            pl.BlockSpec((None, 1, D), lambda i, be, nu: (be[i], 0, 0)),
        ],
        out_specs=pl.BlockSpec((tm * nsub, LANES), row),
        scratch_shapes=[pltpu.VMEM((tm, D), BF16)],
    )
    return pl.pallas_call(
        functools.partial(_expert_kernel, th=th, nsub=nsub),
        out_shape=jax.ShapeDtypeStruct(xs.shape, jnp.uint32),
        grid_spec=grid_spec,
        compiler_params=_cparams(("arbitrary",), VMEM_LIMIT),
        name="moe_experts",
    )(blk_expert, n_used, xs, w_up, w_up, b_up, b_up, w_down, b_down)


def _combine_kernel(dest_ref, dest_next_ref, meta_ref, h_ref, g_ref, b_ref, y_hbm, o_ref, buf, sem, *, alpha, nsub):
    i = pl.program_id(0)
    n = o_ref.shape[0]
    slot = i % 2

    def gather(idx_ref, s):
        for t in range(n):
            for k in range(TOP_K):
                d = pl.multiple_of(idx_ref[t * TOP_K + k] * nsub, nsub)
                pltpu.make_async_copy(y_hbm.at[pl.ds(d, nsub), :], buf.at[s, pl.ds((k * n + t) * nsub, nsub), :],
                                      sem.at[s]).start(priority=k % 2)

    @pl.when(i == 0)
    def _():
        gather(dest_ref, 0)

    @pl.when(i + 1 < pl.num_programs(0))
    def _():
        gather(dest_next_ref, 1 - slot)

    for k in range(TOP_K):
        pltpu.make_async_copy(y_hbm.at[pl.ds(0, n * nsub), :], buf.at[slot, pl.ds(k * n * nsub, n * nsub), :],
                              sem.at[slot]).wait()
    ffn_lo = ffn_hi = None
    for k in range(TOP_K):
        lo, hi = _unpack_pairs(_load_tile_rows(buf, k * n * nsub, n, nsub, lead=(slot,)))
        gate = meta_ref[:, TOP_K + k:TOP_K + k + 1]
        ffn_lo = gate * lo if k == 0 else ffn_lo + gate * lo
        ffn_hi = gate * hi if k == 0 else ffn_hi + gate * hi
    ffn = jnp.concatenate([ffn_lo, ffn_hi], axis=1)
    o_ref[...] = _layer_norm(alpha * h_ref[...] + ffn, g_ref[...], b_ref[...])


def _combine_ln(y, dest_flat, meta, h, g, b, alpha, tt):
    T, D = h.shape
    nsub = D // 2 // LANES
    nt = T // tt
    return pl.pallas_call(
        functools.partial(_combine_kernel, alpha=alpha, nsub=nsub),
        out_shape=jax.ShapeDtypeStruct((T, D), F32),
        grid=(nt,),
        in_specs=[
            pl.BlockSpec((tt * TOP_K,), lambda i: (i,), memory_space=pltpu.SMEM),
            pl.BlockSpec((tt * TOP_K,), lambda i: (jnp.minimum(i + 1, nt - 1),), memory_space=pltpu.SMEM),
            pl.BlockSpec((tt, LANES), lambda i: (i, 0)),
            pl.BlockSpec((tt, D), lambda i: (i, 0)),
            pl.BlockSpec((1, D), lambda i: (0, 0)),
            pl.BlockSpec((1, D), lambda i: (0, 0)),
            pl.BlockSpec(memory_space=pl.ANY),
        ],
        out_specs=pl.BlockSpec((tt, D), lambda i: (i, 0)),
        scratch_shapes=[pltpu.VMEM((2, TOP_K * tt * nsub, LANES), y.dtype), pltpu.SemaphoreType.DMA((2,))],
        compiler_params=_cparams(("arbitrary",), VMEM_LIMIT),
        name="moe_combine_ln2",
    )(dest_flat, dest_flat, meta, h, g, b, y)


def _layer(x, w_in, b_in, pos_k, pos_v, ck_w1, ck_w2, cv_w1, cv_w2, w_br_nsa, w_br_dil, w_out,
           ln1_g, ln1_b, w_router, b_router, w_up, b_up, w_down, b_down, ln2_g, ln2_b, alpha):
    B, S, D = x.shape
    T = B * S
    nd = D // LANES
    G = NSA_KV_GROUPS
    kvw = G * HEAD_DIM
    n_exp = w_router.shape[1]

    o_q = 0
    o_kv = NSA_HEADS * HEAD_DIM
    o_gl = o_kv + 6 * kvw
    o_dil = o_gl + 3 * NSA_HEADS
    o_ga = o_dil + 3 * N_DIL * DIL_W
    o_gb = o_ga + D

    def wcols(a, n):
        return w_in[:, a:a + n], b_in[a:a + n]

    def kv(i):
        return wcols(o_kv + i * kvw, kvw)

    tn = 1024
    gl_w, gl_b = wcols(o_gl, 3 * NSA_HEADS)
    pieces = [wcols(o_ga, D), wcols(o_gb, D), wcols(o_q, NSA_HEADS * HEAD_DIM),
              kv(0), kv(2), kv(4), kv(1), kv(3), kv(5), (gl_w, gl_b)]
    used = sum(p[0].shape[1] for p in pieces)
    n_main = -(-used // tn) * tn
    pieces.append((jnp.zeros((D, n_main - used), F32), jnp.zeros((n_main - used,), F32)))
    w_main = jnp.concatenate([p[0] for p in pieces], axis=1).astype(BF16)
    b_main = jnp.concatenate([p[1] for p in pieces])[None, :]
    t_q = 2 * nd
    t_kc, t_ks, t_kw = t_q + 8, t_q + 10, t_q + 12
    t_vc, t_vs, t_vw = t_q + 14, t_q + 16, t_q + 18
    t_gl = t_q + 20
    tile_id = jnp.arange(n_main // LANES)
    flags_main = jnp.where((tile_id >= t_q) & (tile_id < t_kc), ROPE_Q,
                           jnp.where((tile_id >= t_kc) & (tile_id < t_vc), ROPE, PLAIN)).astype(jnp.int32)
    q_blk0 = t_q // NSA_REP

    pos = jnp.arange(S, dtype=F32)
    inv = ROPE_THETA ** (-jnp.arange(0, HEAD_DIM, 2, dtype=F32) / HEAD_DIM)
    ang = pos[:, None] * inv[None, :]
    cosx = jnp.concatenate([jnp.cos(ang), jnp.cos(ang)], axis=-1)
    sinx = jnp.concatenate([-jnp.sin(ang), jnp.sin(ang)], axis=-1)

    main = _project(x, w_main, b_main, flags_main, cosx, sinx, 1, 1024, tn)

    nC = S // CMP_STRIDE
    w1 = jnp.stack([ck_w1, cv_w1]).astype(BF16)
    w2 = jnp.stack([ck_w2, cv_w2]).astype(BF16)
    kvc = _compress(main, t_kc, t_vc, jnp.stack([pos_k, pos_v]), w1, w2)

    n_slc = S // SEL_LEN
    assert n_slc <= LANES
    c_start = jnp.arange(nC) * CMP_STRIDE
    jb = jnp.arange(LANES)
    overlap = ((c_start[:, None] < (jb[None, :] + 1) * SEL_LEN) & (c_start[:, None] + CMP_LEN > jb[None, :] * SEL_LEN)
               & (jb[None, :] < n_slc) & (c_start[:, None] + CMP_LEN <= S)).astype(BF16)
    tq = min(512, S)
    o_cmp, sel = _cmp_attention(main, kvc, overlap, q_blk0, tq)

    onehot_t = (jnp.arange(S)[:, None] // SEL_LEN == jnp.arange(LANES)[None, :]).astype(BF16)
    o_slc, w_up_b, w_down_b = _sel_attention(main, sel, onehot_t, w_up, w_down, q_blk0, t_ks, t_vs, 256, 512)

    o_win = _band_attention(
        main, (B, G), S,
        lambda bb, g, i: (bb, 0, i, q_blk0 + g),
        lambda bb, g, i: (bb, 0, i, t_kw + g),
        lambda bb, g, i: (bb, 0, i, t_vw + g),
        HEAD_DIM, (B, S, NSA_HEADS * HEAD_DIM), lambda bb, g, i: (bb, i, g), 128, 4, WIN_LEN - 1, False)

    flags_dil = jnp.array([ROPE_Q] * DIL_HEADS + [ROPE] * DIL_HEADS + [PLAIN] * DIL_HEADS, jnp.int32)
    dil_o, dil_lse = [], []
    for gi, (w, d) in enumerate(DIL_CONFIGS):
        wd, bd = wcols(o_dil + gi * 3 * DIL_W, 3 * DIL_W)
        sub = _project(x, wd.astype(BF16), bd[None, :], flags_dil, cosx, sinx, d, 512, 3 * DIL_W)
        L = S // d
        o_g, lse_g = _band_attention(
            sub, (B, d), L,
            lambda bb, r, i: (bb, r, i, 0),
            lambda bb, r, i: (bb, r, i, 1),
            lambda bb, r, i: (bb, r, i, 2),
            DIL_W, (B * d, L, DIL_W), lambda bb, r, i, d=d: (bb * d + r, i, 0), 128, 4, w // d, True)
        dil_o.append(o_g.reshape(T, DIL_W) if d == 1 else o_g.reshape(B, d, L, DIL_W))
        dil_lse.append(lse_g.reshape(T, LANES) if d == 1 else lse_g.reshape(B, d, L, LANES))

    w_r = jnp.concatenate([w_router, jnp.zeros((D, LANES - n_exp), F32)], axis=1).astype(BF16)
    b_r = jnp.concatenate([b_router, jnp.full((LANES - n_exp,), NEG, F32)])[None, :]
    h_f, h_p, meta, cnt = _merge(
        o_cmp.reshape(T, -1), o_slc.reshape(T, -1), o_win.reshape(T, -1), main.reshape(T, n_main),
        t_gl, 0, 1, dil_o, dil_lse, x.reshape(T, D),
        w_br_nsa.astype(BF16), w_br_dil.astype(BF16), w_out.astype(BF16),
        ln1_g[None, :], ln1_b[None, :], w_r, b_r, alpha, min(256, T))

    top_idx = meta[:, 0:TOP_K].astype(jnp.int32)
    rank = meta[:, 2 * TOP_K:3 * TOP_K].astype(jnp.int32)

    tm = 512 if T * TOP_K >= 512 * n_exp else 128
    counts = cnt[0, :n_exp].astype(jnp.int32)
    padded = (counts + tm - 1) // tm * tm
    pad_end = jnp.cumsum(padded)
    pad_start = pad_end - padded
    start_of = jnp.where(top_idx[..., None] == jnp.arange(n_exp), pad_start, 0).sum(-1)
    dest = (start_of + rank).reshape(T * TOP_K)
    n_rows = T * TOP_K + n_exp * tm
    n_blk = n_rows // tm
    blk_start = jnp.arange(n_blk, dtype=jnp.int32) * tm
    blk_expert = jnp.minimum((pad_end[None, :] <= blk_start[:, None]).sum(-1), n_exp - 1).astype(jnp.int32)
    n_used = (pad_end[-1:] // tm).astype(jnp.int32)

    xs = _dispatch(h_p, dest, pad_end.astype(jnp.int32), n_rows, D // 2 // LANES, tm, min(256, T))
    y = _experts(xs, blk_expert, n_used, w_up_b, b_up[:, None, :], w_down_b, b_down[:, None, :], tm, 512)
    out = _combine_ln(y, dest, meta, h_f, ln2_g[None, :], ln2_b[None, :], alpha, min(256, T))
    return out.reshape(B, S, D)


def kernel(x, w_in, b_in, cmp_pos_k, cmp_pos_v, cmp_k_w1, cmp_k_w2, cmp_v_w1, cmp_v_w2, w_br_nsa, w_br_dil,
           w_out, ln1_g, ln1_b, w_router, b_router, w_up, b_up, w_down, b_down, ln2_g, ln2_b):
    depth = w_in.shape[0]
    alpha = (2.0 * depth) ** 0.25
    h = x
    for l in range(depth):
        h = _layer(h, w_in[l], b_in[l], cmp_pos_k[l], cmp_pos_v[l], cmp_k_w1[l], cmp_k_w2[l],
                   cmp_v_w1[l], cmp_v_w2[l], w_br_nsa[l], w_br_dil[l], w_out[l], ln1_g[l], ln1_b[l],
                   w_router[l], b_router[l], w_up[l], b_up[l], w_down[l], b_down[l], ln2_g[l], ln2_b[l], alpha)
    return h
```
